```python
import jax, jax.numpy as jnp
from jax import lax
import numpy as np

D_MODEL = 1024
BATCH = 2
SEQ = 16384
DEPTH = 2

N_HEADS = 8
N_KV_HEADS = 2
HEAD_DIM = 64
ATTN_WIDTH = N_HEADS * HEAD_DIM
KV_WIDTH = N_KV_HEADS * HEAD_DIM
WINDOW = 128
BLOCK = 128
N_BUCKETS = 32
MAX_DISTANCE = 128
N_FOURIER_GROUPS = 4
FOURIER_GROUP_DIM = 128
FOURIER_WIDTH = N_FOURIER_GROUPS * FOURIER_GROUP_DIM
N_BRANCHES = 2
IN_WIDTH = ATTN_WIDTH + 2 * KV_WIDTH + FOURIER_WIDTH + N_BRANCHES * D_MODEL
N_EXPERTS = 16
CAPACITY_FACTOR = 2
D_FF = 2816
EPS = 1e-6
NEG_INF = -1e30

kernel_name = "hybrid_swa_fnet_ec_moe_encoder"


def rms_norm(x, g):
    xf = x.astype(jnp.float32)
    y = xf * lax.rsqrt(jnp.mean(xf * xf, axis=-1, keepdims=True) + EPS)
    return (y * g.astype(jnp.float32)).astype(x.dtype)


def t5_bucket(rel):
    half = N_BUCKETS // 2
    max_exact = half // 2
    ret = (rel > 0).astype(jnp.int32) * half
    n = jnp.abs(rel)
    nf = jnp.maximum(n, 1).astype(jnp.float32)
    large = max_exact + (jnp.log(nf / max_exact) / np.float32(np.log(MAX_DISTANCE / max_exact))
                         * (half - max_exact)).astype(jnp.int32)
    large = jnp.minimum(large, half - 1)
    return ret + jnp.where(n < max_exact, n, large)


def window_attention(q, k, v, sink, rel_bias):
    b, s, _ = q.shape
    nb = s // BLOCK
    g = N_HEADS // N_KV_HEADS
    qb = q.reshape(b, nb, BLOCK, N_KV_HEADS, g, HEAD_DIM)

    def windows(t):
        t = t.reshape(b, s, N_KV_HEADS, HEAD_DIM)
        t = jnp.pad(t, ((0, 0), (BLOCK, BLOCK), (0, 0), (0, 0)))
        t = t.reshape(b, nb + 2, BLOCK, N_KV_HEADS, HEAD_DIM)
        return jnp.concatenate([t[:, :-2], t[:, 1:-1], t[:, 2:]], axis=2)

    kw, vw = windows(k), windows(v)
    q_loc = jnp.arange(BLOCK)
    k_loc = jnp.arange(3 * BLOCK) - BLOCK
    rel = k_loc[None, :] - q_loc[:, None]
    bias = rel_bias[t5_bucket(rel)].astype(jnp.float32)
    bias = bias.transpose(2, 0, 1).reshape(N_KV_HEADS, g, BLOCK, 3 * BLOCK)
    k_abs = jnp.arange(nb)[:, None] * BLOCK + k_loc[None, :]
    valid = (jnp.abs(rel) <= WINDOW)[None] & ((k_abs >= 0) & (k_abs < s))[:, None, :]

    scores = jnp.einsum('bnqkgd,bnskd->bnkgqs', qb, kw,
                        preferred_element_type=jnp.float32) * (HEAD_DIM ** -0.5) + bias
    scores = jnp.where(valid[None, :, None, None], scores, NEG_INF)
    sink_l = sink.astype(jnp.float32).reshape(N_KV_HEADS, g)[:, :, None]
    m = jnp.maximum(scores.max(axis=-1), sink_l)
    p = jnp.exp(scores - m[..., None])
    denom = p.sum(axis=-1, keepdims=True) + jnp.exp(sink_l - m)[..., None]
    probs = (p / denom).astype(v.dtype)
    out = jnp.einsum('bnkgqs,bnskd->bnqkgd', probs, vw)
    return out.reshape(b, s, ATTN_WIDTH)


def fourier_mix(f):
    b, s, _ = f.shape
    fg = f.reshape(b, s, N_FOURIER_GROUPS, FOURIER_GROUP_DIM).astype(jnp.float32)
    y = jnp.fft.fft2(fg, axes=(1, 3), norm="ortho").real
    return y.reshape(b, s, FOURIER_WIDTH).astype(f.dtype)


def expert_choice_ffn(h, w_router, w_gate, w_up, w_down):
    b, s, d = h.shape
    cap = CAPACITY_FACTOR * s // N_EXPERTS
    logits = jnp.einsum('bsd,de->bse', h, w_router, preferred_element_type=jnp.float32)
    affinity = jax.nn.softmax(logits, axis=-1)
    gates, idx = lax.top_k(affinity.transpose(0, 2, 1), cap)
    xs = jax.vmap(lambda t, i: t[i])(h, idx)
    hg = jnp.einsum('becd,edf->becf', xs, w_gate)
    hu = jnp.einsum('becd,edf->becf', xs, w_up)
    y = jnp.einsum('becf,efd->becd', jax.nn.silu(hg) * hu, w_down) * gates[..., None].astype(h.dtype)
    return jax.vmap(lambda i, val: jnp.zeros((s, d), h.dtype).at[i.reshape(-1)].add(val.reshape(-1, d)))(idx, y)


def setup_inputs(seed: int = 0) -> dict:
    key = jax.random.key(seed)
    ks = jax.random.split(key, 14)
    f32 = jnp.float32
    nrm = lambda k, shape, fan_in: jax.random.normal(k, shape, f32) * (fan_in ** -0.5)
    return {
        "x": jax.random.normal(ks[0], (BATCH, SEQ, D_MODEL), f32),
        "rel_bias": jax.random.normal(ks[1], (N_BUCKETS, N_HEADS), f32) * 0.1,
        "g_mix": 1.0 + 0.02 * jax.random.normal(ks[2], (DEPTH, D_MODEL), f32),
        "w_in": nrm(ks[3], (DEPTH, D_MODEL, IN_WIDTH), D_MODEL),
        "attn_sink": jax.random.normal(ks[4], (DEPTH, N_HEADS), f32) * 0.5,
        "w_attn_proj": nrm(ks[5], (DEPTH, ATTN_WIDTH, D_MODEL), ATTN_WIDTH),
        "w_fourier_proj": nrm(ks[6], (DEPTH, FOURIER_WIDTH, D_MODEL), FOURIER_WIDTH),
        "w_out": nrm(ks[7], (DEPTH, D_MODEL, D_MODEL), D_MODEL),
        "g_ffn": 1.0 + 0.02 * jax.random.normal(ks[8], (DEPTH, D_MODEL), f32),
        "w_router": nrm(ks[9], (DEPTH, D_MODEL, N_EXPERTS), D_MODEL),
        "w_exp_gate": nrm(ks[10], (DEPTH, N_EXPERTS, D_MODEL, D_FF), D_MODEL),
        "w_exp_up": nrm(ks[11], (DEPTH, N_EXPERTS, D_MODEL, D_FF), D_MODEL),
        "w_exp_down": nrm(ks[12], (DEPTH, N_EXPERTS, D_FF, D_MODEL), D_FF),
        "g_final": 1.0 + 0.02 * jax.random.normal(ks[13], (D_MODEL,), f32),
    }


def reference(x, rel_bias, g_mix, w_in, attn_sink, w_attn_proj, w_fourier_proj, w_out,
              g_ffn, w_router, w_exp_gate, w_exp_up, w_exp_down, g_final):
    splits = np.cumsum([ATTN_WIDTH, KV_WIDTH, KV_WIDTH, FOURIER_WIDTH, D_MODEL]).tolist()
    for l in range(DEPTH):
        h = rms_norm(x, g_mix[l])
        z = jnp.einsum('bsd,de->bse', h, w_in[l])
        q, k, v, f, gate_a, gate_b = jnp.split(z, splits, axis=-1)
        a = jnp.einsum('bse,ed->bsd', window_attention(q, k, v, attn_sink[l], rel_bias), w_attn_proj[l])
        fo = jnp.einsum('bse,ed->bsd', fourier_mix(f), w_fourier_proj[l])
        merged = jax.nn.sigmoid(gate_a) * a + jax.nn.sigmoid(gate_b) * fo
        x = x + jnp.einsum('bsd,de->bse', merged, w_out[l])
        h2 = rms_norm(x, g_ffn[l])
        x = x + expert_choice_ffn(h2, w_router[l], w_exp_gate[l], w_exp_up[l], w_exp_down[l])
    return rms_norm(x, g_final)
```

```python
import functools

import numpy as np
import jax
import jax.numpy as jnp
from jax import lax
from jax.experimental import pallas as pl
from jax.experimental.pallas import tpu as pltpu

F32 = jnp.float32
BF16 = jnp.bfloat16
I32 = jnp.int32
HIGHEST = lax.Precision.HIGHEST

N_HEADS = 8
N_KV_HEADS = 2
HEAD_DIM = 64
ATTN_WIDTH = N_HEADS * HEAD_DIM
KV_WIDTH = N_KV_HEADS * HEAD_DIM
WINDOW = 128
BLOCK = 128
N_BUCKETS = 32
MAX_DISTANCE = 128
N_FOURIER_GROUPS = 4
FOURIER_GROUP_DIM = 128
FOURIER_WIDTH = N_FOURIER_GROUPS * FOURIER_GROUP_DIM
N_EXPERTS = 16
CAPACITY_FACTOR = 2
EPS = 1e-6
NEG_INF = -1e30

LANES = 128
LANE_BITS = 7
SUBLANES = 8
VMEM_LIMIT = 56 * 1024 * 1024
ZCHUNK = 256


def _cparams(sem):
    return pltpu.CompilerParams(dimension_semantics=sem, vmem_limit_bytes=VMEM_LIMIT)


def _nt_dot(a, b, **kw):
    return lax.dot_general(a, b, (((1,), (1,)), ((), ())), preferred_element_type=F32, **kw)


def _sigmoid(x):
    return 1.0 / (1.0 + jnp.exp(-x))


def _t5_bucket(rel):
    half = N_BUCKETS // 2
    max_exact = half // 2
    ret = (rel > 0).astype(jnp.int32) * half
    n = jnp.abs(rel)
    nf = jnp.maximum(n, 1).astype(jnp.float32)
    large = max_exact + (jnp.log(nf / max_exact) / np.float32(np.log(MAX_DISTANCE / max_exact))
                         * (half - max_exact)).astype(jnp.int32)
    large = jnp.minimum(large, half - 1)
    return ret + jnp.where(n < max_exact, n, large)


def _bias_kernel(relb_ref, bucket_ref, o_ref):
    bk = bucket_ref[...]
    for h in range(N_HEADS):
        acc = jnp.full(bk.shape, NEG_INF, F32)
        for b in range(N_BUCKETS):
            acc = jnp.where(bk == b, relb_ref[b, h], acc)
        o_ref[h] = acc


def _bias_table(rel_bias):
    q_loc = jnp.arange(BLOCK)
    k_loc = jnp.arange(3 * BLOCK) - BLOCK
    rel = k_loc[None, :] - q_loc[:, None]
    bucket = jnp.where(jnp.abs(rel) <= WINDOW, _t5_bucket(rel), -1).astype(I32)
    return pl.pallas_call(
        _bias_kernel,
        out_shape=jax.ShapeDtypeStruct((N_HEADS, BLOCK, 3 * BLOCK), F32),
        in_specs=[pl.BlockSpec(memory_space=pltpu.SMEM),
                  pl.BlockSpec(memory_space=pltpu.VMEM)],
        out_specs=pl.BlockSpec(memory_space=pltpu.VMEM),
        name="rel_bias_table",
    )(rel_bias.astype(F32), bucket)


def _inproj_kernel(x_ref, g_ref, w_ref, q_ref, k_ref, v_ref, f_ref, ga_ref, gb_ref, *, d_model):
    x = x_ref[...]
    ms = jnp.mean(x * x, axis=-1, keepdims=True)
    h = (x * lax.rsqrt(ms + EPS)) * g_ref[...]
    z = jnp.dot(h.astype(BF16), w_ref[...], preferred_element_type=F32)
    o = 0
    q_ref[...] = (z[:, o:o + ATTN_WIDTH] * (HEAD_DIM ** -0.5)).astype(BF16)
    o += ATTN_WIDTH
    k_ref[...] = z[:, o:o + KV_WIDTH].astype(BF16)
    o += KV_WIDTH
    v_ref[...] = z[:, o:o + KV_WIDTH].astype(BF16)
    o += KV_WIDTH
    f_ref[...] = z[:, o:o + FOURIER_WIDTH]
    o += FOURIER_WIDTH
    ga_ref[...] = z[:, o:o + d_model]
    o += d_model
    gb_ref[...] = z[:, o:o + d_model]


def _inproj(x2, g, w_bf16, *, tm):
    n, d = x2.shape
    in_w = w_bf16.shape[1]
    widths = (ATTN_WIDTH, KV_WIDTH, KV_WIDTH, FOURIER_WIDTH, d, d)
    dtypes = (BF16, BF16, BF16, F32, F32, F32)
    row = lambda i: (i, 0)
    return pl.pallas_call(
        functools.partial(_inproj_kernel, d_model=d),
        grid=(n // tm,),
        in_specs=[pl.BlockSpec((tm, d), row),
                  pl.BlockSpec((1, d), lambda i: (0, 0)),
                  pl.BlockSpec((d, in_w), lambda i: (0, 0))],
        out_specs=[pl.BlockSpec((tm, w), row) for w in widths],
        out_shape=[jax.ShapeDtypeStruct((n, w), dt) for w, dt in zip(widths, dtypes)],
        compiler_params=_cparams(("parallel",)),
        name="inproj",
    )(x2, g.reshape(1, d), w_bf16)


def _attn_kernel(sink_ref, q_ref, kp_ref, ko_ref, kn_ref, vp_ref, vo_ref, vn_ref, bias_ref, o_ref,
                 *, n_blocks, blocks_per_step):
    i = pl.program_id(1)
    kcat = jnp.concatenate([kp_ref[...], ko_ref[...], kn_ref[...]], axis=0)
    vcat = jnp.concatenate([vp_ref[...], vo_ref[...], vn_ref[...]], axis=0)
    col = lax.broadcasted_iota(I32, (BLOCK, 3 * BLOCK), 1)
    group = N_HEADS // N_KV_HEADS
    for t in range(blocks_per_step):
        blk = i * blocks_per_step + t
        lo = jnp.where(blk == 0, BLOCK, 0)
        hi = jnp.where(blk == n_blocks - 1, 2 * BLOCK, 3 * BLOCK)
        in_seq = jnp.where(col >= lo, col, 3 * BLOCK) < hi
        kw = kcat[t * BLOCK:(t + 3) * BLOCK]
        vw = vcat[t * BLOCK:(t + 3) * BLOCK]
        qb = q_ref[t * BLOCK:(t + 1) * BLOCK, :]
        for h in range(N_HEADS):
            kv = h // group
            s = _nt_dot(qb[:, h * HEAD_DIM:(h + 1) * HEAD_DIM], kw[:, kv * HEAD_DIM:(kv + 1) * HEAD_DIM])
            s = jnp.where(in_seq, s + bias_ref[h], NEG_INF)
            sink = sink_ref[h]
            m = jnp.maximum(jnp.max(s, axis=-1, keepdims=True), sink)
            p = jnp.exp(s - m)
            denom = jnp.sum(p, axis=-1, keepdims=True) + jnp.exp(sink - m)
            probs = (p * (1.0 / denom)).astype(BF16)
            out = jnp.dot(probs, vw[:, kv * HEAD_DIM:(kv + 1) * HEAD_DIM], preferred_element_type=F32)
            o_ref[t * BLOCK:(t + 1) * BLOCK, h * HEAD_DIM:(h + 1) * HEAD_DIM] = out.astype(o_ref.dtype)


def _attention(q, k, v, sink, bias, *, blocks_per_step):
    b, s, _ = q.shape
    n_blocks = s // BLOCK
    tq = blocks_per_step * BLOCK
    prev = lambda bi, i: (bi, jnp.maximum(i * blocks_per_step - 1, 0), 0)
    own = lambda bi, i: (bi, i, 0)
    nxt = lambda bi, i: (bi, jnp.minimum((i + 1) * blocks_per_step, n_blocks - 1), 0)
    kv_specs = [pl.BlockSpec((None, BLOCK, KV_WIDTH), prev),
                pl.BlockSpec((None, tq, KV_WIDTH), own),
                pl.BlockSpec((None, BLOCK, KV_WIDTH), nxt)]
    return pl.pallas_call(
        functools.partial(_attn_kernel, n_blocks=n_blocks, blocks_per_step=blocks_per_step),
        grid=(b, s // tq),
        in_specs=[pl.BlockSpec(memory_space=pltpu.SMEM),
                  pl.BlockSpec((None, tq, ATTN_WIDTH), own)] + kv_specs + kv_specs
                 + [pl.BlockSpec((N_HEADS, BLOCK, 3 * BLOCK), lambda bi, i: (0, 0, 0))],
        out_specs=pl.BlockSpec((None, tq, ATTN_WIDTH), own),
        out_shape=jax.ShapeDtypeStruct((b, s, ATTN_WIDTH), BF16),
        compiler_params=_cparams(("parallel", "parallel")),
        name="window_attention",
    )(sink.astype(F32), q, k, k, k, v, v, v, bias)


def _dft1_kernel(x_ref, c_ref, s_ref, tr_ref, ti_ref, *, rows):
    c = c_ref[...]
    sn = s_ref[...]
    for jj in range(rows):
        xj = x_ref[:, jj, :]
        tr_ref[:, jj, :] = jnp.dot(c, xj, precision=HIGHEST, preferred_element_type=F32)
        ti_ref[:, jj, :] = -jnp.dot(sn, xj, precision=HIGHEST, preferred_element_type=F32)


def _dft2_kernel(tr_ref, ti_ref, gr_ref, gi_ref, c2_ref, s2_ref, o_ref, *, rows):
    c2 = c2_ref[...]
    s2 = s2_ref[...]
    dot = functools.partial(jnp.dot, precision=HIGHEST, preferred_element_type=F32)
    for aa in range(rows):
        tr = tr_ref[aa]
        ti = ti_ref[aa]
        gr = gr_ref[aa]
        gi = gi_ref[aa]
        zr = dot(gr, tr) - dot(gi, ti)
        zi = dot(gr, ti) + dot(gi, tr)
        for g in range(N_FOURIER_GROUPS):
            sl = slice(g * FOURIER_GROUP_DIM, (g + 1) * FOURIER_GROUP_DIM)
            y = dot(zr[:, sl], c2) + dot(zi[:, sl], s2)
            o_ref[:, aa, sl] = y.astype(o_ref.dtype)


def _fourier_tables(na):
    n = na * na
    k = np.arange(na, dtype=np.float64)
    ang = 2.0 * np.pi * np.outer(k, k) / na
    c1, s1 = np.cos(ang), np.sin(ang)
    ka = np.arange(na, dtype=np.float64)[:, None, None]
    kb = np.arange(na, dtype=np.float64)[None, :, None]
    nl = np.arange(na, dtype=np.float64)[None, None, :]
    th = 2.0 * np.pi * ((nl * (ka + na * kb)) % n) / n
    gr, gi = np.cos(th), -np.sin(th)
    kc = np.arange(FOURIER_GROUP_DIM, dtype=np.float64)
    angc = 2.0 * np.pi * np.outer(kc, kc) / FOURIER_GROUP_DIM
    scale = 1.0 / np.sqrt(float(n) * FOURIER_GROUP_DIM)
    c2, s2 = np.cos(angc) * scale, np.sin(angc) * scale
    f = lambda a: jnp.asarray(a, dtype=F32)
    return f(c1), f(s1), f(gr), f(gi), f(c2), f(s2)


def _fourier(f, tables, *, rows):
    b, s, w = f.shape
    na = int(round(np.sqrt(s)))
    assert na * na == s and na % rows == 0
    c1, s1, gr, gi, c2, s2 = tables
    x4 = f.reshape(b, na, na, w)
    const2 = lambda bi, j: (0, 0)
    tr, ti = pl.pallas_call(
        functools.partial(_dft1_kernel, rows=rows),
        grid=(b, na // rows),
        in_specs=[pl.BlockSpec((None, na, rows, w), lambda bi, j: (bi, 0, j, 0)),
                  pl.BlockSpec((na, na), const2), pl.BlockSpec((na, na), const2)],
        out_specs=[pl.BlockSpec((None, na, rows, w), lambda bi, j: (bi, 0, j, 0))] * 2,
        out_shape=[jax.ShapeDtypeStruct((b, na, na, w), F32)] * 2,
        compiler_params=_cparams(("parallel", "parallel")),
        name="seq_dft_stage1",
    )(x4, c1, s1)
    y = pl.pallas_call(
        functools.partial(_dft2_kernel, rows=rows),
        grid=(b, na // rows),
        in_specs=[pl.BlockSpec((None, rows, na, w), lambda bi, a: (bi, a, 0, 0))] * 2
                 + [pl.BlockSpec((rows, na, na), lambda bi, a: (a, 0, 0))] * 2
                 + [pl.BlockSpec((FOURIER_GROUP_DIM, FOURIER_GROUP_DIM), const2)] * 2,
        out_specs=pl.BlockSpec((None, na, rows, w), lambda bi, a: (bi, 0, a, 0)),
        out_shape=jax.ShapeDtypeStruct((b, na, na, w), F32),
        compiler_params=_cparams(("parallel", "parallel")),
        name="seq_dft_stage2",
    )(tr, ti, gr, gi, c2, s2)
    return y.reshape(b, s, w)


def _outproj_kernel(a_ref, f_ref, ga_ref, gb_ref, x_ref, wa_ref, wf_ref, wo_ref, g_ref, wr_ref,
                    xn_ref, h_ref, aff_ref):
    a = jnp.dot(a_ref[...], wa_ref[...], preferred_element_type=F32)
    fo = jnp.dot(f_ref[...].astype(BF16), wf_ref[...], preferred_element_type=F32)
    merged = _sigmoid(ga_ref[...]) * a + _sigmoid(gb_ref[...]) * fo
    xn = x_ref[...] + jnp.dot(merged.astype(BF16), wo_ref[...], preferred_element_type=F32)
    xn_ref[...] = xn
    ms = jnp.mean(xn * xn, axis=-1, keepdims=True)
    h = (xn * lax.rsqrt(ms + EPS)) * g_ref[...]
    h_ref[...] = h
    logits = _nt_dot(wr_ref[...], h, precision=HIGHEST)
    e = jnp.exp(logits - jnp.max(logits, axis=0, keepdims=True))
    aff_ref[...] = e / jnp.sum(e, axis=0, keepdims=True)


def _outproj(attn, four, ga, gb, x, wa, wf, wo, g, wr_t, *, tm):
    b, s, d = x.shape
    tok = lambda bi, i: (bi, i, 0)
    const = lambda bi, i: (0, 0)
    return pl.pallas_call(
        _outproj_kernel,
        grid=(b, s // tm),
        in_specs=[pl.BlockSpec((None, tm, ATTN_WIDTH), tok),
                  pl.BlockSpec((None, tm, FOURIER_WIDTH), tok),
                  pl.BlockSpec((None, tm, d), tok),
                  pl.BlockSpec((None, tm, d), tok),
                  pl.BlockSpec((None, tm, d), tok),
                  pl.BlockSpec((ATTN_WIDTH, d), const),
                  pl.BlockSpec((FOURIER_WIDTH, d), const),
                  pl.BlockSpec((d, d), const),
                  pl.BlockSpec((1, d), const),
                  pl.BlockSpec((N_EXPERTS, d), const)],
        out_specs=[pl.BlockSpec((None, tm, d), tok),
                   pl.BlockSpec((None, tm, d), tok),
                   pl.BlockSpec((None, N_EXPERTS, tm), lambda bi, i: (bi, 0, i))],
        out_shape=[jax.ShapeDtypeStruct((b, s, d), F32),
                   jax.ShapeDtypeStruct((b, s, d), F32),
                   jax.ShapeDtypeStruct((b, N_EXPERTS, s), F32)],
        compiler_params=_cparams(("parallel", "parallel")),
        name="outproj_router",
    )(attn, four, ga, gb, x, wa, wf, wo, g.reshape(1, d), wr_t)


def _tri(n, kind):
    r = lax.broadcasted_iota(I32, (n, n), 0)
    c = lax.broadcasted_iota(I32, (n, n), 1)
    cond = {"row_le_col": r <= c, "row_lt_col": r < c, "col_le_row": c <= r, "col_lt_row": c < r}[kind]
    return jnp.where(cond, 1.0, 0.0).astype(BF16)


def _split128(v):
    hi = jnp.floor(v * (1.0 / LANES))
    return hi, v - hi * LANES


def _routing_kernel(aff_ref, idx_ref, zdst_ref, zloc_ref, rbase_ref, gt_scr, eq_scr, sel_scr, need_scr,
                    *, cap):
    ne, nj, ni = aff_ref.shape
    bits = lax.bitcast_convert_type(aff_ref[...], I32)

    def bisect(it, thr):
        cand = thr | jnp.left_shift(jnp.int32(1), 30 - it)
        cnt = jnp.sum(jnp.where(bits >= cand, 1.0, 0.0), axis=(1, 2), keepdims=True)
        return jnp.where(cnt >= cap, cand, thr)

    thr = lax.fori_loop(0, 31, bisect, jnp.zeros((ne, 1, 1), I32))
    gt = jnp.where(bits > thr, 1.0, 0.0)
    gt_scr[...] = gt
    eq_scr[...] = jnp.where(bits == thr, 1.0, 0.0)
    need = cap - jnp.sum(gt, axis=(1, 2), keepdims=True)
    need_scr[...] = jnp.broadcast_to(need, (ne, SUBLANES, ni))

    ones_sq = jnp.ones((ni, ni), BF16)
    ones_row = jnp.ones((SUBLANES, ni), BF16)
    ones_rowj = jnp.ones((SUBLANES, nj), BF16)
    u_incl = _tri(ni, "row_le_col")
    uj_strict = _tri(nj, "row_lt_col")
    lj_strict = _tri(nj, "col_lt_row")
    l_incl = _tri(ni, "col_le_row")
    bdot = lambda a, b: jnp.dot(a, b, preferred_element_type=F32)

    def select(e, rbase):
        eq = eq_scr[e]
        eqb = eq.astype(BF16)
        rank = bdot(lj_strict, bdot(eqb, ones_sq).astype(BF16)) + bdot(eqb, u_incl)
        take = jnp.where(rank <= need_scr[e][0:1, :], eq, 0.0)
        sel = jnp.maximum(gt_scr[e], take)
        sel_scr[e] = sel
        n_lane = _nt_dot(ones_row, sel.astype(BF16))
        return rbase + bdot(n_lane.astype(BF16), uj_strict)

    rbase = lax.fori_loop(0, ne, select, jnp.zeros((SUBLANES, nj), F32))
    rbase_ref[...] = rbase.astype(I32)

    c_lane = lax.broadcasted_iota(I32, (nj, cap), 1).astype(F32)
    c_row = lax.broadcasted_iota(I32, (SUBLANES, cap), 1).astype(F32)
    j_sub = lax.broadcasted_iota(I32, (nj, cap), 0).astype(F32)

    def slots(e, carry):
        a_lane, a_rep = carry
        sel = sel_scr[e]
        selb = sel.astype(BF16)
        lc = bdot(selb, u_incl)
        lc_t = _nt_dot(l_incl, selb)
        n_rep = bdot(selb, ones_sq)
        s_rep = bdot(lj_strict, n_rep.astype(BF16)) + n_rep
        n_lane = _nt_dot(ones_row, selb)
        pex_lane = bdot(n_lane.astype(BF16), uj_strict)
        s_wide = jnp.concatenate([s_rep] * (cap // ni), axis=1) if cap > ni else s_rep[:, :cap]
        jc = bdot(ones_rowj, jnp.where(s_wide <= c_lane, 1.0, 0.0).astype(BF16))
        onehot = jnp.where(j_sub == jnp.broadcast_to(jc[0:1], (nj, cap)), 1.0, 0.0).astype(BF16)
        phi, plo = _split128(pex_lane)
        r = c_row - (bdot(phi.astype(BF16), onehot) * LANES + bdot(plo.astype(BF16), onehot))
        lcs = bdot(lc_t.astype(BF16), onehot)
        ic = bdot(ones_row, jnp.where(lcs <= jnp.broadcast_to(r[0:1], (ni, cap)), 1.0, 0.0).astype(BF16))
        idx_ref[e] = (jc * ni + ic).astype(I32)
        zhi, zlo = _split128(rbase + a_lane)
        zbase = bdot(zhi.astype(BF16), onehot) * LANES + bdot(zlo.astype(BF16), onehot)
        zdst_ref[e] = (zbase + r).astype(I32)
        zloc_ref[e] = jnp.where(sel > 0.0, a_rep + lc - 1.0, -1.0)
        return a_lane + n_lane, a_rep + n_rep

    lax.fori_loop(0, ne, slots, (jnp.zeros((SUBLANES, nj), F32), jnp.zeros((nj, ni), F32)))


def _routing(aff, *, cap):
    b, ne, s = aff.shape
    nj = s // LANES
    aff4 = aff.reshape(b, ne, nj, LANES)
    per_b = lambda bi: (bi, 0, 0, 0)
    idx, zdst, zloc, rbase = pl.pallas_call(
        functools.partial(_routing_kernel, cap=cap),
        grid=(b,),
        in_specs=[pl.BlockSpec((None, ne, nj, LANES), per_b)],
        out_specs=[pl.BlockSpec((None, ne, SUBLANES, cap), per_b),
                   pl.BlockSpec((None, ne, SUBLANES, cap), per_b),
                   pl.BlockSpec((None, ne, nj, LANES), per_b),
                   pl.BlockSpec((None, SUBLANES, nj), lambda bi: (bi, 0, 0))],
        out_shape=[jax.ShapeDtypeStruct((b, ne, SUBLANES, cap), I32),
                   jax.ShapeDtypeStruct((b, ne, SUBLANES, cap), I32),
                   jax.ShapeDtypeStruct((b, ne, nj, LANES), F32),
                   jax.ShapeDtypeStruct((b, SUBLANES, nj), I32)],
        scratch_shapes=[pltpu.VMEM((ne, nj, LANES), F32)] * 3 + [pltpu.VMEM((ne, SUBLANES, LANES), F32)],
        compiler_params=_cparams(("parallel",)),
        name="expert_choice_routing",
    )(aff4)
    return idx[:, :, 0, :], zdst[:, :, 0, :], zloc, rbase[:, 0, :]


def _ffn_kernel(idx_ref, zdst_ref, h_hbm, wg_ref, wu_ref, wd_ref, z_hbm, xs, xsb, acc, gsem, ssem,
                *, cap, n_ftiles):
    p = pl.program_id(0)
    j = pl.program_id(1)
    b = p // N_EXPERTS

    def gather_copy(c, tok):
        return pltpu.make_async_copy(h_hbm.at[b, pl.ds(tok, 1), :], xs.at[pl.ds(c, 1), :], gsem)

    def scatter_copy(c, row):
        return pltpu.make_async_copy(acc.at[pl.ds(c, 1), :], z_hbm.at[b, pl.ds(row, 1), :], ssem)

    @pl.when(j == 0)
    def _():
        def start(c, carry):
            gather_copy(c, idx_ref[c >> LANE_BITS, c & (LANES - 1)]).start()
            return carry

        def wait(c, carry):
            gather_copy(c, 0).wait()
            return carry

        lax.fori_loop(0, cap, start, 0)
        lax.fori_loop(0, cap, wait, 0)
        xsb[...] = xs[...].astype(BF16)

    x = xsb[...]
    hg = jnp.dot(x, wg_ref[...], preferred_element_type=F32)
    hu = jnp.dot(x, wu_ref[...], preferred_element_type=F32)
    act = (hg * _sigmoid(hg)) * hu
    part = jnp.dot(act.astype(BF16), wd_ref[...], preferred_element_type=F32)

    @pl.when(j == 0)
    def _():
        acc[...] = part

    @pl.when(j > 0)
    def _():
        acc[...] += part

    @pl.when(j == n_ftiles - 1)
    def _():
        def start(c, carry):
            scatter_copy(c, zdst_ref[c >> LANE_BITS, c & (LANES - 1)]).start()
            return carry

        def wait(c, carry):
            scatter_copy(c, 0).wait()
            return carry

        lax.fori_loop(0, cap, start, 0)
        lax.fori_loop(0, cap, wait, 0)


def _expert_ffn(h, idx, zdst, wg, wu, wd, *, cap, tf):
    b, s, d = h.shape
    ne, _, dff = wg.shape
    n_ftiles = dff // tf
    pairs = b * ne
    tab = lambda a: a.reshape(pairs, cap // LANES, LANES)
    tab_spec = pl.BlockSpec((None, cap // LANES, LANES), lambda p, j: (p, 0, 0), memory_space=pltpu.SMEM)
    return pl.pallas_call(
        functools.partial(_ffn_kernel, cap=cap, n_ftiles=n_ftiles),
        grid=(pairs, n_ftiles),
        in_specs=[tab_spec, tab_spec,
                  pl.BlockSpec(memory_space=pl.ANY),
                  pl.BlockSpec((None, d, tf), lambda p, j: (p % ne, 0, j)),
                  pl.BlockSpec((None, d, tf), lambda p, j: (p % ne, 0, j)),
                  pl.BlockSpec((None, tf, d), lambda p, j: (p % ne, j, 0))],
        out_specs=pl.BlockSpec(memory_space=pl.ANY),
        out_shape=jax.ShapeDtypeStruct((b, ne * cap, d), F32),
        scratch_shapes=[pltpu.VMEM((cap, d), F32), pltpu.VMEM((cap, d), BF16), pltpu.VMEM((cap, d), F32),
                        pltpu.SemaphoreType.DMA(()), pltpu.SemaphoreType.DMA(())],
        compiler_params=_cparams(("arbitrary", "arbitrary")),
        name="expert_swiglu",
    )(tab(idx), tab(zdst), h, wg, wu, wd)


def _combine_kernel(rb_ref, zloc_ref, aff_ref, x_ref, g_ref, z_hbm, o_ref, zbuf, acc, sem,
                    *, z_rows, final_norm):
    b = pl.program_id(0)
    j = pl.program_id(1)
    r0 = rb_ref[b, j]
    r1 = rb_ref[b, j + 1]
    base = (r0 // SUBLANES) * SUBLANES
    n_chunks = (r1 - base + ZCHUNK - 1) // ZCHUNK
    zl = zloc_ref[...]
    af = aff_ref[...]
    row = lax.broadcasted_iota(I32, (ZCHUNK, LANES), 0).astype(F32)
    acc[...] = jnp.zeros_like(acc)

    def chunk(k, carry):
        nominal = base + k * ZCHUNK
        start = pl.multiple_of(jnp.minimum(nominal, z_rows - ZCHUNK), SUBLANES)
        cp = pltpu.make_async_copy(z_hbm.at[b, pl.ds(start, ZCHUNK), :], zbuf, sem)
        cp.start()
        local = row + (start - r0).astype(F32)
        want = jnp.where(local >= jnp.maximum(nominal - r0, 0).astype(F32), local, -2.0)
        gate_t = jnp.zeros((ZCHUNK, LANES), F32)
        for e in range(N_EXPERTS):
            gate_t = gate_t + jnp.where(zl[e:e + 1, :] == want, af[e:e + 1, :], 0.0)
        cp.wait()
        acc[...] += jnp.dot(gate_t.T, zbuf[...], precision=HIGHEST, preferred_element_type=F32)
        return carry

    lax.fori_loop(0, n_chunks, chunk, 0)
    y = x_ref[...] + acc[...]
    if final_norm:
        ms = jnp.mean(y * y, axis=-1, keepdims=True)
        y = (y * lax.rsqrt(ms + EPS)) * g_ref[...]
    o_ref[...] = y


def _combine(x, z, zloc, aff, rbase, g_final, *, final_norm):
    b, s, d = x.shape
    nj = s // LANES
    z_rows = z.shape[1]
    rb = jnp.concatenate([rbase, jnp.full((b, 1), z_rows, I32)], axis=1)
    zloc_t = jnp.transpose(zloc, (0, 2, 1, 3))
    return pl.pallas_call(
        functools.partial(_combine_kernel, z_rows=z_rows, final_norm=final_norm),
        grid_spec=pltpu.PrefetchScalarGridSpec(
            num_scalar_prefetch=1,
            grid=(b, nj),
            in_specs=[pl.BlockSpec((None, None, N_EXPERTS, LANES), lambda bi, j, rb: (bi, j, 0, 0)),
                      pl.BlockSpec((None, N_EXPERTS, LANES), lambda bi, j, rb: (bi, 0, j)),
                      pl.BlockSpec((None, LANES, d), lambda bi, j, rb: (bi, j, 0)),
                      pl.BlockSpec((1, d), lambda bi, j, rb: (0, 0)),
                      pl.BlockSpec(memory_space=pl.ANY)],
            out_specs=pl.BlockSpec((None, LANES, d), lambda bi, j, rb: (bi, j, 0)),
            scratch_shapes=[pltpu.VMEM((ZCHUNK, d), F32), pltpu.VMEM((LANES, d), F32),
                            pltpu.SemaphoreType.DMA(())]),
        out_shape=jax.ShapeDtypeStruct((b, s, d), F32),
        compiler_params=_cparams(("arbitrary", "arbitrary")),
        name="moe_combine",
    )(rb, zloc_t, aff, x, g_final.reshape(1, d), z)


def kernel(x, rel_bias, g_mix, w_in, attn_sink, w_attn_proj, w_fourier_proj, w_out, g_ffn, w_router,
           w_exp_gate, w_exp_up, w_exp_down, g_final):
    b, s, d = x.shape
    depth = g_mix.shape[0]
    cap = CAPACITY_FACTOR * s // N_EXPERTS
    bias = _bias_table(rel_bias)
    tables = _fourier_tables(int(round(np.sqrt(s))))
    for l in range(depth):
        q, k, v, f, ga, gb = _inproj(x.reshape(b * s, d), g_mix[l], w_in[l].astype(BF16), tm=256)
        shp = lambda a: a.reshape(b, s, a.shape[-1])
        attn = _attention(shp(q), shp(k), shp(v), attn_sink[l], bias, blocks_per_step=4)
        four = _fourier(shp(f), tables, rows=8)
        xn, h, aff = _outproj(attn, four, shp(ga), shp(gb), x,
                              w_attn_proj[l].astype(BF16), w_fourier_proj[l].astype(BF16),
                              w_out[l].astype(BF16), g_ffn[l], w_router[l].T, tm=256)
        idx, zdst, zloc, rbase = _routing(aff, cap=cap)
        z = _expert_ffn(h, idx, zdst, w_exp_gate[l].astype(BF16), w_exp_up[l].astype(BF16),
                        w_exp_down[l].astype(BF16), cap=cap, tf=256)
        x = _combine(xn, z, zloc, aff, rbase, g_final, final_norm=(l == depth - 1))
    return x
```

```python
import functools

import numpy as np
import jax
import jax.numpy as jnp
from jax import lax
from jax.experimental import pallas as pl
from jax.experimental.pallas import tpu as pltpu

F32 = jnp.float32
BF16 = jnp.bfloat16
I32 = jnp.int32
HIGHEST = lax.Precision.HIGHEST

N_HEADS = 8
N_KV_HEADS = 2
HEAD_DIM = 64
ATTN_WIDTH = N_HEADS * HEAD_DIM
KV_WIDTH = N_KV_HEADS * HEAD_DIM
WINDOW = 128
BLOCK = 128
N_BUCKETS = 32
MAX_DISTANCE = 128
N_FOURIER_GROUPS = 4
FOURIER_GROUP_DIM = 128
FOURIER_WIDTH = N_FOURIER_GROUPS * FOURIER_GROUP_DIM
N_EXPERTS = 16
CAPACITY_FACTOR = 2
EPS = 1e-6
NEG_INF = -1e30

LANES = 128
LANE_BITS = 7
SUBLANES = 8
VMEM_LIMIT = 56 * 1024 * 1024
ZCHUNK = 256


def _cparams(sem):
    return pltpu.CompilerParams(dimension_semantics=sem, vmem_limit_bytes=VMEM_LIMIT)


def _nt_dot(a, b, **kw):
    return lax.dot_general(a, b, (((1,), (1,)), ((), ())), preferred_element_type=F32, **kw)


def _sigmoid(x):
    return 1.0 / (1.0 + jnp.exp(-x))


def _t5_bucket(rel):
    half = N_BUCKETS // 2
    max_exact = half // 2
    ret = (rel > 0).astype(jnp.int32) * half
    n = jnp.abs(rel)
    nf = jnp.maximum(n, 1).astype(jnp.float32)
    large = max_exact + (jnp.log(nf / max_exact) / np.float32(np.log(MAX_DISTANCE / max_exact))
                         * (half - max_exact)).astype(jnp.int32)
    large = jnp.minimum(large, half - 1)
    return ret + jnp.where(n < max_exact, n, large)


def _bias_kernel(relb_ref, bucket_ref, o_ref):
    bk = bucket_ref[...]
    for h in range(N_HEADS):
        acc = jnp.full(bk.shape, NEG_INF, F32)
        for b in range(N_BUCKETS):
            acc = jnp.where(bk == b, relb_ref[b, h], acc)
        o_ref[h] = acc


def _bias_table(rel_bias):
    q_loc = jnp.arange(BLOCK)
    k_loc = jnp.arange(3 * BLOCK) - BLOCK
    rel = k_loc[None, :] - q_loc[:, None]
    bucket = jnp.where(jnp.abs(rel) <= WINDOW, _t5_bucket(rel), -1).astype(I32)
    return pl.pallas_call(
        _bias_kernel,
        out_shape=jax.ShapeDtypeStruct((N_HEADS, BLOCK, 3 * BLOCK), F32),
        in_specs=[pl.BlockSpec(memory_space=pltpu.SMEM),
                  pl.BlockSpec(memory_space=pltpu.VMEM)],
        out_specs=pl.BlockSpec(memory_space=pltpu.VMEM),
        name="rel_bias_table",
    )(rel_bias.astype(F32), bucket)


def _inproj_kernel(x_ref, g_ref, w_ref, q_ref, k_ref, v_ref, f_ref, ga_ref, gb_ref, *, d_model):
    x = x_ref[...]
    ms = jnp.mean(x * x, axis=-1, keepdims=True)
    h = (x * lax.rsqrt(ms + EPS)) * g_ref[...]
    z = jnp.dot(h.astype(BF16), w_ref[...], preferred_element_type=F32)
    o = 0
    q_ref[...] = (z[:, o:o + ATTN_WIDTH] * (HEAD_DIM ** -0.5)).astype(BF16)
    o += ATTN_WIDTH
    k_ref[...] = z[:, o:o + KV_WIDTH].astype(BF16)
    o += KV_WIDTH
    v_ref[...] = z[:, o:o + KV_WIDTH].astype(BF16)
    o += KV_WIDTH
    f_ref[...] = z[:, o:o + FOURIER_WIDTH]
    o += FOURIER_WIDTH
    ga_ref[...] = z[:, o:o + d_model]
    o += d_model
    gb_ref[...] = z[:, o:o + d_model]


def _inproj(x2, g, w_bf16, *, tm):
    n, d = x2.shape
    in_w = w_bf16.shape[1]
    widths = (ATTN_WIDTH, KV_WIDTH, KV_WIDTH, FOURIER_WIDTH, d, d)
    dtypes = (BF16, BF16, BF16, F32, F32, F32)
    row = lambda i: (i, 0)
    return pl.pallas_call(
        functools.partial(_inproj_kernel, d_model=d),
        grid=(n // tm,),
        in_specs=[pl.BlockSpec((tm, d), row),
                  pl.BlockSpec((1, d), lambda i: (0, 0)),
                  pl.BlockSpec((d, in_w), lambda i: (0, 0))],
        out_specs=[pl.BlockSpec((tm, w), row) for w in widths],
        out_shape=[jax.ShapeDtypeStruct((n, w), dt) for w, dt in zip(widths, dtypes)],
        compiler_params=_cparams(("parallel",)),
        name="inproj",
    )(x2, g.reshape(1, d), w_bf16)


def _attn_kernel(sink_ref, q_ref, kp_ref, ko_ref, kn_ref, vp_ref, vo_ref, vn_ref, bias_ref, o_ref,
                 *, n_blocks, blocks_per_step):
    i = pl.program_id(1)
    kcat = jnp.concatenate([kp_ref[...], ko_ref[...], kn_ref[...]], axis=0)
    vcat = jnp.concatenate([vp_ref[...], vo_ref[...], vn_ref[...]], axis=0)
    col = lax.broadcasted_iota(I32, (BLOCK, 3 * BLOCK), 1)
    group = N_HEADS // N_KV_HEADS
    for t in range(blocks_per_step):
        blk = i * blocks_per_step + t
        lo = jnp.where(blk == 0, BLOCK, 0)
        hi = jnp.where(blk == n_blocks - 1, 2 * BLOCK, 3 * BLOCK)
        in_seq = jnp.where(col >= lo, col, 3 * BLOCK) < hi
        kw = kcat[t * BLOCK:(t + 3) * BLOCK]
        vw = vcat[t * BLOCK:(t + 3) * BLOCK]
        qb = q_ref[t * BLOCK:(t + 1) * BLOCK, :]
        for h in range(N_HEADS):
            kv = h // group
            s = _nt_dot(qb[:, h * HEAD_DIM:(h + 1) * HEAD_DIM], kw[:, kv * HEAD_DIM:(kv + 1) * HEAD_DIM])
            s = jnp.where(in_seq, s + bias_ref[h], NEG_INF)
            sink = sink_ref[h]
            m = jnp.maximum(jnp.max(s, axis=-1, keepdims=True), sink)
            p = jnp.exp(s - m)
            denom = jnp.sum(p, axis=-1, keepdims=True) + jnp.exp(sink - m)
            probs = (p * (1.0 / denom)).astype(BF16)
            out = jnp.dot(probs, vw[:, kv * HEAD_DIM:(kv + 1) * HEAD_DIM], preferred_element_type=F32)
            o_ref[t * BLOCK:(t + 1) * BLOCK, h * HEAD_DIM:(h + 1) * HEAD_DIM] = out.astype(o_ref.dtype)


def _attention(q, k, v, sink, bias, *, blocks_per_step):
    b, s, _ = q.shape
    n_blocks = s // BLOCK
    tq = blocks_per_step * BLOCK
    prev = lambda bi, i: (bi, jnp.maximum(i * blocks_per_step - 1, 0), 0)
    own = lambda bi, i: (bi, i, 0)
    nxt = lambda bi, i: (bi, jnp.minimum((i + 1) * blocks_per_step, n_blocks - 1), 0)
    kv_specs = [pl.BlockSpec((None, BLOCK, KV_WIDTH), prev),
                pl.BlockSpec((None, tq, KV_WIDTH), own),
                pl.BlockSpec((None, BLOCK, KV_WIDTH), nxt)]
    return pl.pallas_call(
        functools.partial(_attn_kernel, n_blocks=n_blocks, blocks_per_step=blocks_per_step),
        grid=(b, s // tq),
        in_specs=[pl.BlockSpec(memory_space=pltpu.SMEM),
                  pl.BlockSpec((None, tq, ATTN_WIDTH), own)] + kv_specs + kv_specs
                 + [pl.BlockSpec((N_HEADS, BLOCK, 3 * BLOCK), lambda bi, i: (0, 0, 0))],
        out_specs=pl.BlockSpec((None, tq, ATTN_WIDTH), own),
        out_shape=jax.ShapeDtypeStruct((b, s, ATTN_WIDTH), BF16),
        compiler_params=_cparams(("parallel", "parallel")),
        name="window_attention",
    )(sink.astype(F32), q, k, k, k, v, v, v, bias)


def _dft1_kernel(x_ref, c_ref, s_ref, tr_ref, ti_ref, *, rows):
    c = c_ref[...]
    sn = s_ref[...]
    for jj in range(rows):
        xj = x_ref[:, jj, :]
        tr_ref[:, jj, :] = jnp.dot(c, xj, precision=HIGHEST, preferred_element_type=F32)
        ti_ref[:, jj, :] = -jnp.dot(sn, xj, precision=HIGHEST, preferred_element_type=F32)


def _dft2_kernel(tr_ref, ti_ref, gr_ref, gi_ref, c2_ref, s2_ref, o_ref, *, rows):
    c2 = c2_ref[...]
    s2 = s2_ref[...]
    dot = functools.partial(jnp.dot, precision=HIGHEST, preferred_element_type=F32)
    for aa in range(rows):
        tr = tr_ref[aa]
        ti = ti_ref[aa]
        gr = gr_ref[aa]
        gi = gi_ref[aa]
        zr = dot(gr, tr) - dot(gi, ti)
        zi = dot(gr, ti) + dot(gi, tr)
        for g in range(N_FOURIER_GROUPS):
            sl = slice(g * FOURIER_GROUP_DIM, (g + 1) * FOURIER_GROUP_DIM)
            y = dot(zr[:, sl], c2) + dot(zi[:, sl], s2)
            o_ref[:, aa, sl] = y.astype(o_ref.dtype)


def _fourier_tables(na):
    n = na * na
    k = np.arange(na, dtype=np.float64)
    ang = 2.0 * np.pi * np.outer(k, k) / na
    c1, s1 = np.cos(ang), np.sin(ang)
    ka = np.arange(na, dtype=np.float64)[:, None, None]
    kb = np.arange(na, dtype=np.float64)[None, :, None]
    nl = np.arange(na, dtype=np.float64)[None, None, :]
    th = 2.0 * np.pi * ((nl * (ka + na * kb)) % n) / n
    gr, gi = np.cos(th), -np.sin(th)
    kc = np.arange(FOURIER_GROUP_DIM, dtype=np.float64)
    angc = 2.0 * np.pi * np.outer(kc, kc) / FOURIER_GROUP_DIM
    scale = 1.0 / np.sqrt(float(n) * FOURIER_GROUP_DIM)
    c2, s2 = np.cos(angc) * scale, np.sin(angc) * scale
    f = lambda a: jnp.asarray(a, dtype=F32)
    return f(c1), f(s1), f(gr), f(gi), f(c2), f(s2)


def _fourier(f, tables, *, rows):
    b, s, w = f.shape
    na = int(round(np.sqrt(s)))
    assert na * na == s and na % rows == 0
    c1, s1, gr, gi, c2, s2 = tables
    x4 = f.reshape(b, na, na, w)
    const2 = lambda bi, j: (0, 0)
    tr, ti = pl.pallas_call(
        functools.partial(_dft1_kernel, rows=rows),
        grid=(b, na // rows),
        in_specs=[pl.BlockSpec((None, na, rows, w), lambda bi, j: (bi, 0, j, 0)),
                  pl.BlockSpec((na, na), const2), pl.BlockSpec((na, na), const2)],
        out_specs=[pl.BlockSpec((None, na, rows, w), lambda bi, j: (bi, 0, j, 0))] * 2,
        out_shape=[jax.ShapeDtypeStruct((b, na, na, w), F32)] * 2,
        compiler_params=_cparams(("parallel", "parallel")),
        name="seq_dft_stage1",
    )(x4, c1, s1)
    y = pl.pallas_call(
        functools.partial(_dft2_kernel, rows=rows),
        grid=(b, na // rows),
        in_specs=[pl.BlockSpec((None, rows, na, w), lambda bi, a: (bi, a, 0, 0))] * 2
                 + [pl.BlockSpec((rows, na, na), lambda bi, a: (a, 0, 0))] * 2
                 + [pl.BlockSpec((FOURIER_GROUP_DIM, FOURIER_GROUP_DIM), const2)] * 2,
        out_specs=pl.BlockSpec((None, na, rows, w), lambda bi, a: (bi, 0, a, 0)),
        out_shape=jax.ShapeDtypeStruct((b, na, na, w), F32),
        compiler_params=_cparams(("parallel", "parallel")),
        name="seq_dft_stage2",
    )(tr, ti, gr, gi, c2, s2)
    return y.reshape(b, s, w)


def _outproj_kernel(a_ref, f_ref, ga_ref, gb_ref, x_ref, wa_ref, wf_ref, wo_ref, g_ref, wr_ref,
                    xn_ref, h_ref, aff_ref):
    a = jnp.dot(a_ref[...], wa_ref[...], preferred_element_type=F32)
    fo = jnp.dot(f_ref[...].astype(BF16), wf_ref[...], preferred_element_type=F32)
    merged = _sigmoid(ga_ref[...]) * a + _sigmoid(gb_ref[...]) * fo
    xn = x_ref[...] + jnp.dot(merged.astype(BF16), wo_ref[...], preferred_element_type=F32)
    xn_ref[...] = xn
    ms = jnp.mean(xn * xn, axis=-1, keepdims=True)
    h = (xn * lax.rsqrt(ms + EPS)) * g_ref[...]
    for c in range(h.shape[1] // LANES):
        h_ref[:, c, :] = h[:, c * LANES:(c + 1) * LANES]
    logits = _nt_dot(wr_ref[...], h, precision=HIGHEST)
    e = jnp.exp(logits - jnp.max(logits, axis=0, keepdims=True))
    aff_ref[...] = e / jnp.sum(e, axis=0, keepdims=True)


def _outproj(attn, four, ga, gb, x, wa, wf, wo, g, wr_t, *, tm):
    b, s, d = x.shape
    tok = lambda bi, i: (bi, i, 0)
    const = lambda bi, i: (0, 0)
    return pl.pallas_call(
        _outproj_kernel,
        grid=(b, s // tm),
        in_specs=[pl.BlockSpec((None, tm, ATTN_WIDTH), tok),
                  pl.BlockSpec((None, tm, FOURIER_WIDTH), tok),
                  pl.BlockSpec((None, tm, d), tok),
                  pl.BlockSpec((None, tm, d), tok),
                  pl.BlockSpec((None, tm, d), tok),
                  pl.BlockSpec((ATTN_WIDTH, d), const),
                  pl.BlockSpec((FOURIER_WIDTH, d), const),
                  pl.BlockSpec((d, d), const),
                  pl.BlockSpec((1, d), const),
                  pl.BlockSpec((N_EXPERTS, d), const)],
        out_specs=[pl.BlockSpec((None, tm, d), tok),
                   pl.BlockSpec((None, tm, d // LANES, LANES), lambda bi, i: (bi, i, 0, 0)),
                   pl.BlockSpec((None, N_EXPERTS, tm), lambda bi, i: (bi, 0, i))],
        out_shape=[jax.ShapeDtypeStruct((b, s, d), F32),
                   jax.ShapeDtypeStruct((b, s, d // LANES, LANES), F32),
                   jax.ShapeDtypeStruct((b, N_EXPERTS, s), F32)],
        compiler_params=_cparams(("parallel", "parallel")),
        name="outproj_router",
    )(attn, four, ga, gb, x, wa, wf, wo, g.reshape(1, d), wr_t)


def _tri(n, kind):
    r = lax.broadcasted_iota(I32, (n, n), 0)
    c = lax.broadcasted_iota(I32, (n, n), 1)
    cond = {"row_le_col": r <= c, "row_lt_col": r < c, "col_le_row": c <= r, "col_lt_row": c < r}[kind]
    return jnp.where(cond, 1.0, 0.0).astype(BF16)


def _split128(v):
    hi = jnp.floor(v * (1.0 / LANES))
    return hi, v - hi * LANES


def _routing_kernel(aff_ref, idx_ref, zdst_ref, zloc_ref, rbase_ref, gt_scr, eq_scr, sel_scr, need_scr,
                    *, cap):
    ne, nj, ni = aff_ref.shape
    bits = lax.bitcast_convert_type(aff_ref[...], I32)

    def bisect(it, thr):
        cand = thr | jnp.left_shift(jnp.int32(1), 30 - it)
        cnt = jnp.sum(jnp.where(bits >= cand, 1.0, 0.0), axis=(1, 2), keepdims=True)
        return jnp.where(cnt >= cap, cand, thr)

    thr = lax.fori_loop(0, 31, bisect, jnp.zeros((ne, 1, 1), I32))
    gt = jnp.where(bits > thr, 1.0, 0.0)
    gt_scr[...] = gt
    eq_scr[...] = jnp.where(bits == thr, 1.0, 0.0)
    need = cap - jnp.sum(gt, axis=(1, 2), keepdims=True)
    need_scr[...] = jnp.broadcast_to(need, (ne, SUBLANES, ni))

    ones_sq = jnp.ones((ni, ni), BF16)
    ones_row = jnp.ones((SUBLANES, ni), BF16)
    ones_rowj = jnp.ones((SUBLANES, nj), BF16)
    u_incl = _tri(ni, "row_le_col")
    uj_strict = _tri(nj, "row_lt_col")
    lj_strict = _tri(nj, "col_lt_row")
    l_incl = _tri(ni, "col_le_row")
    bdot = lambda a, b: jnp.dot(a, b, preferred_element_type=F32)

    def select(e, rbase):
        eq = eq_scr[e]
        eqb = eq.astype(BF16)
        rank = bdot(lj_strict, bdot(eqb, ones_sq).astype(BF16)) + bdot(eqb, u_incl)
        take = jnp.where(rank <= need_scr[e][0:1, :], eq, 0.0)
        sel = jnp.maximum(gt_scr[e], take)
        sel_scr[e] = sel
        n_lane = _nt_dot(ones_row, sel.astype(BF16))
        return rbase + bdot(n_lane.astype(BF16), uj_strict)

    rbase = lax.fori_loop(0, ne, select, jnp.zeros((SUBLANES, nj), F32))
    rbase_ref[...] = rbase.astype(I32)

    c_lane = lax.broadcasted_iota(I32, (nj, cap), 1).astype(F32)
    c_row = lax.broadcasted_iota(I32, (SUBLANES, cap), 1).astype(F32)
    j_sub = lax.broadcasted_iota(I32, (nj, cap), 0).astype(F32)

    def slots(e, carry):
        a_lane, a_rep = carry
        sel = sel_scr[e]
        selb = sel.astype(BF16)
        lc = bdot(selb, u_incl)
        lc_t = _nt_dot(l_incl, selb)
        n_rep = bdot(selb, ones_sq)
        s_rep = bdot(lj_strict, n_rep.astype(BF16)) + n_rep
        n_lane = _nt_dot(ones_row, selb)
        pex_lane = bdot(n_lane.astype(BF16), uj_strict)
        s_wide = jnp.concatenate([s_rep] * (cap // ni), axis=1) if cap > ni else s_rep[:, :cap]
        jc = bdot(ones_rowj, jnp.where(s_wide <= c_lane, 1.0, 0.0).astype(BF16))
        onehot = jnp.where(j_sub == jnp.broadcast_to(jc[0:1], (nj, cap)), 1.0, 0.0).astype(BF16)
        phi, plo = _split128(pex_lane)
        r = c_row - (bdot(phi.astype(BF16), onehot) * LANES + bdot(plo.astype(BF16), onehot))
        lcs = bdot(lc_t.astype(BF16), onehot)
        ic = bdot(ones_row, jnp.where(lcs <= jnp.broadcast_to(r[0:1], (ni, cap)), 1.0, 0.0).astype(BF16))
        idx_ref[e] = (jc * ni + ic).astype(I32)
        zhi, zlo = _split128(rbase + a_lane)
        zbase = bdot(zhi.astype(BF16), onehot) * LANES + bdot(zlo.astype(BF16), onehot)
        zdst_ref[e] = (zbase + r).astype(I32)
        zloc_ref[e] = jnp.where(sel > 0.0, a_rep + lc - 1.0, -1.0)
        return a_lane + n_lane, a_rep + n_rep

    lax.fori_loop(0, ne, slots, (jnp.zeros((SUBLANES, nj), F32), jnp.zeros((nj, ni), F32)))


def _routing(aff, *, cap):
    b, ne, s = aff.shape
    nj = s // LANES
    aff4 = aff.reshape(b, ne, nj, LANES)
    per_b = lambda bi: (bi, 0, 0, 0)
    idx, zdst, zloc, rbase = pl.pallas_call(
        functools.partial(_routing_kernel, cap=cap),
        grid=(b,),
        in_specs=[pl.BlockSpec((None, ne, nj, LANES), per_b)],
        out_specs=[pl.BlockSpec((None, ne, SUBLANES, cap), per_b),
                   pl.BlockSpec((None, ne, SUBLANES, cap), per_b),
                   pl.BlockSpec((None, ne, nj, LANES), per_b),
                   pl.BlockSpec((None, SUBLANES, nj), lambda bi: (bi, 0, 0))],
        out_shape=[jax.ShapeDtypeStruct((b, ne, SUBLANES, cap), I32),
                   jax.ShapeDtypeStruct((b, ne, SUBLANES, cap), I32),
                   jax.ShapeDtypeStruct((b, ne, nj, LANES), F32),
                   jax.ShapeDtypeStruct((b, SUBLANES, nj), I32)],
        scratch_shapes=[pltpu.VMEM((ne, nj, LANES), F32)] * 3 + [pltpu.VMEM((ne, SUBLANES, LANES), F32)],
        compiler_params=_cparams(("parallel",)),
        name="expert_choice_routing",
    )(aff4)
    return idx[:, :, 0, :], zdst[:, :, 0, :], zloc, rbase[:, 0, :]


def _ffn_kernel(idx_cur, idx_nxt, zdst_prv, zdst_cur, h_hbm, wg_ref, wu_ref, wd_ref, z_hbm,
                xs, xsb, acc, gsem, ssem, *, cap, n_pairs, n_ftiles, per_step, m_chunk, z_rows):
    p = pl.program_id(0)
    j = pl.program_id(1)
    n_slots = n_ftiles * per_step
    dt = xs.shape[0] // n_slots
    last_pair = n_pairs - 1
    cur = p & 1
    b_cur = p // N_EXPERTS
    b_nxt = jnp.minimum(p + 1, last_pair) // N_EXPERTS
    b_prv = jnp.maximum(p - 1, 0) // N_EXPERTS

    def table(ref, c):
        cc = jnp.minimum(c, cap - 1)
        return ref[cc >> LANE_BITS, cc & (LANES - 1)]

    def gather(ref, bq, c):
        dst = xs.at[pl.ds(pl.multiple_of(c * dt, dt), dt)]
        return pltpu.make_async_copy(h_hbm.at[bq, table(ref, c)], dst, gsem)

    def scatter(ref, bq, c, buf):
        row = jnp.where(c < cap, table(ref, c), z_rows + c - cap)
        src = acc.at[buf, pl.ds(jnp.minimum(c, cap - 1), 1)]
        return pltpu.make_async_copy(src, z_hbm.at[bq, pl.ds(row, 1)], ssem)

    def wait_gathers():
        pltpu.make_async_copy(xs, xs, gsem).wait()

    def wait_scatters():
        pltpu.make_async_copy(acc.at[0], acc.at[0], ssem).wait()
        if n_slots > cap:
            pad = acc.at[0, pl.ds(0, n_slots - cap)]
            pltpu.make_async_copy(pad, pad, ssem).wait()

    @pl.when((p == 0) & (j == 0))
    def _():
        acc[1] = jnp.zeros(acc.shape[1:], F32)

        def start(c, carry):
            gather(idx_cur, b_cur, c).start()
            return carry

        lax.fori_loop(0, n_slots, start, 0)

    @pl.when(j == 0)
    def _():
        wait_gathers()
        for c in range(dt):
            xsb[:, c * LANES:(c + 1) * LANES] = xs[pl.ds(c, cap, stride=dt), :].astype(BF16)
        acc[cur] = jnp.zeros(acc.shape[1:], F32)

    wg = wg_ref[...].astype(BF16)
    wu = wu_ref[...].astype(BF16)
    wd = wd_ref[...].astype(BF16)
    n_groups = cap // m_chunk
    per_group = per_step // n_groups
    for m in range(n_groups):
        for t in range(per_group):
            c = j * per_step + (m * per_group + t)
            gather(idx_nxt, b_nxt, c).start()
            scatter(zdst_prv, b_prv, c, 1 - cur).start()
        rows = pl.ds(m * m_chunk, m_chunk)
        x = xsb[rows, :]
        hg = jnp.dot(x, wg, preferred_element_type=F32)
        hu = jnp.dot(x, wu, preferred_element_type=F32)
        act = (hg * _sigmoid(hg)) * hu
        acc[cur, rows, :] += jnp.dot(act.astype(BF16), wd, preferred_element_type=F32)

    @pl.when(j == n_ftiles - 1)
    def _():
        wait_scatters()

        @pl.when(p == last_pair)
        def _():
            def start(c, carry):
                scatter(zdst_cur, b_cur, c, cur).start()
                return carry

            lax.fori_loop(0, n_slots, start, 0)
            wait_scatters()
            wait_gathers()


def _expert_ffn(h, idx, zdst, wg, wu, wd, layer, *, cap, tf, m_chunk):
    b, s, dt, _ = h.shape
    d = dt * LANES
    ne, dff = wg.shape[1], wg.shape[3]
    n_ftiles = dff // tf
    n_pairs = b * ne
    n_groups = cap // m_chunk
    unit = int(np.lcm(n_groups, SUBLANES))
    per_step = -(-cap // (n_ftiles * unit)) * unit
    n_slots = n_ftiles * per_step
    z_rows = ne * cap
    assert n_slots - cap <= cap
    tab = lambda a: a.reshape(n_pairs, cap // LANES, LANES)
    tab_spec = lambda f: pl.BlockSpec((None, cap // LANES, LANES), lambda p, j: (f(p), 0, 0),
                                      memory_space=pltpu.SMEM)
    return pl.pallas_call(
        functools.partial(_ffn_kernel, cap=cap, n_pairs=n_pairs, n_ftiles=n_ftiles, per_step=per_step,
                          m_chunk=m_chunk, z_rows=z_rows),
        grid=(n_pairs, n_ftiles),
        in_specs=[tab_spec(lambda p: p),
                  tab_spec(lambda p: jnp.minimum(p + 1, n_pairs - 1)),
                  tab_spec(lambda p: jnp.maximum(p - 1, 0)),
                  tab_spec(lambda p: p),
                  pl.BlockSpec(memory_space=pl.ANY),
                  pl.BlockSpec((None, None, d, tf), lambda p, j: (layer, p % ne, 0, j)),
                  pl.BlockSpec((None, None, d, tf), lambda p, j: (layer, p % ne, 0, j)),
                  pl.BlockSpec((None, None, tf, d), lambda p, j: (layer, p % ne, j, 0))],
        out_specs=pl.BlockSpec(memory_space=pl.ANY),
        out_shape=jax.ShapeDtypeStruct((b, z_rows + n_slots - cap, d), F32),
        scratch_shapes=[pltpu.VMEM((n_slots * dt, LANES), F32), pltpu.VMEM((cap, d), BF16),
                        pltpu.VMEM((2, cap, d), F32),
                        pltpu.SemaphoreType.DMA(()), pltpu.SemaphoreType.DMA(())],
        compiler_params=_cparams(("arbitrary", "arbitrary")),
        name="expert_swiglu",
    )(tab(idx), tab(idx), tab(zdst), tab(zdst), h, wg, wu, wd)


def _combine_kernel(rb_ref, zloc_ref, aff_ref, x_ref, g_ref, z_hbm, o_ref, zbuf, acc, sem, cnt,
                    *, z_rows, n_tiles, final_norm):
    b = pl.program_id(0)
    j = pl.program_id(1)
    nj = pl.num_programs(1)
    t = b * nj + j

    def base_of(bq, jq):
        return (rb_ref[bq, jq] >> 3) << 3

    def window(bq, jq, k):
        return pl.multiple_of(jnp.minimum(base_of(bq, jq) + k * ZCHUNK, z_rows - ZCHUNK), SUBLANES)

    def fetch(bq, start, slot):
        return pltpu.make_async_copy(z_hbm.at[bq, pl.ds(start, ZCHUNK)], zbuf.at[slot], sem.at[slot])

    @pl.when(t == 0)
    def _():
        cnt[0] = 0
        fetch(b, window(b, j, 0), 0).start()

    r0 = rb_ref[b, j]
    base = base_of(b, j)
    n_chunks = jnp.maximum((rb_ref[b, j + 1] - base + ZCHUNK - 1) // ZCHUNK, 1)
    first = cnt[0]
    zl = zloc_ref[...]
    af = aff_ref[...]
    row = lax.broadcasted_iota(I32, (ZCHUNK, LANES), 0).astype(F32)
    acc[...] = jnp.zeros_like(acc)

    def chunk(k, carry):
        slot = (first + k) & 1
        nominal = base + k * ZCHUNK
        start = window(b, j, k)
        fetch(b, start, slot).wait()

        @pl.when(k + 1 < n_chunks)
        def _():
            fetch(b, window(b, j, k + 1), 1 - slot).start()

        local = row + (start - r0).astype(F32)
        want = jnp.where(local >= jnp.maximum(nominal - r0, 0).astype(F32), local, -2.0)
        gate_t = jnp.zeros((ZCHUNK, LANES), F32)
        for e in range(N_EXPERTS):
            gate_t = gate_t + jnp.where(zl[e:e + 1, :] == want, af[e:e + 1, :], 0.0)
        onehot = jnp.where(gate_t != 0.0, 1.0, 0.0).T.astype(BF16)
        scaled = zbuf[slot] * jnp.sum(gate_t, axis=1, keepdims=True)
        hi = scaled.astype(BF16)
        lo = (scaled - hi.astype(F32)).astype(BF16)
        acc[...] += (jnp.dot(onehot, hi, preferred_element_type=F32)
                     + jnp.dot(onehot, lo, preferred_element_type=F32))
        return carry

    lax.fori_loop(0, n_chunks, chunk, 0)
    cnt[0] = first + n_chunks

    @pl.when(t + 1 < n_tiles)
    def _():
        tn = t + 1
        bn = tn // nj
        fetch(bn, window(bn, tn - bn * nj, 0), (first + n_chunks) & 1).start()

    y = x_ref[...] + acc[...]
    if final_norm:
        ms = jnp.mean(y * y, axis=-1, keepdims=True)
        y = (y * lax.rsqrt(ms + EPS)) * g_ref[...]
    o_ref[...] = y


def _combine(x, z, zloc, aff, rbase, g_final, *, z_rows, final_norm):
    b, s, d = x.shape
    nj = s // LANES
    rb = jnp.concatenate([rbase, jnp.full((b, 1), z_rows, I32)], axis=1)
    zloc_t = jnp.transpose(zloc, (0, 2, 1, 3))
    return pl.pallas_call(
        functools.partial(_combine_kernel, z_rows=z_rows, n_tiles=b * nj, final_norm=final_norm),
        grid_spec=pltpu.PrefetchScalarGridSpec(
            num_scalar_prefetch=1,
            grid=(b, nj),
            in_specs=[pl.BlockSpec((None, None, N_EXPERTS, LANES), lambda bi, j, rb: (bi, j, 0, 0)),
                      pl.BlockSpec((None, N_EXPERTS, LANES), lambda bi, j, rb: (bi, 0, j)),
                      pl.BlockSpec((None, LANES, d), lambda bi, j, rb: (bi, j, 0)),
                      pl.BlockSpec((1, d), lambda bi, j, rb: (0, 0)),
                      pl.BlockSpec(memory_space=pl.ANY)],
            out_specs=pl.BlockSpec((None, LANES, d), lambda bi, j, rb: (bi, j, 0)),
            scratch_shapes=[pltpu.VMEM((2, ZCHUNK, d), F32), pltpu.VMEM((LANES, d), F32),
                            pltpu.SemaphoreType.DMA((2,)), pltpu.SMEM((1,), I32)]),
        out_shape=jax.ShapeDtypeStruct((b, s, d), F32),
        compiler_params=_cparams(("arbitrary", "arbitrary")),
        name="moe_combine",
    )(rb, zloc_t, aff, x, g_final.reshape(1, d), z)


def kernel(x, rel_bias, g_mix, w_in, attn_sink, w_attn_proj, w_fourier_proj, w_out, g_ffn, w_router,
           w_exp_gate, w_exp_up, w_exp_down, g_final):
    b, s, d = x.shape
    depth = g_mix.shape[0]
    cap = CAPACITY_FACTOR * s // N_EXPERTS
    bias = _bias_table(rel_bias)
    tables = _fourier_tables(int(round(np.sqrt(s))))
    for l in range(depth):
        q, k, v, f, ga, gb = _inproj(x.reshape(b * s, d), g_mix[l], w_in[l].astype(BF16), tm=256)
        shp = lambda a: a.reshape(b, s, a.shape[-1])
        attn = _attention(shp(q), shp(k), shp(v), attn_sink[l], bias, blocks_per_step=4)
        four = _fourier(shp(f), tables, rows=8)
        xn, h, aff = _outproj(attn, four, shp(ga), shp(gb), x,
                              w_attn_proj[l].astype(BF16), w_fourier_proj[l].astype(BF16),
                              w_out[l].astype(BF16), g_ffn[l], w_router[l].T, tm=256)
        idx, zdst, zloc, rbase = _routing(aff, cap=cap)
        z = _expert_ffn(h, idx, zdst, w_exp_gate, w_exp_up, w_exp_down, l,
                        cap=cap, tf=256, m_chunk=min(cap, 512))
        x = _combine(xn, z, zloc, aff, rbase, g_final, z_rows=N_EXPERTS * cap, final_norm=(l == depth - 1))
    return x
```

```python
import functools

import numpy as np
import jax
import jax.numpy as jnp
from jax import lax
from jax.experimental import pallas as pl
from jax.experimental.pallas import tpu as pltpu

F32 = jnp.float32
BF16 = jnp.bfloat16
I32 = jnp.int32
HIGHEST = lax.Precision.HIGHEST

N_HEADS = 8
N_KV_HEADS = 2
HEAD_DIM = 64
ATTN_WIDTH = N_HEADS * HEAD_DIM
KV_WIDTH = N_KV_HEADS * HEAD_DIM
WINDOW = 128
BLOCK = 128
N_BUCKETS = 32
MAX_DISTANCE = 128
N_FOURIER_GROUPS = 4
FOURIER_GROUP_DIM = 128
FOURIER_WIDTH = N_FOURIER_GROUPS * FOURIER_GROUP_DIM
N_EXPERTS = 16
CAPACITY_FACTOR = 2
EPS = 1e-6
NEG_INF = -1e30

LANES = 128
LANE_BITS = 7
SUBLANES = 8
VMEM_LIMIT = 56 * 1024 * 1024
EWIN = 32


def _cparams(sem):
    return pltpu.CompilerParams(dimension_semantics=sem, vmem_limit_bytes=VMEM_LIMIT)


def _nt_dot(a, b, **kw):
    return lax.dot_general(a, b, (((1,), (1,)), ((), ())), preferred_element_type=F32, **kw)


def _sigmoid(x):
    return 1.0 / (1.0 + jnp.exp(-x))


def _t5_bucket(rel):
    half = N_BUCKETS // 2
    max_exact = half // 2
    ret = (rel > 0).astype(jnp.int32) * half
    n = jnp.abs(rel)
    nf = jnp.maximum(n, 1).astype(jnp.float32)
    large = max_exact + (jnp.log(nf / max_exact) / np.float32(np.log(MAX_DISTANCE / max_exact))
                         * (half - max_exact)).astype(jnp.int32)
    large = jnp.minimum(large, half - 1)
    return ret + jnp.where(n < max_exact, n, large)


def _bias_kernel(relb_ref, bucket_ref, o_ref):
    bk = bucket_ref[...]
    for h in range(N_HEADS):
        acc = jnp.full(bk.shape, NEG_INF, F32)
        for b in range(N_BUCKETS):
            acc = jnp.where(bk == b, relb_ref[b, h], acc)
        o_ref[h] = acc


def _bias_table(rel_bias):
    q_loc = jnp.arange(BLOCK)
    k_loc = jnp.arange(3 * BLOCK) - BLOCK
    rel = k_loc[None, :] - q_loc[:, None]
    bucket = jnp.where(jnp.abs(rel) <= WINDOW, _t5_bucket(rel), -1).astype(I32)
    return pl.pallas_call(
        _bias_kernel,
        out_shape=jax.ShapeDtypeStruct((N_HEADS, BLOCK, 3 * BLOCK), F32),
        in_specs=[pl.BlockSpec(memory_space=pltpu.SMEM),
                  pl.BlockSpec(memory_space=pltpu.VMEM)],
        out_specs=pl.BlockSpec(memory_space=pltpu.VMEM),
        name="rel_bias_table",
    )(rel_bias.astype(F32), bucket)


def _inproj_kernel(x_ref, g_ref, w_ref, q_ref, k_ref, v_ref, f_ref, ga_ref, gb_ref, *, d_model):
    x = x_ref[...]
    ms = jnp.mean(x * x, axis=-1, keepdims=True)
    h = (x * lax.rsqrt(ms + EPS)) * g_ref[...]
    z = jnp.dot(h.astype(BF16), w_ref[...], preferred_element_type=F32)
    o = 0
    q_ref[...] = (z[:, o:o + ATTN_WIDTH] * (HEAD_DIM ** -0.5)).astype(BF16)
    o += ATTN_WIDTH
    k_ref[...] = z[:, o:o + KV_WIDTH].astype(BF16)
    o += KV_WIDTH
    v_ref[...] = z[:, o:o + KV_WIDTH].astype(BF16)
    o += KV_WIDTH
    f_ref[...] = z[:, o:o + FOURIER_WIDTH]
    o += FOURIER_WIDTH
    ga_ref[...] = z[:, o:o + d_model]
    o += d_model
    gb_ref[...] = z[:, o:o + d_model]


def _inproj(x2, g, w_bf16, *, tm):
    n, d = x2.shape
    in_w = w_bf16.shape[1]
    widths = (ATTN_WIDTH, KV_WIDTH, KV_WIDTH, FOURIER_WIDTH, d, d)
    dtypes = (BF16, BF16, BF16, F32, F32, F32)
    row = lambda i: (i, 0)
    return pl.pallas_call(
        functools.partial(_inproj_kernel, d_model=d),
        grid=(n // tm,),
        in_specs=[pl.BlockSpec((tm, d), row),
                  pl.BlockSpec((1, d), lambda i: (0, 0)),
                  pl.BlockSpec((d, in_w), lambda i: (0, 0))],
        out_specs=[pl.BlockSpec((tm, w), row) for w in widths],
        out_shape=[jax.ShapeDtypeStruct((n, w), dt) for w, dt in zip(widths, dtypes)],
        compiler_params=_cparams(("parallel",)),
        name="inproj",
    )(x2, g.reshape(1, d), w_bf16)


def _attn_kernel(sink_ref, q_ref, kp_ref, ko_ref, kn_ref, vp_ref, vo_ref, vn_ref, bias_ref, o_ref,
                 *, n_blocks, blocks_per_step):
    i = pl.program_id(1)
    kcat = jnp.concatenate([kp_ref[...], ko_ref[...], kn_ref[...]], axis=0)
    vcat = jnp.concatenate([vp_ref[...], vo_ref[...], vn_ref[...]], axis=0)
    col = lax.broadcasted_iota(I32, (BLOCK, 3 * BLOCK), 1)
    group = N_HEADS // N_KV_HEADS
    for t in range(blocks_per_step):
        blk = i * blocks_per_step + t
        lo = jnp.where(blk == 0, BLOCK, 0)
        hi = jnp.where(blk == n_blocks - 1, 2 * BLOCK, 3 * BLOCK)
        in_seq = jnp.where(col >= lo, col, 3 * BLOCK) < hi
        kw = kcat[t * BLOCK:(t + 3) * BLOCK]
        vw = vcat[t * BLOCK:(t + 3) * BLOCK]
        qb = q_ref[t * BLOCK:(t + 1) * BLOCK, :]
        for h in range(N_HEADS):
            kv = h // group
            s = _nt_dot(qb[:, h * HEAD_DIM:(h + 1) * HEAD_DIM], kw[:, kv * HEAD_DIM:(kv + 1) * HEAD_DIM])
            s = jnp.where(in_seq, s + bias_ref[h], NEG_INF)
            sink = sink_ref[h]
            m = jnp.maximum(jnp.max(s, axis=-1, keepdims=True), sink)
            p = jnp.exp(s - m)
            denom = jnp.sum(p, axis=-1, keepdims=True) + jnp.exp(sink - m)
            probs = (p * (1.0 / denom)).astype(BF16)
            out = jnp.dot(probs, vw[:, kv * HEAD_DIM:(kv + 1) * HEAD_DIM], preferred_element_type=F32)
            o_ref[t * BLOCK:(t + 1) * BLOCK, h * HEAD_DIM:(h + 1) * HEAD_DIM] = out.astype(o_ref.dtype)


def _attention(q, k, v, sink, bias, *, blocks_per_step):
    b, s, _ = q.shape
    n_blocks = s // BLOCK
    tq = blocks_per_step * BLOCK
    prev = lambda bi, i: (bi, jnp.maximum(i * blocks_per_step - 1, 0), 0)
    own = lambda bi, i: (bi, i, 0)
    nxt = lambda bi, i: (bi, jnp.minimum((i + 1) * blocks_per_step, n_blocks - 1), 0)
    kv_specs = [pl.BlockSpec((None, BLOCK, KV_WIDTH), prev),
                pl.BlockSpec((None, tq, KV_WIDTH), own),
                pl.BlockSpec((None, BLOCK, KV_WIDTH), nxt)]
    return pl.pallas_call(
        functools.partial(_attn_kernel, n_blocks=n_blocks, blocks_per_step=blocks_per_step),
        grid=(b, s // tq),
        in_specs=[pl.BlockSpec(memory_space=pltpu.SMEM),
                  pl.BlockSpec((None, tq, ATTN_WIDTH), own)] + kv_specs + kv_specs
                 + [pl.BlockSpec((N_HEADS, BLOCK, 3 * BLOCK), lambda bi, i: (0, 0, 0))],
        out_specs=pl.BlockSpec((None, tq, ATTN_WIDTH), own),
        out_shape=jax.ShapeDtypeStruct((b, s, ATTN_WIDTH), BF16),
        compiler_params=_cparams(("parallel", "parallel")),
        name="window_attention",
    )(sink.astype(F32), q, k, k, k, v, v, v, bias)


def _dft1_kernel(x_ref, c_ref, s_ref, tr_ref, ti_ref, *, rows):
    c = c_ref[...]
    sn = s_ref[...]
    for jj in range(rows):
        xj = x_ref[:, jj, :]
        tr_ref[:, jj, :] = jnp.dot(c, xj, precision=HIGHEST, preferred_element_type=F32)
        ti_ref[:, jj, :] = -jnp.dot(sn, xj, precision=HIGHEST, preferred_element_type=F32)


def _dft2_kernel(tr_ref, ti_ref, gr_ref, gi_ref, c2_ref, s2_ref, o_ref, *, rows):
    c2 = c2_ref[...]
    s2 = s2_ref[...]
    dot = functools.partial(jnp.dot, precision=HIGHEST, preferred_element_type=F32)
    for aa in range(rows):
        tr = tr_ref[aa]
        ti = ti_ref[aa]
        gr = gr_ref[aa]
        gi = gi_ref[aa]
        zr = dot(gr, tr) - dot(gi, ti)
        zi = dot(gr, ti) + dot(gi, tr)
        for g in range(N_FOURIER_GROUPS):
            sl = slice(g * FOURIER_GROUP_DIM, (g + 1) * FOURIER_GROUP_DIM)
            y = dot(zr[:, sl], c2) + dot(zi[:, sl], s2)
            o_ref[:, aa, sl] = y.astype(o_ref.dtype)


def _fourier_tables(na):
    n = na * na
    k = np.arange(na, dtype=np.float64)
    ang = 2.0 * np.pi * np.outer(k, k) / na
    c1, s1 = np.cos(ang), np.sin(ang)
    ka = np.arange(na, dtype=np.float64)[:, None, None]
    kb = np.arange(na, dtype=np.float64)[None, :, None]
    nl = np.arange(na, dtype=np.float64)[None, None, :]
    th = 2.0 * np.pi * ((nl * (ka + na * kb)) % n) / n
    gr, gi = np.cos(th), -np.sin(th)
    kc = np.arange(FOURIER_GROUP_DIM, dtype=np.float64)
    angc = 2.0 * np.pi * np.outer(kc, kc) / FOURIER_GROUP_DIM
    scale = 1.0 / np.sqrt(float(n) * FOURIER_GROUP_DIM)
    c2, s2 = np.cos(angc) * scale, np.sin(angc) * scale
    f = lambda a: jnp.asarray(a, dtype=F32)
    return f(c1), f(s1), f(gr), f(gi), f(c2), f(s2)


def _fourier(f, tables, *, rows):
    b, s, w = f.shape
    na = int(round(np.sqrt(s)))
    assert na * na == s and na % rows == 0
    c1, s1, gr, gi, c2, s2 = tables
    x4 = f.reshape(b, na, na, w)
    const2 = lambda bi, j: (0, 0)
    tr, ti = pl.pallas_call(
        functools.partial(_dft1_kernel, rows=rows),
        grid=(b, na // rows),
        in_specs=[pl.BlockSpec((None, na, rows, w), lambda bi, j: (bi, 0, j, 0)),
                  pl.BlockSpec((na, na), const2), pl.BlockSpec((na, na), const2)],
        out_specs=[pl.BlockSpec((None, na, rows, w), lambda bi, j: (bi, 0, j, 0))] * 2,
        out_shape=[jax.ShapeDtypeStruct((b, na, na, w), F32)] * 2,
        compiler_params=_cparams(("parallel", "parallel")),
        name="seq_dft_stage1",
    )(x4, c1, s1)
    y = pl.pallas_call(
        functools.partial(_dft2_kernel, rows=rows),
        grid=(b, na // rows),
        in_specs=[pl.BlockSpec((None, rows, na, w), lambda bi, a: (bi, a, 0, 0))] * 2
                 + [pl.BlockSpec((rows, na, na), lambda bi, a: (a, 0, 0))] * 2
                 + [pl.BlockSpec((FOURIER_GROUP_DIM, FOURIER_GROUP_DIM), const2)] * 2,
        out_specs=pl.BlockSpec((None, na, rows, w), lambda bi, a: (bi, 0, a, 0)),
        out_shape=jax.ShapeDtypeStruct((b, na, na, w), F32),
        compiler_params=_cparams(("parallel", "parallel")),
        name="seq_dft_stage2",
    )(tr, ti, gr, gi, c2, s2)
    return y.reshape(b, s, w)


def _outproj_kernel(a_ref, f_ref, ga_ref, gb_ref, x_ref, wa_ref, wf_ref, wo_ref, g_ref, wr_ref,
                    xn_ref, h_ref, aff_ref):
    a = jnp.dot(a_ref[...], wa_ref[...], preferred_element_type=F32)
    fo = jnp.dot(f_ref[...].astype(BF16), wf_ref[...], preferred_element_type=F32)
    merged = _sigmoid(ga_ref[...]) * a + _sigmoid(gb_ref[...]) * fo
    xn = x_ref[...] + jnp.dot(merged.astype(BF16), wo_ref[...], preferred_element_type=F32)
    xn_ref[...] = xn
    ms = jnp.mean(xn * xn, axis=-1, keepdims=True)
    h = (xn * lax.rsqrt(ms + EPS)) * g_ref[...]
    for c in range(h.shape[1] // LANES):
        h_ref[:, c, :] = h[:, c * LANES:(c + 1) * LANES]
    logits = _nt_dot(wr_ref[...], h, precision=HIGHEST)
    e = jnp.exp(logits - jnp.max(logits, axis=0, keepdims=True))
    aff_ref[...] = e / jnp.sum(e, axis=0, keepdims=True)


def _outproj(attn, four, ga, gb, x, wa, wf, wo, g, wr_t, *, tm):
    b, s, d = x.shape
    tok = lambda bi, i: (bi, i, 0)
    const = lambda bi, i: (0, 0)
    return pl.pallas_call(
        _outproj_kernel,
        grid=(b, s // tm),
        in_specs=[pl.BlockSpec((None, tm, ATTN_WIDTH), tok),
                  pl.BlockSpec((None, tm, FOURIER_WIDTH), tok),
                  pl.BlockSpec((None, tm, d), tok),
                  pl.BlockSpec((None, tm, d), tok),
                  pl.BlockSpec((None, tm, d), tok),
                  pl.BlockSpec((ATTN_WIDTH, d), const),
                  pl.BlockSpec((FOURIER_WIDTH, d), const),
                  pl.BlockSpec((d, d), const),
                  pl.BlockSpec((1, d), const),
                  pl.BlockSpec((N_EXPERTS, d), const)],
        out_specs=[pl.BlockSpec((None, tm, d), tok),
                   pl.BlockSpec((None, tm, d // LANES, LANES), lambda bi, i: (bi, i, 0, 0)),
                   pl.BlockSpec((None, N_EXPERTS, tm), lambda bi, i: (bi, 0, i))],
        out_shape=[jax.ShapeDtypeStruct((b, s, d), F32),
                   jax.ShapeDtypeStruct((b, s, d // LANES, LANES), F32),
                   jax.ShapeDtypeStruct((b, N_EXPERTS, s), F32)],
        compiler_params=_cparams(("parallel", "parallel")),
        name="outproj_router",
    )(attn, four, ga, gb, x, wa, wf, wo, g.reshape(1, d), wr_t)


def _tri(n, kind):
    r = lax.broadcasted_iota(I32, (n, n), 0)
    c = lax.broadcasted_iota(I32, (n, n), 1)
    cond = {"row_le_col": r <= c, "row_lt_col": r < c, "col_le_row": c <= r, "col_lt_row": c < r}[kind]
    return jnp.where(cond, 1.0, 0.0).astype(BF16)


def _split128(v):
    hi = jnp.floor(v * (1.0 / LANES))
    return hi, v - hi * LANES


def _routing_kernel(aff_ref, idx_ref, pos_ref, rank_ref, gt_scr, eq_scr, need_scr, *, cap):
    ne, nj, ni = aff_ref.shape
    bits = lax.bitcast_convert_type(aff_ref[...], I32)

    def bisect(it, thr):
        cand = thr | jnp.left_shift(jnp.int32(1), 30 - it)
        cnt = jnp.sum(jnp.where(bits >= cand, 1.0, 0.0), axis=(1, 2), keepdims=True)
        return jnp.where(cnt >= cap, cand, thr)

    thr = lax.fori_loop(0, 31, bisect, jnp.zeros((ne, 1, 1), I32))
    gt = jnp.where(bits > thr, 1.0, 0.0)
    gt_scr[...] = gt
    eq_scr[...] = jnp.where(bits == thr, 1.0, 0.0)
    need = cap - jnp.sum(gt, axis=(1, 2), keepdims=True)
    need_scr[...] = jnp.broadcast_to(need, (ne, SUBLANES, ni))

    ones_sq = jnp.ones((ni, ni), BF16)
    ones_row = jnp.ones((SUBLANES, ni), BF16)
    ones_rowj = jnp.ones((SUBLANES, nj), BF16)
    u_incl = _tri(ni, "row_le_col")
    uj_strict = _tri(nj, "row_lt_col")
    lj_strict = _tri(nj, "col_lt_row")
    l_incl = _tri(ni, "col_le_row")
    bdot = lambda a, b: jnp.dot(a, b, preferred_element_type=F32)

    c_lane = lax.broadcasted_iota(I32, (nj, cap), 1).astype(F32)
    c_row = lax.broadcasted_iota(I32, (SUBLANES, cap), 1).astype(F32)
    j_sub = lax.broadcasted_iota(I32, (nj, cap), 0).astype(F32)

    def slots(e, carry):
        eq = eq_scr[e]
        eqb = eq.astype(BF16)
        tie_rank = bdot(lj_strict, bdot(eqb, ones_sq).astype(BF16)) + bdot(eqb, u_incl)
        take = jnp.where(tie_rank <= need_scr[e][0:1, :], eq, 0.0)
        sel = jnp.maximum(gt_scr[e], take)
        selb = sel.astype(BF16)
        lc = bdot(selb, u_incl)
        lc_t = _nt_dot(l_incl, selb)
        n_rep = bdot(selb, ones_sq)
        s_rep = bdot(lj_strict, n_rep.astype(BF16)) + n_rep
        n_lane = _nt_dot(ones_row, selb)
        pex_lane = bdot(n_lane.astype(BF16), uj_strict)
        s_wide = jnp.concatenate([s_rep] * (cap // ni), axis=1) if cap > ni else s_rep[:, :cap]
        jc = bdot(ones_rowj, jnp.where(s_wide <= c_lane, 1.0, 0.0).astype(BF16))
        onehot = jnp.where(j_sub == jnp.broadcast_to(jc[0:1], (nj, cap)), 1.0, 0.0).astype(BF16)
        phi, plo = _split128(pex_lane)
        r = c_row - (bdot(phi.astype(BF16), onehot) * LANES + bdot(plo.astype(BF16), onehot))
        lcs = bdot(lc_t.astype(BF16), onehot)
        ic = bdot(ones_row, jnp.where(lcs <= jnp.broadcast_to(r[0:1], (ni, cap)), 1.0, 0.0).astype(BF16))
        idx_ref[e] = (jc * ni + ic).astype(I32)
        pos_ref[e] = pex_lane.astype(I32)
        rank_ref[e] = jnp.where(sel > 0.0, lc - 1.0, -1.0)
        return carry

    lax.fori_loop(0, ne, slots, 0)


def _routing(aff, *, cap):
    b, ne, s = aff.shape
    nj = s // LANES
    aff4 = aff.reshape(b, ne, nj, LANES)
    per_b = lambda bi: (bi, 0, 0, 0)
    idx, pos, rank = pl.pallas_call(
        functools.partial(_routing_kernel, cap=cap),
        grid=(b,),
        in_specs=[pl.BlockSpec((None, ne, nj, LANES), per_b)],
        out_specs=[pl.BlockSpec((None, ne, SUBLANES, cap), per_b),
                   pl.BlockSpec((None, ne, SUBLANES, nj), per_b),
                   pl.BlockSpec((None, ne, nj, LANES), per_b)],
        out_shape=[jax.ShapeDtypeStruct((b, ne, SUBLANES, cap), I32),
                   jax.ShapeDtypeStruct((b, ne, SUBLANES, nj), I32),
                   jax.ShapeDtypeStruct((b, ne, nj, LANES), F32)],
        scratch_shapes=[pltpu.VMEM((ne, nj, LANES), F32)] * 2 + [pltpu.VMEM((ne, SUBLANES, LANES), F32)],
        compiler_params=_cparams(("parallel",)),
        name="expert_choice_routing",
    )(aff4)
    return idx[:, :, 0, :], pos[:, :, 0, :], rank


def _ffn_kernel(idx_cur, idx_nxt, h_hbm, wg_ref, wu_ref, wd_ref, y_ref, xs, xsb, gsem,
                *, cap, n_pairs, n_ftiles, per_step, m_chunk):
    p = pl.program_id(0)
    j = pl.program_id(1)
    n_slots = n_ftiles * per_step
    dt = xs.shape[0] // n_slots
    last_pair = n_pairs - 1
    b_cur = p // N_EXPERTS
    b_nxt = jnp.minimum(p + 1, last_pair) // N_EXPERTS

    def gather(ref, bq, c):
        cc = jnp.minimum(c, cap - 1)
        tok = ref[cc >> LANE_BITS, cc & (LANES - 1)]
        dst = xs.at[pl.ds(pl.multiple_of(c * dt, dt), dt)]
        return pltpu.make_async_copy(h_hbm.at[bq, tok], dst, gsem)

    def wait_gathers():
        pltpu.make_async_copy(xs, xs, gsem).wait()

    @pl.when((p == 0) & (j == 0))
    def _():
        def start(c, carry):
            gather(idx_cur, b_cur, c).start()
            return carry

        lax.fori_loop(0, n_slots, start, 0)

    @pl.when(j == 0)
    def _():
        wait_gathers()
        for c in range(dt):
            xsb[:, c * LANES:(c + 1) * LANES] = xs[pl.ds(c, cap, stride=dt), :].astype(BF16)
        y_ref[...] = jnp.zeros_like(y_ref)

    wg = wg_ref[...].astype(BF16)
    wu = wu_ref[...].astype(BF16)
    wd = wd_ref[...].astype(BF16)
    n_groups = cap // m_chunk
    per_group = per_step // n_groups
    for m in range(n_groups):
        for t in range(per_group):
            gather(idx_nxt, b_nxt, j * per_step + (m * per_group + t)).start()
        rows = pl.ds(m * m_chunk, m_chunk)
        x = xsb[rows, :]
        hg = jnp.dot(x, wg, preferred_element_type=F32)
        hu = jnp.dot(x, wu, preferred_element_type=F32)
        act = (hg * _sigmoid(hg)) * hu
        y_ref[rows, :] += jnp.dot(act.astype(BF16), wd, preferred_element_type=F32)

    @pl.when((p == last_pair) & (j == n_ftiles - 1))
    def _():
        wait_gathers()


def _expert_ffn(h, idx, wg, wu, wd, layer, *, cap, tf, m_chunk):
    b, s, dt, _ = h.shape
    d = dt * LANES
    ne, dff = wg.shape[1], wg.shape[3]
    n_ftiles = dff // tf
    n_pairs = b * ne
    n_groups = cap // m_chunk
    per_step = -(-cap // (n_ftiles * n_groups)) * n_groups
    n_slots = n_ftiles * per_step
    tab = idx.reshape(n_pairs, cap // LANES, LANES)
    tab_spec = lambda f: pl.BlockSpec((None, cap // LANES, LANES), lambda p, j: (f(p), 0, 0),
                                      memory_space=pltpu.SMEM)
    return pl.pallas_call(
        functools.partial(_ffn_kernel, cap=cap, n_pairs=n_pairs, n_ftiles=n_ftiles, per_step=per_step,
                          m_chunk=m_chunk),
        grid=(n_pairs, n_ftiles),
        in_specs=[tab_spec(lambda p: p),
                  tab_spec(lambda p: jnp.minimum(p + 1, n_pairs - 1)),
                  pl.BlockSpec(memory_space=pl.ANY),
                  pl.BlockSpec((None, None, d, tf), lambda p, j: (layer, p % ne, 0, j)),
                  pl.BlockSpec((None, None, d, tf), lambda p, j: (layer, p % ne, 0, j)),
                  pl.BlockSpec((None, None, tf, d), lambda p, j: (layer, p % ne, j, 0))],
        out_specs=pl.BlockSpec((None, None, cap, d), lambda p, j: (p // ne, p % ne, 0, 0)),
        out_shape=jax.ShapeDtypeStruct((b, ne, cap, d), F32),
        scratch_shapes=[pltpu.VMEM((n_slots * dt, LANES), F32), pltpu.VMEM((cap, d), BF16),
                        pltpu.SemaphoreType.DMA(())],
        compiler_params=_cparams(("arbitrary", "arbitrary")),
        name="expert_swiglu",
    )(tab, tab, h, wg, wu, wd)


def _combine_kernel(pos_ref, rank_ref, aff_ref, x_ref, g_ref, y_hbm, o_ref, ybuf, acc, sem, cnt,
                    *, cap, n_tiles, final_norm):
    b = pl.program_id(0)
    j = pl.program_id(1)
    nj = pl.num_programs(1)
    t = b * nj + j
    ne = N_EXPERTS

    def nominal(bq, e, jq, r):
        return ((pos_ref[bq * ne + e, jq] >> 3) << 3) + r * EWIN

    def window(bq, e, jq, r):
        return pl.multiple_of(jnp.minimum(nominal(bq, e, jq, r), cap - EWIN), SUBLANES)

    def copies(bq, jq, r, slot):
        return [pltpu.make_async_copy(y_hbm.at[bq, e, pl.ds(window(bq, e, jq, r), EWIN)],
                                      ybuf.at[slot, pl.ds(e * EWIN, EWIN)], sem.at[slot]) for e in range(ne)]

    def start_fetch(bq, jq, r, slot):
        for cp in copies(bq, jq, r, slot):
            cp.start()

    @pl.when(t == 0)
    def _():
        cnt[0] = 0
        start_fetch(b, j, 0, 0)

    span = jnp.int32(1)
    for e in range(ne):
        span = jnp.maximum(span, pos_ref[b * ne + e, j + 1] - nominal(b, e, j, 0))
    n_rounds = (span + EWIN - 1) // EWIN
    first = cnt[0]
    rk = rank_ref[...]
    af = aff_ref[...]
    row = lax.broadcasted_iota(I32, (EWIN, LANES), 0).astype(F32)
    acc[...] = jnp.zeros_like(acc)

    def one_round(r, carry):
        slot = (first + r) & 1
        pltpu.make_async_copy(ybuf.at[slot], ybuf.at[slot], sem.at[slot]).wait()

        @pl.when(r + 1 < n_rounds)
        def _():
            start_fetch(b, j, r + 1, 1 - slot)

        parts = []
        for e in range(ne):
            p0 = pos_ref[b * ne + e, j]
            local = row + (window(b, e, j, r) - p0).astype(F32)
            want = jnp.where(local >= jnp.maximum(nominal(b, e, j, r) - p0, 0).astype(F32), local, -2.0)
            parts.append(jnp.where(rk[e:e + 1, :] == want, af[e:e + 1, :], 0.0))
        gate_t = jnp.concatenate(parts, axis=0)
        onehot = jnp.where(gate_t != 0.0, 1.0, 0.0).T.astype(BF16)
        scaled = ybuf[slot] * jnp.sum(gate_t, axis=1, keepdims=True)
        hi = scaled.astype(BF16)
        lo = (scaled - hi.astype(F32)).astype(BF16)
        acc[...] += (jnp.dot(onehot, hi, preferred_element_type=F32)
                     + jnp.dot(onehot, lo, preferred_element_type=F32))
        return carry

    lax.fori_loop(0, n_rounds, one_round, 0)
    cnt[0] = first + n_rounds

    @pl.when(t + 1 < n_tiles)
    def _():
        tn = t + 1
        bn = tn // nj
        start_fetch(bn, tn - bn * nj, 0, (first + n_rounds) & 1)

    y = x_ref[...] + acc[...]
    if final_norm:
        ms = jnp.mean(y * y, axis=-1, keepdims=True)
        y = (y * lax.rsqrt(ms + EPS)) * g_ref[...]
    o_ref[...] = y


def _combine(x, y, pos, rank, aff, g_final, *, cap, final_norm):
    b, s, d = x.shape
    nj = s // LANES
    ne = N_EXPERTS
    pos_tab = jnp.concatenate([pos, jnp.full((b, ne, 1), cap, I32)], axis=2).reshape(b * ne, nj + 1)
    rank_t = jnp.transpose(rank, (0, 2, 1, 3))
    return pl.pallas_call(
        functools.partial(_combine_kernel, cap=cap, n_tiles=b * nj, final_norm=final_norm),
        grid_spec=pltpu.PrefetchScalarGridSpec(
            num_scalar_prefetch=1,
            grid=(b, nj),
            in_specs=[pl.BlockSpec((None, None, ne, LANES), lambda bi, j, tab: (bi, j, 0, 0)),
                      pl.BlockSpec((None, ne, LANES), lambda bi, j, tab: (bi, 0, j)),
                      pl.BlockSpec((None, LANES, d), lambda bi, j, tab: (bi, j, 0)),
                      pl.BlockSpec((1, d), lambda bi, j, tab: (0, 0)),
                      pl.BlockSpec(memory_space=pl.ANY)],
            out_specs=pl.BlockSpec((None, LANES, d), lambda bi, j, tab: (bi, j, 0)),
            scratch_shapes=[pltpu.VMEM((2, ne * EWIN, d), F32), pltpu.VMEM((LANES, d), F32),
                            pltpu.SemaphoreType.DMA((2,)), pltpu.SMEM((1,), I32)]),
        out_shape=jax.ShapeDtypeStruct((b, s, d), F32),
        compiler_params=_cparams(("arbitrary", "arbitrary")),
        name="moe_combine",
    )(pos_tab, rank_t, aff, x, g_final.reshape(1, d), y)


def kernel(x, rel_bias, g_mix, w_in, attn_sink, w_attn_proj, w_fourier_proj, w_out, g_ffn, w_router,
           w_exp_gate, w_exp_up, w_exp_down, g_final):
    b, s, d = x.shape
    depth = g_mix.shape[0]
    cap = CAPACITY_FACTOR * s // N_EXPERTS
    bias = _bias_table(rel_bias)
    tables = _fourier_tables(int(round(np.sqrt(s))))
    for l in range(depth):
        q, k, v, f, ga, gb = _inproj(x.reshape(b * s, d), g_mix[l], w_in[l].astype(BF16), tm=256)
        shp = lambda a: a.reshape(b, s, a.shape[-1])
        attn = _attention(shp(q), shp(k), shp(v), attn_sink[l], bias, blocks_per_step=4)
        four = _fourier(shp(f), tables, rows=8)
        xn, h, aff = _outproj(attn, four, shp(ga), shp(gb), x,
                              w_attn_proj[l].astype(BF16), w_fourier_proj[l].astype(BF16),
                              w_out[l].astype(BF16), g_ffn[l], w_router[l].T, tm=256)
        idx, pos, rank = _routing(aff, cap=cap)
        y = _expert_ffn(h, idx, w_exp_gate, w_exp_up, w_exp_down, l, cap=cap, tf=256, m_chunk=min(cap, 512))
        x = _combine(xn, y, pos, rank, aff, g_final, cap=cap, final_norm=(l == depth - 1))
    return x
```

```python
import functools

import numpy as np
import jax
import jax.numpy as jnp
from jax import lax
from jax.experimental import pallas as pl
from jax.experimental.pallas import tpu as pltpu

F32 = jnp.float32
BF16 = jnp.bfloat16
I32 = jnp.int32
HIGHEST = lax.Precision.HIGHEST

N_HEADS = 8
N_KV_HEADS = 2
HEAD_DIM = 64
ATTN_WIDTH = N_HEADS * HEAD_DIM
KV_WIDTH = N_KV_HEADS * HEAD_DIM
WINDOW = 128
BLOCK = 128
N_BUCKETS = 32
MAX_DISTANCE = 128
N_FOURIER_GROUPS = 4
FOURIER_GROUP_DIM = 128
FOURIER_WIDTH = N_FOURIER_GROUPS * FOURIER_GROUP_DIM
N_EXPERTS = 16
CAPACITY_FACTOR = 2
EPS = 1e-6
NEG_INF = -1e30

LANES = 128
LANE_BITS = 7
SUBLANES = 8
VMEM_LIMIT = 56 * 1024 * 1024
EWIN = 32


def _cparams(sem):
    return pltpu.CompilerParams(dimension_semantics=sem, vmem_limit_bytes=VMEM_LIMIT)


def _nt_dot(a, b, **kw):
    return lax.dot_general(a, b, (((1,), (1,)), ((), ())), preferred_element_type=F32, **kw)


def _sigmoid(x):
    return 1.0 / (1.0 + jnp.exp(-x))


def _t5_bucket(rel):
    half = N_BUCKETS // 2
    max_exact = half // 2
    ret = (rel > 0).astype(jnp.int32) * half
    n = jnp.abs(rel)
    nf = jnp.maximum(n, 1).astype(jnp.float32)
    large = max_exact + (jnp.log(nf / max_exact) / np.float32(np.log(MAX_DISTANCE / max_exact))
                         * (half - max_exact)).astype(jnp.int32)
    large = jnp.minimum(large, half - 1)
    return ret + jnp.where(n < max_exact, n, large)


def _bias_kernel(relb_ref, bucket_ref, o_ref):
    bk = bucket_ref[...]
    col = lax.broadcasted_iota(I32, bk.shape, 1)
    for h in range(N_HEADS):
        acc = jnp.full(bk.shape, NEG_INF, F32)
        for b in range(N_BUCKETS):
            acc = jnp.where(bk == b, relb_ref[b, h], acc)
        o_ref[0, h] = jnp.where(col >= BLOCK, acc, NEG_INF)
        o_ref[1, h] = acc
        o_ref[2, h] = jnp.where(col < 2 * BLOCK, acc, NEG_INF)


def _bias_table(rel_bias):
    q_loc = jnp.arange(BLOCK)
    k_loc = jnp.arange(3 * BLOCK) - BLOCK
    rel = k_loc[None, :] - q_loc[:, None]
    bucket = jnp.where(jnp.abs(rel) <= WINDOW, _t5_bucket(rel), -1).astype(I32)
    table = pl.pallas_call(
        _bias_kernel,
        out_shape=jax.ShapeDtypeStruct((3, N_HEADS, BLOCK, 3 * BLOCK), F32),
        in_specs=[pl.BlockSpec(memory_space=pltpu.SMEM),
                  pl.BlockSpec(memory_space=pltpu.VMEM)],
        out_specs=pl.BlockSpec(memory_space=pltpu.VMEM),
        name="rel_bias_table",
    )(rel_bias.astype(F32), bucket)
    return table.reshape(3, N_HEADS * BLOCK, 3 * BLOCK)


def _inproj_kernel(x_ref, g_ref, w_ref, q_ref, k_ref, v_ref, f_ref, ga_ref, gb_ref, *, d_model):
    x = x_ref[...]
    ms = jnp.mean(x * x, axis=-1, keepdims=True)
    h = (x * lax.rsqrt(ms + EPS)) * g_ref[...]
    z = jnp.dot(h.astype(BF16), w_ref[...], preferred_element_type=F32)
    o = 0
    q_ref[...] = (z[:, o:o + ATTN_WIDTH] * (HEAD_DIM ** -0.5)).astype(BF16)
    o += ATTN_WIDTH
    k_ref[...] = z[:, o:o + KV_WIDTH].astype(BF16)
    o += KV_WIDTH
    v_ref[...] = z[:, o:o + KV_WIDTH].astype(BF16)
    o += KV_WIDTH
    f_ref[...] = z[:, o:o + FOURIER_WIDTH]
    o += FOURIER_WIDTH
    ga_ref[...] = z[:, o:o + d_model]
    o += d_model
    gb_ref[...] = z[:, o:o + d_model]


def _inproj(x2, g, w_bf16, *, tm):
    n, d = x2.shape
    in_w = w_bf16.shape[1]
    widths = (ATTN_WIDTH, KV_WIDTH, KV_WIDTH, FOURIER_WIDTH, d, d)
    dtypes = (BF16, BF16, BF16, F32, F32, F32)
    row = lambda i: (i, 0)
    return pl.pallas_call(
        functools.partial(_inproj_kernel, d_model=d),
        grid=(n // tm,),
        in_specs=[pl.BlockSpec((tm, d), row),
                  pl.BlockSpec((1, d), lambda i: (0, 0)),
                  pl.BlockSpec((d, in_w), lambda i: (0, 0))],
        out_specs=[pl.BlockSpec((tm, w), row) for w in widths],
        out_shape=[jax.ShapeDtypeStruct((n, w), dt) for w, dt in zip(widths, dtypes)],
        compiler_params=_cparams(("parallel",)),
        name="inproj",
    )(x2, g.reshape(1, d), w_bf16)


def _attn_kernel(sink_ref, q_ref, kp_ref, ko_ref, kn_ref, vp_ref, vo_ref, vn_ref, bias_ref, o_ref,
                 *, n_blocks, blocks_per_step):
    i = pl.program_id(1)
    kcat = jnp.concatenate([kp_ref[...], ko_ref[...], kn_ref[...]], axis=0)
    vcat = jnp.concatenate([vp_ref[...], vo_ref[...], vn_ref[...]], axis=0)
    group = N_HEADS // N_KV_HEADS
    head = lambda a, h: a[:, h * HEAD_DIM:(h + 1) * HEAD_DIM]
    sink = jnp.concatenate([jnp.full((BLOCK, 1), sink_ref[h], F32) for h in range(N_HEADS)], axis=0)
    for t in range(blocks_per_step):
        blk = i * blocks_per_step + t
        kind = jnp.where(blk == 0, 0, jnp.where(blk == n_blocks - 1, 2, 1))
        kw = kcat[t * BLOCK:(t + 3) * BLOCK]
        vw = vcat[t * BLOCK:(t + 3) * BLOCK]
        qb = q_ref[t * BLOCK:(t + 1) * BLOCK, :]
        s = jnp.concatenate([_nt_dot(head(qb, h), head(kw, h // group)) for h in range(N_HEADS)], axis=0)
        s = s + bias_ref[kind]
        m = jnp.maximum(jnp.max(s, axis=-1, keepdims=True), sink)
        p = jnp.exp(s - m)
        denom = jnp.sum(p, axis=-1, keepdims=True) + jnp.exp(sink - m)
        probs = (p * (1.0 / denom)).astype(BF16)
        for h in range(N_HEADS):
            out = jnp.dot(probs[h * BLOCK:(h + 1) * BLOCK], head(vw, h // group), preferred_element_type=F32)
            o_ref[t * BLOCK:(t + 1) * BLOCK, h * HEAD_DIM:(h + 1) * HEAD_DIM] = out.astype(o_ref.dtype)


def _attention(q, k, v, sink, bias, *, blocks_per_step):
    b, s, _ = q.shape
    n_blocks = s // BLOCK
    tq = blocks_per_step * BLOCK
    prev = lambda bi, i: (bi, jnp.maximum(i * blocks_per_step - 1, 0), 0)
    own = lambda bi, i: (bi, i, 0)
    nxt = lambda bi, i: (bi, jnp.minimum((i + 1) * blocks_per_step, n_blocks - 1), 0)
    kv_specs = [pl.BlockSpec((None, BLOCK, KV_WIDTH), prev),
                pl.BlockSpec((None, tq, KV_WIDTH), own),
                pl.BlockSpec((None, BLOCK, KV_WIDTH), nxt)]
    return pl.pallas_call(
        functools.partial(_attn_kernel, n_blocks=n_blocks, blocks_per_step=blocks_per_step),
        grid=(b, s // tq),
        in_specs=[pl.BlockSpec(memory_space=pltpu.SMEM),
                  pl.BlockSpec((None, tq, ATTN_WIDTH), own)] + kv_specs + kv_specs
                 + [pl.BlockSpec((3, N_HEADS * BLOCK, 3 * BLOCK), lambda bi, i: (0, 0, 0))],
        out_specs=pl.BlockSpec((None, tq, ATTN_WIDTH), own),
        out_shape=jax.ShapeDtypeStruct((b, s, ATTN_WIDTH), BF16),
        compiler_params=_cparams(("parallel", "parallel")),
        name="window_attention",
    )(sink.astype(F32), q, k, k, k, v, v, v, bias)


def _dft1_kernel(x_ref, c_ref, s_ref, tr_ref, ti_ref, *, rows):
    c = c_ref[...]
    sn = s_ref[...]
    for jj in range(rows):
        xj = x_ref[:, jj, :]
        tr_ref[:, jj, :] = jnp.dot(c, xj, precision=HIGHEST, preferred_element_type=F32)
        ti_ref[:, jj, :] = -jnp.dot(sn, xj, precision=HIGHEST, preferred_element_type=F32)


def _dft2_kernel(tr_ref, ti_ref, gr_ref, gi_ref, cs_ref, o_ref, *, rows):
    cs = cs_ref[...]
    dot = functools.partial(jnp.dot, precision=HIGHEST, preferred_element_type=F32)
    for aa in range(rows):
        t = jnp.concatenate([tr_ref[aa], ti_ref[aa]], axis=0)
        zr = dot(gr_ref[aa], t)
        zi = dot(gi_ref[aa], t)
        for g in range(N_FOURIER_GROUPS):
            sl = slice(g * FOURIER_GROUP_DIM, (g + 1) * FOURIER_GROUP_DIM)
            y = dot(jnp.concatenate([zr[:, sl], zi[:, sl]], axis=1), cs)
            o_ref[:, aa, sl] = y.astype(o_ref.dtype)


def _fourier_tables(na):
    n = na * na
    k = np.arange(na, dtype=np.float64)
    ang = 2.0 * np.pi * np.outer(k, k) / na
    c1, s1 = np.cos(ang), np.sin(ang)
    ka = np.arange(na, dtype=np.float64)[:, None, None]
    kb = np.arange(na, dtype=np.float64)[None, :, None]
    nl = np.arange(na, dtype=np.float64)[None, None, :]
    th = 2.0 * np.pi * ((nl * (ka + na * kb)) % n) / n
    gr, gi = np.cos(th), -np.sin(th)
    g_re = np.concatenate([gr, -gi], axis=2)
    g_im = np.concatenate([gi, gr], axis=2)
    kc = np.arange(FOURIER_GROUP_DIM, dtype=np.float64)
    angc = 2.0 * np.pi * np.outer(kc, kc) / FOURIER_GROUP_DIM
    scale = 1.0 / np.sqrt(float(n) * FOURIER_GROUP_DIM)
    cs = np.concatenate([np.cos(angc), np.sin(angc)], axis=0) * scale
    f = lambda a: jnp.asarray(a, dtype=F32)
    return f(c1), f(s1), f(g_re), f(g_im), f(cs)


def _fourier(f, tables, *, rows):
    b, s, w = f.shape
    na = int(round(np.sqrt(s)))
    assert na * na == s and na % rows == 0
    c1, s1, g_re, g_im, cs = tables
    x4 = f.reshape(b, na, na, w)
    const2 = lambda bi, j: (0, 0)
    tr, ti = pl.pallas_call(
        functools.partial(_dft1_kernel, rows=rows),
        grid=(b, na // rows),
        in_specs=[pl.BlockSpec((None, na, rows, w), lambda bi, j: (bi, 0, j, 0)),
                  pl.BlockSpec((na, na), const2), pl.BlockSpec((na, na), const2)],
        out_specs=[pl.BlockSpec((None, na, rows, w), lambda bi, j: (bi, 0, j, 0))] * 2,
        out_shape=[jax.ShapeDtypeStruct((b, na, na, w), F32)] * 2,
        compiler_params=_cparams(("parallel", "parallel")),
        name="seq_dft_stage1",
    )(x4, c1, s1)
    y = pl.pallas_call(
        functools.partial(_dft2_kernel, rows=rows),
        grid=(b, na // rows),
        in_specs=[pl.BlockSpec((None, rows, na, w), lambda bi, a: (bi, a, 0, 0))] * 2
                 + [pl.BlockSpec((rows, na, 2 * na), lambda bi, a: (a, 0, 0))] * 2
                 + [pl.BlockSpec((2 * FOURIER_GROUP_DIM, FOURIER_GROUP_DIM), const2)],
        out_specs=pl.BlockSpec((None, na, rows, w), lambda bi, a: (bi, 0, a, 0)),
        out_shape=jax.ShapeDtypeStruct((b, na, na, w), F32),
        compiler_params=_cparams(("parallel", "parallel")),
        name="seq_dft_stage2",
    )(tr, ti, g_re, g_im, cs)
    return y.reshape(b, s, w)


def _outproj_kernel(a_ref, f_ref, ga_ref, gb_ref, x_ref, wa_ref, wf_ref, wo_ref, g_ref, wr_ref,
                    xn_ref, h_ref, aff_ref, *, chunk):
    for r in range(x_ref.shape[0] // chunk):
        rs = pl.ds(r * chunk, chunk)
        a = jnp.dot(a_ref[rs, :], wa_ref[...], preferred_element_type=F32)
        fo = jnp.dot(f_ref[rs, :].astype(BF16), wf_ref[...], preferred_element_type=F32)
        merged = _sigmoid(ga_ref[rs, :]) * a + _sigmoid(gb_ref[rs, :]) * fo
        xn = x_ref[rs, :] + jnp.dot(merged.astype(BF16), wo_ref[...], preferred_element_type=F32)
        xn_ref[rs, :] = xn
        ms = jnp.mean(xn * xn, axis=-1, keepdims=True)
        h = (xn * lax.rsqrt(ms + EPS)) * g_ref[...]
        for c in range(h.shape[1] // LANES):
            h_ref[rs, c, :] = h[:, c * LANES:(c + 1) * LANES]
        logits = _nt_dot(wr_ref[...], h, precision=HIGHEST)
        e = jnp.exp(logits - jnp.max(logits, axis=0, keepdims=True))
        aff_ref[:, rs] = e / jnp.sum(e, axis=0, keepdims=True)


def _outproj(attn, four, ga, gb, x, wa, wf, wo, g, wr_t, *, tm, chunk):
    b, s, d = x.shape
    tok = lambda bi, i: (bi, i, 0)
    const = lambda bi, i: (0, 0)
    return pl.pallas_call(
        functools.partial(_outproj_kernel, chunk=chunk),
        grid=(b, s // tm),
        in_specs=[pl.BlockSpec((None, tm, ATTN_WIDTH), tok),
                  pl.BlockSpec((None, tm, FOURIER_WIDTH), tok),
                  pl.BlockSpec((None, tm, d), tok),
                  pl.BlockSpec((None, tm, d), tok),
                  pl.BlockSpec((None, tm, d), tok),
                  pl.BlockSpec((ATTN_WIDTH, d), const),
                  pl.BlockSpec((FOURIER_WIDTH, d), const),
                  pl.BlockSpec((d, d), const),
                  pl.BlockSpec((1, d), const),
                  pl.BlockSpec((N_EXPERTS, d), const)],
        out_specs=[pl.BlockSpec((None, tm, d), tok),
                   pl.BlockSpec((None, tm, d // LANES, LANES), lambda bi, i: (bi, i, 0, 0)),
                   pl.BlockSpec((None, N_EXPERTS, tm), lambda bi, i: (bi, 0, i))],
        out_shape=[jax.ShapeDtypeStruct((b, s, d), F32),
                   jax.ShapeDtypeStruct((b, s, d // LANES, LANES), F32),
                   jax.ShapeDtypeStruct((b, N_EXPERTS, s), F32)],
        compiler_params=_cparams(("parallel", "parallel")),
        name="outproj_router",
    )(attn, four, ga, gb, x, wa, wf, wo, g.reshape(1, d), wr_t)


def _tri(n, kind):
    r = lax.broadcasted_iota(I32, (n, n), 0)
    c = lax.broadcasted_iota(I32, (n, n), 1)
    cond = {"row_le_col": r <= c, "row_lt_col": r < c, "col_le_row": c <= r, "col_lt_row": c < r}[kind]
    return jnp.where(cond, 1.0, 0.0).astype(BF16)


def _split128(v):
    hi = jnp.floor(v * (1.0 / LANES))
    return hi, v - hi * LANES


def _routing_kernel(aff_ref, idx_ref, pos_ref, rank_ref, gt_scr, eq_scr, need_scr, *, cap):
    ne, nj, ni = aff_ref.shape
    bits = lax.bitcast_convert_type(aff_ref[...], I32)

    def bisect(it, thr):
        cand = thr | jnp.left_shift(jnp.int32(1), 30 - it)
        cnt = jnp.sum(jnp.where(bits >= cand, 1.0, 0.0), axis=(1, 2), keepdims=True)
        return jnp.where(cnt >= cap, cand, thr)

    thr = lax.fori_loop(0, 31, bisect, jnp.zeros((ne, 1, 1), I32))
    gt = jnp.where(bits > thr, 1.0, 0.0)
    gt_scr[...] = gt
    eq_scr[...] = jnp.where(bits == thr, 1.0, 0.0)
    need = cap - jnp.sum(gt, axis=(1, 2), keepdims=True)
    need_scr[...] = jnp.broadcast_to(need, (ne, SUBLANES, ni))

    ones_sq = jnp.ones((ni, ni), BF16)
    ones_row = jnp.ones((SUBLANES, ni), BF16)
    ones_rowj = jnp.ones((SUBLANES, nj), BF16)
    u_incl = _tri(ni, "row_le_col")
    uj_strict = _tri(nj, "row_lt_col")
    lj_strict = _tri(nj, "col_lt_row")
    l_incl = _tri(ni, "col_le_row")
    bdot = lambda a, b: jnp.dot(a, b, preferred_element_type=F32)

    c_lane = lax.broadcasted_iota(I32, (nj, cap), 1).astype(F32)
    c_row = lax.broadcasted_iota(I32, (SUBLANES, cap), 1).astype(F32)
    j_sub = lax.broadcasted_iota(I32, (nj, cap), 0).astype(F32)

    def slots(e, carry):
        eq = eq_scr[e]
        eqb = eq.astype(BF16)
        tie_rank = bdot(lj_strict, bdot(eqb, ones_sq).astype(BF16)) + bdot(eqb, u_incl)
        take = jnp.where(tie_rank <= need_scr[e][0:1, :], eq, 0.0)
        sel = jnp.maximum(gt_scr[e], take)
        selb = sel.astype(BF16)
        lc = bdot(selb, u_incl)
        lc_t = _nt_dot(l_incl, selb)
        n_rep = bdot(selb, ones_sq)
        s_rep = bdot(lj_strict, n_rep.astype(BF16)) + n_rep
        n_lane = _nt_dot(ones_row, selb)
        pex_lane = bdot(n_lane.astype(BF16), uj_strict)
        s_wide = jnp.concatenate([s_rep] * (cap // ni), axis=1) if cap > ni else s_rep[:, :cap]
        jc = bdot(ones_rowj, jnp.where(s_wide <= c_lane, 1.0, 0.0).astype(BF16))
        onehot = jnp.where(j_sub == jnp.broadcast_to(jc[0:1], (nj, cap)), 1.0, 0.0).astype(BF16)
        phi, plo = _split128(pex_lane)
        r = c_row - (bdot(phi.astype(BF16), onehot) * LANES + bdot(plo.astype(BF16), onehot))
        lcs = bdot(lc_t.astype(BF16), onehot)
        ic = bdot(ones_row, jnp.where(lcs <= jnp.broadcast_to(r[0:1], (ni, cap)), 1.0, 0.0).astype(BF16))
        idx_ref[e] = (jc * ni + ic).astype(I32)
        pos_ref[e] = pex_lane.astype(I32)
        rank_ref[e] = jnp.where(sel > 0.0, lc - 1.0, -1.0)
        return carry

    lax.fori_loop(0, ne, slots, 0)


def _routing(aff, *, cap):
    b, ne, s = aff.shape
    nj = s // LANES
    aff4 = aff.reshape(b, ne, nj, LANES)
    per_b = lambda bi: (bi, 0, 0, 0)
    idx, pos, rank = pl.pallas_call(
        functools.partial(_routing_kernel, cap=cap),
        grid=(b,),
        in_specs=[pl.BlockSpec((None, ne, nj, LANES), per_b)],
        out_specs=[pl.BlockSpec((None, ne, SUBLANES, cap), per_b),
                   pl.BlockSpec((None, ne, SUBLANES, nj), per_b),
                   pl.BlockSpec((None, ne, nj, LANES), per_b)],
        out_shape=[jax.ShapeDtypeStruct((b, ne, SUBLANES, cap), I32),
                   jax.ShapeDtypeStruct((b, ne, SUBLANES, nj), I32),
                   jax.ShapeDtypeStruct((b, ne, nj, LANES), F32)],
        scratch_shapes=[pltpu.VMEM((ne, nj, LANES), F32)] * 2 + [pltpu.VMEM((ne, SUBLANES, LANES), F32)],
        compiler_params=_cparams(("parallel",)),
        name="expert_choice_routing",
    )(aff4)
    return idx[:, :, 0, :], pos[:, :, 0, :], rank


def _ffn_kernel(idx_cur, idx_nxt, h_hbm, wg_ref, wu_ref, wd_ref, y_ref, xs, xsb, gsem,
                *, cap, n_pairs, n_ftiles, per_step, m_chunk):
    p = pl.program_id(0)
    j = pl.program_id(1)
    n_slots = n_ftiles * per_step
    dt = xs.shape[0] // n_slots
    last_pair = n_pairs - 1
    b_cur = p // N_EXPERTS
    b_nxt = jnp.minimum(p + 1, last_pair) // N_EXPERTS

    def gather(ref, bq, c):
        cc = jnp.minimum(c, cap - 1)
        tok = ref[cc >> LANE_BITS, cc & (LANES - 1)]
        dst = xs.at[pl.ds(pl.multiple_of(c * dt, dt), dt)]
        return pltpu.make_async_copy(h_hbm.at[bq, tok], dst, gsem)

    def wait_gathers():
        pltpu.make_async_copy(xs, xs, gsem).wait()

    @pl.when((p == 0) & (j == 0))
    def _():
        def start(c, carry):
            gather(idx_cur, b_cur, c).start()
            return carry

        lax.fori_loop(0, n_slots, start, 0)

    @pl.when(j == 0)
    def _():
        wait_gathers()
        for c in range(dt):
            xsb[:, c * LANES:(c + 1) * LANES] = xs[pl.ds(c, cap, stride=dt), :].astype(BF16)
        y_ref[...] = jnp.zeros_like(y_ref)

    wg = wg_ref[...].astype(BF16)
    wu = wu_ref[...].astype(BF16)
    wd = wd_ref[...].astype(BF16)
    n_groups = cap // m_chunk
    per_group = per_step // n_groups
    for m in range(n_groups):
        for t in range(per_group):
            gather(idx_nxt, b_nxt, j * per_step + (m * per_group + t)).start()
        rows = pl.ds(m * m_chunk, m_chunk)
        x = xsb[rows, :]
        hg = jnp.dot(x, wg, preferred_element_type=F32)
        hu = jnp.dot(x, wu, preferred_element_type=F32)
        act = (hg * _sigmoid(hg)) * hu
        y_ref[rows, :] += jnp.dot(act.astype(BF16), wd, preferred_element_type=F32)

    @pl.when((p == last_pair) & (j == n_ftiles - 1))
    def _():
        wait_gathers()


def _expert_ffn(h, idx, wg, wu, wd, layer, *, cap, tf, m_chunk):
    b, s, dt, _ = h.shape
    d = dt * LANES
    ne, dff = wg.shape[1], wg.shape[3]
    n_ftiles = dff // tf
    n_pairs = b * ne
    n_groups = cap // m_chunk
    per_step = -(-cap // (n_ftiles * n_groups)) * n_groups
    n_slots = n_ftiles * per_step
    tab = idx.reshape(n_pairs, cap // LANES, LANES)
    tab_spec = lambda f: pl.BlockSpec((None, cap // LANES, LANES), lambda p, j: (f(p), 0, 0),
                                      memory_space=pltpu.SMEM)
    return pl.pallas_call(
        functools.partial(_ffn_kernel, cap=cap, n_pairs=n_pairs, n_ftiles=n_ftiles, per_step=per_step,
                          m_chunk=m_chunk),
        grid=(n_pairs, n_ftiles),
        in_specs=[tab_spec(lambda p: p),
                  tab_spec(lambda p: jnp.minimum(p + 1, n_pairs - 1)),
                  pl.BlockSpec(memory_space=pl.ANY),
                  pl.BlockSpec((None, None, d, tf), lambda p, j: (layer, p % ne, 0, j)),
                  pl.BlockSpec((None, None, d, tf), lambda p, j: (layer, p % ne, 0, j)),
                  pl.BlockSpec((None, None, tf, d), lambda p, j: (layer, p % ne, j, 0))],
        out_specs=pl.BlockSpec((None, None, cap, d), lambda p, j: (p // ne, p % ne, 0, 0)),
        out_shape=jax.ShapeDtypeStruct((b, ne, cap, d), F32),
        scratch_shapes=[pltpu.VMEM((n_slots * dt, LANES), F32), pltpu.VMEM((cap, d), BF16),
                        pltpu.SemaphoreType.DMA(())],
        compiler_params=_cparams(("arbitrary", "arbitrary")),
        name="expert_swiglu",
    )(tab, tab, h, wg, wu, wd)


def _combine_kernel(pos_ref, rank_ref, aff_ref, x_ref, g_ref, y_hbm, o_ref, ybuf, acc, sem, cnt,
                    *, cap, n_tiles, final_norm):
    b = pl.program_id(0)
    j = pl.program_id(1)
    nj = pl.num_programs(1)
    t = b * nj + j
    ne = N_EXPERTS

    def nominal(bq, e, jq, r):
        return ((pos_ref[bq * ne + e, jq] >> 3) << 3) + r * EWIN

    def window(bq, e, jq, r):
        return pl.multiple_of(jnp.minimum(nominal(bq, e, jq, r), cap - EWIN), SUBLANES)

    def copies(bq, jq, r, slot):
        return [pltpu.make_async_copy(y_hbm.at[bq, e, pl.ds(window(bq, e, jq, r), EWIN)],
                                      ybuf.at[slot, pl.ds(e * EWIN, EWIN)], sem.at[slot]) for e in range(ne)]

    def start_fetch(bq, jq, r, slot):
        for cp in copies(bq, jq, r, slot):
            cp.start()

    @pl.when(t == 0)
    def _():
        cnt[0] = 0
        start_fetch(b, j, 0, 0)

    span = jnp.int32(1)
    for e in range(ne):
        span = jnp.maximum(span, pos_ref[b * ne + e, j + 1] - nominal(b, e, j, 0))
    n_rounds = (span + EWIN - 1) // EWIN
    first = cnt[0]
    rk = rank_ref[...]
    af = aff_ref[...]
    row = lax.broadcasted_iota(I32, (EWIN, LANES), 0).astype(F32)
    acc[...] = jnp.zeros_like(acc)

    def one_round(r, carry):
        slot = (first + r) & 1
        pltpu.make_async_copy(ybuf.at[slot], ybuf.at[slot], sem.at[slot]).wait()

        @pl.when(r + 1 < n_rounds)
        def _():
            start_fetch(b, j, r + 1, 1 - slot)

        @pl.when((r + 1 == n_rounds) & (t + 1 < n_tiles))
        def _():
            tn = t + 1
            bn = tn // nj
            start_fetch(bn, tn - bn * nj, 0, 1 - slot)

        parts = []
        for e in range(ne):
            p0 = pos_ref[b * ne + e, j]
            local = row + (window(b, e, j, r) - p0).astype(F32)
            want = jnp.where(local >= jnp.maximum(nominal(b, e, j, r) - p0, 0).astype(F32), local, -2.0)
            parts.append(jnp.where(rk[e:e + 1, :] == want, af[e:e + 1, :], 0.0))
        gate_t = jnp.concatenate(parts, axis=0)
        onehot = jnp.where(gate_t != 0.0, 1.0, 0.0).T.astype(BF16)
        scaled = ybuf[slot] * jnp.sum(gate_t, axis=1, keepdims=True)
        hi = scaled.astype(BF16)
        lo = (scaled - hi.astype(F32)).astype(BF16)
        acc[...] += (jnp.dot(onehot, hi, preferred_element_type=F32)
                     + jnp.dot(onehot, lo, preferred_element_type=F32))
        return carry

    lax.fori_loop(0, n_rounds, one_round, 0)
    cnt[0] = first + n_rounds
    y = x_ref[...] + acc[...]
    if final_norm:
        ms = jnp.mean(y * y, axis=-1, keepdims=True)
        y = (y * lax.rsqrt(ms + EPS)) * g_ref[...]
    o_ref[...] = y


def _combine(x, y, pos, rank, aff, g_final, *, cap, final_norm):
    b, s, d = x.shape
    nj = s // LANES
    ne = N_EXPERTS
    pos_tab = jnp.concatenate([pos, jnp.full((b, ne, 1), cap, I32)], axis=2).reshape(b * ne, nj + 1)
    rank_t = jnp.transpose(rank, (0, 2, 1, 3))
    return pl.pallas_call(
        functools.partial(_combine_kernel, cap=cap, n_tiles=b * nj, final_norm=final_norm),
        grid_spec=pltpu.PrefetchScalarGridSpec(
            num_scalar_prefetch=1,
            grid=(b, nj),
            in_specs=[pl.BlockSpec((None, None, ne, LANES), lambda bi, j, tab: (bi, j, 0, 0)),
                      pl.BlockSpec((None, ne, LANES), lambda bi, j, tab: (bi, 0, j)),
                      pl.BlockSpec((None, LANES, d), lambda bi, j, tab: (bi, j, 0)),
                      pl.BlockSpec((1, d), lambda bi, j, tab: (0, 0)),
                      pl.BlockSpec(memory_space=pl.ANY)],
            out_specs=pl.BlockSpec((None, LANES, d), lambda bi, j, tab: (bi, j, 0)),
            scratch_shapes=[pltpu.VMEM((2, ne * EWIN, d), F32), pltpu.VMEM((LANES, d), F32),
                            pltpu.SemaphoreType.DMA((2,)), pltpu.SMEM((1,), I32)]),
        out_shape=jax.ShapeDtypeStruct((b, s, d), F32),
        compiler_params=_cparams(("arbitrary", "arbitrary")),
        name="moe_combine",
    )(pos_tab, rank_t, aff, x, g_final.reshape(1, d), y)


def kernel(x, rel_bias, g_mix, w_in, attn_sink, w_attn_proj, w_fourier_proj, w_out, g_ffn, w_router,
           w_exp_gate, w_exp_up, w_exp_down, g_final):
    b, s, d = x.shape
    depth = g_mix.shape[0]
    cap = CAPACITY_FACTOR * s // N_EXPERTS
    bias = _bias_table(rel_bias)
    tables = _fourier_tables(int(round(np.sqrt(s))))
    for l in range(depth):
        q, k, v, f, ga, gb = _inproj(x.reshape(b * s, d), g_mix[l], w_in[l].astype(BF16), tm=256)
        shp = lambda a: a.reshape(b, s, a.shape[-1])
        attn = _attention(shp(q), shp(k), shp(v), attn_sink[l], bias, blocks_per_step=4)
        four = _fourier(shp(f), tables, rows=8)
        xn, h, aff = _outproj(attn, four, shp(ga), shp(gb), x,
                              w_attn_proj[l].astype(BF16), w_fourier_proj[l].astype(BF16),
                              w_out[l].astype(BF16), g_ffn[l], w_router[l].T, tm=512, chunk=256)
        idx, pos, rank = _routing(aff, cap=cap)
        y = _expert_ffn(h, idx, w_exp_gate, w_exp_up, w_exp_down, l, cap=cap, tf=256, m_chunk=min(cap, 512))
        x = _combine(xn, y, pos, rank, aff, g_final, cap=cap, final_norm=(l == depth - 1))
    return x
```

```python
import functools

import numpy as np
import jax
import jax.numpy as jnp
from jax import lax
from jax.experimental import pallas as pl
from jax.experimental.pallas import tpu as pltpu

F32 = jnp.float32
BF16 = jnp.bfloat16
I32 = jnp.int32
HIGHEST = lax.Precision.HIGHEST

N_HEADS = 8
N_KV_HEADS = 2
HEAD_DIM = 64
ATTN_WIDTH = N_HEADS * HEAD_DIM
KV_WIDTH = N_KV_HEADS * HEAD_DIM
WINDOW = 128
BLOCK = 128
N_BUCKETS = 32
MAX_DISTANCE = 128
N_FOURIER_GROUPS = 4
FOURIER_GROUP_DIM = 128
FOURIER_WIDTH = N_FOURIER_GROUPS * FOURIER_GROUP_DIM
N_EXPERTS = 16
CAPACITY_FACTOR = 2
EPS = 1e-6
NEG_INF = -1e30

LANES = 128
LANE_BITS = 7
SUBLANES = 8
VMEM_LIMIT = 56 * 1024 * 1024
EWIN = 32


def _cparams(sem):
    return pltpu.CompilerParams(dimension_semantics=sem, vmem_limit_bytes=VMEM_LIMIT)


def _nt_dot(a, b, **kw):
    return lax.dot_general(a, b, (((1,), (1,)), ((), ())), preferred_element_type=F32, **kw)


def _sigmoid(x):
    return 1.0 / (1.0 + jnp.exp(-x))


def _t5_bucket(rel):
    half = N_BUCKETS // 2
    max_exact = half // 2
    ret = (rel > 0).astype(jnp.int32) * half
    n = jnp.abs(rel)
    nf = jnp.maximum(n, 1).astype(jnp.float32)
    large = max_exact + (jnp.log(nf / max_exact) / np.float32(np.log(MAX_DISTANCE / max_exact))
                         * (half - max_exact)).astype(jnp.int32)
    large = jnp.minimum(large, half - 1)
    return ret + jnp.where(n < max_exact, n, large)


def _bias_kernel(relb_ref, bucket_ref, o_ref):
    bk = bucket_ref[...]
    col = lax.broadcasted_iota(I32, bk.shape, 1)
    for h in range(N_HEADS):
        acc = jnp.full(bk.shape, NEG_INF, F32)
        for b in range(N_BUCKETS):
            acc = jnp.where(bk == b, relb_ref[b, h], acc)
        o_ref[0, h] = jnp.where(col >= BLOCK, acc, NEG_INF)
        o_ref[1, h] = acc
        o_ref[2, h] = jnp.where(col < 2 * BLOCK, acc, NEG_INF)


def _bias_table(rel_bias):
    q_loc = jnp.arange(BLOCK)
    k_loc = jnp.arange(3 * BLOCK) - BLOCK
    rel = k_loc[None, :] - q_loc[:, None]
    bucket = jnp.where(jnp.abs(rel) <= WINDOW, _t5_bucket(rel), -1).astype(I32)
    table = pl.pallas_call(
        _bias_kernel,
        out_shape=jax.ShapeDtypeStruct((3, N_HEADS, BLOCK, 3 * BLOCK), F32),
        in_specs=[pl.BlockSpec(memory_space=pltpu.SMEM),
                  pl.BlockSpec(memory_space=pltpu.VMEM)],
        out_specs=pl.BlockSpec(memory_space=pltpu.VMEM),
        name="rel_bias_table",
    )(rel_bias.astype(F32), bucket)
    return table.reshape(3, N_HEADS * BLOCK, 3 * BLOCK)


def _inproj_kernel(x_ref, g_ref, w_ref, q_ref, k_ref, v_ref, f_ref, ga_ref, gb_ref, *, d_model):
    x = x_ref[...]
    ms = jnp.mean(x * x, axis=-1, keepdims=True)
    h = (x * lax.rsqrt(ms + EPS)) * g_ref[...]
    z = jnp.dot(h.astype(BF16), w_ref[...], preferred_element_type=F32)
    o = 0
    q_ref[...] = (z[:, o:o + ATTN_WIDTH] * (HEAD_DIM ** -0.5)).astype(BF16)
    o += ATTN_WIDTH
    k_ref[...] = z[:, o:o + KV_WIDTH].astype(BF16)
    o += KV_WIDTH
    v_ref[...] = z[:, o:o + KV_WIDTH].astype(BF16)
    o += KV_WIDTH
    f_ref[...] = z[:, o:o + FOURIER_WIDTH]
    o += FOURIER_WIDTH
    ga_ref[...] = z[:, o:o + d_model]
    o += d_model
    gb_ref[...] = z[:, o:o + d_model]


def _inproj(x2, g, w_bf16, *, tm):
    n, d = x2.shape
    in_w = w_bf16.shape[1]
    widths = (ATTN_WIDTH, KV_WIDTH, KV_WIDTH, FOURIER_WIDTH, d, d)
    dtypes = (BF16, BF16, BF16, F32, F32, F32)
    row = lambda i: (i, 0)
    return pl.pallas_call(
        functools.partial(_inproj_kernel, d_model=d),
        grid=(n // tm,),
        in_specs=[pl.BlockSpec((tm, d), row),
                  pl.BlockSpec((1, d), lambda i: (0, 0)),
                  pl.BlockSpec((d, in_w), lambda i: (0, 0))],
        out_specs=[pl.BlockSpec((tm, w), row) for w in widths],
        out_shape=[jax.ShapeDtypeStruct((n, w), dt) for w, dt in zip(widths, dtypes)],
        compiler_params=_cparams(("parallel",)),
        name="inproj",
    )(x2, g.reshape(1, d), w_bf16)


def _attn_kernel(sink_ref, q_ref, kp_ref, ko_ref, kn_ref, vp_ref, vo_ref, vn_ref, bias_ref, o_ref,
                 *, n_blocks, blocks_per_step):
    i = pl.program_id(1)
    kcat = jnp.concatenate([kp_ref[...], ko_ref[...], kn_ref[...]], axis=0)
    vcat = jnp.concatenate([vp_ref[...], vo_ref[...], vn_ref[...]], axis=0)
    group = N_HEADS // N_KV_HEADS
    head = lambda a, h: a[:, h * HEAD_DIM:(h + 1) * HEAD_DIM]
    sink = jnp.concatenate([jnp.full((BLOCK, 1), sink_ref[h], F32) for h in range(N_HEADS)], axis=0)
    ones = jnp.ones((3 * BLOCK, LANES), BF16)
    for t in range(blocks_per_step):
        blk = i * blocks_per_step + t
        kind = jnp.where(blk == 0, 0, jnp.where(blk == n_blocks - 1, 2, 1))
        kw = kcat[t * BLOCK:(t + 3) * BLOCK]
        vw = vcat[t * BLOCK:(t + 3) * BLOCK]
        qb = q_ref[t * BLOCK:(t + 1) * BLOCK, :]
        s = jnp.concatenate([_nt_dot(head(qb, h), head(kw, h // group)) for h in range(N_HEADS)], axis=0)
        s = s + bias_ref[kind]
        m = jnp.maximum(jnp.max(s, axis=-1, keepdims=True), sink)
        p = jnp.exp(s - m).astype(BF16)
        denom = jnp.dot(p, ones, preferred_element_type=F32) + jnp.exp(sink - m)
        inv = 1.0 / denom
        for h in range(N_HEADS):
            rows = slice(h * BLOCK, (h + 1) * BLOCK)
            out = jnp.dot(p[rows], head(vw, h // group), preferred_element_type=F32) * inv[rows, :HEAD_DIM]
            o_ref[t * BLOCK:(t + 1) * BLOCK, h * HEAD_DIM:(h + 1) * HEAD_DIM] = out.astype(o_ref.dtype)


def _attention(q, k, v, sink, bias, *, blocks_per_step):
    b, s, _ = q.shape
    n_blocks = s // BLOCK
    tq = blocks_per_step * BLOCK
    prev = lambda bi, i: (bi, jnp.maximum(i * blocks_per_step - 1, 0), 0)
    own = lambda bi, i: (bi, i, 0)
    nxt = lambda bi, i: (bi, jnp.minimum((i + 1) * blocks_per_step, n_blocks - 1), 0)
    kv_specs = [pl.BlockSpec((None, BLOCK, KV_WIDTH), prev),
                pl.BlockSpec((None, tq, KV_WIDTH), own),
                pl.BlockSpec((None, BLOCK, KV_WIDTH), nxt)]
    return pl.pallas_call(
        functools.partial(_attn_kernel, n_blocks=n_blocks, blocks_per_step=blocks_per_step),
        grid=(b, s // tq),
        in_specs=[pl.BlockSpec(memory_space=pltpu.SMEM),
                  pl.BlockSpec((None, tq, ATTN_WIDTH), own)] + kv_specs + kv_specs
                 + [pl.BlockSpec((3, N_HEADS * BLOCK, 3 * BLOCK), lambda bi, i: (0, 0, 0))],
        out_specs=pl.BlockSpec((None, tq, ATTN_WIDTH), own),
        out_shape=jax.ShapeDtypeStruct((b, s, ATTN_WIDTH), BF16),
        compiler_params=_cparams(("parallel", "parallel")),
        name="window_attention",
    )(sink.astype(F32), q, k, k, k, v, v, v, bias)


def _hi_lo(x):
    hi = x.astype(BF16)
    return hi, (x - hi.astype(F32)).astype(BF16)


def _dft1_kernel(x_ref, a_ref, tr_ref, ti_ref, x2, tr2, ti2, *, rows):
    a = a_ref[...]
    na = a.shape[0] // 2
    n_slabs = x2.shape[0]
    lanes = lambda c: slice(c * LANES, (c + 1) * LANES)
    for c in range(n_slabs):
        x2[c] = x_ref[:, :, lanes(c)].reshape(x2.shape[1:])
    for jj in range(rows):
        strided = pl.ds(jj, na, stride=rows)
        hi, lo = _hi_lo(jnp.concatenate([x2[c, strided, :] for c in range(n_slabs)], axis=1))
        t = jnp.dot(a, jnp.concatenate([hi, lo, hi], axis=0), preferred_element_type=F32)
        for c in range(n_slabs):
            tr2[c, strided, :] = t[:na, lanes(c)]
            ti2[c, strided, :] = t[na:, lanes(c)]
    for c in range(n_slabs):
        tr_ref[:, :, lanes(c)] = tr2[c].reshape(na, rows, LANES)
        ti_ref[:, :, lanes(c)] = ti2[c].reshape(na, rows, LANES)


def _dft2_kernel(tr_ref, ti_ref, gr_ref, gi_ref, cs_ref, o_ref, o2, *, rows):
    cs = cs_ref[...]
    na = o_ref.shape[0]
    for aa in range(rows):
        hi, lo = _hi_lo(jnp.concatenate([tr_ref[aa], ti_ref[aa]], axis=0))
        t = jnp.concatenate([hi, lo, hi], axis=0)
        zr = jnp.dot(gr_ref[aa], t, preferred_element_type=F32)
        zi = jnp.dot(gi_ref[aa], t, preferred_element_type=F32)
        for g in range(N_FOURIER_GROUPS):
            sl = slice(g * FOURIER_GROUP_DIM, (g + 1) * FOURIER_GROUP_DIM)
            hi, lo = _hi_lo(jnp.concatenate([zr[:, sl], zi[:, sl]], axis=1))
            y = jnp.dot(jnp.concatenate([hi, lo, hi], axis=1), cs, preferred_element_type=F32)
            o2[g, pl.ds(aa, na, stride=rows), :] = y
    for g in range(N_FOURIER_GROUPS):
        sl = slice(g * FOURIER_GROUP_DIM, (g + 1) * FOURIER_GROUP_DIM)
        o_ref[:, :, sl] = o2[g].reshape(na, rows, FOURIER_GROUP_DIM)


def _fourier_tables(na):
    def pieces(a, axis):
        hi = a.astype(BF16)
        lo = (a - hi.astype(np.float64)).astype(BF16)
        return jnp.asarray(np.concatenate([hi, hi, lo], axis=axis))

    n = na * na
    k = np.arange(na, dtype=np.float64)
    ang = 2.0 * np.pi * np.outer(k, k) / na
    a1 = pieces(np.concatenate([np.cos(ang), -np.sin(ang)], axis=0), 1)
    ka = np.arange(na, dtype=np.float64)[:, None, None]
    kb = np.arange(na, dtype=np.float64)[None, :, None]
    nl = np.arange(na, dtype=np.float64)[None, None, :]
    th = 2.0 * np.pi * ((nl * (ka + na * kb)) % n) / n
    gr, gi = np.cos(th), -np.sin(th)
    g_re = pieces(np.concatenate([gr, -gi], axis=2), 2)
    g_im = pieces(np.concatenate([gi, gr], axis=2), 2)
    kc = np.arange(FOURIER_GROUP_DIM, dtype=np.float64)
    angc = 2.0 * np.pi * np.outer(kc, kc) / FOURIER_GROUP_DIM
    scale = 1.0 / np.sqrt(float(n) * FOURIER_GROUP_DIM)
    cs = pieces(np.concatenate([np.cos(angc), np.sin(angc)], axis=0) * scale, 0)
    return a1, g_re, g_im, cs


def _fourier(f, tables, *, rows):
    b, s, w = f.shape
    na = int(round(np.sqrt(s)))
    assert na * na == s and na % rows == 0
    a1, g_re, g_im, cs = tables
    x4 = f.reshape(b, na, na, w)
    const2 = lambda bi, j: (0, 0)
    tr, ti = pl.pallas_call(
        functools.partial(_dft1_kernel, rows=rows),
        grid=(b, na // rows),
        in_specs=[pl.BlockSpec((None, na, rows, w), lambda bi, j: (bi, 0, j, 0)),
                  pl.BlockSpec(a1.shape, const2)],
        out_specs=[pl.BlockSpec((None, na, rows, w), lambda bi, j: (bi, 0, j, 0))] * 2,
        out_shape=[jax.ShapeDtypeStruct((b, na, na, w), F32)] * 2,
        scratch_shapes=[pltpu.VMEM((w // LANES, na * rows, LANES), F32)] * 3,
        compiler_params=_cparams(("parallel", "parallel")),
        name="seq_dft_stage1",
    )(x4, a1)
    y = pl.pallas_call(
        functools.partial(_dft2_kernel, rows=rows),
        grid=(b, na // rows),
        in_specs=[pl.BlockSpec((None, rows, na, w), lambda bi, a: (bi, a, 0, 0))] * 2
                 + [pl.BlockSpec((rows,) + g_re.shape[1:], lambda bi, a: (a, 0, 0))] * 2
                 + [pl.BlockSpec(cs.shape, const2)],
        out_specs=pl.BlockSpec((None, na, rows, w), lambda bi, a: (bi, 0, a, 0)),
        out_shape=jax.ShapeDtypeStruct((b, na, na, w), F32),
        scratch_shapes=[pltpu.VMEM((N_FOURIER_GROUPS, na * rows, FOURIER_GROUP_DIM), F32)],
        compiler_params=_cparams(("parallel", "parallel")),
        name="seq_dft_stage2",
    )(tr, ti, g_re, g_im, cs)
    return y.reshape(b, s, w)


def _outproj_kernel(a_ref, f_ref, ga_ref, gb_ref, x_ref, wa_ref, wf_ref, wo_ref, g_ref, wr_ref,
                    xn_ref, h_ref, aff_ref, *, chunk):
    for r in range(x_ref.shape[0] // chunk):
        rs = pl.ds(r * chunk, chunk)
        a = jnp.dot(a_ref[rs, :], wa_ref[...], preferred_element_type=F32)
        fo = jnp.dot(f_ref[rs, :].astype(BF16), wf_ref[...], preferred_element_type=F32)
        merged = _sigmoid(ga_ref[rs, :]) * a + _sigmoid(gb_ref[rs, :]) * fo
        xn = x_ref[rs, :] + jnp.dot(merged.astype(BF16), wo_ref[...], preferred_element_type=F32)
        xn_ref[rs, :] = xn
        ms = jnp.mean(xn * xn, axis=-1, keepdims=True)
        h = (xn * lax.rsqrt(ms + EPS)) * g_ref[...]
        dt = h.shape[1] // LANES
        for c in range(dt):
            h_ref[pl.ds(r * chunk * dt + c, chunk, stride=dt), :] = h[:, c * LANES:(c + 1) * LANES]
        hi, lo = _hi_lo(h)
        logits = _nt_dot(wr_ref[...], jnp.concatenate([hi, lo, hi], axis=1))
        e = jnp.exp(logits - jnp.max(logits, axis=0, keepdims=True))
        aff_ref[:, rs] = e / jnp.sum(e, axis=0, keepdims=True)


def _outproj(attn, four, ga, gb, x, wa, wf, wo, g, wr_t, *, tm, chunk):
    b, s, d = x.shape
    dt = d // LANES
    tok = lambda bi, i: (bi, i, 0)
    const = lambda bi, i: (0, 0)
    wr_hi, wr_lo = _hi_lo(wr_t)
    wr_cat = jnp.concatenate([wr_hi, wr_hi, wr_lo], axis=1)
    xn, h, aff = pl.pallas_call(
        functools.partial(_outproj_kernel, chunk=chunk),
        grid=(b, s // tm),
        in_specs=[pl.BlockSpec((None, tm, ATTN_WIDTH), tok),
                  pl.BlockSpec((None, tm, FOURIER_WIDTH), tok),
                  pl.BlockSpec((None, tm, d), tok),
                  pl.BlockSpec((None, tm, d), tok),
                  pl.BlockSpec((None, tm, d), tok),
                  pl.BlockSpec((ATTN_WIDTH, d), const),
                  pl.BlockSpec((FOURIER_WIDTH, d), const),
                  pl.BlockSpec((d, d), const),
                  pl.BlockSpec((1, d), const),
                  pl.BlockSpec((N_EXPERTS, 3 * d), const)],
        out_specs=[pl.BlockSpec((None, tm, d), tok),
                   pl.BlockSpec((None, tm * dt, LANES), tok),
                   pl.BlockSpec((None, N_EXPERTS, tm), lambda bi, i: (bi, 0, i))],
        out_shape=[jax.ShapeDtypeStruct((b, s, d), F32),
                   jax.ShapeDtypeStruct((b, s * dt, LANES), F32),
                   jax.ShapeDtypeStruct((b, N_EXPERTS, s), F32)],
        compiler_params=_cparams(("parallel", "parallel")),
        name="outproj_router",
    )(attn, four, ga, gb, x, wa, wf, wo, g.reshape(1, d), wr_cat)
    return xn, h.reshape(b, s, dt, LANES), aff


def _tri(n, kind):
    r = lax.broadcasted_iota(I32, (n, n), 0)
    c = lax.broadcasted_iota(I32, (n, n), 1)
    cond = {"row_le_col": r <= c, "row_lt_col": r < c, "col_le_row": c <= r, "col_lt_row": c < r}[kind]
    return jnp.where(cond, 1.0, 0.0).astype(BF16)


def _split128(v):
    hi = jnp.floor(v * (1.0 / LANES))
    return hi, v - hi * LANES


def _routing_kernel(aff_ref, idx_ref, pos_ref, rank_ref, gt_scr, eq_scr, need_scr, *, cap):
    ne, nj, ni = aff_ref.shape
    bits = lax.bitcast_convert_type(aff_ref[...], I32)

    def bisect(it, thr):
        cand = thr | jnp.left_shift(jnp.int32(1), 30 - it)
        cnt = jnp.sum(jnp.where(bits >= cand, 1.0, 0.0), axis=(1, 2), keepdims=True)
        return jnp.where(cnt >= cap, cand, thr)

    thr = lax.fori_loop(0, 31, bisect, jnp.zeros((ne, 1, 1), I32))
    gt = jnp.where(bits > thr, 1.0, 0.0)
    gt_scr[...] = gt
    eq_scr[...] = jnp.where(bits == thr, 1.0, 0.0)
    need = cap - jnp.sum(gt, axis=(1, 2), keepdims=True)
    need_scr[...] = jnp.broadcast_to(need, (ne, SUBLANES, ni))

    ones_sq = jnp.ones((ni, ni), BF16)
    ones_row = jnp.ones((SUBLANES, ni), BF16)
    ones_rowj = jnp.ones((SUBLANES, nj), BF16)
    u_incl = _tri(ni, "row_le_col")
    uj_strict = _tri(nj, "row_lt_col")
    lj_strict = _tri(nj, "col_lt_row")
    l_incl = _tri(ni, "col_le_row")
    bdot = lambda a, b: jnp.dot(a, b, preferred_element_type=F32)

    c_lane = lax.broadcasted_iota(I32, (nj, cap), 1).astype(F32)
    c_row = lax.broadcasted_iota(I32, (SUBLANES, cap), 1).astype(F32)
    j_sub = lax.broadcasted_iota(I32, (nj, cap), 0).astype(F32)

    def slots(e, carry):
        eq = eq_scr[e]
        eqb = eq.astype(BF16)
        tie_rank = bdot(lj_strict, bdot(eqb, ones_sq).astype(BF16)) + bdot(eqb, u_incl)
        take = jnp.where(tie_rank <= need_scr[e][0:1, :], eq, 0.0)
        sel = jnp.maximum(gt_scr[e], take)
        selb = sel.astype(BF16)
        lc = bdot(selb, u_incl)
        lc_t = _nt_dot(l_incl, selb)
        n_rep = bdot(selb, ones_sq)
        s_rep = bdot(lj_strict, n_rep.astype(BF16)) + n_rep
        n_lane = _nt_dot(ones_row, selb)
        pex_lane = bdot(n_lane.astype(BF16), uj_strict)
        s_wide = jnp.concatenate([s_rep] * (cap // ni), axis=1) if cap > ni else s_rep[:, :cap]
        jc = bdot(ones_rowj, jnp.where(s_wide <= c_lane, 1.0, 0.0).astype(BF16))
        onehot = jnp.where(j_sub == jnp.broadcast_to(jc[0:1], (nj, cap)), 1.0, 0.0).astype(BF16)
        phi, plo = _split128(pex_lane)
        r = c_row - (bdot(phi.astype(BF16), onehot) * LANES + bdot(plo.astype(BF16), onehot))
        lcs = bdot(lc_t.astype(BF16), onehot)
        ic = bdot(ones_row, jnp.where(lcs <= jnp.broadcast_to(r[0:1], (ni, cap)), 1.0, 0.0).astype(BF16))
        idx_ref[e] = (jc * ni + ic).astype(I32)
        pos_ref[e] = pex_lane.astype(I32)
        rank_ref[e] = jnp.where(sel > 0.0, lc - 1.0, -1.0)
        return carry

    lax.fori_loop(0, ne, slots, 0)


def _routing(aff, *, cap):
    b, ne, s = aff.shape
    nj = s // LANES
    aff4 = aff.reshape(b, ne, nj, LANES)
    per_b = lambda bi: (bi, 0, 0, 0)
    idx, pos, rank = pl.pallas_call(
        functools.partial(_routing_kernel, cap=cap),
        grid=(b,),
        in_specs=[pl.BlockSpec((None, ne, nj, LANES), per_b)],
        out_specs=[pl.BlockSpec((None, ne, SUBLANES, cap), per_b),
                   pl.BlockSpec((None, ne, SUBLANES, nj), per_b),
                   pl.BlockSpec((None, ne, nj, LANES), per_b)],
        out_shape=[jax.ShapeDtypeStruct((b, ne, SUBLANES, cap), I32),
                   jax.ShapeDtypeStruct((b, ne, SUBLANES, nj), I32),
                   jax.ShapeDtypeStruct((b, ne, nj, LANES), F32)],
        scratch_shapes=[pltpu.VMEM((ne, nj, LANES), F32)] * 2 + [pltpu.VMEM((ne, SUBLANES, LANES), F32)],
        compiler_params=_cparams(("parallel",)),
        name="expert_choice_routing",
    )(aff4)
    return idx[:, :, 0, :], pos[:, :, 0, :], rank


def _ffn_kernel(idx_cur, idx_nxt, h_hbm, wg_ref, wu_ref, wd_ref, y_ref, xs, xsb, gsem,
                *, cap, n_pairs, n_ftiles, per_step, m_chunk):
    p = pl.program_id(0)
    j = pl.program_id(1)
    n_slots = n_ftiles * per_step
    dt = xs.shape[0] // n_slots
    last_pair = n_pairs - 1
    b_cur = p // N_EXPERTS
    b_nxt = jnp.minimum(p + 1, last_pair) // N_EXPERTS

    def gather(ref, bq, c):
        cc = jnp.minimum(c, cap - 1)
        tok = ref[cc >> LANE_BITS, cc & (LANES - 1)]
        dst = xs.at[pl.ds(pl.multiple_of(c * dt, dt), dt)]
        return pltpu.make_async_copy(h_hbm.at[bq, tok], dst, gsem)

    def wait_gathers():
        pltpu.make_async_copy(xs, xs, gsem).wait()

    @pl.when((p == 0) & (j == 0))
    def _():
        def start(c, carry):
            gather(idx_cur, b_cur, c).start()
            return carry

        lax.fori_loop(0, n_slots, start, 0)

    @pl.when(j == 0)
    def _():
        wait_gathers()
        for c in range(dt):
            xsb[:, c * LANES:(c + 1) * LANES] = xs[pl.ds(c, cap, stride=dt), :].astype(BF16)
        y_ref[...] = jnp.zeros_like(y_ref)

    wg = wg_ref[...].astype(BF16)
    wu = wu_ref[...].astype(BF16)
    wd = wd_ref[...].astype(BF16)
    n_groups = cap // m_chunk
    per_group = per_step // n_groups
    for m in range(n_groups):
        for t in range(per_group):
            gather(idx_nxt, b_nxt, j * per_step + (m * per_group + t)).start()
        rows = pl.ds(m * m_chunk, m_chunk)
        x = xsb[rows, :]
        hg = jnp.dot(x, wg, preferred_element_type=F32)
        hu = jnp.dot(x, wu, preferred_element_type=F32)
        act = (hg * _sigmoid(hg)) * hu
        y_ref[rows, :] += jnp.dot(act.astype(BF16), wd, preferred_element_type=F32)

    @pl.when((p == last_pair) & (j == n_ftiles - 1))
    def _():
        wait_gathers()


def _expert_ffn(h, idx, wg, wu, wd, layer, *, cap, tf, m_chunk):
    b, s, dt, _ = h.shape
    d = dt * LANES
    ne, dff = wg.shape[1], wg.shape[3]
    n_ftiles = dff // tf
    n_pairs = b * ne
    n_groups = cap // m_chunk
    per_step = -(-cap // (n_ftiles * n_groups)) * n_groups
    n_slots = n_ftiles * per_step
    tab = idx.reshape(n_pairs, cap // LANES, LANES)
    tab_spec = lambda f: pl.BlockSpec((None, cap // LANES, LANES), lambda p, j: (f(p), 0, 0),
                                      memory_space=pltpu.SMEM)
    return pl.pallas_call(
        functools.partial(_ffn_kernel, cap=cap, n_pairs=n_pairs, n_ftiles=n_ftiles, per_step=per_step,
                          m_chunk=m_chunk),
        grid=(n_pairs, n_ftiles),
        in_specs=[tab_spec(lambda p: p),
                  tab_spec(lambda p: jnp.minimum(p + 1, n_pairs - 1)),
                  pl.BlockSpec(memory_space=pl.ANY),
                  pl.BlockSpec((None, None, d, tf), lambda p, j: (layer, p % ne, 0, j)),
                  pl.BlockSpec((None, None, d, tf), lambda p, j: (layer, p % ne, 0, j)),
                  pl.BlockSpec((None, None, tf, d), lambda p, j: (layer, p % ne, j, 0))],
        out_specs=pl.BlockSpec((None, None, cap, d), lambda p, j: (p // ne, p % ne, 0, 0)),
        out_shape=jax.ShapeDtypeStruct((b, ne, cap, d), F32),
        scratch_shapes=[pltpu.VMEM((n_slots * dt, LANES), F32), pltpu.VMEM((cap, d), BF16),
                        pltpu.SemaphoreType.DMA(())],
        compiler_params=_cparams(("arbitrary", "arbitrary")),
        name="expert_swiglu",
    )(tab, tab, h, wg, wu, wd)


def _combine_kernel(pos_ref, rank_ref, aff_ref, x_ref, g_ref, y_hbm, o_ref, ybuf, acc, sem, cnt,
                    *, cap, n_tiles, final_norm):
    b = pl.program_id(0)
    j = pl.program_id(1)
    nj = pl.num_programs(1)
    t = b * nj + j
    ne = N_EXPERTS

    def nominal(bq, e, jq, r):
        return ((pos_ref[bq * ne + e, jq] >> 3) << 3) + r * EWIN

    def window(bq, e, jq, r):
        return pl.multiple_of(jnp.minimum(nominal(bq, e, jq, r), cap - EWIN), SUBLANES)

    def copies(bq, jq, r, slot):
        return [pltpu.make_async_copy(y_hbm.at[bq, e, pl.ds(window(bq, e, jq, r), EWIN)],
                                      ybuf.at[slot, pl.ds(e * EWIN, EWIN)], sem.at[slot]) for e in range(ne)]

    def start_fetch(bq, jq, r, slot):
        for cp in copies(bq, jq, r, slot):
            cp.start()

    @pl.when(t == 0)
    def _():
        cnt[0] = 0
        start_fetch(b, j, 0, 0)

    span = jnp.int32(1)
    for e in range(ne):
        span = jnp.maximum(span, pos_ref[b * ne + e, j + 1] - nominal(b, e, j, 0))
    n_rounds = (span + EWIN - 1) // EWIN
    first = cnt[0]
    rk = rank_ref[...]
    af = aff_ref[...]
    row = lax.broadcasted_iota(I32, (EWIN, LANES), 0).astype(F32)
    acc[...] = jnp.zeros_like(acc)

    def one_round(r, carry):
        slot = (first + r) & 1
        pltpu.make_async_copy(ybuf.at[slot], ybuf.at[slot], sem.at[slot]).wait()

        @pl.when(r + 1 < n_rounds)
        def _():
            start_fetch(b, j, r + 1, 1 - slot)

        @pl.when((r + 1 == n_rounds) & (t + 1 < n_tiles))
        def _():
            tn = t + 1
            bn = tn // nj
            start_fetch(bn, tn - bn * nj, 0, 1 - slot)

        parts = []
        for e in range(ne):
            p0 = pos_ref[b * ne + e, j]
            local = row + (window(b, e, j, r) - p0).astype(F32)
            want = jnp.where(local >= jnp.maximum(nominal(b, e, j, r) - p0, 0).astype(F32), local, -2.0)
            parts.append(jnp.where(rk[e:e + 1, :] == want, af[e:e + 1, :], 0.0))
        gate_t = jnp.concatenate(parts, axis=0)
        onehot = jnp.where(gate_t != 0.0, 1.0, 0.0).T.astype(BF16)
        scaled = ybuf[slot] * jnp.sum(gate_t, axis=1, keepdims=True)
        hi = scaled.astype(BF16)
        lo = (scaled - hi.astype(F32)).astype(BF16)
        acc[...] += (jnp.dot(onehot, hi, preferred_element_type=F32)
                     + jnp.dot(onehot, lo, preferred_element_type=F32))
        return carry

    lax.fori_loop(0, n_rounds, one_round, 0)
    cnt[0] = first + n_rounds
    y = x_ref[...] + acc[...]
    if final_norm:
        ms = jnp.mean(y * y, axis=-1, keepdims=True)
        y = (y * lax.rsqrt(ms + EPS)) * g_ref[...]
    o_ref[...] = y


def _combine(x, y, pos, rank, aff, g_final, *, cap, final_norm):
    b, s, d = x.shape
    nj = s // LANES
    ne = N_EXPERTS
    pos_tab = jnp.concatenate([pos, jnp.full((b, ne, 1), cap, I32)], axis=2).reshape(b * ne, nj + 1)
    rank_t = jnp.transpose(rank, (0, 2, 1, 3))
    return pl.pallas_call(
        functools.partial(_combine_kernel, cap=cap, n_tiles=b * nj, final_norm=final_norm),
        grid_spec=pltpu.PrefetchScalarGridSpec(
            num_scalar_prefetch=1,
            grid=(b, nj),
            in_specs=[pl.BlockSpec((None, None, ne, LANES), lambda bi, j, tab: (bi, j, 0, 0)),
                      pl.BlockSpec((None, ne, LANES), lambda bi, j, tab: (bi, 0, j)),
                      pl.BlockSpec((None, LANES, d), lambda bi, j, tab: (bi, j, 0)),
                      pl.BlockSpec((1, d), lambda bi, j, tab: (0, 0)),
                      pl.BlockSpec(memory_space=pl.ANY)],
            out_specs=pl.BlockSpec((None, LANES, d), lambda bi, j, tab: (bi, j, 0)),
            scratch_shapes=[pltpu.VMEM((2, ne * EWIN, d), F32), pltpu.VMEM((LANES, d), F32),
                            pltpu.SemaphoreType.DMA((2,)), pltpu.SMEM((1,), I32)]),
        out_shape=jax.ShapeDtypeStruct((b, s, d), F32),
        compiler_params=_cparams(("arbitrary", "arbitrary")),
        name="moe_combine",
    )(pos_tab, rank_t, aff, x, g_final.reshape(1, d), y)


def kernel(x, rel_bias, g_mix, w_in, attn_sink, w_attn_proj, w_fourier_proj, w_out, g_ffn, w_router,
           w_exp_gate, w_exp_up, w_exp_down, g_final):
    b, s, d = x.shape
    depth = g_mix.shape[0]
    cap = CAPACITY_FACTOR * s // N_EXPERTS
    bias = _bias_table(rel_bias)
    tables = _fourier_tables(int(round(np.sqrt(s))))
    for l in range(depth):
        q, k, v, f, ga, gb = _inproj(x.reshape(b * s, d), g_mix[l], w_in[l].astype(BF16), tm=256)
        shp = lambda a: a.reshape(b, s, a.shape[-1])
        attn = _attention(shp(q), shp(k), shp(v), attn_sink[l], bias, blocks_per_step=4)
        four = _fourier(shp(f), tables, rows=8)
        xn, h, aff = _outproj(attn, four, shp(ga), shp(gb), x,
                              w_attn_proj[l].astype(BF16), w_fourier_proj[l].astype(BF16),
                              w_out[l].astype(BF16), g_ffn[l], w_router[l].T, tm=512, chunk=256)
        idx, pos, rank = _routing(aff, cap=cap)
        y = _expert_ffn(h, idx, w_exp_gate, w_exp_up, w_exp_down, l, cap=cap, tf=256, m_chunk=min(cap, 512))
        x = _combine(xn, y, pos, rank, aff, g_final, cap=cap, final_norm=(l == depth - 1))
    return x
```

```python
import functools

import numpy as np
import jax
import jax.numpy as jnp
from jax import lax
from jax.experimental import pallas as pl
from jax.experimental.pallas import tpu as pltpu

F32 = jnp.float32
BF16 = jnp.bfloat16
I32 = jnp.int32
HIGHEST = lax.Precision.HIGHEST

N_HEADS = 8
N_KV_HEADS = 2
HEAD_DIM = 64
ATTN_WIDTH = N_HEADS * HEAD_DIM
KV_WIDTH = N_KV_HEADS * HEAD_DIM
WINDOW = 128
BLOCK = 128
N_BUCKETS = 32
MAX_DISTANCE = 128
N_FOURIER_GROUPS = 4
FOURIER_GROUP_DIM = 128
FOURIER_WIDTH = N_FOURIER_GROUPS * FOURIER_GROUP_DIM
N_EXPERTS = 16
CAPACITY_FACTOR = 2
EPS = 1e-6
NEG_INF = -1e30

LANES = 128
LANE_BITS = 7
SUBLANES = 8
VMEM_LIMIT = 56 * 1024 * 1024
EWIN = 32
DFT_PASSES = 1


def _cparams(sem):
    return pltpu.CompilerParams(dimension_semantics=sem, vmem_limit_bytes=VMEM_LIMIT)


def _nt_dot(a, b, **kw):
    return lax.dot_general(a, b, (((1,), (1,)), ((), ())), preferred_element_type=F32, **kw)


def _sigmoid(x):
    return 1.0 / (1.0 + jnp.exp(-x))


def _t5_bucket(rel):
    half = N_BUCKETS // 2
    max_exact = half // 2
    ret = (rel > 0).astype(jnp.int32) * half
    n = jnp.abs(rel)
    nf = jnp.maximum(n, 1).astype(jnp.float32)
    large = max_exact + (jnp.log(nf / max_exact) / np.float32(np.log(MAX_DISTANCE / max_exact))
                         * (half - max_exact)).astype(jnp.int32)
    large = jnp.minimum(large, half - 1)
    return ret + jnp.where(n < max_exact, n, large)


def _bias_kernel(relb_ref, bucket_ref, o_ref):
    bk = bucket_ref[...]
    col = lax.broadcasted_iota(I32, bk.shape, 1)
    for h in range(N_HEADS):
        acc = jnp.full(bk.shape, NEG_INF, F32)
        for b in range(N_BUCKETS):
            acc = jnp.where(bk == b, relb_ref[b, h], acc)
        o_ref[0, h] = jnp.where(col >= BLOCK, acc, NEG_INF)
        o_ref[1, h] = acc
        o_ref[2, h] = jnp.where(col < 2 * BLOCK, acc, NEG_INF)


def _bias_table(rel_bias):
    q_loc = jnp.arange(BLOCK)
    k_loc = jnp.arange(3 * BLOCK) - BLOCK
    rel = k_loc[None, :] - q_loc[:, None]
    bucket = jnp.where(jnp.abs(rel) <= WINDOW, _t5_bucket(rel), -1).astype(I32)
    table = pl.pallas_call(
        _bias_kernel,
        out_shape=jax.ShapeDtypeStruct((3, N_HEADS, BLOCK, 3 * BLOCK), F32),
        in_specs=[pl.BlockSpec(memory_space=pltpu.SMEM),
                  pl.BlockSpec(memory_space=pltpu.VMEM)],
        out_specs=pl.BlockSpec(memory_space=pltpu.VMEM),
        name="rel_bias_table",
    )(rel_bias.astype(F32), bucket)
    return table.reshape(3, N_HEADS * BLOCK, 3 * BLOCK)


def _inproj_kernel(x_ref, g_ref, w_ref, q_ref, k_ref, v_ref, f_ref, ga_ref, gb_ref, *, d_model):
    x = x_ref[...]
    ms = jnp.mean(x * x, axis=-1, keepdims=True)
    h = (x * lax.rsqrt(ms + EPS)) * g_ref[...]
    z = jnp.dot(h.astype(BF16), w_ref[...], preferred_element_type=F32)
    o = 0
    q_ref[...] = (z[:, o:o + ATTN_WIDTH] * (HEAD_DIM ** -0.5)).astype(BF16)
    o += ATTN_WIDTH
    k_ref[...] = z[:, o:o + KV_WIDTH].astype(BF16)
    o += KV_WIDTH
    v_ref[...] = z[:, o:o + KV_WIDTH].astype(BF16)
    o += KV_WIDTH
    f_ref[...] = z[:, o:o + FOURIER_WIDTH]
    o += FOURIER_WIDTH
    ga_ref[...] = z[:, o:o + d_model]
    o += d_model
    gb_ref[...] = z[:, o:o + d_model]


def _inproj(x2, g, w_bf16, *, tm):
    n, d = x2.shape
    in_w = w_bf16.shape[1]
    widths = (ATTN_WIDTH, KV_WIDTH, KV_WIDTH, FOURIER_WIDTH, d, d)
    dtypes = (BF16, BF16, BF16, F32, F32, F32)
    row = lambda i: (i, 0)
    return pl.pallas_call(
        functools.partial(_inproj_kernel, d_model=d),
        grid=(n // tm,),
        in_specs=[pl.BlockSpec((tm, d), row),
                  pl.BlockSpec((1, d), lambda i: (0, 0)),
                  pl.BlockSpec((d, in_w), lambda i: (0, 0))],
        out_specs=[pl.BlockSpec((tm, w), row) for w in widths],
        out_shape=[jax.ShapeDtypeStruct((n, w), dt) for w, dt in zip(widths, dtypes)],
        compiler_params=_cparams(("parallel",)),
        name="inproj",
    )(x2, g.reshape(1, d), w_bf16)


def _attn_kernel(sink_ref, q_ref, kp_ref, ko_ref, kn_ref, vp_ref, vo_ref, vn_ref, bias_ref, o_ref,
                 *, n_blocks, blocks_per_step):
    i = pl.program_id(1)
    kcat = jnp.concatenate([kp_ref[...], ko_ref[...], kn_ref[...]], axis=0)
    vcat = jnp.concatenate([vp_ref[...], vo_ref[...], vn_ref[...]], axis=0)
    group = N_HEADS // N_KV_HEADS
    head = lambda a, h: a[:, h * HEAD_DIM:(h + 1) * HEAD_DIM]
    sink = jnp.concatenate([jnp.full((BLOCK, LANES), sink_ref[h], F32) for h in range(N_HEADS)], axis=0)
    ones = jnp.ones((3 * BLOCK, LANES), BF16)
    for t in range(blocks_per_step):
        blk = i * blocks_per_step + t
        kind = jnp.where(blk == 0, 0, jnp.where(blk == n_blocks - 1, 2, 1))
        kw = kcat[t * BLOCK:(t + 3) * BLOCK]
        vw = vcat[t * BLOCK:(t + 3) * BLOCK]
        qb = q_ref[t * BLOCK:(t + 1) * BLOCK, :]
        s = jnp.concatenate([_nt_dot(head(qb, h), head(kw, h // group)) for h in range(N_HEADS)], axis=0)
        s = s + bias_ref[kind]
        m = jnp.maximum(jnp.broadcast_to(jnp.max(s, axis=-1, keepdims=True), sink.shape), sink)
        p = jnp.exp(s - jnp.concatenate([m] * 3, axis=1)).astype(BF16)
        denom = jnp.dot(p, ones, preferred_element_type=F32) + jnp.exp(sink - m)
        inv = 1.0 / denom
        for h in range(N_HEADS):
            rows = slice(h * BLOCK, (h + 1) * BLOCK)
            out = jnp.dot(p[rows], head(vw, h // group), preferred_element_type=F32) * inv[rows, :HEAD_DIM]
            o_ref[t * BLOCK:(t + 1) * BLOCK, h * HEAD_DIM:(h + 1) * HEAD_DIM] = out.astype(o_ref.dtype)


def _attention(q, k, v, sink, bias, *, blocks_per_step):
    b, s, _ = q.shape
    n_blocks = s // BLOCK
    tq = blocks_per_step * BLOCK
    prev = lambda bi, i: (bi, jnp.maximum(i * blocks_per_step - 1, 0), 0)
    own = lambda bi, i: (bi, i, 0)
    nxt = lambda bi, i: (bi, jnp.minimum((i + 1) * blocks_per_step, n_blocks - 1), 0)
    kv_specs = [pl.BlockSpec((None, BLOCK, KV_WIDTH), prev),
                pl.BlockSpec((None, tq, KV_WIDTH), own),
                pl.BlockSpec((None, BLOCK, KV_WIDTH), nxt)]
    return pl.pallas_call(
        functools.partial(_attn_kernel, n_blocks=n_blocks, blocks_per_step=blocks_per_step),
        grid=(b, s // tq),
        in_specs=[pl.BlockSpec(memory_space=pltpu.SMEM),
                  pl.BlockSpec((None, tq, ATTN_WIDTH), own)] + kv_specs + kv_specs
                 + [pl.BlockSpec((3, N_HEADS * BLOCK, 3 * BLOCK), lambda bi, i: (0, 0, 0))],
        out_specs=pl.BlockSpec((None, tq, ATTN_WIDTH), own),
        out_shape=jax.ShapeDtypeStruct((b, s, ATTN_WIDTH), BF16),
        compiler_params=_cparams(("parallel", "parallel")),
        name="window_attention",
    )(sink.astype(F32), q, k, k, k, v, v, v, bias)


def _hi_lo(x):
    hi = x.astype(BF16)
    return hi, (x - hi.astype(F32)).astype(BF16)


def _dft_operand(x, axis):
    if DFT_PASSES == 1:
        return x.astype(BF16)
    hi, lo = _hi_lo(x)
    return jnp.concatenate([hi, lo, hi], axis=axis)


def _dft1_kernel(x_ref, a_ref, tr_ref, ti_ref, x2, tr2, ti2, *, rows):
    a = a_ref[...]
    na = a.shape[0] // 2
    n_slabs = x2.shape[0]
    lanes = lambda c: slice(c * LANES, (c + 1) * LANES)
    for c in range(n_slabs):
        x2[c] = x_ref[:, :, lanes(c)].reshape(x2.shape[1:])
    for jj in range(rows):
        strided = pl.ds(jj, na, stride=rows)
        xj = jnp.concatenate([x2[c, strided, :] for c in range(n_slabs)], axis=1)
        t = jnp.dot(a, _dft_operand(xj, 0), preferred_element_type=F32)
        for c in range(n_slabs):
            tr2[c, strided, :] = t[:na, lanes(c)]
            ti2[c, strided, :] = t[na:, lanes(c)]
    for c in range(n_slabs):
        tr_ref[:, :, lanes(c)] = tr2[c].reshape(na, rows, LANES)
        ti_ref[:, :, lanes(c)] = ti2[c].reshape(na, rows, LANES)


def _dft2_kernel(tr_ref, ti_ref, gr_ref, gi_ref, cs_ref, o_ref, o2, *, rows):
    cs = cs_ref[...]
    na = o_ref.shape[0]
    for aa in range(rows):
        t = _dft_operand(jnp.concatenate([tr_ref[aa], ti_ref[aa]], axis=0), 0)
        zr = jnp.dot(gr_ref[aa], t, preferred_element_type=F32)
        zi = jnp.dot(gi_ref[aa], t, preferred_element_type=F32)
        for g in range(N_FOURIER_GROUPS):
            sl = slice(g * FOURIER_GROUP_DIM, (g + 1) * FOURIER_GROUP_DIM)
            z = _dft_operand(jnp.concatenate([zr[:, sl], zi[:, sl]], axis=1), 1)
            y = jnp.dot(z, cs, preferred_element_type=F32)
            o2[g, pl.ds(aa, na, stride=rows), :] = y
    for g in range(N_FOURIER_GROUPS):
        sl = slice(g * FOURIER_GROUP_DIM, (g + 1) * FOURIER_GROUP_DIM)
        o_ref[:, :, sl] = o2[g].reshape(na, rows, FOURIER_GROUP_DIM)


def _fourier_tables(na):
    def pieces(a, axis):
        hi = a.astype(BF16)
        if DFT_PASSES == 1:
            return jnp.asarray(hi)
        lo = (a - hi.astype(np.float64)).astype(BF16)
        return jnp.asarray(np.concatenate([hi, hi, lo], axis=axis))

    n = na * na
    k = np.arange(na, dtype=np.float64)
    ang = 2.0 * np.pi * np.outer(k, k) / na
    a1 = pieces(np.concatenate([np.cos(ang), -np.sin(ang)], axis=0), 1)
    ka = np.arange(na, dtype=np.float64)[:, None, None]
    kb = np.arange(na, dtype=np.float64)[None, :, None]
    nl = np.arange(na, dtype=np.float64)[None, None, :]
    th = 2.0 * np.pi * ((nl * (ka + na * kb)) % n) / n
    gr, gi = np.cos(th), -np.sin(th)
    g_re = pieces(np.concatenate([gr, -gi], axis=2), 2)
    g_im = pieces(np.concatenate([gi, gr], axis=2), 2)
    kc = np.arange(FOURIER_GROUP_DIM, dtype=np.float64)
    angc = 2.0 * np.pi * np.outer(kc, kc) / FOURIER_GROUP_DIM
    scale = 1.0 / np.sqrt(float(n) * FOURIER_GROUP_DIM)
    cs = pieces(np.concatenate([np.cos(angc), np.sin(angc)], axis=0) * scale, 0)
    return a1, g_re, g_im, cs


def _fourier(f, tables, *, rows):
    b, s, w = f.shape
    na = int(round(np.sqrt(s)))
    assert na * na == s and na % rows == 0
    a1, g_re, g_im, cs = tables
    x4 = f.reshape(b, na, na, w)
    const2 = lambda bi, j: (0, 0)
    tr, ti = pl.pallas_call(
        functools.partial(_dft1_kernel, rows=rows),
        grid=(b, na // rows),
        in_specs=[pl.BlockSpec((None, na, rows, w), lambda bi, j: (bi, 0, j, 0)),
                  pl.BlockSpec(a1.shape, const2)],
        out_specs=[pl.BlockSpec((None, na, rows, w), lambda bi, j: (bi, 0, j, 0))] * 2,
        out_shape=[jax.ShapeDtypeStruct((b, na, na, w), F32)] * 2,
        scratch_shapes=[pltpu.VMEM((w // LANES, na * rows, LANES), F32)] * 3,
        compiler_params=_cparams(("parallel", "parallel")),
        name="seq_dft_stage1",
    )(x4, a1)
    y = pl.pallas_call(
        functools.partial(_dft2_kernel, rows=rows),
        grid=(b, na // rows),
        in_specs=[pl.BlockSpec((None, rows, na, w), lambda bi, a: (bi, a, 0, 0))] * 2
                 + [pl.BlockSpec((rows,) + g_re.shape[1:], lambda bi, a: (a, 0, 0))] * 2
                 + [pl.BlockSpec(cs.shape, const2)],
        out_specs=pl.BlockSpec((None, na, rows, w), lambda bi, a: (bi, 0, a, 0)),
        out_shape=jax.ShapeDtypeStruct((b, na, na, w), F32),
        scratch_shapes=[pltpu.VMEM((N_FOURIER_GROUPS, na * rows, FOURIER_GROUP_DIM), F32)],
        compiler_params=_cparams(("parallel", "parallel")),
        name="seq_dft_stage2",
    )(tr, ti, g_re, g_im, cs)
    return y.reshape(b, s, w)


def _outproj_kernel(a_ref, f_ref, ga_ref, gb_ref, x_ref, wa_ref, wf_ref, wo_ref, g_ref, wr_ref,
                    xn_ref, h_ref, aff_ref, *, chunk):
    for r in range(x_ref.shape[0] // chunk):
        rs = pl.ds(r * chunk, chunk)
        a = jnp.dot(a_ref[rs, :], wa_ref[...], preferred_element_type=F32)
        fo = jnp.dot(f_ref[rs, :].astype(BF16), wf_ref[...], preferred_element_type=F32)
        merged = _sigmoid(ga_ref[rs, :]) * a + _sigmoid(gb_ref[rs, :]) * fo
        xn = x_ref[rs, :] + jnp.dot(merged.astype(BF16), wo_ref[...], preferred_element_type=F32)
        xn_ref[rs, :] = xn
        ms = jnp.mean(xn * xn, axis=-1, keepdims=True)
        h = (xn * lax.rsqrt(ms + EPS)) * g_ref[...]
        dt = h.shape[1] // LANES
        for c in range(dt):
            h_ref[pl.ds(r * chunk * dt + c, chunk, stride=dt), :] = h[:, c * LANES:(c + 1) * LANES]
        hi, lo = _hi_lo(h)
        logits = _nt_dot(wr_ref[...], jnp.concatenate([hi, lo, hi], axis=1))
        e = jnp.exp(logits - jnp.max(logits, axis=0, keepdims=True))
        aff_ref[:, rs] = e / jnp.sum(e, axis=0, keepdims=True)


def _outproj(attn, four, ga, gb, x, wa, wf, wo, g, wr_t, *, tm, chunk):
    b, s, d = x.shape
    dt = d // LANES
    tok = lambda bi, i: (bi, i, 0)
    const = lambda bi, i: (0, 0)
    wr_hi, wr_lo = _hi_lo(wr_t)
    wr_cat = jnp.concatenate([wr_hi, wr_hi, wr_lo], axis=1)
    xn, h, aff = pl.pallas_call(
        functools.partial(_outproj_kernel, chunk=chunk),
        grid=(b, s // tm),
        in_specs=[pl.BlockSpec((None, tm, ATTN_WIDTH), tok),
                  pl.BlockSpec((None, tm, FOURIER_WIDTH), tok),
                  pl.BlockSpec((None, tm, d), tok),
                  pl.BlockSpec((None, tm, d), tok),
                  pl.BlockSpec((None, tm, d), tok),
                  pl.BlockSpec((ATTN_WIDTH, d), const),
                  pl.BlockSpec((FOURIER_WIDTH, d), const),
                  pl.BlockSpec((d, d), const),
                  pl.BlockSpec((1, d), const),
                  pl.BlockSpec((N_EXPERTS, 3 * d), const)],
        out_specs=[pl.BlockSpec((None, tm, d), tok),
                   pl.BlockSpec((None, tm * dt, LANES), tok),
                   pl.BlockSpec((None, N_EXPERTS, tm), lambda bi, i: (bi, 0, i))],
        out_shape=[jax.ShapeDtypeStruct((b, s, d), F32),
                   jax.ShapeDtypeStruct((b, s * dt, LANES), F32),
                   jax.ShapeDtypeStruct((b, N_EXPERTS, s), F32)],
        compiler_params=_cparams(("parallel", "parallel")),
        name="outproj_router",
    )(attn, four, ga, gb, x, wa, wf, wo, g.reshape(1, d), wr_cat)
    return xn, h.reshape(b, s, dt, LANES), aff


def _tri(n, kind):
    r = lax.broadcasted_iota(I32, (n, n), 0)
    c = lax.broadcasted_iota(I32, (n, n), 1)
    cond = {"row_le_col": r <= c, "row_lt_col": r < c, "col_le_row": c <= r, "col_lt_row": c < r}[kind]
    return jnp.where(cond, 1.0, 0.0).astype(BF16)


def _split128(v):
    hi = jnp.floor(v * (1.0 / LANES))
    return hi, v - hi * LANES


def _routing_kernel(aff_ref, idx_ref, pos_ref, rank_ref, gt_scr, eq_scr, need_scr, *, cap):
    ne, nj, ni = aff_ref.shape
    bits = lax.bitcast_convert_type(aff_ref[...], I32)

    def bisect(it, thr):
        cand = thr | jnp.left_shift(jnp.int32(1), 30 - it)
        cnt = jnp.sum(jnp.where(bits >= cand, 1.0, 0.0), axis=(1, 2), keepdims=True)
        return jnp.where(cnt >= cap, cand, thr)

    thr = lax.fori_loop(0, 31, bisect, jnp.zeros((ne, 1, 1), I32))
    gt = jnp.where(bits > thr, 1.0, 0.0)
    gt_scr[...] = gt
    eq_scr[...] = jnp.where(bits == thr, 1.0, 0.0)
    need = cap - jnp.sum(gt, axis=(1, 2), keepdims=True)
    need_scr[...] = jnp.broadcast_to(need, (ne, SUBLANES, ni))

    ones_sq = jnp.ones((ni, ni), BF16)
    ones_row = jnp.ones((SUBLANES, ni), BF16)
    ones_rowj = jnp.ones((SUBLANES, nj), BF16)
    u_incl = _tri(ni, "row_le_col")
    uj_strict = _tri(nj, "row_lt_col")
    lj_strict = _tri(nj, "col_lt_row")
    l_incl = _tri(ni, "col_le_row")
    bdot = lambda a, b: jnp.dot(a, b, preferred_element_type=F32)

    c_lane = lax.broadcasted_iota(I32, (nj, cap), 1).astype(F32)
    c_row = lax.broadcasted_iota(I32, (SUBLANES, cap), 1).astype(F32)
    j_sub = lax.broadcasted_iota(I32, (nj, cap), 0).astype(F32)

    def slots(e, carry):
        eq = eq_scr[e]
        eqb = eq.astype(BF16)
        tie_rank = bdot(lj_strict, bdot(eqb, ones_sq).astype(BF16)) + bdot(eqb, u_incl)
        take = jnp.where(tie_rank <= need_scr[e][0:1, :], eq, 0.0)
        sel = jnp.maximum(gt_scr[e], take)
        selb = sel.astype(BF16)
        lc = bdot(selb, u_incl)
        lc_t = _nt_dot(l_incl, selb)
        n_rep = bdot(selb, ones_sq)
        s_rep = bdot(lj_strict, n_rep.astype(BF16)) + n_rep
        n_lane = _nt_dot(ones_row, selb)
        pex_lane = bdot(n_lane.astype(BF16), uj_strict)
        s_wide = jnp.concatenate([s_rep] * (cap // ni), axis=1) if cap > ni else s_rep[:, :cap]
        jc = bdot(ones_rowj, jnp.where(s_wide <= c_lane, 1.0, 0.0).astype(BF16))
        onehot = jnp.where(j_sub == jnp.broadcast_to(jc[0:1], (nj, cap)), 1.0, 0.0).astype(BF16)
        phi, plo = _split128(pex_lane)
        r = c_row - (bdot(phi.astype(BF16), onehot) * LANES + bdot(plo.astype(BF16), onehot))
        lcs = bdot(lc_t.astype(BF16), onehot)
        ic = bdot(ones_row, jnp.where(lcs <= jnp.broadcast_to(r[0:1], (ni, cap)), 1.0, 0.0).astype(BF16))
        idx_ref[e] = (jc * ni + ic).astype(I32)
        pos_ref[e] = pex_lane.astype(I32)
        rank_ref[e] = jnp.where(sel > 0.0, lc - 1.0, -1.0)
        return carry

    lax.fori_loop(0, ne, slots, 0)


def _routing(aff, *, cap):
    b, ne, s = aff.shape
    nj = s // LANES
    aff4 = aff.reshape(b, ne, nj, LANES)
    per_b = lambda bi: (bi, 0, 0, 0)
    idx, pos, rank = pl.pallas_call(
        functools.partial(_routing_kernel, cap=cap),
        grid=(b,),
        in_specs=[pl.BlockSpec((None, ne, nj, LANES), per_b)],
        out_specs=[pl.BlockSpec((None, ne, SUBLANES, cap), per_b),
                   pl.BlockSpec((None, ne, SUBLANES, nj), per_b),
                   pl.BlockSpec((None, ne, nj, LANES), per_b)],
        out_shape=[jax.ShapeDtypeStruct((b, ne, SUBLANES, cap), I32),
                   jax.ShapeDtypeStruct((b, ne, SUBLANES, nj), I32),
                   jax.ShapeDtypeStruct((b, ne, nj, LANES), F32)],
        scratch_shapes=[pltpu.VMEM((ne, nj, LANES), F32)] * 2 + [pltpu.VMEM((ne, SUBLANES, LANES), F32)],
        compiler_params=_cparams(("parallel",)),
        name="expert_choice_routing",
    )(aff4)
    return idx[:, :, 0, :], pos[:, :, 0, :], rank


def _ffn_kernel(idx_cur, idx_nxt, h_hbm, wg_ref, wu_ref, wd_ref, y_ref, xs, xsb, gsem,
                *, cap, n_pairs, n_ftiles, per_step, m_chunk):
    p = pl.program_id(0)
    j = pl.program_id(1)
    n_slots = n_ftiles * per_step
    dt = xs.shape[0] // n_slots
    last_pair = n_pairs - 1
    b_cur = p // N_EXPERTS
    b_nxt = jnp.minimum(p + 1, last_pair) // N_EXPERTS

    def gather(ref, bq, c):
        cc = jnp.minimum(c, cap - 1)
        tok = ref[cc >> LANE_BITS, cc & (LANES - 1)]
        dst = xs.at[pl.ds(pl.multiple_of(c * dt, dt), dt)]
        return pltpu.make_async_copy(h_hbm.at[bq, tok], dst, gsem)

    def wait_gathers():
        pltpu.make_async_copy(xs, xs, gsem).wait()

    @pl.when((p == 0) & (j == 0))
    def _():
        def start(c, carry):
            gather(idx_cur, b_cur, c).start()
            return carry

        lax.fori_loop(0, n_slots, start, 0)

    @pl.when(j == 0)
    def _():
        wait_gathers()
        for c in range(dt):
            xsb[:, c * LANES:(c + 1) * LANES] = xs[pl.ds(c, cap, stride=dt), :].astype(BF16)
        y_ref[...] = jnp.zeros_like(y_ref)

    wg = wg_ref[...].astype(BF16)
    wu = wu_ref[...].astype(BF16)
    wd = wd_ref[...].astype(BF16)
    n_groups = cap // m_chunk
    per_group = per_step // n_groups
    for m in range(n_groups):
        for t in range(per_group):
            gather(idx_nxt, b_nxt, j * per_step + (m * per_group + t)).start()
        rows = pl.ds(m * m_chunk, m_chunk)
        x = xsb[rows, :]
        hg = jnp.dot(x, wg, preferred_element_type=F32)
        hu = jnp.dot(x, wu, preferred_element_type=F32)
        act = (hg * _sigmoid(hg)) * hu
        y_ref[rows, :] += jnp.dot(act.astype(BF16), wd, preferred_element_type=F32)

    @pl.when((p == last_pair) & (j == n_ftiles - 1))
    def _():
        wait_gathers()


def _expert_ffn(h, idx, wg, wu, wd, layer, *, cap, tf, m_chunk):
    b, s, dt, _ = h.shape
    d = dt * LANES
    ne, dff = wg.shape[1], wg.shape[3]
    n_ftiles = dff // tf
    n_pairs = b * ne
    n_groups = cap // m_chunk
    per_step = -(-cap // (n_ftiles * n_groups)) * n_groups
    n_slots = n_ftiles * per_step
    tab = idx.reshape(n_pairs, cap // LANES, LANES)
    tab_spec = lambda f: pl.BlockSpec((None, cap // LANES, LANES), lambda p, j: (f(p), 0, 0),
                                      memory_space=pltpu.SMEM)
    return pl.pallas_call(
        functools.partial(_ffn_kernel, cap=cap, n_pairs=n_pairs, n_ftiles=n_ftiles, per_step=per_step,
                          m_chunk=m_chunk),
        grid=(n_pairs, n_ftiles),
        in_specs=[tab_spec(lambda p: p),
                  tab_spec(lambda p: jnp.minimum(p + 1, n_pairs - 1)),
                  pl.BlockSpec(memory_space=pl.ANY),
                  pl.BlockSpec((None, None, d, tf), lambda p, j: (layer, p % ne, 0, j)),
                  pl.BlockSpec((None, None, d, tf), lambda p, j: (layer, p % ne, 0, j)),
                  pl.BlockSpec((None, None, tf, d), lambda p, j: (layer, p % ne, j, 0))],
        out_specs=pl.BlockSpec((None, None, cap, d), lambda p, j: (p // ne, p % ne, 0, 0)),
        out_shape=jax.ShapeDtypeStruct((b, ne, cap, d), F32),
        scratch_shapes=[pltpu.VMEM((n_slots * dt, LANES), F32), pltpu.VMEM((cap, d), BF16),
                        pltpu.SemaphoreType.DMA(())],
        compiler_params=_cparams(("arbitrary", "arbitrary")),
        name="expert_swiglu",
    )(tab, tab, h, wg, wu, wd)


def _combine_kernel(pos_ref, rank_ref, aff_ref, x_ref, g_ref, y_hbm, o_ref, ybuf, acc, sem, cnt,
                    *, cap, n_tiles, final_norm):
    b = pl.program_id(0)
    j = pl.program_id(1)
    nj = pl.num_programs(1)
    t = b * nj + j
    ne = N_EXPERTS

    def nominal(bq, e, jq, r):
        return ((pos_ref[bq * ne + e, jq] >> 3) << 3) + r * EWIN

    def window(bq, e, jq, r):
        return pl.multiple_of(jnp.minimum(nominal(bq, e, jq, r), cap - EWIN), SUBLANES)

    def copies(bq, jq, r, slot):
        return [pltpu.make_async_copy(y_hbm.at[bq, e, pl.ds(window(bq, e, jq, r), EWIN)],
                                      ybuf.at[slot, pl.ds(e * EWIN, EWIN)], sem.at[slot]) for e in range(ne)]

    def start_fetch(bq, jq, r, slot):
        for cp in copies(bq, jq, r, slot):
            cp.start()

    @pl.when(t == 0)
    def _():
        cnt[0] = 0
        start_fetch(b, j, 0, 0)

    span = jnp.int32(1)
    for e in range(ne):
        span = jnp.maximum(span, pos_ref[b * ne + e, j + 1] - nominal(b, e, j, 0))
    n_rounds = (span + EWIN - 1) // EWIN
    first = cnt[0]
    rk = rank_ref[...]
    af = aff_ref[...]
    row = lax.broadcasted_iota(I32, (EWIN, LANES), 0).astype(F32)
    acc[...] = jnp.zeros_like(acc)

    def one_round(r, carry):
        slot = (first + r) & 1
        pltpu.make_async_copy(ybuf.at[slot], ybuf.at[slot], sem.at[slot]).wait()

        @pl.when(r + 1 < n_rounds)
        def _():
            start_fetch(b, j, r + 1, 1 - slot)

        @pl.when((r + 1 == n_rounds) & (t + 1 < n_tiles))
        def _():
            tn = t + 1
            bn = tn // nj
            start_fetch(bn, tn - bn * nj, 0, 1 - slot)

        parts = []
        for e in range(ne):
            p0 = pos_ref[b * ne + e, j]
            local = row + (window(b, e, j, r) - p0).astype(F32)
            want = jnp.where(local >= jnp.maximum(nominal(b, e, j, r) - p0, 0).astype(F32), local, -2.0)
            parts.append(jnp.where(rk[e:e + 1, :] == want, af[e:e + 1, :], 0.0))
        gate_t = jnp.concatenate(parts, axis=0)
        onehot = jnp.where(gate_t != 0.0, 1.0, 0.0).T.astype(BF16)
        scaled = ybuf[slot] * jnp.sum(gate_t, axis=1, keepdims=True)
        hi = scaled.astype(BF16)
        lo = (scaled - hi.astype(F32)).astype(BF16)
        acc[...] += (jnp.dot(onehot, hi, preferred_element_type=F32)
                     + jnp.dot(onehot, lo, preferred_element_type=F32))
        return carry

    lax.fori_loop(0, n_rounds, one_round, 0)
    cnt[0] = first + n_rounds
    y = x_ref[...] + acc[...]
    if final_norm:
        ms = jnp.mean(y * y, axis=-1, keepdims=True)
        y = (y * lax.rsqrt(ms + EPS)) * g_ref[...]
    o_ref[...] = y


def _combine(x, y, pos, rank, aff, g_final, *, cap, final_norm):
    b, s, d = x.shape
    nj = s // LANES
    ne = N_EXPERTS
    pos_tab = jnp.concatenate([pos, jnp.full((b, ne, 1), cap, I32)], axis=2).reshape(b * ne, nj + 1)
    rank_t = jnp.transpose(rank, (0, 2, 1, 3))
    return pl.pallas_call(
        functools.partial(_combine_kernel, cap=cap, n_tiles=b * nj, final_norm=final_norm),
        grid_spec=pltpu.PrefetchScalarGridSpec(
            num_scalar_prefetch=1,
            grid=(b, nj),
            in_specs=[pl.BlockSpec((None, None, ne, LANES), lambda bi, j, tab: (bi, j, 0, 0)),
                      pl.BlockSpec((None, ne, LANES), lambda bi, j, tab: (bi, 0, j)),
                      pl.BlockSpec((None, LANES, d), lambda bi, j, tab: (bi, j, 0)),
                      pl.BlockSpec((1, d), lambda bi, j, tab: (0, 0)),
                      pl.BlockSpec(memory_space=pl.ANY)],
            out_specs=pl.BlockSpec((None, LANES, d), lambda bi, j, tab: (bi, j, 0)),
            scratch_shapes=[pltpu.VMEM((2, ne * EWIN, d), F32), pltpu.VMEM((LANES, d), F32),
                            pltpu.SemaphoreType.DMA((2,)), pltpu.SMEM((1,), I32)]),
        out_shape=jax.ShapeDtypeStruct((b, s, d), F32),
        compiler_params=_cparams(("arbitrary", "arbitrary")),
        name="moe_combine",
    )(pos_tab, rank_t, aff, x, g_final.reshape(1, d), y)


def kernel(x, rel_bias, g_mix, w_in, attn_sink, w_attn_proj, w_fourier_proj, w_out, g_ffn, w_router,
           w_exp_gate, w_exp_up, w_exp_down, g_final):
    b, s, d = x.shape
    depth = g_mix.shape[0]
    cap = CAPACITY_FACTOR * s // N_EXPERTS
    bias = _bias_table(rel_bias)
    tables = _fourier_tables(int(round(np.sqrt(s))))
    for l in range(depth):
        q, k, v, f, ga, gb = _inproj(x.reshape(b * s, d), g_mix[l], w_in[l].astype(BF16), tm=256)
        shp = lambda a: a.reshape(b, s, a.shape[-1])
        attn = _attention(shp(q), shp(k), shp(v), attn_sink[l], bias, blocks_per_step=4)
        four = _fourier(shp(f), tables, rows=8)
        xn, h, aff = _outproj(attn, four, shp(ga), shp(gb), x,
                              w_attn_proj[l].astype(BF16), w_fourier_proj[l].astype(BF16),
                              w_out[l].astype(BF16), g_ffn[l], w_router[l].T, tm=512, chunk=256)
        idx, pos, rank = _routing(aff, cap=cap)
        y = _expert_ffn(h, idx, w_exp_gate, w_exp_up, w_exp_down, l, cap=cap, tf=256, m_chunk=min(cap, 512))
        x = _combine(xn, y, pos, rank, aff, g_final, cap=cap, final_norm=(l == depth - 1))
    return x
```

```python
import functools

import numpy as np
import jax
import jax.numpy as jnp
from jax import lax
from jax.experimental import pallas as pl
from jax.experimental.pallas import tpu as pltpu

F32 = jnp.float32
BF16 = jnp.bfloat16
I32 = jnp.int32
HIGHEST = lax.Precision.HIGHEST

N_HEADS = 8
N_KV_HEADS = 2
HEAD_DIM = 64
ATTN_WIDTH = N_HEADS * HEAD_DIM
KV_WIDTH = N_KV_HEADS * HEAD_DIM
WINDOW = 128
BLOCK = 128
N_BUCKETS = 32
MAX_DISTANCE = 128
N_FOURIER_GROUPS = 4
FOURIER_GROUP_DIM = 128
FOURIER_WIDTH = N_FOURIER_GROUPS * FOURIER_GROUP_DIM
N_EXPERTS = 16
CAPACITY_FACTOR = 2
EPS = 1e-6
NEG_INF = -1e30

LANES = 128
LANE_BITS = 7
SUBLANES = 8
VMEM_LIMIT = 56 * 1024 * 1024
EWIN = 32
DFT_PASSES = 1
COMBINE_AHEAD = 3


def _cparams(sem):
    return pltpu.CompilerParams(dimension_semantics=sem, vmem_limit_bytes=VMEM_LIMIT)


def _nt_dot(a, b, **kw):
    return lax.dot_general(a, b, (((1,), (1,)), ((), ())), preferred_element_type=F32, **kw)


def _sigmoid(x):
    return 1.0 / (1.0 + jnp.exp(-x))


def _t5_bucket(rel):
    half = N_BUCKETS // 2
    max_exact = half // 2
    ret = (rel > 0).astype(jnp.int32) * half
    n = jnp.abs(rel)
    nf = jnp.maximum(n, 1).astype(jnp.float32)
    large = max_exact + (jnp.log(nf / max_exact) / np.float32(np.log(MAX_DISTANCE / max_exact))
                         * (half - max_exact)).astype(jnp.int32)
    large = jnp.minimum(large, half - 1)
    return ret + jnp.where(n < max_exact, n, large)


def _bias_kernel(relb_ref, bucket_ref, o_ref):
    bk = bucket_ref[...]
    col = lax.broadcasted_iota(I32, bk.shape, 1)
    for h in range(N_HEADS):
        acc = jnp.full(bk.shape, NEG_INF, F32)
        for b in range(N_BUCKETS):
            acc = jnp.where(bk == b, relb_ref[b, h], acc)
        o_ref[0, h] = jnp.where(col >= BLOCK, acc, NEG_INF)
        o_ref[1, h] = acc
        o_ref[2, h] = jnp.where(col < 2 * BLOCK, acc, NEG_INF)


def _bias_table(rel_bias):
    q_loc = jnp.arange(BLOCK)
    k_loc = jnp.arange(3 * BLOCK) - BLOCK
    rel = k_loc[None, :] - q_loc[:, None]
    bucket = jnp.where(jnp.abs(rel) <= WINDOW, _t5_bucket(rel), -1).astype(I32)
    table = pl.pallas_call(
        _bias_kernel,
        out_shape=jax.ShapeDtypeStruct((3, N_HEADS, BLOCK, 3 * BLOCK), F32),
        in_specs=[pl.BlockSpec(memory_space=pltpu.SMEM),
                  pl.BlockSpec(memory_space=pltpu.VMEM)],
        out_specs=pl.BlockSpec(memory_space=pltpu.VMEM),
        name="rel_bias_table",
    )(rel_bias.astype(F32), bucket)
    return table.reshape(3, N_HEADS * BLOCK, 3 * BLOCK)


def _inproj_kernel(x_ref, g_ref, w_ref, q_ref, k_ref, v_ref, f_ref, ga_ref, gb_ref, *, d_model):
    x = x_ref[...]
    ms = jnp.mean(x * x, axis=-1, keepdims=True)
    h = (x * lax.rsqrt(ms + EPS)) * g_ref[...]
    z = jnp.dot(h.astype(BF16), w_ref[...], preferred_element_type=F32)
    o = 0
    q_ref[...] = (z[:, o:o + ATTN_WIDTH] * (HEAD_DIM ** -0.5)).astype(BF16)
    o += ATTN_WIDTH
    k_ref[...] = z[:, o:o + KV_WIDTH].astype(BF16)
    o += KV_WIDTH
    v_ref[...] = z[:, o:o + KV_WIDTH].astype(BF16)
    o += KV_WIDTH
    f_ref[...] = z[:, o:o + FOURIER_WIDTH]
    o += FOURIER_WIDTH
    ga_ref[...] = z[:, o:o + d_model]
    o += d_model
    gb_ref[...] = z[:, o:o + d_model]


def _inproj(x2, g, w_bf16, *, tm):
    n, d = x2.shape
    in_w = w_bf16.shape[1]
    widths = (ATTN_WIDTH, KV_WIDTH, KV_WIDTH, FOURIER_WIDTH, d, d)
    dtypes = (BF16, BF16, BF16, F32, F32, F32)
    row = lambda i: (i, 0)
    return pl.pallas_call(
        functools.partial(_inproj_kernel, d_model=d),
        grid=(n // tm,),
        in_specs=[pl.BlockSpec((tm, d), row),
                  pl.BlockSpec((1, d), lambda i: (0, 0)),
                  pl.BlockSpec((d, in_w), lambda i: (0, 0))],
        out_specs=[pl.BlockSpec((tm, w), row) for w in widths],
        out_shape=[jax.ShapeDtypeStruct((n, w), dt) for w, dt in zip(widths, dtypes)],
        compiler_params=_cparams(("parallel",)),
        name="inproj",
    )(x2, g.reshape(1, d), w_bf16)


def _attn_kernel(sink_ref, q_ref, kp_ref, ko_ref, kn_ref, vp_ref, vo_ref, vn_ref, bias_ref, o_ref,
                 *, n_blocks, blocks_per_step):
    i = pl.program_id(1)
    kcat = jnp.concatenate([kp_ref[...], ko_ref[...], kn_ref[...]], axis=0)
    vcat = jnp.concatenate([vp_ref[...], vo_ref[...], vn_ref[...]], axis=0)
    group = N_HEADS // N_KV_HEADS
    head = lambda a, h: a[:, h * HEAD_DIM:(h + 1) * HEAD_DIM]
    sink = jnp.concatenate([jnp.full((BLOCK, LANES), sink_ref[h], F32) for h in range(N_HEADS)], axis=0)
    ones = jnp.ones((3 * BLOCK, LANES), BF16)
    for t in range(blocks_per_step):
        blk = i * blocks_per_step + t
        kind = jnp.where(blk == 0, 0, jnp.where(blk == n_blocks - 1, 2, 1))
        kw = kcat[t * BLOCK:(t + 3) * BLOCK]
        vw = vcat[t * BLOCK:(t + 3) * BLOCK]
        qb = q_ref[t * BLOCK:(t + 1) * BLOCK, :]
        s = jnp.concatenate([_nt_dot(head(qb, h), head(kw, h // group)) for h in range(N_HEADS)], axis=0)
        s = s + bias_ref[kind]
        m = jnp.maximum(jnp.broadcast_to(jnp.max(s, axis=-1, keepdims=True), sink.shape), sink)
        p = jnp.exp(s - jnp.concatenate([m] * 3, axis=1)).astype(BF16)
        denom = jnp.dot(p, ones, preferred_element_type=F32) + jnp.exp(sink - m)
        inv = 1.0 / denom
        for h in range(N_HEADS):
            rows = slice(h * BLOCK, (h + 1) * BLOCK)
            out = jnp.dot(p[rows], head(vw, h // group), preferred_element_type=F32) * inv[rows, :HEAD_DIM]
            o_ref[t * BLOCK:(t + 1) * BLOCK, h * HEAD_DIM:(h + 1) * HEAD_DIM] = out.astype(o_ref.dtype)


def _attention(q, k, v, sink, bias, *, blocks_per_step):
    b, s, _ = q.shape
    n_blocks = s // BLOCK
    tq = blocks_per_step * BLOCK
    prev = lambda bi, i: (bi, jnp.maximum(i * blocks_per_step - 1, 0), 0)
    own = lambda bi, i: (bi, i, 0)
    nxt = lambda bi, i: (bi, jnp.minimum((i + 1) * blocks_per_step, n_blocks - 1), 0)
    kv_specs = [pl.BlockSpec((None, BLOCK, KV_WIDTH), prev),
                pl.BlockSpec((None, tq, KV_WIDTH), own),
                pl.BlockSpec((None, BLOCK, KV_WIDTH), nxt)]
    return pl.pallas_call(
        functools.partial(_attn_kernel, n_blocks=n_blocks, blocks_per_step=blocks_per_step),
        grid=(b, s // tq),
        in_specs=[pl.BlockSpec(memory_space=pltpu.SMEM),
                  pl.BlockSpec((None, tq, ATTN_WIDTH), own)] + kv_specs + kv_specs
                 + [pl.BlockSpec((3, N_HEADS * BLOCK, 3 * BLOCK), lambda bi, i: (0, 0, 0))],
        out_specs=pl.BlockSpec((None, tq, ATTN_WIDTH), own),
        out_shape=jax.ShapeDtypeStruct((b, s, ATTN_WIDTH), BF16),
        compiler_params=_cparams(("parallel", "parallel")),
        name="window_attention",
    )(sink.astype(F32), q, k, k, k, v, v, v, bias)


def _hi_lo(x):
    hi = x.astype(BF16)
    return hi, (x - hi.astype(F32)).astype(BF16)


def _dft_operand(x, axis):
    if DFT_PASSES == 1:
        return x.astype(BF16)
    hi, lo = _hi_lo(x)
    return jnp.concatenate([hi, lo, hi], axis=axis)


def _dft1_kernel(x_ref, a_ref, tr_ref, ti_ref, x2, tr2, ti2, *, rows):
    a = a_ref[...]
    na = a.shape[0] // 2
    n_slabs = x2.shape[0]
    lanes = lambda c: slice(c * LANES, (c + 1) * LANES)
    for c in range(n_slabs):
        x2[c] = x_ref[:, :, lanes(c)].reshape(x2.shape[1:])
    for jj in range(rows):
        strided = pl.ds(jj, na, stride=rows)
        xj = jnp.concatenate([x2[c, strided, :] for c in range(n_slabs)], axis=1)
        t = jnp.dot(a, _dft_operand(xj, 0), preferred_element_type=F32)
        for c in range(n_slabs):
            tr2[c, strided, :] = t[:na, lanes(c)]
            ti2[c, strided, :] = t[na:, lanes(c)]
    for c in range(n_slabs):
        tr_ref[:, :, lanes(c)] = tr2[c].reshape(na, rows, LANES)
        ti_ref[:, :, lanes(c)] = ti2[c].reshape(na, rows, LANES)


def _dft2_kernel(tr_ref, ti_ref, gr_ref, gi_ref, cs_ref, o_ref, o2, *, rows):
    cs = cs_ref[...]
    na = o_ref.shape[0]
    for aa in range(rows):
        t = _dft_operand(jnp.concatenate([tr_ref[aa], ti_ref[aa]], axis=0), 0)
        zr = jnp.dot(gr_ref[aa], t, preferred_element_type=F32)
        zi = jnp.dot(gi_ref[aa], t, preferred_element_type=F32)
        for g in range(N_FOURIER_GROUPS):
            sl = slice(g * FOURIER_GROUP_DIM, (g + 1) * FOURIER_GROUP_DIM)
            z = _dft_operand(jnp.concatenate([zr[:, sl], zi[:, sl]], axis=1), 1)
            y = jnp.dot(z, cs, preferred_element_type=F32)
            o2[g, pl.ds(aa, na, stride=rows), :] = y
    for g in range(N_FOURIER_GROUPS):
        sl = slice(g * FOURIER_GROUP_DIM, (g + 1) * FOURIER_GROUP_DIM)
        o_ref[:, :, sl] = o2[g].reshape(na, rows, FOURIER_GROUP_DIM)


def _fourier_tables(na):
    def pieces(a, axis):
        hi = a.astype(BF16)
        if DFT_PASSES == 1:
            return jnp.asarray(hi)
        lo = (a - hi.astype(np.float64)).astype(BF16)
        return jnp.asarray(np.concatenate([hi, hi, lo], axis=axis))

    n = na * na
    k = np.arange(na, dtype=np.float64)
    ang = 2.0 * np.pi * np.outer(k, k) / na
    a1 = pieces(np.concatenate([np.cos(ang), -np.sin(ang)], axis=0), 1)
    ka = np.arange(na, dtype=np.float64)[:, None, None]
    kb = np.arange(na, dtype=np.float64)[None, :, None]
    nl = np.arange(na, dtype=np.float64)[None, None, :]
    th = 2.0 * np.pi * ((nl * (ka + na * kb)) % n) / n
    gr, gi = np.cos(th), -np.sin(th)
    g_re = pieces(np.concatenate([gr, -gi], axis=2), 2)
    g_im = pieces(np.concatenate([gi, gr], axis=2), 2)
    kc = np.arange(FOURIER_GROUP_DIM, dtype=np.float64)
    angc = 2.0 * np.pi * np.outer(kc, kc) / FOURIER_GROUP_DIM
    scale = 1.0 / np.sqrt(float(n) * FOURIER_GROUP_DIM)
    cs = pieces(np.concatenate([np.cos(angc), np.sin(angc)], axis=0) * scale, 0)
    return a1, g_re, g_im, cs


def _fourier(f, tables, *, rows):
    b, s, w = f.shape
    na = int(round(np.sqrt(s)))
    assert na * na == s and na % rows == 0
    a1, g_re, g_im, cs = tables
    x4 = f.reshape(b, na, na, w)
    const2 = lambda bi, j: (0, 0)
    tr, ti = pl.pallas_call(
        functools.partial(_dft1_kernel, rows=rows),
        grid=(b, na // rows),
        in_specs=[pl.BlockSpec((None, na, rows, w), lambda bi, j: (bi, 0, j, 0)),
                  pl.BlockSpec(a1.shape, const2)],
        out_specs=[pl.BlockSpec((None, na, rows, w), lambda bi, j: (bi, 0, j, 0))] * 2,
        out_shape=[jax.ShapeDtypeStruct((b, na, na, w), F32)] * 2,
        scratch_shapes=[pltpu.VMEM((w // LANES, na * rows, LANES), F32)] * 3,
        compiler_params=_cparams(("parallel", "parallel")),
        name="seq_dft_stage1",
    )(x4, a1)
    y = pl.pallas_call(
        functools.partial(_dft2_kernel, rows=rows),
        grid=(b, na // rows),
        in_specs=[pl.BlockSpec((None, rows, na, w), lambda bi, a: (bi, a, 0, 0))] * 2
                 + [pl.BlockSpec((rows,) + g_re.shape[1:], lambda bi, a: (a, 0, 0))] * 2
                 + [pl.BlockSpec(cs.shape, const2)],
        out_specs=pl.BlockSpec((None, na, rows, w), lambda bi, a: (bi, 0, a, 0)),
        out_shape=jax.ShapeDtypeStruct((b, na, na, w), F32),
        scratch_shapes=[pltpu.VMEM((N_FOURIER_GROUPS, na * rows, FOURIER_GROUP_DIM), F32)],
        compiler_params=_cparams(("parallel", "parallel")),
        name="seq_dft_stage2",
    )(tr, ti, g_re, g_im, cs)
    return y.reshape(b, s, w)


def _outproj_kernel(a_ref, f_ref, ga_ref, gb_ref, x_ref, wa_ref, wf_ref, wo_ref, g_ref, wr_ref,
                    xn_ref, h_ref, aff_ref, *, chunk):
    for r in range(x_ref.shape[0] // chunk):
        rs = pl.ds(r * chunk, chunk)
        a = jnp.dot(a_ref[rs, :], wa_ref[...], preferred_element_type=F32)
        fo = jnp.dot(f_ref[rs, :].astype(BF16), wf_ref[...], preferred_element_type=F32)
        merged = _sigmoid(ga_ref[rs, :]) * a + _sigmoid(gb_ref[rs, :]) * fo
        xn = x_ref[rs, :] + jnp.dot(merged.astype(BF16), wo_ref[...], preferred_element_type=F32)
        xn_ref[rs, :] = xn
        ms = jnp.mean(xn * xn, axis=-1, keepdims=True)
        h = (xn * lax.rsqrt(ms + EPS)) * g_ref[...]
        dt = h.shape[1] // LANES
        for c in range(dt):
            h_ref[pl.ds(r * chunk * dt + c, chunk, stride=dt), :] = h[:, c * LANES:(c + 1) * LANES]
        hi, lo = _hi_lo(h)
        logits = _nt_dot(wr_ref[...], jnp.concatenate([hi, lo, hi], axis=1))
        e = jnp.exp(logits - jnp.max(logits, axis=0, keepdims=True))
        aff_ref[:, rs] = e / jnp.sum(e, axis=0, keepdims=True)


def _outproj(attn, four, ga, gb, x, wa, wf, wo, g, wr_t, *, tm, chunk):
    b, s, d = x.shape
    dt = d // LANES
    tok = lambda bi, i: (bi, i, 0)
    const = lambda bi, i: (0, 0)
    wr_hi, wr_lo = _hi_lo(wr_t)
    wr_cat = jnp.concatenate([wr_hi, wr_hi, wr_lo], axis=1)
    xn, h, aff = pl.pallas_call(
        functools.partial(_outproj_kernel, chunk=chunk),
        grid=(b, s // tm),
        in_specs=[pl.BlockSpec((None, tm, ATTN_WIDTH), tok),
                  pl.BlockSpec((None, tm, FOURIER_WIDTH), tok),
                  pl.BlockSpec((None, tm, d), tok),
                  pl.BlockSpec((None, tm, d), tok),
                  pl.BlockSpec((None, tm, d), tok),
                  pl.BlockSpec((ATTN_WIDTH, d), const),
                  pl.BlockSpec((FOURIER_WIDTH, d), const),
                  pl.BlockSpec((d, d), const),
                  pl.BlockSpec((1, d), const),
                  pl.BlockSpec((N_EXPERTS, 3 * d), const)],
        out_specs=[pl.BlockSpec((None, tm, d), tok),
                   pl.BlockSpec((None, tm * dt, LANES), tok),
                   pl.BlockSpec((None, N_EXPERTS, tm), lambda bi, i: (bi, 0, i))],
        out_shape=[jax.ShapeDtypeStruct((b, s, d), F32),
                   jax.ShapeDtypeStruct((b, s * dt, LANES), F32),
                   jax.ShapeDtypeStruct((b, N_EXPERTS, s), F32)],
        compiler_params=_cparams(("parallel", "parallel")),
        name="outproj_router",
    )(attn, four, ga, gb, x, wa, wf, wo, g.reshape(1, d), wr_cat)
    return xn, h.reshape(b, s, dt, LANES), aff


def _tri(n, kind):
    r = lax.broadcasted_iota(I32, (n, n), 0)
    c = lax.broadcasted_iota(I32, (n, n), 1)
    cond = {"row_le_col": r <= c, "row_lt_col": r < c, "col_le_row": c <= r, "col_lt_row": c < r}[kind]
    return jnp.where(cond, 1.0, 0.0).astype(BF16)


def _split128(v):
    hi = jnp.floor(v * (1.0 / LANES))
    return hi, v - hi * LANES


def _routing_kernel(aff_ref, idx_ref, pos_ref, rank_ref, gt_scr, eq_scr, need_scr, *, cap):
    ne, nj, ni = aff_ref.shape
    bits = lax.bitcast_convert_type(aff_ref[...], I32)

    def bisect(it, thr):
        cand = thr | jnp.left_shift(jnp.int32(1), 30 - it)
        cnt = jnp.sum(jnp.where(bits >= cand, 1.0, 0.0), axis=(1, 2), keepdims=True)
        return jnp.where(cnt >= cap, cand, thr)

    thr = lax.fori_loop(0, 31, bisect, jnp.zeros((ne, 1, 1), I32))
    gt = jnp.where(bits > thr, 1.0, 0.0)
    gt_scr[...] = gt
    eq_scr[...] = jnp.where(bits == thr, 1.0, 0.0)
    need = cap - jnp.sum(gt, axis=(1, 2), keepdims=True)
    need_scr[...] = jnp.broadcast_to(need, (ne, SUBLANES, ni))

    ones_sq = jnp.ones((ni, ni), BF16)
    ones_row = jnp.ones((SUBLANES, ni), BF16)
    ones_rowj = jnp.ones((SUBLANES, nj), BF16)
    u_incl = _tri(ni, "row_le_col")
    uj_strict = _tri(nj, "row_lt_col")
    lj_strict = _tri(nj, "col_lt_row")
    l_incl = _tri(ni, "col_le_row")
    bdot = lambda a, b: jnp.dot(a, b, preferred_element_type=F32)

    c_lane = lax.broadcasted_iota(I32, (nj, cap), 1).astype(F32)
    c_row = lax.broadcasted_iota(I32, (SUBLANES, cap), 1).astype(F32)
    j_sub = lax.broadcasted_iota(I32, (nj, cap), 0).astype(F32)

    def slots(e, carry):
        eq = eq_scr[e]
        eqb = eq.astype(BF16)
        tie_rank = bdot(lj_strict, bdot(eqb, ones_sq).astype(BF16)) + bdot(eqb, u_incl)
        take = jnp.where(tie_rank <= need_scr[e][0:1, :], eq, 0.0)
        sel = jnp.maximum(gt_scr[e], take)
        selb = sel.astype(BF16)
        lc = bdot(selb, u_incl)
        lc_t = _nt_dot(l_incl, selb)
        n_rep = bdot(selb, ones_sq)
        s_rep = bdot(lj_strict, n_rep.astype(BF16)) + n_rep
        n_lane = _nt_dot(ones_row, selb)
        pex_lane = bdot(n_lane.astype(BF16), uj_strict)
        s_wide = jnp.concatenate([s_rep] * (cap // ni), axis=1) if cap > ni else s_rep[:, :cap]
        jc = bdot(ones_rowj, jnp.where(s_wide <= c_lane, 1.0, 0.0).astype(BF16))
        onehot = jnp.where(j_sub == jnp.broadcast_to(jc[0:1], (nj, cap)), 1.0, 0.0).astype(BF16)
        phi, plo = _split128(pex_lane)
        r = c_row - (bdot(phi.astype(BF16), onehot) * LANES + bdot(plo.astype(BF16), onehot))
        lcs = bdot(lc_t.astype(BF16), onehot)
        ic = bdot(ones_row, jnp.where(lcs <= jnp.broadcast_to(r[0:1], (ni, cap)), 1.0, 0.0).astype(BF16))
        idx_ref[e] = (jc * ni + ic).astype(I32)
        pos_ref[e] = pex_lane.astype(I32)
        rank_ref[e] = jnp.where(sel > 0.0, lc - 1.0, -1.0)
        return carry

    lax.fori_loop(0, ne, slots, 0)


def _routing(aff, *, cap):
    b, ne, s = aff.shape
    nj = s // LANES
    aff4 = aff.reshape(b, ne, nj, LANES)
    per_b = lambda bi: (bi, 0, 0, 0)
    idx, pos, rank = pl.pallas_call(
        functools.partial(_routing_kernel, cap=cap),
        grid=(b,),
        in_specs=[pl.BlockSpec((None, ne, nj, LANES), per_b)],
        out_specs=[pl.BlockSpec((None, ne, SUBLANES, cap), per_b),
                   pl.BlockSpec((None, ne, SUBLANES, nj), per_b),
                   pl.BlockSpec((None, ne, nj, LANES), per_b)],
        out_shape=[jax.ShapeDtypeStruct((b, ne, SUBLANES, cap), I32),
                   jax.ShapeDtypeStruct((b, ne, SUBLANES, nj), I32),
                   jax.ShapeDtypeStruct((b, ne, nj, LANES), F32)],
        scratch_shapes=[pltpu.VMEM((ne, nj, LANES), F32)] * 2 + [pltpu.VMEM((ne, SUBLANES, LANES), F32)],
        compiler_params=_cparams(("parallel",)),
        name="expert_choice_routing",
    )(aff4)
    return idx[:, :, 0, :], pos[:, :, 0, :], rank


def _ffn_kernel(idx_cur, idx_nxt, h_hbm, wg_ref, wu_ref, wd_ref, y_ref, xs, xsb, gsem,
                *, cap, n_pairs, n_ftiles, per_step, m_chunk):
    p = pl.program_id(0)
    j = pl.program_id(1)
    n_slots = n_ftiles * per_step
    dt = xs.shape[0] // n_slots
    last_pair = n_pairs - 1
    b_cur = p // N_EXPERTS
    b_nxt = jnp.minimum(p + 1, last_pair) // N_EXPERTS

    def gather(ref, bq, c):
        cc = jnp.minimum(c, cap - 1)
        tok = ref[cc >> LANE_BITS, cc & (LANES - 1)]
        dst = xs.at[pl.ds(pl.multiple_of(c * dt, dt), dt)]
        return pltpu.make_async_copy(h_hbm.at[bq, tok], dst, gsem)

    def wait_gathers():
        pltpu.make_async_copy(xs, xs, gsem).wait()

    @pl.when((p == 0) & (j == 0))
    def _():
        def start(c, carry):
            gather(idx_cur, b_cur, c).start()
            return carry

        lax.fori_loop(0, n_slots, start, 0)

    @pl.when(j == 0)
    def _():
        wait_gathers()
        for c in range(dt):
            xsb[:, c * LANES:(c + 1) * LANES] = xs[pl.ds(c, cap, stride=dt), :].astype(BF16)
        y_ref[...] = jnp.zeros_like(y_ref)

    wg = wg_ref[...].astype(BF16)
    wu = wu_ref[...].astype(BF16)
    wd = wd_ref[...].astype(BF16)
    n_groups = cap // m_chunk
    per_group = per_step // n_groups
    for m in range(n_groups):
        for t in range(per_group):
            gather(idx_nxt, b_nxt, j * per_step + (m * per_group + t)).start()
        rows = pl.ds(m * m_chunk, m_chunk)
        x = xsb[rows, :]
        hg = jnp.dot(x, wg, preferred_element_type=F32)
        hu = jnp.dot(x, wu, preferred_element_type=F32)
        act = (hg * _sigmoid(hg)) * hu
        y_ref[rows, :] += jnp.dot(act.astype(BF16), wd, preferred_element_type=F32)

    @pl.when((p == last_pair) & (j == n_ftiles - 1))
    def _():
        wait_gathers()


def _expert_ffn(h, idx, wg, wu, wd, layer, *, cap, tf, m_chunk):
    b, s, dt, _ = h.shape
    d = dt * LANES
    ne, dff = wg.shape[1], wg.shape[3]
    n_ftiles = dff // tf
    n_pairs = b * ne
    n_groups = cap // m_chunk
    per_step = -(-cap // (n_ftiles * n_groups)) * n_groups
    n_slots = n_ftiles * per_step
    tab = idx.reshape(n_pairs, cap // LANES, LANES)
    tab_spec = lambda f: pl.BlockSpec((None, cap // LANES, LANES), lambda p, j: (f(p), 0, 0),
                                      memory_space=pltpu.SMEM)
    return pl.pallas_call(
        functools.partial(_ffn_kernel, cap=cap, n_pairs=n_pairs, n_ftiles=n_ftiles, per_step=per_step,
                          m_chunk=m_chunk),
        grid=(n_pairs, n_ftiles),
        in_specs=[tab_spec(lambda p: p),
                  tab_spec(lambda p: jnp.minimum(p + 1, n_pairs - 1)),
                  pl.BlockSpec(memory_space=pl.ANY),
                  pl.BlockSpec((None, None, d, tf), lambda p, j: (layer, p % ne, 0, j)),
                  pl.BlockSpec((None, None, d, tf), lambda p, j: (layer, p % ne, 0, j)),
                  pl.BlockSpec((None, None, tf, d), lambda p, j: (layer, p % ne, j, 0))],
        out_specs=pl.BlockSpec((None, None, cap, d), lambda p, j: (p // ne, p % ne, 0, 0)),
        out_shape=jax.ShapeDtypeStruct((b, ne, cap, d), F32),
        scratch_shapes=[pltpu.VMEM((n_slots * dt, LANES), F32), pltpu.VMEM((cap, d), BF16),
                        pltpu.SemaphoreType.DMA(())],
        compiler_params=_cparams(("arbitrary", "arbitrary")),
        name="expert_swiglu",
    )(tab, tab, h, wg, wu, wd)


def _combine_kernel(pos_ref, rank_ref, aff_ref, x_ref, g_ref, y_hbm, o_ref, ybuf, acc, sem,
                    *, cap, nj, n_tiles, final_norm):
    b = pl.program_id(0)
    j = pl.program_id(1)
    t = b * nj + j
    ne = N_EXPERTS
    ring = COMBINE_AHEAD + 1
    assert ring & (ring - 1) == 0
    in_ring = lambda k: k & (ring - 1)

    def nominal(bq, e, jq, r):
        return ((pos_ref[bq * ne + e, jq] >> 3) << 3) + r * EWIN

    def window(bq, e, jq, r):
        return pl.multiple_of(jnp.minimum(nominal(bq, e, jq, r), cap - EWIN), SUBLANES)

    def start_fetch(ahead, r, slot):
        jq = j + ahead
        wrap = (jq >= nj).astype(I32)
        bq = b + wrap
        jq = jq - wrap * nj
        past = bq * nj + jq >= n_tiles
        bq = jnp.where(past, n_tiles // nj - 1, bq)
        jq = jnp.where(past, nj - 1, jq)
        for e in range(ne):
            pltpu.make_async_copy(y_hbm.at[bq, e, pl.ds(window(bq, e, jq, r), EWIN)],
                                  ybuf.at[slot, pl.ds(e * EWIN, EWIN)], sem.at[slot]).start()

    def wait_fetch(slot):
        pltpu.make_async_copy(ybuf.at[slot], ybuf.at[slot], sem.at[slot]).wait()

    rk = rank_ref[...]
    af = aff_ref[...]
    row = lax.broadcasted_iota(I32, (EWIN, LANES), 0).astype(F32)

    def add_round(r, slot):
        parts = []
        for e in range(ne):
            p0 = pos_ref[b * ne + e, j]
            local = row + (window(b, e, j, r) - p0).astype(F32)
            want = jnp.where(local >= jnp.maximum(nominal(b, e, j, r) - p0, 0).astype(F32), local, -2.0)
            parts.append(jnp.where(rk[e:e + 1, :] == want, af[e:e + 1, :], 0.0))
        gate_t = jnp.concatenate(parts, axis=0)
        onehot = jnp.where(gate_t != 0.0, 1.0, 0.0).T.astype(BF16)
        scaled = ybuf[slot] * jnp.sum(gate_t, axis=1, keepdims=True)
        hi = scaled.astype(BF16)
        lo = (scaled - hi.astype(F32)).astype(BF16)
        return jnp.dot(onehot, hi, preferred_element_type=F32) + jnp.dot(onehot, lo, preferred_element_type=F32)

    @pl.when(t == 0)
    def _():
        for ahead in range(COMBINE_AHEAD):
            start_fetch(ahead, 0, ahead)

    start_fetch(COMBINE_AHEAD, 0, in_ring(t + COMBINE_AHEAD))
    slot0 = in_ring(t)
    wait_fetch(slot0)
    acc[...] = add_round(0, slot0)

    span = jnp.int32(1)
    for e in range(ne):
        span = jnp.maximum(span, pos_ref[b * ne + e, j + 1] - nominal(b, e, j, 0))
    n_rounds = (span + EWIN - 1) // EWIN

    def extra_round(r, carry):
        start_fetch(0, r, ring)
        wait_fetch(ring)
        acc[...] += add_round(r, ring)
        return carry

    lax.fori_loop(1, n_rounds, extra_round, 0)

    @pl.when(t == n_tiles - 1)
    def _():
        for ahead in range(1, COMBINE_AHEAD + 1):
            wait_fetch(in_ring(t + ahead))

    y = x_ref[...] + acc[...]
    if final_norm:
        ms = jnp.mean(y * y, axis=-1, keepdims=True)
        y = (y * lax.rsqrt(ms + EPS)) * g_ref[...]
    o_ref[...] = y


def _combine(x, y, pos, rank, aff, g_final, *, cap, final_norm):
    b, s, d = x.shape
    nj = s // LANES
    ne = N_EXPERTS
    pos_tab = jnp.concatenate([pos, jnp.full((b, ne, 1), cap, I32)], axis=2).reshape(b * ne, nj + 1)
    rank_t = jnp.transpose(rank, (0, 2, 1, 3))
    return pl.pallas_call(
        functools.partial(_combine_kernel, cap=cap, nj=nj, n_tiles=b * nj, final_norm=final_norm),
        grid_spec=pltpu.PrefetchScalarGridSpec(
            num_scalar_prefetch=1,
            grid=(b, nj),
            in_specs=[pl.BlockSpec((None, None, ne, LANES), lambda bi, j, tab: (bi, j, 0, 0)),
                      pl.BlockSpec((None, ne, LANES), lambda bi, j, tab: (bi, 0, j)),
                      pl.BlockSpec((None, LANES, d), lambda bi, j, tab: (bi, j, 0)),
                      pl.BlockSpec((1, d), lambda bi, j, tab: (0, 0)),
                      pl.BlockSpec(memory_space=pl.ANY)],
            out_specs=pl.BlockSpec((None, LANES, d), lambda bi, j, tab: (bi, j, 0)),
            scratch_shapes=[pltpu.VMEM((COMBINE_AHEAD + 2, ne * EWIN, d), F32), pltpu.VMEM((LANES, d), F32),
                            pltpu.SemaphoreType.DMA((COMBINE_AHEAD + 2,))]),
        out_shape=jax.ShapeDtypeStruct((b, s, d), F32),
        compiler_params=_cparams(("arbitrary", "arbitrary")),
        name="moe_combine",
    )(pos_tab, rank_t, aff, x, g_final.reshape(1, d), y)


def kernel(x, rel_bias, g_mix, w_in, attn_sink, w_attn_proj, w_fourier_proj, w_out, g_ffn, w_router,
           w_exp_gate, w_exp_up, w_exp_down, g_final):
    b, s, d = x.shape
    depth = g_mix.shape[0]
    cap = CAPACITY_FACTOR * s // N_EXPERTS
    bias = _bias_table(rel_bias)
    tables = _fourier_tables(int(round(np.sqrt(s))))
    for l in range(depth):
        q, k, v, f, ga, gb = _inproj(x.reshape(b * s, d), g_mix[l], w_in[l].astype(BF16), tm=256)
        shp = lambda a: a.reshape(b, s, a.shape[-1])
        attn = _attention(shp(q), shp(k), shp(v), attn_sink[l], bias, blocks_per_step=4)
        four = _fourier(shp(f), tables, rows=8)
        xn, h, aff = _outproj(attn, four, shp(ga), shp(gb), x,
                              w_attn_proj[l].astype(BF16), w_fourier_proj[l].astype(BF16),
                              w_out[l].astype(BF16), g_ffn[l], w_router[l].T, tm=512, chunk=256)
        idx, pos, rank = _routing(aff, cap=cap)
        y = _expert_ffn(h, idx, w_exp_gate, w_exp_up, w_exp_down, l, cap=cap, tf=256, m_chunk=min(cap, 512))
        x = _combine(xn, y, pos, rank, aff, g_final, cap=cap, final_norm=(l == depth - 1))
    return x
```

```python
import functools

import numpy as np
import jax
import jax.numpy as jnp
from jax import lax
from jax.experimental import pallas as pl
from jax.experimental.pallas import tpu as pltpu

F32 = jnp.float32
BF16 = jnp.bfloat16
I32 = jnp.int32
HIGHEST = lax.Precision.HIGHEST

N_HEADS = 8
N_KV_HEADS = 2
HEAD_DIM = 64
ATTN_WIDTH = N_HEADS * HEAD_DIM
KV_WIDTH = N_KV_HEADS * HEAD_DIM
WINDOW = 128
BLOCK = 128
N_BUCKETS = 32
MAX_DISTANCE = 128
N_FOURIER_GROUPS = 4
FOURIER_GROUP_DIM = 128
FOURIER_WIDTH = N_FOURIER_GROUPS * FOURIER_GROUP_DIM
N_EXPERTS = 16
CAPACITY_FACTOR = 2
EPS = 1e-6
NEG_INF = -1e30

LANES = 128
LANE_BITS = 7
SUBLANES = 8
VMEM_LIMIT = 56 * 1024 * 1024
EWIN = 32
DFT_PASSES = 1
COMBINE_AHEAD = 3


def _cparams(sem):
    return pltpu.CompilerParams(dimension_semantics=sem, vmem_limit_bytes=VMEM_LIMIT)


def _nt_dot(a, b, **kw):
    return lax.dot_general(a, b, (((1,), (1,)), ((), ())), preferred_element_type=F32, **kw)


def _sigmoid(x):
    return 1.0 / (1.0 + jnp.exp(-x))


def _t5_bucket(rel):
    half = N_BUCKETS // 2
    max_exact = half // 2
    ret = (rel > 0).astype(jnp.int32) * half
    n = jnp.abs(rel)
    nf = jnp.maximum(n, 1).astype(jnp.float32)
    large = max_exact + (jnp.log(nf / max_exact) / np.float32(np.log(MAX_DISTANCE / max_exact))
                         * (half - max_exact)).astype(jnp.int32)
    large = jnp.minimum(large, half - 1)
    return ret + jnp.where(n < max_exact, n, large)


def _bias_kernel(relb_ref, bucket_ref, o_ref):
    bk = bucket_ref[...]
    col = lax.broadcasted_iota(I32, bk.shape, 1)
    for h in range(N_HEADS):
        acc = jnp.full(bk.shape, NEG_INF, F32)
        for b in range(N_BUCKETS):
            acc = jnp.where(bk == b, relb_ref[b, h], acc)
        o_ref[0, h] = jnp.where(col >= BLOCK, acc, NEG_INF)
        o_ref[1, h] = acc
        o_ref[2, h] = jnp.where(col < 2 * BLOCK, acc, NEG_INF)


def _bias_table(rel_bias):
    q_loc = jnp.arange(BLOCK)
    k_loc = jnp.arange(3 * BLOCK) - BLOCK
    rel = k_loc[None, :] - q_loc[:, None]
    bucket = jnp.where(jnp.abs(rel) <= WINDOW, _t5_bucket(rel), -1).astype(I32)
    table = pl.pallas_call(
        _bias_kernel,
        out_shape=jax.ShapeDtypeStruct((3, N_HEADS, BLOCK, 3 * BLOCK), F32),
        in_specs=[pl.BlockSpec(memory_space=pltpu.SMEM),
                  pl.BlockSpec(memory_space=pltpu.VMEM)],
        out_specs=pl.BlockSpec(memory_space=pltpu.VMEM),
        name="rel_bias_table",
    )(rel_bias.astype(F32), bucket)
    return table.reshape(3, N_HEADS * BLOCK, 3 * BLOCK)


def _inproj_kernel(x_ref, g_ref, w_ref, q_ref, k_ref, v_ref, f_ref, ga_ref, gb_ref, *, d_model):
    x = x_ref[...]
    ms = jnp.mean(x * x, axis=-1, keepdims=True)
    h = (x * lax.rsqrt(ms + EPS)) * g_ref[...]
    z = jnp.dot(h.astype(BF16), w_ref[...], preferred_element_type=F32)
    o = 0
    q_ref[...] = (z[:, o:o + ATTN_WIDTH] * (HEAD_DIM ** -0.5)).astype(BF16)
    o += ATTN_WIDTH
    k_ref[...] = z[:, o:o + KV_WIDTH].astype(BF16)
    o += KV_WIDTH
    v_ref[...] = z[:, o:o + KV_WIDTH].astype(BF16)
    o += KV_WIDTH
    f_ref[...] = z[:, o:o + FOURIER_WIDTH]
    o += FOURIER_WIDTH
    ga_ref[...] = _sigmoid(z[:, o:o + d_model]).astype(BF16)
    o += d_model
    gb_ref[...] = _sigmoid(z[:, o:o + d_model]).astype(BF16)


def _inproj(x2, g, w_bf16, *, tm):
    n, d = x2.shape
    in_w = w_bf16.shape[1]
    widths = (ATTN_WIDTH, KV_WIDTH, KV_WIDTH, FOURIER_WIDTH, d, d)
    dtypes = (BF16, BF16, BF16, F32, BF16, BF16)
    row = lambda i: (i, 0)
    return pl.pallas_call(
        functools.partial(_inproj_kernel, d_model=d),
        grid=(n // tm,),
        in_specs=[pl.BlockSpec((tm, d), row),
                  pl.BlockSpec((1, d), lambda i: (0, 0)),
                  pl.BlockSpec((d, in_w), lambda i: (0, 0))],
        out_specs=[pl.BlockSpec((tm, w), row) for w in widths],
        out_shape=[jax.ShapeDtypeStruct((n, w), dt) for w, dt in zip(widths, dtypes)],
        compiler_params=_cparams(("parallel",)),
        name="inproj",
    )(x2, g.reshape(1, d), w_bf16)


def _attn_kernel(sink_ref, q_ref, kp_ref, ko_ref, kn_ref, vp_ref, vo_ref, vn_ref, bias_ref, o_ref,
                 *, n_blocks, blocks_per_step):
    i = pl.program_id(1)
    kcat = jnp.concatenate([kp_ref[...], ko_ref[...], kn_ref[...]], axis=0)
    vcat = jnp.concatenate([vp_ref[...], vo_ref[...], vn_ref[...]], axis=0)
    group = N_HEADS // N_KV_HEADS
    head = lambda a, h: a[:, h * HEAD_DIM:(h + 1) * HEAD_DIM]
    sink = jnp.concatenate([jnp.full((BLOCK, LANES), sink_ref[h], F32) for h in range(N_HEADS)], axis=0)
    ones = jnp.ones((3 * BLOCK, LANES), BF16)
    for t in range(blocks_per_step):
        blk = i * blocks_per_step + t
        kind = jnp.where(blk == 0, 0, jnp.where(blk == n_blocks - 1, 2, 1))
        kw = kcat[t * BLOCK:(t + 3) * BLOCK]
        vw = vcat[t * BLOCK:(t + 3) * BLOCK]
        qb = q_ref[t * BLOCK:(t + 1) * BLOCK, :]
        s = jnp.concatenate([_nt_dot(head(qb, h), head(kw, h // group)) for h in range(N_HEADS)], axis=0)
        s = s + bias_ref[kind]
        m = jnp.maximum(jnp.broadcast_to(jnp.max(s, axis=-1, keepdims=True), sink.shape), sink)
        p = jnp.exp(s - jnp.concatenate([m] * 3, axis=1)).astype(BF16)
        denom = jnp.dot(p, ones, preferred_element_type=F32) + jnp.exp(sink - m)
        inv = 1.0 / denom
        for h in range(N_HEADS):
            rows = slice(h * BLOCK, (h + 1) * BLOCK)
            out = jnp.dot(p[rows], head(vw, h // group), preferred_element_type=F32) * inv[rows, :HEAD_DIM]
            o_ref[t * BLOCK:(t + 1) * BLOCK, h * HEAD_DIM:(h + 1) * HEAD_DIM] = out.astype(o_ref.dtype)


def _attention(q, k, v, sink, bias, *, blocks_per_step):
    b, s, _ = q.shape
    n_blocks = s // BLOCK
    tq = blocks_per_step * BLOCK
    prev = lambda bi, i: (bi, jnp.maximum(i * blocks_per_step - 1, 0), 0)
    own = lambda bi, i: (bi, i, 0)
    nxt = lambda bi, i: (bi, jnp.minimum((i + 1) * blocks_per_step, n_blocks - 1), 0)
    kv_specs = [pl.BlockSpec((None, BLOCK, KV_WIDTH), prev),
                pl.BlockSpec((None, tq, KV_WIDTH), own),
                pl.BlockSpec((None, BLOCK, KV_WIDTH), nxt)]
    return pl.pallas_call(
        functools.partial(_attn_kernel, n_blocks=n_blocks, blocks_per_step=blocks_per_step),
        grid=(b, s // tq),
        in_specs=[pl.BlockSpec(memory_space=pltpu.SMEM),
                  pl.BlockSpec((None, tq, ATTN_WIDTH), own)] + kv_specs + kv_specs
                 + [pl.BlockSpec((3, N_HEADS * BLOCK, 3 * BLOCK), lambda bi, i: (0, 0, 0))],
        out_specs=pl.BlockSpec((None, tq, ATTN_WIDTH), own),
        out_shape=jax.ShapeDtypeStruct((b, s, ATTN_WIDTH), BF16),
        compiler_params=_cparams(("parallel", "parallel")),
        name="window_attention",
    )(sink.astype(F32), q, k, k, k, v, v, v, bias)


def _hi_lo(x):
    hi = x.astype(BF16)
    return hi, (x - hi.astype(F32)).astype(BF16)


def _dft_operand(x, axis):
    if DFT_PASSES == 1:
        return x.astype(BF16)
    hi, lo = _hi_lo(x)
    return jnp.concatenate([hi, lo, hi], axis=axis)


def _dft1_kernel(x_ref, a_ref, tr_ref, ti_ref, x2, tr2, ti2, *, rows):
    a = a_ref[...]
    na = a.shape[0] // 2
    n_slabs = x2.shape[0]
    lanes = lambda c: slice(c * LANES, (c + 1) * LANES)
    for c in range(n_slabs):
        x2[c] = x_ref[:, :, lanes(c)].reshape(x2.shape[1:])
    for jj in range(rows):
        strided = pl.ds(jj, na, stride=rows)
        xj = jnp.concatenate([x2[c, strided, :] for c in range(n_slabs)], axis=1)
        t = jnp.dot(a, _dft_operand(xj, 0), preferred_element_type=F32)
        for c in range(n_slabs):
            tr2[c, strided, :] = t[:na, lanes(c)]
            ti2[c, strided, :] = t[na:, lanes(c)]
    for c in range(n_slabs):
        tr_ref[:, :, lanes(c)] = tr2[c].reshape(na, rows, LANES)
        ti_ref[:, :, lanes(c)] = ti2[c].reshape(na, rows, LANES)


def _dft2_kernel(tr_ref, ti_ref, gr_ref, gi_ref, cs_ref, o_ref, o2, *, rows):
    cs = cs_ref[...]
    na = o_ref.shape[0]
    for aa in range(rows):
        t = _dft_operand(jnp.concatenate([tr_ref[aa], ti_ref[aa]], axis=0), 0)
        zr = jnp.dot(gr_ref[aa], t, preferred_element_type=F32)
        zi = jnp.dot(gi_ref[aa], t, preferred_element_type=F32)
        for g in range(N_FOURIER_GROUPS):
            sl = slice(g * FOURIER_GROUP_DIM, (g + 1) * FOURIER_GROUP_DIM)
            z = _dft_operand(jnp.concatenate([zr[:, sl], zi[:, sl]], axis=1), 1)
            y = jnp.dot(z, cs, preferred_element_type=F32)
            o2[g, pl.ds(aa, na, stride=rows), :] = y
    for g in range(N_FOURIER_GROUPS):
        sl = slice(g * FOURIER_GROUP_DIM, (g + 1) * FOURIER_GROUP_DIM)
        o_ref[:, :, sl] = o2[g].reshape(na, rows, FOURIER_GROUP_DIM)


def _fourier_tables(na):
    def pieces(a, axis):
        hi = a.astype(BF16)
        if DFT_PASSES == 1:
            return jnp.asarray(hi)
        lo = (a - hi.astype(np.float64)).astype(BF16)
        return jnp.asarray(np.concatenate([hi, hi, lo], axis=axis))

    n = na * na
    k = np.arange(na, dtype=np.float64)
    ang = 2.0 * np.pi * np.outer(k, k) / na
    a1 = pieces(np.concatenate([np.cos(ang), -np.sin(ang)], axis=0), 1)
    ka = np.arange(na, dtype=np.float64)[:, None, None]
    kb = np.arange(na, dtype=np.float64)[None, :, None]
    nl = np.arange(na, dtype=np.float64)[None, None, :]
    th = 2.0 * np.pi * ((nl * (ka + na * kb)) % n) / n
    gr, gi = np.cos(th), -np.sin(th)
    g_re = pieces(np.concatenate([gr, -gi], axis=2), 2)
    g_im = pieces(np.concatenate([gi, gr], axis=2), 2)
    kc = np.arange(FOURIER_GROUP_DIM, dtype=np.float64)
    angc = 2.0 * np.pi * np.outer(kc, kc) / FOURIER_GROUP_DIM
    scale = 1.0 / np.sqrt(float(n) * FOURIER_GROUP_DIM)
    cs = pieces(np.concatenate([np.cos(angc), np.sin(angc)], axis=0) * scale, 0)
    return a1, g_re, g_im, cs


def _fourier(f, tables, *, rows):
    b, s, w = f.shape
    na = int(round(np.sqrt(s)))
    assert na * na == s and na % rows == 0
    a1, g_re, g_im, cs = tables
    x4 = f.reshape(b, na, na, w)
    const2 = lambda bi, j: (0, 0)
    tr, ti = pl.pallas_call(
        functools.partial(_dft1_kernel, rows=rows),
        grid=(b, na // rows),
        in_specs=[pl.BlockSpec((None, na, rows, w), lambda bi, j: (bi, 0, j, 0)),
                  pl.BlockSpec(a1.shape, const2)],
        out_specs=[pl.BlockSpec((None, na, rows, w), lambda bi, j: (bi, 0, j, 0))] * 2,
        out_shape=[jax.ShapeDtypeStruct((b, na, na, w), F32)] * 2,
        scratch_shapes=[pltpu.VMEM((w // LANES, na * rows, LANES), F32)] * 3,
        compiler_params=_cparams(("parallel", "parallel")),
        name="seq_dft_stage1",
    )(x4, a1)
    y = pl.pallas_call(
        functools.partial(_dft2_kernel, rows=rows),
        grid=(b, na // rows),
        in_specs=[pl.BlockSpec((None, rows, na, w), lambda bi, a: (bi, a, 0, 0))] * 2
                 + [pl.BlockSpec((rows,) + g_re.shape[1:], lambda bi, a: (a, 0, 0))] * 2
                 + [pl.BlockSpec(cs.shape, const2)],
        out_specs=pl.BlockSpec((None, na, rows, w), lambda bi, a: (bi, 0, a, 0)),
        out_shape=jax.ShapeDtypeStruct((b, na, na, w), F32),
        scratch_shapes=[pltpu.VMEM((N_FOURIER_GROUPS, na * rows, FOURIER_GROUP_DIM), F32)],
        compiler_params=_cparams(("parallel", "parallel")),
        name="seq_dft_stage2",
    )(tr, ti, g_re, g_im, cs)
    return y.reshape(b, s, w)


def _outproj_kernel(a_ref, f_ref, ga_ref, gb_ref, x_ref, wa_ref, wf_ref, wo_ref, g_ref, wr_ref,
                    xn_ref, h_ref, aff_ref, *, chunk):
    for r in range(x_ref.shape[0] // chunk):
        rs = pl.ds(r * chunk, chunk)
        a = jnp.dot(a_ref[rs, :], wa_ref[...], preferred_element_type=F32)
        fo = jnp.dot(f_ref[rs, :].astype(BF16), wf_ref[...], preferred_element_type=F32)
        merged = ga_ref[rs, :].astype(F32) * a + gb_ref[rs, :].astype(F32) * fo
        xn = x_ref[rs, :] + jnp.dot(merged.astype(BF16), wo_ref[...], preferred_element_type=F32)
        xn_ref[rs, :] = xn
        ms = jnp.mean(xn * xn, axis=-1, keepdims=True)
        h = (xn * lax.rsqrt(ms + EPS)) * g_ref[...]
        dt = h.shape[1] // LANES
        for c in range(dt):
            h_ref[pl.ds(r * chunk * dt + c, chunk, stride=dt), :] = h[:, c * LANES:(c + 1) * LANES]
        logits = _nt_dot(wr_ref[...], h.astype(BF16))
        e = jnp.exp(logits - jnp.max(logits, axis=0, keepdims=True))
        aff_ref[:, rs] = e / jnp.sum(e, axis=0, keepdims=True)


def _outproj(attn, four, ga, gb, x, wa, wf, wo, g, wr_t, *, tm, chunk):
    b, s, d = x.shape
    dt = d // LANES
    tok = lambda bi, i: (bi, i, 0)
    const = lambda bi, i: (0, 0)
    xn, h, aff = pl.pallas_call(
        functools.partial(_outproj_kernel, chunk=chunk),
        grid=(b, s // tm),
        in_specs=[pl.BlockSpec((None, tm, ATTN_WIDTH), tok),
                  pl.BlockSpec((None, tm, FOURIER_WIDTH), tok),
                  pl.BlockSpec((None, tm, d), tok),
                  pl.BlockSpec((None, tm, d), tok),
                  pl.BlockSpec((None, tm, d), tok),
                  pl.BlockSpec((ATTN_WIDTH, d), const),
                  pl.BlockSpec((FOURIER_WIDTH, d), const),
                  pl.BlockSpec((d, d), const),
                  pl.BlockSpec((1, d), const),
                  pl.BlockSpec((N_EXPERTS, d), const)],
        out_specs=[pl.BlockSpec((None, tm, d), tok),
                   pl.BlockSpec((None, tm * dt, LANES), tok),
                   pl.BlockSpec((None, N_EXPERTS, tm), lambda bi, i: (bi, 0, i))],
        out_shape=[jax.ShapeDtypeStruct((b, s, d), F32),
                   jax.ShapeDtypeStruct((b, s * dt, LANES), F32),
                   jax.ShapeDtypeStruct((b, N_EXPERTS, s), F32)],
        compiler_params=_cparams(("parallel", "parallel")),
        name="outproj_router",
    )(attn, four, ga, gb, x, wa, wf, wo, g.reshape(1, d), wr_t.astype(BF16))
    return xn, h.reshape(b, s, dt, LANES), aff


def _tri(n, kind):
    r = lax.broadcasted_iota(I32, (n, n), 0)
    c = lax.broadcasted_iota(I32, (n, n), 1)
    cond = {"row_le_col": r <= c, "row_lt_col": r < c, "col_le_row": c <= r, "col_lt_row": c < r}[kind]
    return jnp.where(cond, 1.0, 0.0).astype(BF16)


def _split128(v):
    hi = jnp.floor(v * (1.0 / LANES))
    return hi, v - hi * LANES


def _routing_kernel(aff_ref, idx_ref, pos_ref, rank_ref, gt_scr, eq_scr, need_scr, *, cap):
    ne, nj, ni = aff_ref.shape
    bits = lax.bitcast_convert_type(aff_ref[...], I32)

    def bisect(it, thr):
        cand = thr | jnp.left_shift(jnp.int32(1), 30 - it)
        cnt = jnp.sum(jnp.where(bits >= cand, 1.0, 0.0), axis=(1, 2), keepdims=True)
        return jnp.where(cnt >= cap, cand, thr)

    thr = lax.fori_loop(0, 31, bisect, jnp.zeros((ne, 1, 1), I32))
    gt = jnp.where(bits > thr, 1.0, 0.0)
    gt_scr[...] = gt
    eq_scr[...] = jnp.where(bits == thr, 1.0, 0.0)
    need = cap - jnp.sum(gt, axis=(1, 2), keepdims=True)
    need_scr[...] = jnp.broadcast_to(need, (ne, SUBLANES, ni))

    ones_sq = jnp.ones((ni, ni), BF16)
    ones_row = jnp.ones((SUBLANES, ni), BF16)
    ones_rowj = jnp.ones((SUBLANES, nj), BF16)
    u_incl = _tri(ni, "row_le_col")
    uj_strict = _tri(nj, "row_lt_col")
    lj_strict = _tri(nj, "col_lt_row")
    l_incl = _tri(ni, "col_le_row")
    bdot = lambda a, b: jnp.dot(a, b, preferred_element_type=F32)

    c_lane = lax.broadcasted_iota(I32, (nj, cap), 1).astype(F32)
    c_row = lax.broadcasted_iota(I32, (SUBLANES, cap), 1).astype(F32)
    j_sub = lax.broadcasted_iota(I32, (nj, cap), 0).astype(F32)

    def slots(e, carry):
        eq = eq_scr[e]
        eqb = eq.astype(BF16)
        tie_rank = bdot(lj_strict, bdot(eqb, ones_sq).astype(BF16)) + bdot(eqb, u_incl)
        take = jnp.where(tie_rank <= need_scr[e][0:1, :], eq, 0.0)
        sel = jnp.maximum(gt_scr[e], take)
        selb = sel.astype(BF16)
        lc = bdot(selb, u_incl)
        lc_t = _nt_dot(l_incl, selb)
        n_rep = bdot(selb, ones_sq)
        s_rep = bdot(lj_strict, n_rep.astype(BF16)) + n_rep
        n_lane = _nt_dot(ones_row, selb)
        pex_lane = bdot(n_lane.astype(BF16), uj_strict)
        s_wide = jnp.concatenate([s_rep] * (cap // ni), axis=1) if cap > ni else s_rep[:, :cap]
        jc = bdot(ones_rowj, jnp.where(s_wide <= c_lane, 1.0, 0.0).astype(BF16))
        onehot = jnp.where(j_sub == jnp.broadcast_to(jc[0:1], (nj, cap)), 1.0, 0.0).astype(BF16)
        phi, plo = _split128(pex_lane)
        r = c_row - (bdot(phi.astype(BF16), onehot) * LANES + bdot(plo.astype(BF16), onehot))
        lcs = bdot(lc_t.astype(BF16), onehot)
        ic = bdot(ones_row, jnp.where(lcs <= jnp.broadcast_to(r[0:1], (ni, cap)), 1.0, 0.0).astype(BF16))
        idx_ref[e] = (jc * ni + ic).astype(I32)
        pos_ref[e] = pex_lane.astype(I32)
        rank_ref[e] = jnp.where(sel > 0.0, lc - 1.0, -1.0)
        return carry

    lax.fori_loop(0, ne, slots, 0)


def _routing(aff, *, cap):
    b, ne, s = aff.shape
    nj = s // LANES
    aff4 = aff.reshape(b, ne, nj, LANES)
    per_b = lambda bi: (bi, 0, 0, 0)
    idx, pos, rank = pl.pallas_call(
        functools.partial(_routing_kernel, cap=cap),
        grid=(b,),
        in_specs=[pl.BlockSpec((None, ne, nj, LANES), per_b)],
        out_specs=[pl.BlockSpec((None, ne, SUBLANES, cap), per_b),
                   pl.BlockSpec((None, ne, SUBLANES, nj), per_b),
                   pl.BlockSpec((None, ne, nj, LANES), per_b)],
        out_shape=[jax.ShapeDtypeStruct((b, ne, SUBLANES, cap), I32),
                   jax.ShapeDtypeStruct((b, ne, SUBLANES, nj), I32),
                   jax.ShapeDtypeStruct((b, ne, nj, LANES), F32)],
        scratch_shapes=[pltpu.VMEM((ne, nj, LANES), F32)] * 2 + [pltpu.VMEM((ne, SUBLANES, LANES), F32)],
        compiler_params=_cparams(("parallel",)),
        name="expert_choice_routing",
    )(aff4)
    return idx[:, :, 0, :], pos[:, :, 0, :], rank


def _ffn_kernel(idx_cur, idx_nxt, h_hbm, wg_ref, wu_ref, wd_ref, y_ref, xs, xsb, gsem,
                *, cap, n_pairs, n_ftiles, per_step, m_chunk):
    p = pl.program_id(0)
    j = pl.program_id(1)
    n_slots = n_ftiles * per_step
    dt = xs.shape[0] // n_slots
    last_pair = n_pairs - 1
    b_cur = p // N_EXPERTS
    b_nxt = jnp.minimum(p + 1, last_pair) // N_EXPERTS

    def gather(ref, bq, c):
        cc = jnp.minimum(c, cap - 1)
        tok = ref[cc >> LANE_BITS, cc & (LANES - 1)]
        dst = xs.at[pl.ds(pl.multiple_of(c * dt, dt), dt)]
        return pltpu.make_async_copy(h_hbm.at[bq, tok], dst, gsem)

    def wait_gathers():
        pltpu.make_async_copy(xs, xs, gsem).wait()

    @pl.when((p == 0) & (j == 0))
    def _():
        def start(c, carry):
            gather(idx_cur, b_cur, c).start()
            return carry

        lax.fori_loop(0, n_slots, start, 0)

    @pl.when(j == 0)
    def _():
        wait_gathers()
        for c in range(dt):
            xsb[:, c * LANES:(c + 1) * LANES] = xs[pl.ds(c, cap, stride=dt), :].astype(BF16)
        y_ref[...] = jnp.zeros_like(y_ref)

    wg = wg_ref[...].astype(BF16)
    wu = wu_ref[...].astype(BF16)
    wd = wd_ref[...].astype(BF16)
    n_groups = cap // m_chunk
    per_group = per_step // n_groups
    for m in range(n_groups):
        for t in range(per_group):
            gather(idx_nxt, b_nxt, j * per_step + (m * per_group + t)).start()
        rows = pl.ds(m * m_chunk, m_chunk)
        x = xsb[rows, :]
        hg = jnp.dot(x, wg, preferred_element_type=F32)
        hu = jnp.dot(x, wu, preferred_element_type=F32)
        act = (hg * _sigmoid(hg)) * hu
        y_ref[rows, :] += jnp.dot(act.astype(BF16), wd, preferred_element_type=F32)

    @pl.when((p == last_pair) & (j == n_ftiles - 1))
    def _():
        wait_gathers()


def _expert_ffn(h, idx, wg, wu, wd, layer, *, cap, tf, m_chunk):
    b, s, dt, _ = h.shape
    d = dt * LANES
    ne, dff = wg.shape[1], wg.shape[3]
    n_ftiles = dff // tf
    n_pairs = b * ne
    n_groups = cap // m_chunk
    per_step = -(-cap // (n_ftiles * n_groups)) * n_groups
    n_slots = n_ftiles * per_step
    tab = idx.reshape(n_pairs, cap // LANES, LANES)
    tab_spec = lambda f: pl.BlockSpec((None, cap // LANES, LANES), lambda p, j: (f(p), 0, 0),
                                      memory_space=pltpu.SMEM)
    return pl.pallas_call(
        functools.partial(_ffn_kernel, cap=cap, n_pairs=n_pairs, n_ftiles=n_ftiles, per_step=per_step,
                          m_chunk=m_chunk),
        grid=(n_pairs, n_ftiles),
        in_specs=[tab_spec(lambda p: p),
                  tab_spec(lambda p: jnp.minimum(p + 1, n_pairs - 1)),
                  pl.BlockSpec(memory_space=pl.ANY),
                  pl.BlockSpec((None, None, d, tf), lambda p, j: (layer, p % ne, 0, j)),
                  pl.BlockSpec((None, None, d, tf), lambda p, j: (layer, p % ne, 0, j)),
                  pl.BlockSpec((None, None, tf, d), lambda p, j: (layer, p % ne, j, 0))],
        out_specs=pl.BlockSpec((None, None, cap, d), lambda p, j: (p // ne, p % ne, 0, 0)),
        out_shape=jax.ShapeDtypeStruct((b, ne, cap, d), F32),
        scratch_shapes=[pltpu.VMEM((n_slots * dt, LANES), F32), pltpu.VMEM((cap, d), BF16),
                        pltpu.SemaphoreType.DMA(())],
        compiler_params=_cparams(("arbitrary", "arbitrary")),
        name="expert_swiglu",
    )(tab, tab, h, wg, wu, wd)


def _combine_kernel(pos_ref, rank_ref, aff_ref, x_ref, g_ref, y_hbm, o_ref, ybuf, acc, sem,
                    *, cap, nj, n_tiles, final_norm):
    b = pl.program_id(0)
    j = pl.program_id(1)
    t = b * nj + j
    ne = N_EXPERTS
    ring = COMBINE_AHEAD + 1
    assert ring & (ring - 1) == 0
    in_ring = lambda k: k & (ring - 1)

    def nominal(bq, e, jq, r):
        return ((pos_ref[bq * ne + e, jq] >> 3) << 3) + r * EWIN

    def window(bq, e, jq, r):
        return pl.multiple_of(jnp.minimum(nominal(bq, e, jq, r), cap - EWIN), SUBLANES)

    def start_fetch(ahead, r, slot):
        jq = j + ahead
        wrap = (jq >= nj).astype(I32)
        bq = b + wrap
        jq = jq - wrap * nj
        past = bq * nj + jq >= n_tiles
        bq = jnp.where(past, n_tiles // nj - 1, bq)
        jq = jnp.where(past, nj - 1, jq)
        for e in range(ne):
            pltpu.make_async_copy(y_hbm.at[bq, e, pl.ds(window(bq, e, jq, r), EWIN)],
                                  ybuf.at[slot, pl.ds(e * EWIN, EWIN)], sem.at[slot]).start()

    def wait_fetch(slot):
        pltpu.make_async_copy(ybuf.at[slot], ybuf.at[slot], sem.at[slot]).wait()

    rk = rank_ref[...]
    af = aff_ref[...]
    row = lax.broadcasted_iota(I32, (EWIN, LANES), 0).astype(F32)

    def add_round(r, slot):
        parts = []
        for e in range(ne):
            p0 = pos_ref[b * ne + e, j]
            local = row + (window(b, e, j, r) - p0).astype(F32)
            want = jnp.where(local >= jnp.maximum(nominal(b, e, j, r) - p0, 0).astype(F32), local, -2.0)
            parts.append(jnp.where(rk[e:e + 1, :] == want, af[e:e + 1, :], 0.0))
        gate_t = jnp.concatenate(parts, axis=0)
        onehot = jnp.where(gate_t != 0.0, 1.0, 0.0).T.astype(BF16)
        scaled = ybuf[slot] * jnp.sum(gate_t, axis=1, keepdims=True)
        hi = scaled.astype(BF16)
        lo = (scaled - hi.astype(F32)).astype(BF16)
        return jnp.dot(onehot, hi, preferred_element_type=F32) + jnp.dot(onehot, lo, preferred_element_type=F32)

    @pl.when(t == 0)
    def _():
        for ahead in range(COMBINE_AHEAD):
            start_fetch(ahead, 0, ahead)

    start_fetch(COMBINE_AHEAD, 0, in_ring(t + COMBINE_AHEAD))
    slot0 = in_ring(t)
    wait_fetch(slot0)
    acc[...] = add_round(0, slot0)

    span = jnp.int32(1)
    for e in range(ne):
        span = jnp.maximum(span, pos_ref[b * ne + e, j + 1] - nominal(b, e, j, 0))
    n_rounds = (span + EWIN - 1) // EWIN

    def extra_round(r, carry):
        start_fetch(0, r, ring)
        wait_fetch(ring)
        acc[...] += add_round(r, ring)
        return carry

    lax.fori_loop(1, n_rounds, extra_round, 0)

    @pl.when(t == n_tiles - 1)
    def _():
        for ahead in range(1, COMBINE_AHEAD + 1):
            wait_fetch(in_ring(t + ahead))

    y = x_ref[...] + acc[...]
    if final_norm:
        ms = jnp.mean(y * y, axis=-1, keepdims=True)
        y = (y * lax.rsqrt(ms + EPS)) * g_ref[...]
    o_ref[...] = y


def _combine(x, y, pos, rank, aff, g_final, *, cap, final_norm):
    b, s, d = x.shape
    nj = s // LANES
    ne = N_EXPERTS
    pos_tab = jnp.concatenate([pos, jnp.full((b, ne, 1), cap, I32)], axis=2).reshape(b * ne, nj + 1)
    rank_t = jnp.transpose(rank, (0, 2, 1, 3))
    return pl.pallas_call(
        functools.partial(_combine_kernel, cap=cap, nj=nj, n_tiles=b * nj, final_norm=final_norm),
        grid_spec=pltpu.PrefetchScalarGridSpec(
            num_scalar_prefetch=1,
            grid=(b, nj),
            in_specs=[pl.BlockSpec((None, None, ne, LANES), lambda bi, j, tab: (bi, j, 0, 0)),
                      pl.BlockSpec((None, ne, LANES), lambda bi, j, tab: (bi, 0, j)),
                      pl.BlockSpec((None, LANES, d), lambda bi, j, tab: (bi, j, 0)),
                      pl.BlockSpec((1, d), lambda bi, j, tab: (0, 0)),
                      pl.BlockSpec(memory_space=pl.ANY)],
            out_specs=pl.BlockSpec((None, LANES, d), lambda bi, j, tab: (bi, j, 0)),
            scratch_shapes=[pltpu.VMEM((COMBINE_AHEAD + 2, ne * EWIN, d), F32), pltpu.VMEM((LANES, d), F32),
                            pltpu.SemaphoreType.DMA((COMBINE_AHEAD + 2,))]),
        out_shape=jax.ShapeDtypeStruct((b, s, d), F32),
        compiler_params=_cparams(("arbitrary", "arbitrary")),
        name="moe_combine",
    )(pos_tab, rank_t, aff, x, g_final.reshape(1, d), y)


def kernel(x, rel_bias, g_mix, w_in, attn_sink, w_attn_proj, w_fourier_proj, w_out, g_ffn, w_router,
           w_exp_gate, w_exp_up, w_exp_down, g_final):
    b, s, d = x.shape
    depth = g_mix.shape[0]
    cap = CAPACITY_FACTOR * s // N_EXPERTS
    bias = _bias_table(rel_bias)
    tables = _fourier_tables(int(round(np.sqrt(s))))
    for l in range(depth):
        q, k, v, f, ga, gb = _inproj(x.reshape(b * s, d), g_mix[l], w_in[l].astype(BF16), tm=512)
        shp = lambda a: a.reshape(b, s, a.shape[-1])
        attn = _attention(shp(q), shp(k), shp(v), attn_sink[l], bias, blocks_per_step=4)
        four = _fourier(shp(f), tables, rows=8)
        xn, h, aff = _outproj(attn, four, shp(ga), shp(gb), x,
                              w_attn_proj[l].astype(BF16), w_fourier_proj[l].astype(BF16),
                              w_out[l].astype(BF16), g_ffn[l], w_router[l].T, tm=512, chunk=256)
        idx, pos, rank = _routing(aff, cap=cap)
        y = _expert_ffn(h, idx, w_exp_gate, w_exp_up, w_exp_down, l, cap=cap, tf=256, m_chunk=min(cap, 512))
        x = _combine(xn, y, pos, rank, aff, g_final, cap=cap, final_norm=(l == depth - 1))
    return x
```

```python
import functools

import numpy as np
import jax
import jax.numpy as jnp
from jax import lax
from jax.experimental import pallas as pl
from jax.experimental.pallas import tpu as pltpu

F32 = jnp.float32
BF16 = jnp.bfloat16
I32 = jnp.int32
HIGHEST = lax.Precision.HIGHEST

N_HEADS = 8
N_KV_HEADS = 2
HEAD_DIM = 64
ATTN_WIDTH = N_HEADS * HEAD_DIM
KV_WIDTH = N_KV_HEADS * HEAD_DIM
WINDOW = 128
BLOCK = 128
N_BUCKETS = 32
MAX_DISTANCE = 128
N_FOURIER_GROUPS = 4
FOURIER_GROUP_DIM = 128
FOURIER_WIDTH = N_FOURIER_GROUPS * FOURIER_GROUP_DIM
N_EXPERTS = 16
CAPACITY_FACTOR = 2
EPS = 1e-6
NEG_INF = -1e30

LANES = 128
LANE_BITS = 7
SUBLANES = 8
VMEM_LIMIT = 56 * 1024 * 1024
EWIN = 32
DFT_PASSES = 1
COMBINE_AHEAD = 3


def _cparams(sem):
    return pltpu.CompilerParams(dimension_semantics=sem, vmem_limit_bytes=VMEM_LIMIT)


def _nt_dot(a, b, **kw):
    return lax.dot_general(a, b, (((1,), (1,)), ((), ())), preferred_element_type=F32, **kw)


def _sigmoid(x):
    return 1.0 / (1.0 + jnp.exp(-x))


def _t5_bucket(rel):
    half = N_BUCKETS // 2
    max_exact = half // 2
    ret = (rel > 0).astype(jnp.int32) * half
    n = jnp.abs(rel)
    nf = jnp.maximum(n, 1).astype(jnp.float32)
    large = max_exact + (jnp.log(nf / max_exact) / np.float32(np.log(MAX_DISTANCE / max_exact))
                         * (half - max_exact)).astype(jnp.int32)
    large = jnp.minimum(large, half - 1)
    return ret + jnp.where(n < max_exact, n, large)


def _bias_kernel(relb_ref, bucket_ref, o_ref):
    bk = bucket_ref[...]
    col = lax.broadcasted_iota(I32, bk.shape, 1)
    for h in range(N_HEADS):
        acc = jnp.full(bk.shape, NEG_INF, F32)
        for b in range(N_BUCKETS):
            acc = jnp.where(bk == b, relb_ref[b, h], acc)
        o_ref[0, h] = jnp.where(col >= BLOCK, acc, NEG_INF)
        o_ref[1, h] = acc
        o_ref[2, h] = jnp.where(col < 2 * BLOCK, acc, NEG_INF)


def _bias_table(rel_bias):
    q_loc = jnp.arange(BLOCK)
    k_loc = jnp.arange(3 * BLOCK) - BLOCK
    rel = k_loc[None, :] - q_loc[:, None]
    bucket = jnp.where(jnp.abs(rel) <= WINDOW, _t5_bucket(rel), -1).astype(I32)
    table = pl.pallas_call(
        _bias_kernel,
        out_shape=jax.ShapeDtypeStruct((3, N_HEADS, BLOCK, 3 * BLOCK), F32),
        in_specs=[pl.BlockSpec(memory_space=pltpu.SMEM),
                  pl.BlockSpec(memory_space=pltpu.VMEM)],
        out_specs=pl.BlockSpec(memory_space=pltpu.VMEM),
        name="rel_bias_table",
    )(rel_bias.astype(F32), bucket)
    return table.reshape(3, N_HEADS * BLOCK, 3 * BLOCK)


def _inproj_kernel(x_ref, g_ref, w_ref, q_ref, k_ref, v_ref, vs_ref, f_ref, ga_ref, gb_ref, *, d_model):
    x = x_ref[...]
    ms = jnp.mean(x * x, axis=-1, keepdims=True)
    h = (x * lax.rsqrt(ms + EPS)) * g_ref[...]
    z = jnp.dot(h.astype(BF16), w_ref[...], preferred_element_type=F32)
    o = 0
    q_ref[...] = (z[:, o:o + ATTN_WIDTH] * (HEAD_DIM ** -0.5)).astype(BF16)
    o += ATTN_WIDTH
    k_ref[...] = z[:, o:o + KV_WIDTH].astype(BF16)
    o += KV_WIDTH
    v_ref[...] = z[:, o:o + KV_WIDTH].astype(BF16)
    half = KV_WIDTH // 2
    vs_ref[...] = jnp.concatenate([z[:, o + half:o + KV_WIDTH], z[:, o:o + half]], axis=1).astype(BF16)
    o += KV_WIDTH
    f_ref[...] = z[:, o:o + FOURIER_WIDTH]
    o += FOURIER_WIDTH
    ga_ref[...] = _sigmoid(z[:, o:o + d_model]).astype(BF16)
    o += d_model
    gb_ref[...] = _sigmoid(z[:, o:o + d_model]).astype(BF16)


def _inproj(x2, g, w_bf16, *, tm):
    n, d = x2.shape
    in_w = w_bf16.shape[1]
    widths = (ATTN_WIDTH, KV_WIDTH, KV_WIDTH, KV_WIDTH, FOURIER_WIDTH, d, d)
    dtypes = (BF16, BF16, BF16, BF16, F32, BF16, BF16)
    row = lambda i: (i, 0)
    return pl.pallas_call(
        functools.partial(_inproj_kernel, d_model=d),
        grid=(n // tm,),
        in_specs=[pl.BlockSpec((tm, d), row),
                  pl.BlockSpec((1, d), lambda i: (0, 0)),
                  pl.BlockSpec((d, in_w), lambda i: (0, 0))],
        out_specs=[pl.BlockSpec((tm, w), row) for w in widths],
        out_shape=[jax.ShapeDtypeStruct((n, w), dt) for w, dt in zip(widths, dtypes)],
        compiler_params=_cparams(("parallel",)),
        name="inproj",
    )(x2, g.reshape(1, d), w_bf16)


def _attn_kernel(sink_ref, q_ref, kp_ref, ko_ref, kn_ref, vp_ref, vo_ref, vn_ref, wp_ref, wo_ref, wn_ref,
                 bias_ref, o_ref, *, n_blocks, blocks_per_step):
    i = pl.program_id(1)
    kcat = jnp.concatenate([kp_ref[...], ko_ref[...], kn_ref[...]], axis=0)
    vcat = jnp.concatenate([vp_ref[...], vo_ref[...], vn_ref[...]], axis=0)
    wcat = jnp.concatenate([wp_ref[...], wo_ref[...], wn_ref[...]], axis=0)
    group = N_HEADS // N_KV_HEADS
    head = lambda a, h: a[:, h * HEAD_DIM:(h + 1) * HEAD_DIM]
    sink = jnp.concatenate([jnp.full((BLOCK, LANES), sink_ref[h], F32) for h in range(N_HEADS)], axis=0)
    low_v = lax.broadcasted_iota(I32, vcat.shape, 1) < HEAD_DIM
    low_o = lax.broadcasted_iota(I32, (BLOCK, LANES), 1) < HEAD_DIM
    vx = {(kv, half): jnp.where(low_v if half == 0 else jnp.logical_not(low_v),
                                vcat if kv == half else wcat, jnp.ones_like(vcat))
          for kv in range(N_KV_HEADS) for half in range(2)}
    for t in range(blocks_per_step):
        blk = i * blocks_per_step + t
        kind = jnp.where(blk == 0, 0, jnp.where(blk == n_blocks - 1, 2, 1))
        kw = kcat[t * BLOCK:(t + 3) * BLOCK]
        qb = q_ref[t * BLOCK:(t + 1) * BLOCK, :]
        s = jnp.concatenate([_nt_dot(head(qb, h), head(kw, h // group)) for h in range(N_HEADS)], axis=0)
        s = s + bias_ref[kind]
        m = jnp.maximum(jnp.broadcast_to(jnp.max(s, axis=-1, keepdims=True), sink.shape), sink)
        p = jnp.exp(s - jnp.concatenate([m] * 3, axis=1)).astype(BF16)
        sink_term = jnp.exp(sink - m)
        for pair in range(N_HEADS // 2):
            halves = []
            for half in range(2):
                h = 2 * pair + half
                rows = slice(h * BLOCK, (h + 1) * BLOCK)
                out = jnp.dot(p[rows], vx[h // group, half][t * BLOCK:(t + 3) * BLOCK],
                              preferred_element_type=F32)
                denom = pltpu.roll(out, HEAD_DIM, axis=1) + sink_term[rows]
                halves.append(out * (1.0 / denom))
            tile = jnp.where(low_o, halves[0], halves[1])
            o_ref[t * BLOCK:(t + 1) * BLOCK, pair * LANES:(pair + 1) * LANES] = tile.astype(o_ref.dtype)


def _attention(q, k, v, v_swapped, sink, bias, *, blocks_per_step):
    b, s, _ = q.shape
    assert N_KV_HEADS == 2 and KV_WIDTH == LANES and N_HEADS % 2 == 0
    n_blocks = s // BLOCK
    tq = blocks_per_step * BLOCK
    prev = lambda bi, i: (bi, jnp.maximum(i * blocks_per_step - 1, 0), 0)
    own = lambda bi, i: (bi, i, 0)
    nxt = lambda bi, i: (bi, jnp.minimum((i + 1) * blocks_per_step, n_blocks - 1), 0)
    kv_specs = [pl.BlockSpec((None, BLOCK, KV_WIDTH), prev),
                pl.BlockSpec((None, tq, KV_WIDTH), own),
                pl.BlockSpec((None, BLOCK, KV_WIDTH), nxt)]
    return pl.pallas_call(
        functools.partial(_attn_kernel, n_blocks=n_blocks, blocks_per_step=blocks_per_step),
        grid=(b, s // tq),
        in_specs=[pl.BlockSpec(memory_space=pltpu.SMEM),
                  pl.BlockSpec((None, tq, ATTN_WIDTH), own)] + kv_specs * 3
                 + [pl.BlockSpec((3, N_HEADS * BLOCK, 3 * BLOCK), lambda bi, i: (0, 0, 0))],
        out_specs=pl.BlockSpec((None, tq, ATTN_WIDTH), own),
        out_shape=jax.ShapeDtypeStruct((b, s, ATTN_WIDTH), BF16),
        compiler_params=_cparams(("parallel", "parallel")),
        name="window_attention",
    )(sink.astype(F32), q, k, k, k, v, v, v, v_swapped, v_swapped, v_swapped, bias)


def _hi_lo(x):
    hi = x.astype(BF16)
    return hi, (x - hi.astype(F32)).astype(BF16)


def _dft_operand(x, axis):
    if DFT_PASSES == 1:
        return x.astype(BF16)
    hi, lo = _hi_lo(x)
    return jnp.concatenate([hi, lo, hi], axis=axis)


def _dft1_kernel(x_ref, a_ref, tr_ref, ti_ref, x2, *, rows):
    a = a_ref[...]
    na = a.shape[0] // 2
    n_slabs = x2.shape[0]
    lanes = lambda c: slice(c * LANES, (c + 1) * LANES)
    for c in range(n_slabs):
        x2[c] = x_ref[:, :, lanes(c)].reshape(x2.shape[1:])
    for jj in range(rows):
        strided = pl.ds(jj, na, stride=rows)
        xj = jnp.concatenate([x2[c, strided, :] for c in range(n_slabs)], axis=1)
        t = jnp.dot(a, _dft_operand(xj, 0), preferred_element_type=F32)
        tr_ref[:, jj, :] = t[:na]
        ti_ref[:, jj, :] = t[na:]


def _dft2_kernel(tr_ref, ti_ref, gr_ref, gi_ref, cs_ref, o_ref, o2, *, rows):
    cs = cs_ref[...]
    na = o_ref.shape[0]
    for aa in range(rows):
        t = _dft_operand(jnp.concatenate([tr_ref[aa], ti_ref[aa]], axis=0), 0)
        zr = jnp.dot(gr_ref[aa], t, preferred_element_type=F32)
        zi = jnp.dot(gi_ref[aa], t, preferred_element_type=F32)
        for g in range(N_FOURIER_GROUPS):
            sl = slice(g * FOURIER_GROUP_DIM, (g + 1) * FOURIER_GROUP_DIM)
            z = _dft_operand(jnp.concatenate([zr[:, sl], zi[:, sl]], axis=1), 1)
            y = jnp.dot(z, cs, preferred_element_type=F32)
            o2[g, pl.ds(aa, na, stride=rows), :] = y
    for g in range(N_FOURIER_GROUPS):
        sl = slice(g * FOURIER_GROUP_DIM, (g + 1) * FOURIER_GROUP_DIM)
        o_ref[:, :, sl] = o2[g].reshape(na, rows, FOURIER_GROUP_DIM)


def _fourier_tables(na):
    def pieces(a, axis):
        hi = a.astype(BF16)
        if DFT_PASSES == 1:
            return jnp.asarray(hi)
        lo = (a - hi.astype(np.float64)).astype(BF16)
        return jnp.asarray(np.concatenate([hi, hi, lo], axis=axis))

    n = na * na
    k = np.arange(na, dtype=np.float64)
    ang = 2.0 * np.pi * np.outer(k, k) / na
    a1 = pieces(np.concatenate([np.cos(ang), -np.sin(ang)], axis=0), 1)
    ka = np.arange(na, dtype=np.float64)[:, None, None]
    kb = np.arange(na, dtype=np.float64)[None, :, None]
    nl = np.arange(na, dtype=np.float64)[None, None, :]
    th = 2.0 * np.pi * ((nl * (ka + na * kb)) % n) / n
    gr, gi = np.cos(th), -np.sin(th)
    g_re = pieces(np.concatenate([gr, -gi], axis=2), 2)
    g_im = pieces(np.concatenate([gi, gr], axis=2), 2)
    kc = np.arange(FOURIER_GROUP_DIM, dtype=np.float64)
    angc = 2.0 * np.pi * np.outer(kc, kc) / FOURIER_GROUP_DIM
    scale = 1.0 / np.sqrt(float(n) * FOURIER_GROUP_DIM)
    cs = pieces(np.concatenate([np.cos(angc), np.sin(angc)], axis=0) * scale, 0)
    return a1, g_re, g_im, cs


def _fourier(f, tables, *, rows):
    b, s, w = f.shape
    na = int(round(np.sqrt(s)))
    assert na * na == s and na % rows == 0
    a1, g_re, g_im, cs = tables
    x4 = f.reshape(b, na, na, w)
    const2 = lambda bi, j: (0, 0)
    tr, ti = pl.pallas_call(
        functools.partial(_dft1_kernel, rows=rows),
        grid=(b, na // rows),
        in_specs=[pl.BlockSpec((None, na, rows, w), lambda bi, j: (bi, 0, j, 0)),
                  pl.BlockSpec(a1.shape, const2)],
        out_specs=[pl.BlockSpec((None, na, rows, w), lambda bi, j: (bi, 0, j, 0))] * 2,
        out_shape=[jax.ShapeDtypeStruct((b, na, na, w), F32)] * 2,
        scratch_shapes=[pltpu.VMEM((w // LANES, na * rows, LANES), F32)],
        compiler_params=_cparams(("parallel", "parallel")),
        name="seq_dft_stage1",
    )(x4, a1)
    y = pl.pallas_call(
        functools.partial(_dft2_kernel, rows=rows),
        grid=(b, na // rows),
        in_specs=[pl.BlockSpec((None, rows, na, w), lambda bi, a: (bi, a, 0, 0))] * 2
                 + [pl.BlockSpec((rows,) + g_re.shape[1:], lambda bi, a: (a, 0, 0))] * 2
                 + [pl.BlockSpec(cs.shape, const2)],
        out_specs=pl.BlockSpec((None, na, rows, w), lambda bi, a: (bi, 0, a, 0)),
        out_shape=jax.ShapeDtypeStruct((b, na, na, w), F32),
        scratch_shapes=[pltpu.VMEM((N_FOURIER_GROUPS, na * rows, FOURIER_GROUP_DIM), F32)],
        compiler_params=_cparams(("parallel", "parallel")),
        name="seq_dft_stage2",
    )(tr, ti, g_re, g_im, cs)
    return y.reshape(b, s, w)


def _outproj_kernel(a_ref, f_ref, ga_ref, gb_ref, x_ref, wa_ref, wf_ref, wo_ref, g_ref, wr_ref,
                    xn_ref, h_ref, aff_ref, *, chunk):
    for r in range(x_ref.shape[0] // chunk):
        rs = pl.ds(r * chunk, chunk)
        a = jnp.dot(a_ref[rs, :], wa_ref[...], preferred_element_type=F32)
        fo = jnp.dot(f_ref[rs, :].astype(BF16), wf_ref[...], preferred_element_type=F32)
        merged = ga_ref[rs, :].astype(F32) * a + gb_ref[rs, :].astype(F32) * fo
        xn = x_ref[rs, :] + jnp.dot(merged.astype(BF16), wo_ref[...], preferred_element_type=F32)
        xn_ref[rs, :] = xn
        ms = jnp.mean(xn * xn, axis=-1, keepdims=True)
        h = (xn * lax.rsqrt(ms + EPS)) * g_ref[...]
        dt = h.shape[1] // LANES
        for c in range(dt):
            h_ref[pl.ds(r * chunk * dt + c, chunk, stride=dt), :] = h[:, c * LANES:(c + 1) * LANES]
        logits = _nt_dot(wr_ref[...], h.astype(BF16))
        e = jnp.exp(logits - jnp.max(logits, axis=0, keepdims=True))
        aff_ref[:, rs] = e / jnp.sum(e, axis=0, keepdims=True)


def _outproj(attn, four, ga, gb, x, wa, wf, wo, g, wr_t, *, tm, chunk):
    b, s, d = x.shape
    dt = d // LANES
    tok = lambda bi, i: (bi, i, 0)
    const = lambda bi, i: (0, 0)
    xn, h, aff = pl.pallas_call(
        functools.partial(_outproj_kernel, chunk=chunk),
        grid=(b, s // tm),
        in_specs=[pl.BlockSpec((None, tm, ATTN_WIDTH), tok),
                  pl.BlockSpec((None, tm, FOURIER_WIDTH), tok),
                  pl.BlockSpec((None, tm, d), tok),
                  pl.BlockSpec((None, tm, d), tok),
                  pl.BlockSpec((None, tm, d), tok),
                  pl.BlockSpec((ATTN_WIDTH, d), const),
                  pl.BlockSpec((FOURIER_WIDTH, d), const),
                  pl.BlockSpec((d, d), const),
                  pl.BlockSpec((1, d), const),
                  pl.BlockSpec((N_EXPERTS, d), const)],
        out_specs=[pl.BlockSpec((None, tm, d), tok),
                   pl.BlockSpec((None, tm * dt, LANES), tok),
                   pl.BlockSpec((None, N_EXPERTS, tm), lambda bi, i: (bi, 0, i))],
        out_shape=[jax.ShapeDtypeStruct((b, s, d), F32),
                   jax.ShapeDtypeStruct((b, s * dt, LANES), F32),
                   jax.ShapeDtypeStruct((b, N_EXPERTS, s), F32)],
        compiler_params=_cparams(("parallel", "parallel")),
        name="outproj_router",
    )(attn, four, ga, gb, x, wa, wf, wo, g.reshape(1, d), wr_t.astype(BF16))
    return xn, h.reshape(b, s, dt, LANES), aff


def _tri(n, kind):
    r = lax.broadcasted_iota(I32, (n, n), 0)
    c = lax.broadcasted_iota(I32, (n, n), 1)
    cond = {"row_le_col": r <= c, "row_lt_col": r < c, "col_le_row": c <= r, "col_lt_row": c < r}[kind]
    return jnp.where(cond, 1.0, 0.0).astype(BF16)


def _split128(v):
    hi = jnp.floor(v * (1.0 / LANES))
    return hi, v - hi * LANES


def _routing_kernel(aff_ref, idx_ref, pos_ref, rank_ref, gt_scr, eq_scr, need_scr, *, cap):
    ne, nj, ni = aff_ref.shape
    bits = lax.bitcast_convert_type(aff_ref[...], I32)

    def bisect(it, thr):
        cand = thr | jnp.left_shift(jnp.int32(1), 30 - it)
        cnt = jnp.sum(jnp.where(bits >= cand, 1.0, 0.0), axis=(1, 2), keepdims=True)
        return jnp.where(cnt >= cap, cand, thr)

    thr = lax.fori_loop(0, 31, bisect, jnp.zeros((ne, 1, 1), I32))
    gt = jnp.where(bits > thr, 1.0, 0.0)
    gt_scr[...] = gt
    eq_scr[...] = jnp.where(bits == thr, 1.0, 0.0)
    need = cap - jnp.sum(gt, axis=(1, 2), keepdims=True)
    need_scr[...] = jnp.broadcast_to(need, (ne, SUBLANES, ni))

    ones_sq = jnp.ones((ni, ni), BF16)
    ones_row = jnp.ones((SUBLANES, ni), BF16)
    ones_rowj = jnp.ones((SUBLANES, nj), BF16)
    u_incl = _tri(ni, "row_le_col")
    uj_strict = _tri(nj, "row_lt_col")
    lj_strict = _tri(nj, "col_lt_row")
    l_incl = _tri(ni, "col_le_row")
    bdot = lambda a, b: jnp.dot(a, b, preferred_element_type=F32)

    c_lane = lax.broadcasted_iota(I32, (nj, cap), 1).astype(F32)
    c_row = lax.broadcasted_iota(I32, (SUBLANES, cap), 1).astype(F32)
    j_sub = lax.broadcasted_iota(I32, (nj, cap), 0).astype(F32)

    def slots(e, carry):
        eq = eq_scr[e]
        eqb = eq.astype(BF16)
        tie_rank = bdot(lj_strict, bdot(eqb, ones_sq).astype(BF16)) + bdot(eqb, u_incl)
        take = jnp.where(tie_rank <= need_scr[e][0:1, :], eq, 0.0)
        sel = jnp.maximum(gt_scr[e], take)
        selb = sel.astype(BF16)
        lc = bdot(selb, u_incl)
        lc_t = _nt_dot(l_incl, selb)
        n_rep = bdot(selb, ones_sq)
        s_rep = bdot(lj_strict, n_rep.astype(BF16)) + n_rep
        n_lane = _nt_dot(ones_row, selb)
        pex_lane = bdot(n_lane.astype(BF16), uj_strict)
        s_wide = jnp.concatenate([s_rep] * (cap // ni), axis=1) if cap > ni else s_rep[:, :cap]
        jc = bdot(ones_rowj, jnp.where(s_wide <= c_lane, 1.0, 0.0).astype(BF16))
        onehot = jnp.where(j_sub == jnp.broadcast_to(jc[0:1], (nj, cap)), 1.0, 0.0).astype(BF16)
        phi, plo = _split128(pex_lane)
        r = c_row - (bdot(phi.astype(BF16), onehot) * LANES + bdot(plo.astype(BF16), onehot))
        lcs = bdot(lc_t.astype(BF16), onehot)
        ic = bdot(ones_row, jnp.where(lcs <= jnp.broadcast_to(r[0:1], (ni, cap)), 1.0, 0.0).astype(BF16))
        idx_ref[e] = (jc * ni + ic).astype(I32)
        pos_ref[e] = pex_lane.astype(I32)
        rank_ref[e] = jnp.where(sel > 0.0, lc - 1.0, -1.0)
        return carry

    lax.fori_loop(0, ne, slots, 0)


def _routing(aff, *, cap):
    b, ne, s = aff.shape
    nj = s // LANES
    aff4 = aff.reshape(b, ne, nj, LANES)
    per_b = lambda bi: (bi, 0, 0, 0)
    idx, pos, rank = pl.pallas_call(
        functools.partial(_routing_kernel, cap=cap),
        grid=(b,),
        in_specs=[pl.BlockSpec((None, ne, nj, LANES), per_b)],
        out_specs=[pl.BlockSpec((None, ne, SUBLANES, cap), per_b),
                   pl.BlockSpec((None, ne, SUBLANES, nj), per_b),
                   pl.BlockSpec((None, ne, nj, LANES), per_b)],
        out_shape=[jax.ShapeDtypeStruct((b, ne, SUBLANES, cap), I32),
                   jax.ShapeDtypeStruct((b, ne, SUBLANES, nj), I32),
                   jax.ShapeDtypeStruct((b, ne, nj, LANES), F32)],
        scratch_shapes=[pltpu.VMEM((ne, nj, LANES), F32)] * 2 + [pltpu.VMEM((ne, SUBLANES, LANES), F32)],
        compiler_params=_cparams(("parallel",)),
        name="expert_choice_routing",
    )(aff4)
    return idx[:, :, 0, :], pos[:, :, 0, :], rank


def _ffn_kernel(idx_cur, idx_nxt, h_hbm, wg_ref, wu_ref, wd_ref, y_ref, xs, xsb, gsem,
                *, cap, n_pairs, n_ftiles, per_step, m_chunk):
    p = pl.program_id(0)
    j = pl.program_id(1)
    n_slots = n_ftiles * per_step
    dt = xs.shape[0] // n_slots
    last_pair = n_pairs - 1
    b_cur = p // N_EXPERTS
    b_nxt = jnp.minimum(p + 1, last_pair) // N_EXPERTS

    def gather(ref, bq, c):
        cc = jnp.minimum(c, cap - 1)
        tok = ref[cc >> LANE_BITS, cc & (LANES - 1)]
        dst = xs.at[pl.ds(pl.multiple_of(c * dt, dt), dt)]
        return pltpu.make_async_copy(h_hbm.at[bq, tok], dst, gsem)

    def wait_gathers():
        pltpu.make_async_copy(xs, xs, gsem).wait()

    @pl.when((p == 0) & (j == 0))
    def _():
        def start(c, carry):
            gather(idx_cur, b_cur, c).start()
            return carry

        lax.fori_loop(0, n_slots, start, 0)

    @pl.when(j == 0)
    def _():
        wait_gathers()
        for c in range(dt):
            xsb[:, c * LANES:(c + 1) * LANES] = xs[pl.ds(c, cap, stride=dt), :].astype(BF16)
        y_ref[...] = jnp.zeros_like(y_ref)

    wg = wg_ref[...].astype(BF16)
    wu = wu_ref[...].astype(BF16)
    wd = wd_ref[...].astype(BF16)
    n_groups = cap // m_chunk
    per_group = per_step // n_groups
    for m in range(n_groups):
        for t in range(per_group):
            gather(idx_nxt, b_nxt, j * per_step + (m * per_group + t)).start()
        rows = pl.ds(m * m_chunk, m_chunk)
        x = xsb[rows, :]
        hg = jnp.dot(x, wg, preferred_element_type=F32)
        hu = jnp.dot(x, wu, preferred_element_type=F32)
        act = (hg * _sigmoid(hg)) * hu
        y_ref[rows, :] += jnp.dot(act.astype(BF16), wd, preferred_element_type=F32)

    @pl.when((p == last_pair) & (j == n_ftiles - 1))
    def _():
        wait_gathers()


def _expert_ffn(h, idx, wg, wu, wd, layer, *, cap, tf, m_chunk):
    b, s, dt, _ = h.shape
    d = dt * LANES
    ne, dff = wg.shape[1], wg.shape[3]
    n_ftiles = dff // tf
    n_pairs = b * ne
    n_groups = cap // m_chunk
    per_step = -(-cap // (n_ftiles * n_groups)) * n_groups
    n_slots = n_ftiles * per_step
    tab = idx.reshape(n_pairs, cap // LANES, LANES)
    tab_spec = lambda f: pl.BlockSpec((None, cap // LANES, LANES), lambda p, j: (f(p), 0, 0),
                                      memory_space=pltpu.SMEM)
    return pl.pallas_call(
        functools.partial(_ffn_kernel, cap=cap, n_pairs=n_pairs, n_ftiles=n_ftiles, per_step=per_step,
                          m_chunk=m_chunk),
        grid=(n_pairs, n_ftiles),
        in_specs=[tab_spec(lambda p: p),
                  tab_spec(lambda p: jnp.minimum(p + 1, n_pairs - 1)),
                  pl.BlockSpec(memory_space=pl.ANY),
                  pl.BlockSpec((None, None, d, tf), lambda p, j: (layer, p % ne, 0, j)),
                  pl.BlockSpec((None, None, d, tf), lambda p, j: (layer, p % ne, 0, j)),
                  pl.BlockSpec((None, None, tf, d), lambda p, j: (layer, p % ne, j, 0))],
        out_specs=pl.BlockSpec((None, None, cap, d), lambda p, j: (p // ne, p % ne, 0, 0)),
        out_shape=jax.ShapeDtypeStruct((b, ne, cap, d), F32),
        scratch_shapes=[pltpu.VMEM((n_slots * dt, LANES), F32), pltpu.VMEM((cap, d), BF16),
                        pltpu.SemaphoreType.DMA(())],
        compiler_params=_cparams(("arbitrary", "arbitrary")),
        name="expert_swiglu",
    )(tab, tab, h, wg, wu, wd)


def _combine_kernel(pos_ref, rank_ref, aff_ref, x_ref, g_ref, y_hbm, o_ref, ybuf, acc, sem,
                    *, cap, nj, n_tiles, final_norm):
    b = pl.program_id(0)
    j = pl.program_id(1)
    t = b * nj + j
    ne = N_EXPERTS
    ring = COMBINE_AHEAD + 1
    assert ring & (ring - 1) == 0
    in_ring = lambda k: k & (ring - 1)

    def nominal(bq, e, jq, r):
        return ((pos_ref[bq * ne + e, jq] >> 3) << 3) + r * EWIN

    def window(bq, e, jq, r):
        return pl.multiple_of(jnp.minimum(nominal(bq, e, jq, r), cap - EWIN), SUBLANES)

    def start_fetch(ahead, r, slot):
        jq = j + ahead
        wrap = (jq >= nj).astype(I32)
        bq = b + wrap
        jq = jq - wrap * nj
        past = bq * nj + jq >= n_tiles
        bq = jnp.where(past, n_tiles // nj - 1, bq)
        jq = jnp.where(past, nj - 1, jq)
        for e in range(ne):
            pltpu.make_async_copy(y_hbm.at[bq, e, pl.ds(window(bq, e, jq, r), EWIN)],
                                  ybuf.at[slot, pl.ds(e * EWIN, EWIN)], sem.at[slot]).start()

    def wait_fetch(slot):
        pltpu.make_async_copy(ybuf.at[slot], ybuf.at[slot], sem.at[slot]).wait()

    rk = rank_ref[...]
    af = aff_ref[...]
    row = lax.broadcasted_iota(I32, (EWIN, LANES), 0).astype(F32)

    def add_round(r, slot):
        parts = []
        for e in range(ne):
            p0 = pos_ref[b * ne + e, j]
            local = row + (window(b, e, j, r) - p0).astype(F32)
            want = jnp.where(local >= jnp.maximum(nominal(b, e, j, r) - p0, 0).astype(F32), local, -2.0)
            parts.append(jnp.where(rk[e:e + 1, :] == want, af[e:e + 1, :], 0.0))
        gate_t = jnp.concatenate(parts, axis=0)
        onehot = jnp.where(gate_t != 0.0, 1.0, 0.0).T.astype(BF16)
        scaled = ybuf[slot] * jnp.sum(gate_t, axis=1, keepdims=True)
        hi = scaled.astype(BF16)
        lo = (scaled - hi.astype(F32)).astype(BF16)
        return jnp.dot(onehot, hi, preferred_element_type=F32) + jnp.dot(onehot, lo, preferred_element_type=F32)

    @pl.when(t == 0)
    def _():
        for ahead in range(COMBINE_AHEAD):
            start_fetch(ahead, 0, ahead)

    start_fetch(COMBINE_AHEAD, 0, in_ring(t + COMBINE_AHEAD))
    slot0 = in_ring(t)
    wait_fetch(slot0)
    acc[...] = add_round(0, slot0)

    span = jnp.int32(1)
    for e in range(ne):
        span = jnp.maximum(span, pos_ref[b * ne + e, j + 1] - nominal(b, e, j, 0))
    n_rounds = (span + EWIN - 1) // EWIN

    def extra_round(r, carry):
        start_fetch(0, r, ring)
        wait_fetch(ring)
        acc[...] += add_round(r, ring)
        return carry

    lax.fori_loop(1, n_rounds, extra_round, 0)

    @pl.when(t == n_tiles - 1)
    def _():
        for ahead in range(1, COMBINE_AHEAD + 1):
            wait_fetch(in_ring(t + ahead))

    y = x_ref[...] + acc[...]
    if final_norm:
        ms = jnp.mean(y * y, axis=-1, keepdims=True)
        y = (y * lax.rsqrt(ms + EPS)) * g_ref[...]
    o_ref[...] = y


def _combine(x, y, pos, rank, aff, g_final, *, cap, final_norm):
    b, s, d = x.shape
    nj = s // LANES
    ne = N_EXPERTS
    pos_tab = jnp.concatenate([pos, jnp.full((b, ne, 1), cap, I32)], axis=2).reshape(b * ne, nj + 1)
    rank_t = jnp.transpose(rank, (0, 2, 1, 3))
    return pl.pallas_call(
        functools.partial(_combine_kernel, cap=cap, nj=nj, n_tiles=b * nj, final_norm=final_norm),
        grid_spec=pltpu.PrefetchScalarGridSpec(
            num_scalar_prefetch=1,
            grid=(b, nj),
            in_specs=[pl.BlockSpec((None, None, ne, LANES), lambda bi, j, tab: (bi, j, 0, 0)),
                      pl.BlockSpec((None, ne, LANES), lambda bi, j, tab: (bi, 0, j)),
                      pl.BlockSpec((None, LANES, d), lambda bi, j, tab: (bi, j, 0)),
                      pl.BlockSpec((1, d), lambda bi, j, tab: (0, 0)),
                      pl.BlockSpec(memory_space=pl.ANY)],
            out_specs=pl.BlockSpec((None, LANES, d), lambda bi, j, tab: (bi, j, 0)),
            scratch_shapes=[pltpu.VMEM((COMBINE_AHEAD + 2, ne * EWIN, d), F32), pltpu.VMEM((LANES, d), F32),
                            pltpu.SemaphoreType.DMA((COMBINE_AHEAD + 2,))]),
        out_shape=jax.ShapeDtypeStruct((b, s, d), F32),
        compiler_params=_cparams(("arbitrary", "arbitrary")),
        name="moe_combine",
    )(pos_tab, rank_t, aff, x, g_final.reshape(1, d), y)


def kernel(x, rel_bias, g_mix, w_in, attn_sink, w_attn_proj, w_fourier_proj, w_out, g_ffn, w_router,
           w_exp_gate, w_exp_up, w_exp_down, g_final):
    b, s, d = x.shape
    depth = g_mix.shape[0]
    cap = CAPACITY_FACTOR * s // N_EXPERTS
    bias = _bias_table(rel_bias)
    tables = _fourier_tables(int(round(np.sqrt(s))))
    for l in range(depth):
        q, k, v, vs, f, ga, gb = _inproj(x.reshape(b * s, d), g_mix[l], w_in[l].astype(BF16), tm=512)
        shp = lambda a: a.reshape(b, s, a.shape[-1])
        attn = _attention(shp(q), shp(k), shp(v), shp(vs), attn_sink[l], bias, blocks_per_step=4)
        four = _fourier(shp(f), tables, rows=8)
        xn, h, aff = _outproj(attn, four, shp(ga), shp(gb), x,
                              w_attn_proj[l].astype(BF16), w_fourier_proj[l].astype(BF16),
                              w_out[l].astype(BF16), g_ffn[l], w_router[l].T, tm=512, chunk=256)
        idx, pos, rank = _routing(aff, cap=cap)
        y = _expert_ffn(h, idx, w_exp_gate, w_exp_up, w_exp_down, l, cap=cap, tf=256, m_chunk=min(cap, 512))
        x = _combine(xn, y, pos, rank, aff, g_final, cap=cap, final_norm=(l == depth - 1))
    return x
```

```python
import functools

import numpy as np
import jax
import jax.numpy as jnp
from jax import lax
from jax.experimental import pallas as pl
from jax.experimental.pallas import tpu as pltpu

F32 = jnp.float32
BF16 = jnp.bfloat16
I32 = jnp.int32
HIGHEST = lax.Precision.HIGHEST

N_HEADS = 8
N_KV_HEADS = 2
HEAD_DIM = 64
ATTN_WIDTH = N_HEADS * HEAD_DIM
KV_WIDTH = N_KV_HEADS * HEAD_DIM
WINDOW = 128
BLOCK = 128
N_BUCKETS = 32
MAX_DISTANCE = 128
N_FOURIER_GROUPS = 4
FOURIER_GROUP_DIM = 128
FOURIER_WIDTH = N_FOURIER_GROUPS * FOURIER_GROUP_DIM
N_EXPERTS = 16
CAPACITY_FACTOR = 2
EPS = 1e-6
NEG_INF = -1e30

LANES = 128
LANE_BITS = 7
SUBLANES = 8
VMEM_LIMIT = 56 * 1024 * 1024
EWIN = 32
DFT_PASSES = 1
COMBINE_AHEAD = 3


def _cparams(sem):
    return pltpu.CompilerParams(dimension_semantics=sem, vmem_limit_bytes=VMEM_LIMIT)


def _nt_dot(a, b, **kw):
    return lax.dot_general(a, b, (((1,), (1,)), ((), ())), preferred_element_type=F32, **kw)


def _sigmoid(x):
    return 1.0 / (1.0 + jnp.exp(-x))


def _t5_bucket(rel):
    half = N_BUCKETS // 2
    max_exact = half // 2
    ret = (rel > 0).astype(jnp.int32) * half
    n = jnp.abs(rel)
    nf = jnp.maximum(n, 1).astype(jnp.float32)
    large = max_exact + (jnp.log(nf / max_exact) / np.float32(np.log(MAX_DISTANCE / max_exact))
                         * (half - max_exact)).astype(jnp.int32)
    large = jnp.minimum(large, half - 1)
    return ret + jnp.where(n < max_exact, n, large)


def _bias_kernel(relb_ref, bucket_ref, o_ref):
    bk = bucket_ref[...]
    col = lax.broadcasted_iota(I32, bk.shape, 1)
    for h in range(N_HEADS):
        acc = jnp.full(bk.shape, NEG_INF, F32)
        for b in range(N_BUCKETS):
            acc = jnp.where(bk == b, relb_ref[b, h], acc)
        o_ref[0, h] = jnp.where(col >= BLOCK, acc, NEG_INF)
        o_ref[1, h] = acc
        o_ref[2, h] = jnp.where(col < 2 * BLOCK, acc, NEG_INF)


def _bias_table(rel_bias):
    q_loc = jnp.arange(BLOCK)
    k_loc = jnp.arange(3 * BLOCK) - BLOCK
    rel = k_loc[None, :] - q_loc[:, None]
    bucket = jnp.where(jnp.abs(rel) <= WINDOW, _t5_bucket(rel), -1).astype(I32)
    table = pl.pallas_call(
        _bias_kernel,
        out_shape=jax.ShapeDtypeStruct((3, N_HEADS, BLOCK, 3 * BLOCK), F32),
        in_specs=[pl.BlockSpec(memory_space=pltpu.SMEM),
                  pl.BlockSpec(memory_space=pltpu.VMEM)],
        out_specs=pl.BlockSpec(memory_space=pltpu.VMEM),
        name="rel_bias_table",
    )(rel_bias.astype(F32), bucket)
    return table.reshape(3, N_HEADS * BLOCK, 3 * BLOCK)


def _inproj_kernel(x_ref, g_ref, w_ref, q_ref, k_ref, v_ref, vs_ref, f_ref, ga_ref, gb_ref, *, d_model, chunk):
    for r in range(x_ref.shape[0] // chunk):
        rs = pl.ds(r * chunk, chunk)
        x = x_ref[rs, :]
        ms = jnp.mean(x * x, axis=-1, keepdims=True)
        h = (x * lax.rsqrt(ms + EPS)) * g_ref[...]
        z = jnp.dot(h.astype(BF16), w_ref[...], preferred_element_type=F32)
        o = 0
        q_ref[rs, :] = (z[:, o:o + ATTN_WIDTH] * (HEAD_DIM ** -0.5)).astype(BF16)
        o += ATTN_WIDTH
        k_ref[rs, :] = z[:, o:o + KV_WIDTH].astype(BF16)
        o += KV_WIDTH
        v_ref[rs, :] = z[:, o:o + KV_WIDTH].astype(BF16)
        half = KV_WIDTH // 2
        vs_ref[rs, :] = jnp.concatenate([z[:, o + half:o + KV_WIDTH], z[:, o:o + half]], axis=1).astype(BF16)
        o += KV_WIDTH
        f_ref[rs, :] = z[:, o:o + FOURIER_WIDTH]
        o += FOURIER_WIDTH
        ga_ref[rs, :] = _sigmoid(z[:, o:o + d_model]).astype(BF16)
        o += d_model
        gb_ref[rs, :] = _sigmoid(z[:, o:o + d_model]).astype(BF16)


def _inproj(x2, g, w_bf16, *, tm, chunk):
    n, d = x2.shape
    in_w = w_bf16.shape[1]
    widths = (ATTN_WIDTH, KV_WIDTH, KV_WIDTH, KV_WIDTH, FOURIER_WIDTH, d, d)
    dtypes = (BF16, BF16, BF16, BF16, F32, BF16, BF16)
    row = lambda i: (i, 0)
    return pl.pallas_call(
        functools.partial(_inproj_kernel, d_model=d, chunk=chunk),
        grid=(n // tm,),
        in_specs=[pl.BlockSpec((tm, d), row),
                  pl.BlockSpec((1, d), lambda i: (0, 0)),
                  pl.BlockSpec((d, in_w), lambda i: (0, 0))],
        out_specs=[pl.BlockSpec((tm, w), row) for w in widths],
        out_shape=[jax.ShapeDtypeStruct((n, w), dt) for w, dt in zip(widths, dtypes)],
        compiler_params=_cparams(("parallel",)),
        name="inproj",
    )(x2, g.reshape(1, d), w_bf16)


def _attn_kernel(sink_ref, q_ref, kp_ref, ko_ref, kn_ref, vp_ref, vo_ref, vn_ref, wp_ref, wo_ref, wn_ref,
                 bias_ref, o_ref, *, n_blocks, blocks_per_step):
    i = pl.program_id(1)
    kcat = jnp.concatenate([kp_ref[...], ko_ref[...], kn_ref[...]], axis=0)
    vcat = jnp.concatenate([vp_ref[...], vo_ref[...], vn_ref[...]], axis=0)
    wcat = jnp.concatenate([wp_ref[...], wo_ref[...], wn_ref[...]], axis=0)
    group = N_HEADS // N_KV_HEADS
    head = lambda a, h: a[:, h * HEAD_DIM:(h + 1) * HEAD_DIM]
    sink = jnp.concatenate([jnp.full((BLOCK, LANES), sink_ref[h], F32) for h in range(N_HEADS)], axis=0)
    low_v = lax.broadcasted_iota(I32, vcat.shape, 1) < HEAD_DIM
    low_o = lax.broadcasted_iota(I32, (BLOCK, LANES), 1) < HEAD_DIM
    vx = {(kv, half): jnp.where(low_v if half == 0 else jnp.logical_not(low_v),
                                vcat if kv == half else wcat, jnp.ones_like(vcat))
          for kv in range(N_KV_HEADS) for half in range(2)}
    for t in range(blocks_per_step):
        blk = i * blocks_per_step + t
        kind = jnp.where(blk == 0, 0, jnp.where(blk == n_blocks - 1, 2, 1))
        kw = kcat[t * BLOCK:(t + 3) * BLOCK]
        qb = q_ref[t * BLOCK:(t + 1) * BLOCK, :]
        s = jnp.concatenate([_nt_dot(head(qb, h), head(kw, h // group)) for h in range(N_HEADS)], axis=0)
        s = s + bias_ref[kind]
        m = jnp.maximum(jnp.broadcast_to(jnp.max(s, axis=-1, keepdims=True), sink.shape), sink)
        p = jnp.exp(s - jnp.concatenate([m] * 3, axis=1)).astype(BF16)
        sink_term = jnp.exp(sink - m)
        for pair in range(N_HEADS // 2):
            halves = []
            for half in range(2):
                h = 2 * pair + half
                rows = slice(h * BLOCK, (h + 1) * BLOCK)
                out = jnp.dot(p[rows], vx[h // group, half][t * BLOCK:(t + 3) * BLOCK],
                              preferred_element_type=F32)
                denom = pltpu.roll(out, HEAD_DIM, axis=1) + sink_term[rows]
                halves.append(out * (1.0 / denom))
            tile = jnp.where(low_o, halves[0], halves[1])
            o_ref[t * BLOCK:(t + 1) * BLOCK, pair * LANES:(pair + 1) * LANES] = tile.astype(o_ref.dtype)


def _attention(q, k, v, v_swapped, sink, bias, *, blocks_per_step):
    b, s, _ = q.shape
    assert N_KV_HEADS == 2 and KV_WIDTH == LANES and N_HEADS % 2 == 0
    n_blocks = s // BLOCK
    tq = blocks_per_step * BLOCK
    prev = lambda bi, i: (bi, jnp.maximum(i * blocks_per_step - 1, 0), 0)
    own = lambda bi, i: (bi, i, 0)
    nxt = lambda bi, i: (bi, jnp.minimum((i + 1) * blocks_per_step, n_blocks - 1), 0)
    kv_specs = [pl.BlockSpec((None, BLOCK, KV_WIDTH), prev),
                pl.BlockSpec((None, tq, KV_WIDTH), own),
                pl.BlockSpec((None, BLOCK, KV_WIDTH), nxt)]
    return pl.pallas_call(
        functools.partial(_attn_kernel, n_blocks=n_blocks, blocks_per_step=blocks_per_step),
        grid=(b, s // tq),
        in_specs=[pl.BlockSpec(memory_space=pltpu.SMEM),
                  pl.BlockSpec((None, tq, ATTN_WIDTH), own)] + kv_specs * 3
                 + [pl.BlockSpec((3, N_HEADS * BLOCK, 3 * BLOCK), lambda bi, i: (0, 0, 0))],
        out_specs=pl.BlockSpec((None, tq, ATTN_WIDTH), own),
        out_shape=jax.ShapeDtypeStruct((b, s, ATTN_WIDTH), BF16),
        compiler_params=_cparams(("parallel", "parallel")),
        name="window_attention",
    )(sink.astype(F32), q, k, k, k, v, v, v, v_swapped, v_swapped, v_swapped, bias)


def _hi_lo(x):
    hi = x.astype(BF16)
    return hi, (x - hi.astype(F32)).astype(BF16)


def _dft_operand(x, axis):
    if DFT_PASSES == 1:
        return x.astype(BF16)
    hi, lo = _hi_lo(x)
    return jnp.concatenate([hi, lo, hi], axis=axis)


def _dft1_kernel(x_ref, a_ref, tr_ref, ti_ref, x2, *, rows):
    a = a_ref[...]
    na = a.shape[0] // 2
    n_slabs = x2.shape[0]
    lanes = lambda c: slice(c * LANES, (c + 1) * LANES)
    for c in range(n_slabs):
        x2[c] = x_ref[:, :, lanes(c)].reshape(x2.shape[1:])
    for jj in range(rows):
        strided = pl.ds(jj, na, stride=rows)
        xj = jnp.concatenate([x2[c, strided, :] for c in range(n_slabs)], axis=1)
        t = jnp.dot(a, _dft_operand(xj, 0), preferred_element_type=F32)
        tr_ref[:, jj, :] = t[:na]
        ti_ref[:, jj, :] = t[na:]


def _dft2_kernel(tr_ref, ti_ref, gr_ref, gi_ref, cs_ref, o_ref, o2, *, rows):
    cs = cs_ref[...]
    na = o_ref.shape[0]
    for aa in range(rows):
        t = _dft_operand(jnp.concatenate([tr_ref[aa], ti_ref[aa]], axis=0), 0)
        zr = jnp.dot(gr_ref[aa], t, preferred_element_type=F32)
        zi = jnp.dot(gi_ref[aa], t, preferred_element_type=F32)
        for g in range(N_FOURIER_GROUPS):
            sl = slice(g * FOURIER_GROUP_DIM, (g + 1) * FOURIER_GROUP_DIM)
            z = _dft_operand(jnp.concatenate([zr[:, sl], zi[:, sl]], axis=1), 1)
            y = jnp.dot(z, cs, preferred_element_type=F32)
            o2[g, pl.ds(aa, na, stride=rows), :] = y
    for g in range(N_FOURIER_GROUPS):
        sl = slice(g * FOURIER_GROUP_DIM, (g + 1) * FOURIER_GROUP_DIM)
        o_ref[:, :, sl] = o2[g].reshape(na, rows, FOURIER_GROUP_DIM)


def _fourier_tables(na):
    def pieces(a, axis):
        hi = a.astype(BF16)
        if DFT_PASSES == 1:
            return jnp.asarray(hi)
        lo = (a - hi.astype(np.float64)).astype(BF16)
        return jnp.asarray(np.concatenate([hi, hi, lo], axis=axis))

    n = na * na
    k = np.arange(na, dtype=np.float64)
    ang = 2.0 * np.pi * np.outer(k, k) / na
    a1 = pieces(np.concatenate([np.cos(ang), -np.sin(ang)], axis=0), 1)
    ka = np.arange(na, dtype=np.float64)[:, None, None]
    kb = np.arange(na, dtype=np.float64)[None, :, None]
    nl = np.arange(na, dtype=np.float64)[None, None, :]
    th = 2.0 * np.pi * ((nl * (ka + na * kb)) % n) / n
    gr, gi = np.cos(th), -np.sin(th)
    g_re = pieces(np.concatenate([gr, -gi], axis=2), 2)
    g_im = pieces(np.concatenate([gi, gr], axis=2), 2)
    kc = np.arange(FOURIER_GROUP_DIM, dtype=np.float64)
    angc = 2.0 * np.pi * np.outer(kc, kc) / FOURIER_GROUP_DIM
    scale = 1.0 / np.sqrt(float(n) * FOURIER_GROUP_DIM)
    cs = pieces(np.concatenate([np.cos(angc), np.sin(angc)], axis=0) * scale, 0)
    return a1, g_re, g_im, cs


def _fourier(f, tables, *, rows):
    b, s, w = f.shape
    na = int(round(np.sqrt(s)))
    assert na * na == s and na % rows == 0
    a1, g_re, g_im, cs = tables
    x4 = f.reshape(b, na, na, w)
    const2 = lambda bi, j: (0, 0)
    tr, ti = pl.pallas_call(
        functools.partial(_dft1_kernel, rows=rows),
        grid=(b, na // rows),
        in_specs=[pl.BlockSpec((None, na, rows, w), lambda bi, j: (bi, 0, j, 0)),
                  pl.BlockSpec(a1.shape, const2)],
        out_specs=[pl.BlockSpec((None, na, rows, w), lambda bi, j: (bi, 0, j, 0))] * 2,
        out_shape=[jax.ShapeDtypeStruct((b, na, na, w), F32)] * 2,
        scratch_shapes=[pltpu.VMEM((w // LANES, na * rows, LANES), F32)],
        compiler_params=_cparams(("parallel", "parallel")),
        name="seq_dft_stage1",
    )(x4, a1)
    y = pl.pallas_call(
        functools.partial(_dft2_kernel, rows=rows),
        grid=(b, na // rows),
        in_specs=[pl.BlockSpec((None, rows, na, w), lambda bi, a: (bi, a, 0, 0))] * 2
                 + [pl.BlockSpec((rows,) + g_re.shape[1:], lambda bi, a: (a, 0, 0))] * 2
                 + [pl.BlockSpec(cs.shape, const2)],
        out_specs=pl.BlockSpec((None, na, rows, w), lambda bi, a: (bi, 0, a, 0)),
        out_shape=jax.ShapeDtypeStruct((b, na, na, w), F32),
        scratch_shapes=[pltpu.VMEM((N_FOURIER_GROUPS, na * rows, FOURIER_GROUP_DIM), F32)],
        compiler_params=_cparams(("parallel", "parallel")),
        name="seq_dft_stage2",
    )(tr, ti, g_re, g_im, cs)
    return y.reshape(b, s, w)


def _outproj_kernel(a_ref, f_ref, ga_ref, gb_ref, x_ref, wa_ref, wf_ref, wo_ref, g_ref, wr_ref,
                    xn_ref, h_ref, aff_ref, *, chunk):
    for r in range(x_ref.shape[0] // chunk):
        rs = pl.ds(r * chunk, chunk)
        a = jnp.dot(a_ref[rs, :], wa_ref[...], preferred_element_type=F32)
        fo = jnp.dot(f_ref[rs, :].astype(BF16), wf_ref[...], preferred_element_type=F32)
        merged = ga_ref[rs, :].astype(F32) * a + gb_ref[rs, :].astype(F32) * fo
        xn = x_ref[rs, :] + jnp.dot(merged.astype(BF16), wo_ref[...], preferred_element_type=F32)
        xn_ref[rs, :] = xn
        ms = jnp.mean(xn * xn, axis=-1, keepdims=True)
        h = (xn * lax.rsqrt(ms + EPS)) * g_ref[...]
        dt = h.shape[1] // LANES
        for c in range(dt):
            h_ref[pl.ds(r * chunk * dt + c, chunk, stride=dt), :] = h[:, c * LANES:(c + 1) * LANES]
        logits = _nt_dot(wr_ref[...], h.astype(BF16))
        e = jnp.exp(logits - jnp.max(logits, axis=0, keepdims=True))
        aff_ref[:, rs] = e / jnp.sum(e, axis=0, keepdims=True)


def _outproj(attn, four, ga, gb, x, wa, wf, wo, g, wr_t, *, tm, chunk):
    b, s, d = x.shape
    dt = d // LANES
    tok = lambda bi, i: (bi, i, 0)
    const = lambda bi, i: (0, 0)
    xn, h, aff = pl.pallas_call(
        functools.partial(_outproj_kernel, chunk=chunk),
        grid=(b, s // tm),
        in_specs=[pl.BlockSpec((None, tm, ATTN_WIDTH), tok),
                  pl.BlockSpec((None, tm, FOURIER_WIDTH), tok),
                  pl.BlockSpec((None, tm, d), tok),
                  pl.BlockSpec((None, tm, d), tok),
                  pl.BlockSpec((None, tm, d), tok),
                  pl.BlockSpec((ATTN_WIDTH, d), const),
                  pl.BlockSpec((FOURIER_WIDTH, d), const),
                  pl.BlockSpec((d, d), const),
                  pl.BlockSpec((1, d), const),
                  pl.BlockSpec((N_EXPERTS, d), const)],
        out_specs=[pl.BlockSpec((None, tm, d), tok),
                   pl.BlockSpec((None, tm * dt, LANES), tok),
                   pl.BlockSpec((None, N_EXPERTS, tm), lambda bi, i: (bi, 0, i))],
        out_shape=[jax.ShapeDtypeStruct((b, s, d), F32),
                   jax.ShapeDtypeStruct((b, s * dt, LANES), F32),
                   jax.ShapeDtypeStruct((b, N_EXPERTS, s), F32)],
        compiler_params=_cparams(("parallel", "parallel")),
        name="outproj_router",
    )(attn, four, ga, gb, x, wa, wf, wo, g.reshape(1, d), wr_t.astype(BF16))
    return xn, h.reshape(b, s, dt, LANES), aff


def _tri(n, kind):
    r = lax.broadcasted_iota(I32, (n, n), 0)
    c = lax.broadcasted_iota(I32, (n, n), 1)
    cond = {"row_le_col": r <= c, "row_lt_col": r < c, "col_le_row": c <= r, "col_lt_row": c < r}[kind]
    return jnp.where(cond, 1.0, 0.0).astype(BF16)


def _split128(v):
    hi = jnp.floor(v * (1.0 / LANES))
    return hi, v - hi * LANES


def _routing_kernel(aff_ref, idx_ref, pos_ref, rank_ref, gt_scr, eq_scr, need_scr, *, cap):
    ne, nj, ni = aff_ref.shape
    bits = lax.bitcast_convert_type(aff_ref[...], I32)

    def bisect(it, thr):
        cand = thr | jnp.left_shift(jnp.int32(1), 30 - it)
        cnt = jnp.sum(jnp.where(bits >= cand, 1.0, 0.0), axis=(1, 2), keepdims=True)
        return jnp.where(cnt >= cap, cand, thr)

    thr = lax.fori_loop(0, 31, bisect, jnp.zeros((ne, 1, 1), I32))
    gt = jnp.where(bits > thr, 1.0, 0.0)
    gt_scr[...] = gt
    eq_scr[...] = jnp.where(bits == thr, 1.0, 0.0)
    need = cap - jnp.sum(gt, axis=(1, 2), keepdims=True)
    need_scr[...] = jnp.broadcast_to(need, (ne, SUBLANES, ni))

    ones_sq = jnp.ones((ni, ni), BF16)
    ones_row = jnp.ones((SUBLANES, ni), BF16)
    ones_rowj = jnp.ones((SUBLANES, nj), BF16)
    u_incl = _tri(ni, "row_le_col")
    uj_strict = _tri(nj, "row_lt_col")
    lj_strict = _tri(nj, "col_lt_row")
    l_incl = _tri(ni, "col_le_row")
    bdot = lambda a, b: jnp.dot(a, b, preferred_element_type=F32)

    c_lane = lax.broadcasted_iota(I32, (nj, cap), 1).astype(F32)
    c_row = lax.broadcasted_iota(I32, (SUBLANES, cap), 1).astype(F32)
    j_sub = lax.broadcasted_iota(I32, (nj, cap), 0).astype(F32)

    def slots(e, carry):
        eq = eq_scr[e]
        eqb = eq.astype(BF16)
        tie_rank = bdot(lj_strict, bdot(eqb, ones_sq).astype(BF16)) + bdot(eqb, u_incl)
        take = jnp.where(tie_rank <= need_scr[e][0:1, :], eq, 0.0)
        sel = jnp.maximum(gt_scr[e], take)
        selb = sel.astype(BF16)
        lc = bdot(selb, u_incl)
        lc_t = _nt_dot(l_incl, selb)
        n_rep = bdot(selb, ones_sq)
        s_rep = bdot(lj_strict, n_rep.astype(BF16)) + n_rep
        n_lane = _nt_dot(ones_row, selb)
        pex_lane = bdot(n_lane.astype(BF16), uj_strict)
        s_wide = jnp.concatenate([s_rep] * (cap // ni), axis=1) if cap > ni else s_rep[:, :cap]
        jc = bdot(ones_rowj, jnp.where(s_wide <= c_lane, 1.0, 0.0).astype(BF16))
        onehot = jnp.where(j_sub == jnp.broadcast_to(jc[0:1], (nj, cap)), 1.0, 0.0).astype(BF16)
        phi, plo = _split128(pex_lane)
        r = c_row - (bdot(phi.astype(BF16), onehot) * LANES + bdot(plo.astype(BF16), onehot))
        lcs = bdot(lc_t.astype(BF16), onehot)
        ic = bdot(ones_row, jnp.where(lcs <= jnp.broadcast_to(r[0:1], (ni, cap)), 1.0, 0.0).astype(BF16))
        idx_ref[e] = (jc * ni + ic).astype(I32)
        pos_ref[e] = pex_lane.astype(I32)
        rank_ref[e] = jnp.where(sel > 0.0, lc - 1.0, -1.0)
        return carry

    lax.fori_loop(0, ne, slots, 0)


def _routing(aff, *, cap):
    b, ne, s = aff.shape
    nj = s // LANES
    aff4 = aff.reshape(b, ne, nj, LANES)
    per_b = lambda bi: (bi, 0, 0, 0)
    idx, pos, rank = pl.pallas_call(
        functools.partial(_routing_kernel, cap=cap),
        grid=(b,),
        in_specs=[pl.BlockSpec((None, ne, nj, LANES), per_b)],
        out_specs=[pl.BlockSpec((None, ne, SUBLANES, cap), per_b),
                   pl.BlockSpec((None, ne, SUBLANES, nj), per_b),
                   pl.BlockSpec((None, ne, nj, LANES), per_b)],
        out_shape=[jax.ShapeDtypeStruct((b, ne, SUBLANES, cap), I32),
                   jax.ShapeDtypeStruct((b, ne, SUBLANES, nj), I32),
                   jax.ShapeDtypeStruct((b, ne, nj, LANES), F32)],
        scratch_shapes=[pltpu.VMEM((ne, nj, LANES), F32)] * 2 + [pltpu.VMEM((ne, SUBLANES, LANES), F32)],
        compiler_params=_cparams(("parallel",)),
        name="expert_choice_routing",
    )(aff4)
    return idx[:, :, 0, :], pos[:, :, 0, :], rank


def _ffn_kernel(idx_cur, idx_nxt, h_hbm, wg_ref, wu_ref, wd_ref, y_ref, xs, xsb, gsem,
                *, cap, n_pairs, n_ftiles, per_step, m_chunk):
    p = pl.program_id(0)
    j = pl.program_id(1)
    n_slots = n_ftiles * per_step
    dt = xs.shape[0] // n_slots
    last_pair = n_pairs - 1
    b_cur = p // N_EXPERTS
    b_nxt = jnp.minimum(p + 1, last_pair) // N_EXPERTS

    def gather(ref, bq, c):
        cc = jnp.minimum(c, cap - 1)
        tok = ref[cc >> LANE_BITS, cc & (LANES - 1)]
        dst = xs.at[pl.ds(pl.multiple_of(c * dt, dt), dt)]
        return pltpu.make_async_copy(h_hbm.at[bq, tok], dst, gsem)

    def wait_gathers():
        pltpu.make_async_copy(xs, xs, gsem).wait()

    @pl.when((p == 0) & (j == 0))
    def _():
        def start(c, carry):
            gather(idx_cur, b_cur, c).start()
            return carry

        lax.fori_loop(0, n_slots, start, 0)

    @pl.when(j == 0)
    def _():
        wait_gathers()
        for c in range(dt):
            xsb[:, c * LANES:(c + 1) * LANES] = xs[pl.ds(c, cap, stride=dt), :].astype(BF16)
        y_ref[...] = jnp.zeros_like(y_ref)

    wg = wg_ref[...].astype(BF16)
    wu = wu_ref[...].astype(BF16)
    wd = wd_ref[...].astype(BF16)
    n_groups = cap // m_chunk
    per_group = per_step // n_groups
    for m in range(n_groups):
        for t in range(per_group):
            gather(idx_nxt, b_nxt, j * per_step + (m * per_group + t)).start()
        rows = pl.ds(m * m_chunk, m_chunk)
        x = xsb[rows, :]
        hg = jnp.dot(x, wg, preferred_element_type=F32)
        hu = jnp.dot(x, wu, preferred_element_type=F32)
        act = (hg * _sigmoid(hg)) * hu
        y_ref[rows, :] += jnp.dot(act.astype(BF16), wd, preferred_element_type=F32)

    @pl.when((p == last_pair) & (j == n_ftiles - 1))
    def _():
        wait_gathers()


def _expert_ffn(h, idx, wg, wu, wd, layer, *, cap, tf, m_chunk):
    b, s, dt, _ = h.shape
    d = dt * LANES
    ne, dff = wg.shape[1], wg.shape[3]
    n_ftiles = dff // tf
    n_pairs = b * ne
    n_groups = cap // m_chunk
    per_step = -(-cap // (n_ftiles * n_groups)) * n_groups
    n_slots = n_ftiles * per_step
    tab = idx.reshape(n_pairs, cap // LANES, LANES)
    tab_spec = lambda f: pl.BlockSpec((None, cap // LANES, LANES), lambda p, j: (f(p), 0, 0),
                                      memory_space=pltpu.SMEM)
    return pl.pallas_call(
        functools.partial(_ffn_kernel, cap=cap, n_pairs=n_pairs, n_ftiles=n_ftiles, per_step=per_step,
                          m_chunk=m_chunk),
        grid=(n_pairs, n_ftiles),
        in_specs=[tab_spec(lambda p: p),
                  tab_spec(lambda p: jnp.minimum(p + 1, n_pairs - 1)),
                  pl.BlockSpec(memory_space=pl.ANY),
                  pl.BlockSpec((None, None, d, tf), lambda p, j: (layer, p % ne, 0, j)),
                  pl.BlockSpec((None, None, d, tf), lambda p, j: (layer, p % ne, 0, j)),
                  pl.BlockSpec((None, None, tf, d), lambda p, j: (layer, p % ne, j, 0))],
        out_specs=pl.BlockSpec((None, None, cap, d), lambda p, j: (p // ne, p % ne, 0, 0)),
        out_shape=jax.ShapeDtypeStruct((b, ne, cap, d), F32),
        scratch_shapes=[pltpu.VMEM((n_slots * dt, LANES), F32), pltpu.VMEM((cap, d), BF16),
                        pltpu.SemaphoreType.DMA(())],
        compiler_params=_cparams(("arbitrary", "arbitrary")),
        name="expert_swiglu",
    )(tab, tab, h, wg, wu, wd)


def _combine_kernel(pos_ref, rank_ref, aff_ref, x_ref, g_ref, y_hbm, o_ref, ybuf, acc, sem,
                    *, cap, nj, n_tiles, final_norm):
    b = pl.program_id(0)
    j = pl.program_id(1)
    t = b * nj + j
    ne = N_EXPERTS
    ring = COMBINE_AHEAD + 1
    assert ring & (ring - 1) == 0
    in_ring = lambda k: k & (ring - 1)

    def nominal(bq, e, jq, r):
        return ((pos_ref[bq * ne + e, jq] >> 3) << 3) + r * EWIN

    def window(bq, e, jq, r):
        return pl.multiple_of(jnp.minimum(nominal(bq, e, jq, r), cap - EWIN), SUBLANES)

    def start_fetch(ahead, r, slot):
        jq = j + ahead
        wrap = (jq >= nj).astype(I32)
        bq = b + wrap
        jq = jq - wrap * nj
        past = bq * nj + jq >= n_tiles
        bq = jnp.where(past, n_tiles // nj - 1, bq)
        jq = jnp.where(past, nj - 1, jq)
        for e in range(ne):
            pltpu.make_async_copy(y_hbm.at[bq, e, pl.ds(window(bq, e, jq, r), EWIN)],
                                  ybuf.at[slot, pl.ds(e * EWIN, EWIN)], sem.at[slot]).start()

    def wait_fetch(slot):
        pltpu.make_async_copy(ybuf.at[slot], ybuf.at[slot], sem.at[slot]).wait()

    rk = rank_ref[...]
    af = aff_ref[...]
    row = lax.broadcasted_iota(I32, (EWIN, LANES), 0).astype(F32)

    def add_round(r, slot):
        parts = []
        for e in range(ne):
            p0 = pos_ref[b * ne + e, j]
            local = row + (window(b, e, j, r) - p0).astype(F32)
            want = jnp.where(local >= jnp.maximum(nominal(b, e, j, r) - p0, 0).astype(F32), local, -2.0)
            parts.append(jnp.where(rk[e:e + 1, :] == want, af[e:e + 1, :], 0.0))
        gate_t = jnp.concatenate(parts, axis=0)
        onehot = jnp.where(gate_t != 0.0, 1.0, 0.0).T.astype(BF16)
        scaled = ybuf[slot] * jnp.sum(gate_t, axis=1, keepdims=True)
        hi = scaled.astype(BF16)
        lo = (scaled - hi.astype(F32)).astype(BF16)
        return jnp.dot(onehot, hi, preferred_element_type=F32) + jnp.dot(onehot, lo, preferred_element_type=F32)

    @pl.when(t == 0)
    def _():
        for ahead in range(COMBINE_AHEAD):
            start_fetch(ahead, 0, ahead)

    start_fetch(COMBINE_AHEAD, 0, in_ring(t + COMBINE_AHEAD))
    slot0 = in_ring(t)
    wait_fetch(slot0)
    acc[...] = add_round(0, slot0)

    span = jnp.int32(1)
    for e in range(ne):
        span = jnp.maximum(span, pos_ref[b * ne + e, j + 1] - nominal(b, e, j, 0))
    n_rounds = (span + EWIN - 1) // EWIN

    def extra_round(r, carry):
        start_fetch(0, r, ring)
        wait_fetch(ring)
        acc[...] += add_round(r, ring)
        return carry

    lax.fori_loop(1, n_rounds, extra_round, 0)

    @pl.when(t == n_tiles - 1)
    def _():
        for ahead in range(1, COMBINE_AHEAD + 1):
            wait_fetch(in_ring(t + ahead))

    y = x_ref[...] + acc[...]
    if final_norm:
        ms = jnp.mean(y * y, axis=-1, keepdims=True)
        y = (y * lax.rsqrt(ms + EPS)) * g_ref[...]
    o_ref[...] = y


def _combine(x, y, pos, rank, aff, g_final, *, cap, final_norm):
    b, s, d = x.shape
    nj = s // LANES
    ne = N_EXPERTS
    pos_tab = jnp.concatenate([pos, jnp.full((b, ne, 1), cap, I32)], axis=2).reshape(b * ne, nj + 1)
    rank_t = jnp.transpose(rank, (0, 2, 1, 3))
    return pl.pallas_call(
        functools.partial(_combine_kernel, cap=cap, nj=nj, n_tiles=b * nj, final_norm=final_norm),
        grid_spec=pltpu.PrefetchScalarGridSpec(
            num_scalar_prefetch=1,
            grid=(b, nj),
            in_specs=[pl.BlockSpec((None, None, ne, LANES), lambda bi, j, tab: (bi, j, 0, 0)),
                      pl.BlockSpec((None, ne, LANES), lambda bi, j, tab: (bi, 0, j)),
                      pl.BlockSpec((None, LANES, d), lambda bi, j, tab: (bi, j, 0)),
                      pl.BlockSpec((1, d), lambda bi, j, tab: (0, 0)),
                      pl.BlockSpec(memory_space=pl.ANY)],
            out_specs=pl.BlockSpec((None, LANES, d), lambda bi, j, tab: (bi, j, 0)),
            scratch_shapes=[pltpu.VMEM((COMBINE_AHEAD + 2, ne * EWIN, d), F32), pltpu.VMEM((LANES, d), F32),
                            pltpu.SemaphoreType.DMA((COMBINE_AHEAD + 2,))]),
        out_shape=jax.ShapeDtypeStruct((b, s, d), F32),
        compiler_params=_cparams(("arbitrary", "arbitrary")),
        name="moe_combine",
    )(pos_tab, rank_t, aff, x, g_final.reshape(1, d), y)


def kernel(x, rel_bias, g_mix, w_in, attn_sink, w_attn_proj, w_fourier_proj, w_out, g_ffn, w_router,
           w_exp_gate, w_exp_up, w_exp_down, g_final):
    b, s, d = x.shape
    depth = g_mix.shape[0]
    cap = CAPACITY_FACTOR * s // N_EXPERTS
    bias = _bias_table(rel_bias)
    tables = _fourier_tables(int(round(np.sqrt(s))))
    for l in range(depth):
        q, k, v, vs, f, ga, gb = _inproj(x.reshape(b * s, d), g_mix[l], w_in[l].astype(BF16), tm=1024, chunk=256)
        shp = lambda a: a.reshape(b, s, a.shape[-1])
        attn = _attention(shp(q), shp(k), shp(v), shp(vs), attn_sink[l], bias, blocks_per_step=4)
        four = _fourier(shp(f), tables, rows=8)
        xn, h, aff = _outproj(attn, four, shp(ga), shp(gb), x,
                              w_attn_proj[l].astype(BF16), w_fourier_proj[l].astype(BF16),
                              w_out[l].astype(BF16), g_ffn[l], w_router[l].T, tm=512, chunk=256)
        idx, pos, rank = _routing(aff, cap=cap)
        y = _expert_ffn(h, idx, w_exp_gate, w_exp_up, w_exp_down, l, cap=cap, tf=256, m_chunk=min(cap, 512))
        x = _combine(xn, y, pos, rank, aff, g_final, cap=cap, final_norm=(l == depth - 1))
    return x
```

```python
import functools

import numpy as np
import jax
import jax.numpy as jnp
from jax import lax
from jax.experimental import pallas as pl
from jax.experimental.pallas import tpu as pltpu

F32 = jnp.float32
BF16 = jnp.bfloat16
I32 = jnp.int32

N_HEADS = 8
N_KV_HEADS = 2
HEAD_DIM = 64
ATTN_WIDTH = N_HEADS * HEAD_DIM
KV_WIDTH = N_KV_HEADS * HEAD_DIM
WINDOW = 128
BLOCK = 128
N_BUCKETS = 32
MAX_DISTANCE = 128
N_FOURIER_GROUPS = 4
FOURIER_GROUP_DIM = 128
FOURIER_WIDTH = N_FOURIER_GROUPS * FOURIER_GROUP_DIM
N_EXPERTS = 16
CAPACITY_FACTOR = 2
EPS = 1e-6
NEG_INF = -1e30

LANES = 128
LANE_BITS = 7
SUBLANES = 8
VMEM_LIMIT = 56 * 1024 * 1024
EWIN = 32
DFT_PASSES = 1
COMBINE_AHEAD = 3


def _cparams(sem):
    return pltpu.CompilerParams(dimension_semantics=sem, vmem_limit_bytes=VMEM_LIMIT)


def _nt_dot(a, b, **kw):
    return lax.dot_general(a, b, (((1,), (1,)), ((), ())), preferred_element_type=F32, **kw)


def _sigmoid(x):
    return 1.0 / (1.0 + jnp.exp(-x))


def _t5_bucket(rel):
    half = N_BUCKETS // 2
    max_exact = half // 2
    ret = (rel > 0).astype(jnp.int32) * half
    n = jnp.abs(rel)
    nf = jnp.maximum(n, 1).astype(jnp.float32)
    large = max_exact + (jnp.log(nf / max_exact) / np.float32(np.log(MAX_DISTANCE / max_exact))
                         * (half - max_exact)).astype(jnp.int32)
    large = jnp.minimum(large, half - 1)
    return ret + jnp.where(n < max_exact, n, large)


def _bias_kernel(relb_ref, bucket_ref, o_ref):
    bk = bucket_ref[...]
    col = lax.broadcasted_iota(I32, bk.shape, 1)
    for h in range(N_HEADS):
        acc = jnp.full(bk.shape, NEG_INF, F32)
        for b in range(N_BUCKETS):
            acc = jnp.where(bk == b, relb_ref[b, h], acc)
        o_ref[0, h] = jnp.where(col >= BLOCK, acc, NEG_INF)
        o_ref[1, h] = acc
        o_ref[2, h] = jnp.where(col < 2 * BLOCK, acc, NEG_INF)


def _bias_table(rel_bias):
    q_loc = jnp.arange(BLOCK)
    k_loc = jnp.arange(3 * BLOCK) - BLOCK
    rel = k_loc[None, :] - q_loc[:, None]
    bucket = jnp.where(jnp.abs(rel) <= WINDOW, _t5_bucket(rel), -1).astype(I32)
    table = pl.pallas_call(
        _bias_kernel,
        out_shape=jax.ShapeDtypeStruct((3, N_HEADS, BLOCK, 3 * BLOCK), F32),
        in_specs=[pl.BlockSpec(memory_space=pltpu.SMEM),
                  pl.BlockSpec(memory_space=pltpu.VMEM)],
        out_specs=pl.BlockSpec(memory_space=pltpu.VMEM),
        name="rel_bias_table",
    )(rel_bias.astype(F32), bucket)
    return table.reshape(3, N_HEADS * BLOCK, 3 * BLOCK)


def _inproj_kernel(x_ref, g_ref, w_ref, q_ref, k_ref, v_ref, vs_ref, f_ref, ga_ref, gb_ref, *, d_model, chunk):
    for r in range(x_ref.shape[0] // chunk):
        rs = pl.ds(r * chunk, chunk)
        x = x_ref[rs, :]
        ms = jnp.mean(x * x, axis=-1, keepdims=True)
        h = (x * lax.rsqrt(ms + EPS)) * g_ref[...]
        z = jnp.dot(h.astype(BF16), w_ref[...], preferred_element_type=F32)
        o = 0
        q_ref[rs, :] = (z[:, o:o + ATTN_WIDTH] * (HEAD_DIM ** -0.5)).astype(BF16)
        o += ATTN_WIDTH
        k_ref[rs, :] = z[:, o:o + KV_WIDTH].astype(BF16)
        o += KV_WIDTH
        v_ref[rs, :] = z[:, o:o + KV_WIDTH].astype(BF16)
        half = KV_WIDTH // 2
        vs_ref[rs, :] = jnp.concatenate([z[:, o + half:o + KV_WIDTH], z[:, o:o + half]], axis=1).astype(BF16)
        o += KV_WIDTH
        f_ref[rs, :] = z[:, o:o + FOURIER_WIDTH]
        o += FOURIER_WIDTH
        ga_ref[rs, :] = _sigmoid(z[:, o:o + d_model]).astype(BF16)
        o += d_model
        gb_ref[rs, :] = _sigmoid(z[:, o:o + d_model]).astype(BF16)


def _inproj(x2, g, w_bf16, *, tm, chunk):
    n, d = x2.shape
    in_w = w_bf16.shape[1]
    widths = (ATTN_WIDTH, KV_WIDTH, KV_WIDTH, KV_WIDTH, FOURIER_WIDTH, d, d)
    dtypes = (BF16, BF16, BF16, BF16, F32, BF16, BF16)
    row = lambda i: (i, 0)
    return pl.pallas_call(
        functools.partial(_inproj_kernel, d_model=d, chunk=chunk),
        grid=(n // tm,),
        in_specs=[pl.BlockSpec((tm, d), row),
                  pl.BlockSpec((1, d), lambda i: (0, 0)),
                  pl.BlockSpec((d, in_w), lambda i: (0, 0))],
        out_specs=[pl.BlockSpec((tm, w), row) for w in widths],
        out_shape=[jax.ShapeDtypeStruct((n, w), dt) for w, dt in zip(widths, dtypes)],
        compiler_params=_cparams(("parallel",)),
        name="inproj",
    )(x2, g.reshape(1, d), w_bf16)


def _attn_kernel(sink_ref, q_ref, kp_ref, ko_ref, kn_ref, vp_ref, vo_ref, vn_ref, wp_ref, wo_ref, wn_ref,
                 bias_ref, o_ref, *, n_blocks, blocks_per_step):
    i = pl.program_id(1)
    kcat = jnp.concatenate([kp_ref[...], ko_ref[...], kn_ref[...]], axis=0)
    vcat = jnp.concatenate([vp_ref[...], vo_ref[...], vn_ref[...]], axis=0)
    wcat = jnp.concatenate([wp_ref[...], wo_ref[...], wn_ref[...]], axis=0)
    group = N_HEADS // N_KV_HEADS
    head = lambda a, h: a[:, h * HEAD_DIM:(h + 1) * HEAD_DIM]
    sink = jnp.concatenate([jnp.full((BLOCK, LANES), sink_ref[h], F32) for h in range(N_HEADS)], axis=0)
    low_v = lax.broadcasted_iota(I32, vcat.shape, 1) < HEAD_DIM
    low_o = lax.broadcasted_iota(I32, (BLOCK, LANES), 1) < HEAD_DIM
    vx = {(kv, half): jnp.where(low_v if half == 0 else jnp.logical_not(low_v),
                                vcat if kv == half else wcat, jnp.ones_like(vcat))
          for kv in range(N_KV_HEADS) for half in range(2)}
    for t in range(blocks_per_step):
        blk = i * blocks_per_step + t
        kind = jnp.where(blk == 0, 0, jnp.where(blk == n_blocks - 1, 2, 1))
        kw = kcat[t * BLOCK:(t + 3) * BLOCK]
        qb = q_ref[t * BLOCK:(t + 1) * BLOCK, :]
        s = jnp.concatenate([_nt_dot(head(qb, h), head(kw, h // group)) for h in range(N_HEADS)], axis=0)
        s = s + bias_ref[kind]
        m = jnp.maximum(jnp.broadcast_to(jnp.max(s, axis=-1, keepdims=True), sink.shape), sink)
        p = jnp.exp(s - jnp.concatenate([m] * 3, axis=1)).astype(BF16)
        sink_term = jnp.exp(sink - m)
        for pair in range(N_HEADS // 2):
            halves = []
            for half in range(2):
                h = 2 * pair + half
                rows = slice(h * BLOCK, (h + 1) * BLOCK)
                out = jnp.dot(p[rows], vx[h // group, half][t * BLOCK:(t + 3) * BLOCK],
                              preferred_element_type=F32)
                denom = pltpu.roll(out, HEAD_DIM, axis=1) + sink_term[rows]
                halves.append(out * (1.0 / denom))
            tile = jnp.where(low_o, halves[0], halves[1])
            o_ref[t * BLOCK:(t + 1) * BLOCK, pair * LANES:(pair + 1) * LANES] = tile.astype(o_ref.dtype)


def _attention(q, k, v, v_swapped, sink, bias, *, blocks_per_step):
    b, s, _ = q.shape
    assert N_KV_HEADS == 2 and KV_WIDTH == LANES and N_HEADS % 2 == 0
    n_blocks = s // BLOCK
    tq = blocks_per_step * BLOCK
    prev = lambda bi, i: (bi, jnp.maximum(i * blocks_per_step - 1, 0), 0)
    own = lambda bi, i: (bi, i, 0)
    nxt = lambda bi, i: (bi, jnp.minimum((i + 1) * blocks_per_step, n_blocks - 1), 0)
    kv_specs = [pl.BlockSpec((None, BLOCK, KV_WIDTH), prev),
                pl.BlockSpec((None, tq, KV_WIDTH), own),
                pl.BlockSpec((None, BLOCK, KV_WIDTH), nxt)]
    return pl.pallas_call(
        functools.partial(_attn_kernel, n_blocks=n_blocks, blocks_per_step=blocks_per_step),
        grid=(b, s // tq),
        in_specs=[pl.BlockSpec(memory_space=pltpu.SMEM),
                  pl.BlockSpec((None, tq, ATTN_WIDTH), own)] + kv_specs * 3
                 + [pl.BlockSpec((3, N_HEADS * BLOCK, 3 * BLOCK), lambda bi, i: (0, 0, 0))],
        out_specs=pl.BlockSpec((None, tq, ATTN_WIDTH), own),
        out_shape=jax.ShapeDtypeStruct((b, s, ATTN_WIDTH), BF16),
        compiler_params=_cparams(("parallel", "parallel")),
        name="window_attention",
    )(sink.astype(F32), q, k, k, k, v, v, v, v_swapped, v_swapped, v_swapped, bias)


def _hi_lo(x):
    hi = x.astype(BF16)
    return hi, (x - hi.astype(F32)).astype(BF16)


def _dft_operand(x, axis):
    if DFT_PASSES == 1:
        return x.astype(BF16)
    hi, lo = _hi_lo(x)
    return jnp.concatenate([hi, lo, hi], axis=axis)


def _dft1_kernel(x_ref, a_ref, tr_ref, ti_ref, x2, *, rows):
    a = a_ref[...]
    na = a.shape[0] // 2
    n_slabs = x2.shape[0]
    lanes = lambda c: slice(c * LANES, (c + 1) * LANES)
    for c in range(n_slabs):
        x2[c] = x_ref[:, :, lanes(c)].reshape(x2.shape[1:])
    for jj in range(rows):
        strided = pl.ds(jj, na, stride=rows)
        xj = jnp.concatenate([x2[c, strided, :] for c in range(n_slabs)], axis=1)
        t = jnp.dot(a, _dft_operand(xj, 0), preferred_element_type=F32)
        tr_ref[:, jj, :] = t[:na]
        ti_ref[:, jj, :] = t[na:]


def _dft2_kernel(tr_ref, ti_ref, gr_ref, gi_ref, cs_ref, o_ref, o2, *, rows):
    cs = cs_ref[...]
    na = o_ref.shape[0]
    for aa in range(rows):
        t = _dft_operand(jnp.concatenate([tr_ref[aa], ti_ref[aa]], axis=0), 0)
        zr = jnp.dot(gr_ref[aa], t, preferred_element_type=F32)
        zi = jnp.dot(gi_ref[aa], t, preferred_element_type=F32)
        for g in range(N_FOURIER_GROUPS):
            sl = slice(g * FOURIER_GROUP_DIM, (g + 1) * FOURIER_GROUP_DIM)
            z = _dft_operand(jnp.concatenate([zr[:, sl], zi[:, sl]], axis=1), 1)
            y = jnp.dot(z, cs, preferred_element_type=F32)
            o2[g, pl.ds(aa, na, stride=rows), :] = y
    for g in range(N_FOURIER_GROUPS):
        sl = slice(g * FOURIER_GROUP_DIM, (g + 1) * FOURIER_GROUP_DIM)
        o_ref[:, :, sl] = o2[g].reshape(na, rows, FOURIER_GROUP_DIM)


def _fourier_tables(na):
    def pieces(a, axis):
        hi = a.astype(BF16)
        if DFT_PASSES == 1:
            return jnp.asarray(hi)
        lo = (a - hi.astype(np.float64)).astype(BF16)
        return jnp.asarray(np.concatenate([hi, hi, lo], axis=axis))

    n = na * na
    k = np.arange(na, dtype=np.float64)
    ang = 2.0 * np.pi * np.outer(k, k) / na
    a1 = pieces(np.concatenate([np.cos(ang), -np.sin(ang)], axis=0), 1)
    ka = np.arange(na, dtype=np.float64)[:, None, None]
    kb = np.arange(na, dtype=np.float64)[None, :, None]
    nl = np.arange(na, dtype=np.float64)[None, None, :]
    th = 2.0 * np.pi * ((nl * (ka + na * kb)) % n) / n
    gr, gi = np.cos(th), -np.sin(th)
    g_re = pieces(np.concatenate([gr, -gi], axis=2), 2)
    g_im = pieces(np.concatenate([gi, gr], axis=2), 2)
    kc = np.arange(FOURIER_GROUP_DIM, dtype=np.float64)
    angc = 2.0 * np.pi * np.outer(kc, kc) / FOURIER_GROUP_DIM
    scale = 1.0 / np.sqrt(float(n) * FOURIER_GROUP_DIM)
    cs = pieces(np.concatenate([np.cos(angc), np.sin(angc)], axis=0) * scale, 0)
    return a1, g_re, g_im, cs


def _fourier(f, tables, *, rows):
    b, s, w = f.shape
    na = int(round(np.sqrt(s)))
    assert na * na == s and na % rows == 0
    a1, g_re, g_im, cs = tables
    x4 = f.reshape(b, na, na, w)
    const2 = lambda bi, j: (0, 0)
    tr, ti = pl.pallas_call(
        functools.partial(_dft1_kernel, rows=rows),
        grid=(b, na // rows),
        in_specs=[pl.BlockSpec((None, na, rows, w), lambda bi, j: (bi, 0, j, 0)),
                  pl.BlockSpec(a1.shape, const2)],
        out_specs=[pl.BlockSpec((None, na, rows, w), lambda bi, j: (bi, 0, j, 0))] * 2,
        out_shape=[jax.ShapeDtypeStruct((b, na, na, w), F32)] * 2,
        scratch_shapes=[pltpu.VMEM((w // LANES, na * rows, LANES), F32)],
        compiler_params=_cparams(("parallel", "parallel")),
        name="seq_dft_stage1",
    )(x4, a1)
    y = pl.pallas_call(
        functools.partial(_dft2_kernel, rows=rows),
        grid=(b, na // rows),
        in_specs=[pl.BlockSpec((None, rows, na, w), lambda bi, a: (bi, a, 0, 0))] * 2
                 + [pl.BlockSpec((rows,) + g_re.shape[1:], lambda bi, a: (a, 0, 0))] * 2
                 + [pl.BlockSpec(cs.shape, const2)],
        out_specs=pl.BlockSpec((None, na, rows, w), lambda bi, a: (bi, 0, a, 0)),
        out_shape=jax.ShapeDtypeStruct((b, na, na, w), F32),
        scratch_shapes=[pltpu.VMEM((N_FOURIER_GROUPS, na * rows, FOURIER_GROUP_DIM), F32)],
        compiler_params=_cparams(("parallel", "parallel")),
        name="seq_dft_stage2",
    )(tr, ti, g_re, g_im, cs)
    return y.reshape(b, s, w)


def _outproj_kernel(a_ref, f_ref, ga_ref, gb_ref, x_ref, wa_ref, wf_ref, wo_ref, g_ref, wr_ref,
                    xn_ref, h_ref, aff_ref, *, chunk):
    for r in range(x_ref.shape[0] // chunk):
        rs = pl.ds(r * chunk, chunk)
        a = jnp.dot(a_ref[rs, :], wa_ref[...], preferred_element_type=F32)
        fo = jnp.dot(f_ref[rs, :].astype(BF16), wf_ref[...], preferred_element_type=F32)
        merged = ga_ref[rs, :].astype(F32) * a + gb_ref[rs, :].astype(F32) * fo
        xn = x_ref[rs, :] + jnp.dot(merged.astype(BF16), wo_ref[...], preferred_element_type=F32)
        xn_ref[rs, :] = xn
        ms = jnp.mean(xn * xn, axis=-1, keepdims=True)
        h = (xn * lax.rsqrt(ms + EPS)) * g_ref[...]
        dt = h.shape[1] // LANES
        for c in range(dt):
            h_ref[pl.ds(r * chunk * dt + c, chunk, stride=dt), :] = h[:, c * LANES:(c + 1) * LANES]
        logits = _nt_dot(wr_ref[...], h.astype(BF16))
        e = jnp.exp(logits - jnp.max(logits, axis=0, keepdims=True))
        aff_ref[:, rs] = e / jnp.sum(e, axis=0, keepdims=True)


def _outproj(attn, four, ga, gb, x, wa, wf, wo, g, wr_t, *, tm, chunk):
    b, s, d = x.shape
    dt = d // LANES
    tok = lambda bi, i: (bi, i, 0)
    const = lambda bi, i: (0, 0)
    xn, h, aff = pl.pallas_call(
        functools.partial(_outproj_kernel, chunk=chunk),
        grid=(b, s // tm),
        in_specs=[pl.BlockSpec((None, tm, ATTN_WIDTH), tok),
                  pl.BlockSpec((None, tm, FOURIER_WIDTH), tok),
                  pl.BlockSpec((None, tm, d), tok),
                  pl.BlockSpec((None, tm, d), tok),
                  pl.BlockSpec((None, tm, d), tok),
                  pl.BlockSpec((ATTN_WIDTH, d), const),
                  pl.BlockSpec((FOURIER_WIDTH, d), const),
                  pl.BlockSpec((d, d), const),
                  pl.BlockSpec((1, d), const),
                  pl.BlockSpec((N_EXPERTS, d), const)],
        out_specs=[pl.BlockSpec((None, tm, d), tok),
                   pl.BlockSpec((None, tm * dt, LANES), tok),
                   pl.BlockSpec((None, N_EXPERTS, tm), lambda bi, i: (bi, 0, i))],
        out_shape=[jax.ShapeDtypeStruct((b, s, d), F32),
                   jax.ShapeDtypeStruct((b, s * dt, LANES), F32),
                   jax.ShapeDtypeStruct((b, N_EXPERTS, s), F32)],
        compiler_params=_cparams(("parallel", "parallel")),
        name="outproj_router",
    )(attn, four, ga, gb, x, wa, wf, wo, g.reshape(1, d), wr_t.astype(BF16))
    return xn, h.reshape(b, s, dt, LANES), aff


def _tri(n, kind):
    r = lax.broadcasted_iota(I32, (n, n), 0)
    c = lax.broadcasted_iota(I32, (n, n), 1)
    cond = {"row_le_col": r <= c, "row_lt_col": r < c, "col_le_row": c <= r, "col_lt_row": c < r}[kind]
    return jnp.where(cond, 1.0, 0.0).astype(BF16)


def _split128(v):
    hi = jnp.floor(v * (1.0 / LANES))
    return hi, v - hi * LANES


def _routing_kernel(aff_ref, idx_ref, pos_ref, rank_ref, gt_scr, eq_scr, need_scr, *, cap):
    ne, nj, ni = aff_ref.shape
    bits = lax.bitcast_convert_type(aff_ref[...], I32)

    def bisect(it, thr):
        cand = thr | jnp.left_shift(jnp.int32(1), 30 - it)
        cnt = jnp.sum(jnp.where(bits >= cand, 1.0, 0.0), axis=(1, 2), keepdims=True)
        return jnp.where(cnt >= cap, cand, thr)

    thr = lax.fori_loop(0, 31, bisect, jnp.zeros((ne, 1, 1), I32))
    gt = jnp.where(bits > thr, 1.0, 0.0)
    gt_scr[...] = gt
    eq_scr[...] = jnp.where(bits == thr, 1.0, 0.0)
    need = cap - jnp.sum(gt, axis=(1, 2), keepdims=True)
    need_scr[...] = jnp.broadcast_to(need, (ne, SUBLANES, ni))

    ones_sq = jnp.ones((ni, ni), BF16)
    ones_row = jnp.ones((SUBLANES, ni), BF16)
    ones_rowj = jnp.ones((SUBLANES, nj), BF16)
    u_incl = _tri(ni, "row_le_col")
    uj_strict = _tri(nj, "row_lt_col")
    lj_strict = _tri(nj, "col_lt_row")
    l_incl = _tri(ni, "col_le_row")
    bdot = lambda a, b: jnp.dot(a, b, preferred_element_type=F32)

    c_lane = lax.broadcasted_iota(I32, (nj, cap), 1).astype(F32)
    c_row = lax.broadcasted_iota(I32, (SUBLANES, cap), 1).astype(F32)
    j_sub = lax.broadcasted_iota(I32, (nj, cap), 0).astype(F32)

    def slots(e, carry):
        eq = eq_scr[e]
        eqb = eq.astype(BF16)
        tie_rank = bdot(lj_strict, bdot(eqb, ones_sq).astype(BF16)) + bdot(eqb, u_incl)
        take = jnp.where(tie_rank <= need_scr[e][0:1, :], eq, 0.0)
        sel = jnp.maximum(gt_scr[e], take)
        selb = sel.astype(BF16)
        lc = bdot(selb, u_incl)
        lc_t = _nt_dot(l_incl, selb)
        n_rep = bdot(selb, ones_sq)
        s_rep = bdot(lj_strict, n_rep.astype(BF16)) + n_rep
        n_lane = _nt_dot(ones_row, selb)
        pex_lane = bdot(n_lane.astype(BF16), uj_strict)
        s_wide = jnp.concatenate([s_rep] * (cap // ni), axis=1) if cap > ni else s_rep[:, :cap]
        jc = bdot(ones_rowj, jnp.where(s_wide <= c_lane, 1.0, 0.0).astype(BF16))
        onehot = jnp.where(j_sub == jnp.broadcast_to(jc[0:1], (nj, cap)), 1.0, 0.0).astype(BF16)
        phi, plo = _split128(pex_lane)
        r = c_row - (bdot(phi.astype(BF16), onehot) * LANES + bdot(plo.astype(BF16), onehot))
        lcs = bdot(lc_t.astype(BF16), onehot)
        ic = bdot(ones_row, jnp.where(lcs <= jnp.broadcast_to(r[0:1], (ni, cap)), 1.0, 0.0).astype(BF16))
        idx_ref[e] = (jc * ni + ic).astype(I32)
        pos_ref[e] = pex_lane.astype(I32)
        rank_ref[e] = jnp.where(sel > 0.0, lc - 1.0, -1.0)
        return carry

    lax.fori_loop(0, ne, slots, 0)


def _routing(aff, *, cap):
    b, ne, s = aff.shape
    nj = s // LANES
    aff4 = aff.reshape(b, ne, nj, LANES)
    per_b = lambda bi: (bi, 0, 0, 0)
    idx, pos, rank = pl.pallas_call(
        functools.partial(_routing_kernel, cap=cap),
        grid=(b,),
        in_specs=[pl.BlockSpec((None, ne, nj, LANES), per_b)],
        out_specs=[pl.BlockSpec((None, ne, SUBLANES, cap), per_b),
                   pl.BlockSpec((None, ne, SUBLANES, nj), per_b),
                   pl.BlockSpec((None, ne, nj, LANES), per_b)],
        out_shape=[jax.ShapeDtypeStruct((b, ne, SUBLANES, cap), I32),
                   jax.ShapeDtypeStruct((b, ne, SUBLANES, nj), I32),
                   jax.ShapeDtypeStruct((b, ne, nj, LANES), F32)],
        scratch_shapes=[pltpu.VMEM((ne, nj, LANES), F32)] * 2 + [pltpu.VMEM((ne, SUBLANES, LANES), F32)],
        compiler_params=_cparams(("parallel",)),
        name="expert_choice_routing",
    )(aff4)
    return idx[:, :, 0, :], pos[:, :, 0, :], rank


def _ffn_kernel(idx_cur, idx_nxt, h_hbm, wg_ref, wu_ref, wd_ref, y_ref, xs, xsb, gsem,
                *, cap, n_pairs, n_ftiles, per_step, m_chunk):
    p = pl.program_id(0)
    j = pl.program_id(1)
    n_slots = n_ftiles * per_step
    dt = xs.shape[0] // n_slots
    last_pair = n_pairs - 1
    b_cur = p // N_EXPERTS
    b_nxt = jnp.minimum(p + 1, last_pair) // N_EXPERTS

    def gather(ref, bq, c):
        cc = jnp.minimum(c, cap - 1)
        tok = ref[cc >> LANE_BITS, cc & (LANES - 1)]
        dst = xs.at[pl.ds(pl.multiple_of(c * dt, dt), dt)]
        return pltpu.make_async_copy(h_hbm.at[bq, tok], dst, gsem)

    def wait_gathers():
        pltpu.make_async_copy(xs, xs, gsem).wait()

    @pl.when((p == 0) & (j == 0))
    def _():
        def start(c, carry):
            gather(idx_cur, b_cur, c).start()
            return carry

        lax.fori_loop(0, n_slots, start, 0)

    @pl.when(j == 0)
    def _():
        wait_gathers()
        for c in range(dt):
            xsb[:, c * LANES:(c + 1) * LANES] = xs[pl.ds(c, cap, stride=dt), :].astype(BF16)
        y_ref[...] = jnp.zeros_like(y_ref)

    wg = wg_ref[...].astype(BF16)
    wu = wu_ref[...].astype(BF16)
    wd = wd_ref[...].astype(BF16)
    n_groups = cap // m_chunk
    per_group = per_step // n_groups
    for m in range(n_groups):
        for t in range(per_group):
            gather(idx_nxt, b_nxt, j * per_step + (m * per_group + t)).start()
        rows = pl.ds(m * m_chunk, m_chunk)
        x = xsb[rows, :]
        hg = jnp.dot(x, wg, preferred_element_type=F32)
        hu = jnp.dot(x, wu, preferred_element_type=F32)
        act = (hg * _sigmoid(hg)) * hu
        y_ref[rows, :] += jnp.dot(act.astype(BF16), wd, preferred_element_type=F32)

    @pl.when((p == last_pair) & (j == n_ftiles - 1))
    def _():
        wait_gathers()


def _expert_ffn(h, idx, wg, wu, wd, layer, *, cap, tf, m_chunk):
    b, s, dt, _ = h.shape
    d = dt * LANES
    ne, dff = wg.shape[1], wg.shape[3]
    n_ftiles = dff // tf
    n_pairs = b * ne
    n_groups = cap // m_chunk
    per_step = -(-cap // (n_ftiles * n_groups)) * n_groups
    n_slots = n_ftiles * per_step
    tab = idx.reshape(n_pairs, cap // LANES, LANES)
    tab_spec = lambda f: pl.BlockSpec((None, cap // LANES, LANES), lambda p, j: (f(p), 0, 0),
                                      memory_space=pltpu.SMEM)
    return pl.pallas_call(
        functools.partial(_ffn_kernel, cap=cap, n_pairs=n_pairs, n_ftiles=n_ftiles, per_step=per_step,
                          m_chunk=m_chunk),
        grid=(n_pairs, n_ftiles),
        in_specs=[tab_spec(lambda p: p),
                  tab_spec(lambda p: jnp.minimum(p + 1, n_pairs - 1)),
                  pl.BlockSpec(memory_space=pl.ANY),
                  pl.BlockSpec((None, None, d, tf), lambda p, j: (layer, p % ne, 0, j)),
                  pl.BlockSpec((None, None, d, tf), lambda p, j: (layer, p % ne, 0, j)),
                  pl.BlockSpec((None, None, tf, d), lambda p, j: (layer, p % ne, j, 0))],
        out_specs=pl.BlockSpec((None, None, cap, d), lambda p, j: (p // ne, p % ne, 0, 0)),
        out_shape=jax.ShapeDtypeStruct((b, ne, cap, d), F32),
        scratch_shapes=[pltpu.VMEM((n_slots * dt, LANES), F32), pltpu.VMEM((cap, d), BF16),
                        pltpu.SemaphoreType.DMA(())],
        compiler_params=_cparams(("arbitrary", "arbitrary")),
        name="expert_swiglu",
    )(tab, tab, h, wg, wu, wd)


def _combine_kernel(pos_ref, rank_ref, aff_ref, x_ref, g_ref, y_hbm, o_ref, ybuf, acc, sem,
                    *, cap, nj, n_tiles, final_norm):
    b = pl.program_id(0)
    j = pl.program_id(1)
    t = b * nj + j
    ne = N_EXPERTS
    ring = COMBINE_AHEAD + 1
    assert ring & (ring - 1) == 0
    in_ring = lambda k: k & (ring - 1)

    def nominal(bq, e, jq, r):
        return ((pos_ref[bq * ne + e, jq] >> 3) << 3) + r * EWIN

    def window(bq, e, jq, r):
        return pl.multiple_of(jnp.minimum(nominal(bq, e, jq, r), cap - EWIN), SUBLANES)

    def start_fetch(ahead, r, slot):
        jq = j + ahead
        wrap = (jq >= nj).astype(I32)
        bq = b + wrap
        jq = jq - wrap * nj
        past = bq * nj + jq >= n_tiles
        bq = jnp.where(past, n_tiles // nj - 1, bq)
        jq = jnp.where(past, nj - 1, jq)
        for e in range(ne):
            pltpu.make_async_copy(y_hbm.at[bq, e, pl.ds(window(bq, e, jq, r), EWIN)],
                                  ybuf.at[slot, pl.ds(e * EWIN, EWIN)], sem.at[slot]).start()

    def wait_fetch(slot):
        pltpu.make_async_copy(ybuf.at[slot], ybuf.at[slot], sem.at[slot]).wait()

    rk = rank_ref[...]
    af = aff_ref[...]
    row = lax.broadcasted_iota(I32, (EWIN, LANES), 0).astype(F32)

    def add_round(r, slot):
        parts = []
        for e in range(ne):
            p0 = pos_ref[b * ne + e, j]
            local = row + (window(b, e, j, r) - p0).astype(F32)
            want = jnp.where(local >= jnp.maximum(nominal(b, e, j, r) - p0, 0).astype(F32), local, -2.0)
            parts.append(jnp.where(rk[e:e + 1, :] == want, af[e:e + 1, :], 0.0))
        gate_t = jnp.concatenate(parts, axis=0)
        onehot = jnp.where(gate_t != 0.0, 1.0, 0.0).T.astype(BF16)
        row_gate = jnp.sum(gate_t, axis=1, keepdims=True)
        scaled = jnp.where(row_gate != 0.0, ybuf[slot] * row_gate, 0.0)
        hi = scaled.astype(BF16)
        lo = (scaled - hi.astype(F32)).astype(BF16)
        return jnp.dot(onehot, hi, preferred_element_type=F32) + jnp.dot(onehot, lo, preferred_element_type=F32)

    @pl.when(t == 0)
    def _():
        for ahead in range(COMBINE_AHEAD):
            start_fetch(ahead, 0, ahead)

    start_fetch(COMBINE_AHEAD, 0, in_ring(t + COMBINE_AHEAD))
    slot0 = in_ring(t)
    wait_fetch(slot0)
    acc[...] = add_round(0, slot0)

    span = jnp.int32(1)
    for e in range(ne):
        span = jnp.maximum(span, pos_ref[b * ne + e, j + 1] - nominal(b, e, j, 0))
    n_rounds = (span + EWIN - 1) // EWIN

    def extra_round(r, carry):
        start_fetch(0, r, ring)
        wait_fetch(ring)
        acc[...] += add_round(r, ring)
        return carry

    lax.fori_loop(1, n_rounds, extra_round, 0)

    @pl.when(t == n_tiles - 1)
    def _():
        for ahead in range(1, COMBINE_AHEAD + 1):
            wait_fetch(in_ring(t + ahead))

    y = x_ref[...] + acc[...]
    if final_norm:
        ms = jnp.mean(y * y, axis=-1, keepdims=True)
        y = (y * lax.rsqrt(ms + EPS)) * g_ref[...]
    o_ref[...] = y


def _combine(x, y, pos, rank, aff, g_final, *, cap, final_norm):
    b, s, d = x.shape
    nj = s // LANES
    ne = N_EXPERTS
    pos_tab = jnp.concatenate([pos, jnp.full((b, ne, 1), cap, I32)], axis=2).reshape(b * ne, nj + 1)
    rank_t = jnp.transpose(rank, (0, 2, 1, 3))
    return pl.pallas_call(
        functools.partial(_combine_kernel, cap=cap, nj=nj, n_tiles=b * nj, final_norm=final_norm),
        grid_spec=pltpu.PrefetchScalarGridSpec(
            num_scalar_prefetch=1,
            grid=(b, nj),
            in_specs=[pl.BlockSpec((None, None, ne, LANES), lambda bi, j, tab: (bi, j, 0, 0)),
                      pl.BlockSpec((None, ne, LANES), lambda bi, j, tab: (bi, 0, j)),
                      pl.BlockSpec((None, LANES, d), lambda bi, j, tab: (bi, j, 0)),
                      pl.BlockSpec((1, d), lambda bi, j, tab: (0, 0)),
                      pl.BlockSpec(memory_space=pl.ANY)],
            out_specs=pl.BlockSpec((None, LANES, d), lambda bi, j, tab: (bi, j, 0)),
            scratch_shapes=[pltpu.VMEM((COMBINE_AHEAD + 2, ne * EWIN, d), F32), pltpu.VMEM((LANES, d), F32),
                            pltpu.SemaphoreType.DMA((COMBINE_AHEAD + 2,))]),
        out_shape=jax.ShapeDtypeStruct((b, s, d), F32),
        compiler_params=_cparams(("arbitrary", "arbitrary")),
        name="moe_combine",
    )(pos_tab, rank_t, aff, x, g_final.reshape(1, d), y)


def kernel(x, rel_bias, g_mix, w_in, attn_sink, w_attn_proj, w_fourier_proj, w_out, g_ffn, w_router,
           w_exp_gate, w_exp_up, w_exp_down, g_final):
    b, s, d = x.shape
    depth = g_mix.shape[0]
    cap = CAPACITY_FACTOR * s // N_EXPERTS
    bias = _bias_table(rel_bias)
    tables = _fourier_tables(int(round(np.sqrt(s))))
    for l in range(depth):
        q, k, v, vs, f, ga, gb = _inproj(x.reshape(b * s, d), g_mix[l], w_in[l].astype(BF16), tm=1024, chunk=256)
        shp = lambda a: a.reshape(b, s, a.shape[-1])
        attn = _attention(shp(q), shp(k), shp(v), shp(vs), attn_sink[l], bias, blocks_per_step=4)
        four = _fourier(shp(f), tables, rows=8)
        xn, h, aff = _outproj(attn, four, shp(ga), shp(gb), x,
                              w_attn_proj[l].astype(BF16), w_fourier_proj[l].astype(BF16),
                              w_out[l].astype(BF16), g_ffn[l], w_router[l].T, tm=512, chunk=256)
        idx, pos, rank = _routing(aff, cap=cap)
        y = _expert_ffn(h, idx, w_exp_gate, w_exp_up, w_exp_down, l, cap=cap, tf=256, m_chunk=min(cap, 512))
        x = _combine(xn, y, pos, rank, aff, g_final, cap=cap, final_norm=(l == depth - 1))
    return x
```

```python
import functools

import numpy as np
import jax
import jax.numpy as jnp
from jax import lax
from jax.experimental import pallas as pl
from jax.experimental.pallas import tpu as pltpu

F32 = jnp.float32
BF16 = jnp.bfloat16
I32 = jnp.int32

N_HEADS = 8
N_KV_HEADS = 2
HEAD_DIM = 64
ATTN_WIDTH = N_HEADS * HEAD_DIM
KV_WIDTH = N_KV_HEADS * HEAD_DIM
WINDOW = 128
BLOCK = 128
N_BUCKETS = 32
MAX_DISTANCE = 128
N_FOURIER_GROUPS = 4
FOURIER_GROUP_DIM = 128
FOURIER_WIDTH = N_FOURIER_GROUPS * FOURIER_GROUP_DIM
N_EXPERTS = 16
CAPACITY_FACTOR = 2
EPS = 1e-6
NEG_INF = -1e30

LANES = 128
LANE_BITS = 7
SUBLANES = 8
VMEM_LIMIT = 56 * 1024 * 1024
EWIN = 32
DFT_PASSES = 1
COMBINE_AHEAD = 3


def _cparams(sem):
    return pltpu.CompilerParams(dimension_semantics=sem, vmem_limit_bytes=VMEM_LIMIT)


def _nt_dot(a, b, **kw):
    return lax.dot_general(a, b, (((1,), (1,)), ((), ())), preferred_element_type=F32, **kw)


def _sigmoid(x):
    return 1.0 / (1.0 + jnp.exp(-x))


def _t5_bucket(rel):
    half = N_BUCKETS // 2
    max_exact = half // 2
    ret = (rel > 0).astype(jnp.int32) * half
    n = jnp.abs(rel)
    nf = jnp.maximum(n, 1).astype(jnp.float32)
    large = max_exact + (jnp.log(nf / max_exact) / np.float32(np.log(MAX_DISTANCE / max_exact))
                         * (half - max_exact)).astype(jnp.int32)
    large = jnp.minimum(large, half - 1)
    return ret + jnp.where(n < max_exact, n, large)


def _bias_kernel(relb_ref, bucket_ref, o_ref):
    bk = bucket_ref[...]
    col = lax.broadcasted_iota(I32, bk.shape, 1)
    for h in range(N_HEADS):
        acc = jnp.full(bk.shape, NEG_INF, F32)
        for b in range(N_BUCKETS):
            acc = jnp.where(bk == b, relb_ref[b, h], acc)
        o_ref[0, h] = jnp.where(col >= BLOCK, acc, NEG_INF)
        o_ref[1, h] = acc
        o_ref[2, h] = jnp.where(col < 2 * BLOCK, acc, NEG_INF)


def _bias_table(rel_bias):
    q_loc = jnp.arange(BLOCK)
    k_loc = jnp.arange(3 * BLOCK) - BLOCK
    rel = k_loc[None, :] - q_loc[:, None]
    bucket = jnp.where(jnp.abs(rel) <= WINDOW, _t5_bucket(rel), -1).astype(I32)
    table = pl.pallas_call(
        _bias_kernel,
        out_shape=jax.ShapeDtypeStruct((3, N_HEADS, BLOCK, 3 * BLOCK), F32),
        in_specs=[pl.BlockSpec(memory_space=pltpu.SMEM),
                  pl.BlockSpec(memory_space=pltpu.VMEM)],
        out_specs=pl.BlockSpec(memory_space=pltpu.VMEM),
        name="rel_bias_table",
    )(rel_bias.astype(F32), bucket)
    return table.reshape(3, N_HEADS * BLOCK, 3 * BLOCK)


def _inproj_kernel(x_ref, g_ref, w_ref, q_ref, k_ref, v_ref, vs_ref, f_ref, ga_ref, gb_ref, *, d_model, chunk):
    for r in range(x_ref.shape[0] // chunk):
        rs = pl.ds(r * chunk, chunk)
        x = x_ref[rs, :]
        ms = jnp.mean(x * x, axis=-1, keepdims=True)
        h = (x * lax.rsqrt(ms + EPS)) * g_ref[...]
        z = jnp.dot(h.astype(BF16), w_ref[...], preferred_element_type=F32)
        o = 0
        q_ref[rs, :] = (z[:, o:o + ATTN_WIDTH] * (HEAD_DIM ** -0.5)).astype(BF16)
        o += ATTN_WIDTH
        k_ref[rs, :] = z[:, o:o + KV_WIDTH].astype(BF16)
        o += KV_WIDTH
        v_ref[rs, :] = z[:, o:o + KV_WIDTH].astype(BF16)
        half = KV_WIDTH // 2
        vs_ref[rs, :] = jnp.concatenate([z[:, o + half:o + KV_WIDTH], z[:, o:o + half]], axis=1).astype(BF16)
        o += KV_WIDTH
        f_ref[rs, :] = z[:, o:o + FOURIER_WIDTH]
        o += FOURIER_WIDTH
        ga_ref[rs, :] = _sigmoid(z[:, o:o + d_model]).astype(BF16)
        o += d_model
        gb_ref[rs, :] = _sigmoid(z[:, o:o + d_model]).astype(BF16)


def _inproj(x2, g, w_bf16, *, tm, chunk):
    n, d = x2.shape
    in_w = w_bf16.shape[1]
    widths = (ATTN_WIDTH, KV_WIDTH, KV_WIDTH, KV_WIDTH, FOURIER_WIDTH, d, d)
    dtypes = (BF16, BF16, BF16, BF16, F32, BF16, BF16)
    row = lambda i: (i, 0)
    return pl.pallas_call(
        functools.partial(_inproj_kernel, d_model=d, chunk=chunk),
        grid=(n // tm,),
        in_specs=[pl.BlockSpec((tm, d), row),
                  pl.BlockSpec((1, d), lambda i: (0, 0)),
                  pl.BlockSpec((d, in_w), lambda i: (0, 0))],
        out_specs=[pl.BlockSpec((tm, w), row) for w in widths],
        out_shape=[jax.ShapeDtypeStruct((n, w), dt) for w, dt in zip(widths, dtypes)],
        compiler_params=_cparams(("parallel",)),
        name="inproj",
    )(x2, g.reshape(1, d), w_bf16)


def _attn_kernel(sink_ref, q_ref, kp_ref, ko_ref, kn_ref, vp_ref, vo_ref, vn_ref, wp_ref, wo_ref, wn_ref,
                 bias_ref, o_ref, *, n_blocks, blocks_per_step):
    i = pl.program_id(1)
    kcat = jnp.concatenate([kp_ref[...], ko_ref[...], kn_ref[...]], axis=0)
    vcat = jnp.concatenate([vp_ref[...], vo_ref[...], vn_ref[...]], axis=0)
    wcat = jnp.concatenate([wp_ref[...], wo_ref[...], wn_ref[...]], axis=0)
    group = N_HEADS // N_KV_HEADS
    head = lambda a, h: a[:, h * HEAD_DIM:(h + 1) * HEAD_DIM]
    sink = jnp.concatenate([jnp.full((BLOCK, LANES), sink_ref[h], F32) for h in range(N_HEADS)], axis=0)
    low_v = lax.broadcasted_iota(I32, vcat.shape, 1) < HEAD_DIM
    low_o = lax.broadcasted_iota(I32, (BLOCK, LANES), 1) < HEAD_DIM
    vx = {(kv, half): jnp.where(low_v if half == 0 else jnp.logical_not(low_v),
                                vcat if kv == half else wcat, jnp.ones_like(vcat))
          for kv in range(N_KV_HEADS) for half in range(2)}
    for t in range(blocks_per_step):
        blk = i * blocks_per_step + t
        kind = jnp.where(blk == 0, 0, jnp.where(blk == n_blocks - 1, 2, 1))
        kw = kcat[t * BLOCK:(t + 3) * BLOCK]
        qb = q_ref[t * BLOCK:(t + 1) * BLOCK, :]
        s = jnp.concatenate([_nt_dot(head(qb, h), head(kw, h // group)) for h in range(N_HEADS)], axis=0)
        s = s + bias_ref[kind]
        m = jnp.maximum(jnp.broadcast_to(jnp.max(s, axis=-1, keepdims=True), sink.shape), sink)
        p = jnp.exp(s - jnp.concatenate([m] * 3, axis=1)).astype(BF16)
        sink_term = jnp.exp(sink - m)
        for pair in range(N_HEADS // 2):
            halves = []
            for half in range(2):
                h = 2 * pair + half
                rows = slice(h * BLOCK, (h + 1) * BLOCK)
                out = jnp.dot(p[rows], vx[h // group, half][t * BLOCK:(t + 3) * BLOCK],
                              preferred_element_type=F32)
                denom = pltpu.roll(out, HEAD_DIM, axis=1) + sink_term[rows]
                halves.append(out * (1.0 / denom))
            tile = jnp.where(low_o, halves[0], halves[1])
            o_ref[t * BLOCK:(t + 1) * BLOCK, pair * LANES:(pair + 1) * LANES] = tile.astype(o_ref.dtype)


def _attention(q, k, v, v_swapped, sink, bias, *, blocks_per_step):
    b, s, _ = q.shape
    assert N_KV_HEADS == 2 and KV_WIDTH == LANES and N_HEADS % 2 == 0
    n_blocks = s // BLOCK
    tq = blocks_per_step * BLOCK
    prev = lambda bi, i: (bi, jnp.maximum(i * blocks_per_step - 1, 0), 0)
    own = lambda bi, i: (bi, i, 0)
    nxt = lambda bi, i: (bi, jnp.minimum((i + 1) * blocks_per_step, n_blocks - 1), 0)
    kv_specs = [pl.BlockSpec((None, BLOCK, KV_WIDTH), prev),
                pl.BlockSpec((None, tq, KV_WIDTH), own),
                pl.BlockSpec((None, BLOCK, KV_WIDTH), nxt)]
    return pl.pallas_call(
        functools.partial(_attn_kernel, n_blocks=n_blocks, blocks_per_step=blocks_per_step),
        grid=(b, s // tq),
        in_specs=[pl.BlockSpec(memory_space=pltpu.SMEM),
                  pl.BlockSpec((None, tq, ATTN_WIDTH), own)] + kv_specs * 3
                 + [pl.BlockSpec((3, N_HEADS * BLOCK, 3 * BLOCK), lambda bi, i: (0, 0, 0))],
        out_specs=pl.BlockSpec((None, tq, ATTN_WIDTH), own),
        out_shape=jax.ShapeDtypeStruct((b, s, ATTN_WIDTH), BF16),
        compiler_params=_cparams(("parallel", "parallel")),
        name="window_attention",
    )(sink.astype(F32), q, k, k, k, v, v, v, v_swapped, v_swapped, v_swapped, bias)


def _hi_lo(x):
    hi = x.astype(BF16)
    return hi, (x - hi.astype(F32)).astype(BF16)


def _dft_operand(x, axis):
    if DFT_PASSES == 1:
        return x.astype(BF16)
    hi, lo = _hi_lo(x)
    return jnp.concatenate([hi, lo, hi], axis=axis)


def _dft1_kernel(x_ref, a_ref, tr_ref, ti_ref, x2, *, rows):
    a = a_ref[...]
    na = a.shape[0] // 2
    n_slabs = x2.shape[0]
    lanes = lambda c: slice(c * LANES, (c + 1) * LANES)
    for c in range(n_slabs):
        x2[c] = x_ref[:, :, lanes(c)].reshape(x2.shape[1:])
    for jj in range(rows):
        strided = pl.ds(jj, na, stride=rows)
        xj = jnp.concatenate([x2[c, strided, :] for c in range(n_slabs)], axis=1)
        t = jnp.dot(a, _dft_operand(xj, 0), preferred_element_type=F32)
        tr_ref[:, jj, :] = t[:na]
        ti_ref[:, jj, :] = t[na:]


def _dft2_kernel(tr_ref, ti_ref, gr_ref, gi_ref, cs_ref, o_ref, o2, *, rows):
    cs = cs_ref[...]
    na = o_ref.shape[0]
    for aa in range(rows):
        t = _dft_operand(jnp.concatenate([tr_ref[aa], ti_ref[aa]], axis=0), 0)
        zr = jnp.dot(gr_ref[aa], t, preferred_element_type=F32)
        zi = jnp.dot(gi_ref[aa], t, preferred_element_type=F32)
        for g in range(N_FOURIER_GROUPS):
            sl = slice(g * FOURIER_GROUP_DIM, (g + 1) * FOURIER_GROUP_DIM)
            z = _dft_operand(jnp.concatenate([zr[:, sl], zi[:, sl]], axis=1), 1)
            y = jnp.dot(z, cs, preferred_element_type=F32)
            o2[g, pl.ds(aa, na, stride=rows), :] = y
    for g in range(N_FOURIER_GROUPS):
        sl = slice(g * FOURIER_GROUP_DIM, (g + 1) * FOURIER_GROUP_DIM)
        o_ref[:, :, sl] = o2[g].reshape(na, rows, FOURIER_GROUP_DIM)


def _fourier_tables(na):
    def pieces(a, axis):
        hi = a.astype(BF16)
        if DFT_PASSES == 1:
            return jnp.asarray(hi)
        lo = (a - hi.astype(np.float64)).astype(BF16)
        return jnp.asarray(np.concatenate([hi, hi, lo], axis=axis))

    n = na * na
    k = np.arange(na, dtype=np.float64)
    ang = 2.0 * np.pi * np.outer(k, k) / na
    a1 = pieces(np.concatenate([np.cos(ang), -np.sin(ang)], axis=0), 1)
    ka = np.arange(na, dtype=np.float64)[:, None, None]
    kb = np.arange(na, dtype=np.float64)[None, :, None]
    nl = np.arange(na, dtype=np.float64)[None, None, :]
    th = 2.0 * np.pi * ((nl * (ka + na * kb)) % n) / n
    gr, gi = np.cos(th), -np.sin(th)
    g_re = pieces(np.concatenate([gr, -gi], axis=2), 2)
    g_im = pieces(np.concatenate([gi, gr], axis=2), 2)
    kc = np.arange(FOURIER_GROUP_DIM, dtype=np.float64)
    angc = 2.0 * np.pi * np.outer(kc, kc) / FOURIER_GROUP_DIM
    scale = 1.0 / np.sqrt(float(n) * FOURIER_GROUP_DIM)
    cs = pieces(np.concatenate([np.cos(angc), np.sin(angc)], axis=0) * scale, 0)
    return a1, g_re, g_im, cs


def _fourier(f, tables, *, rows):
    b, s, w = f.shape
    na = int(round(np.sqrt(s)))
    assert na * na == s and na % rows == 0
    a1, g_re, g_im, cs = tables
    x4 = f.reshape(b, na, na, w)
    const2 = lambda bi, j: (0, 0)
    tr, ti = pl.pallas_call(
        functools.partial(_dft1_kernel, rows=rows),
        grid=(b, na // rows),
        in_specs=[pl.BlockSpec((None, na, rows, w), lambda bi, j: (bi, 0, j, 0)),
                  pl.BlockSpec(a1.shape, const2)],
        out_specs=[pl.BlockSpec((None, na, rows, w), lambda bi, j: (bi, 0, j, 0))] * 2,
        out_shape=[jax.ShapeDtypeStruct((b, na, na, w), F32)] * 2,
        scratch_shapes=[pltpu.VMEM((w // LANES, na * rows, LANES), F32)],
        compiler_params=_cparams(("parallel", "parallel")),
        name="seq_dft_stage1",
    )(x4, a1)
    y = pl.pallas_call(
        functools.partial(_dft2_kernel, rows=rows),
        grid=(b, na // rows),
        in_specs=[pl.BlockSpec((None, rows, na, w), lambda bi, a: (bi, a, 0, 0))] * 2
                 + [pl.BlockSpec((rows,) + g_re.shape[1:], lambda bi, a: (a, 0, 0))] * 2
                 + [pl.BlockSpec(cs.shape, const2)],
        out_specs=pl.BlockSpec((None, na, rows, w), lambda bi, a: (bi, 0, a, 0)),
        out_shape=jax.ShapeDtypeStruct((b, na, na, w), F32),
        scratch_shapes=[pltpu.VMEM((N_FOURIER_GROUPS, na * rows, FOURIER_GROUP_DIM), F32)],
        compiler_params=_cparams(("parallel", "parallel")),
        name="seq_dft_stage2",
    )(tr, ti, g_re, g_im, cs)
    return y.reshape(b, s, w)


def _outproj_kernel(a_ref, f_ref, ga_ref, gb_ref, x_ref, wa_ref, wf_ref, wo_ref, g_ref, wr_ref,
                    xn_ref, h_ref, aff_ref, *, chunk):
    for r in range(x_ref.shape[0] // chunk):
        rs = pl.ds(r * chunk, chunk)
        a = jnp.dot(a_ref[rs, :], wa_ref[...], preferred_element_type=F32)
        fo = jnp.dot(f_ref[rs, :].astype(BF16), wf_ref[...], preferred_element_type=F32)
        merged = ga_ref[rs, :].astype(F32) * a + gb_ref[rs, :].astype(F32) * fo
        xn = x_ref[rs, :] + jnp.dot(merged.astype(BF16), wo_ref[...], preferred_element_type=F32)
        xn_ref[rs, :] = xn
        ms = jnp.mean(xn * xn, axis=-1, keepdims=True)
        h = (xn * lax.rsqrt(ms + EPS)) * g_ref[...]
        dt = h.shape[1] // LANES
        for c in range(dt):
            h_ref[pl.ds(r * chunk * dt + c, chunk, stride=dt), :] = h[:, c * LANES:(c + 1) * LANES]
        logits = _nt_dot(wr_ref[...], h.astype(BF16))
        e = jnp.exp(logits - jnp.max(logits, axis=0, keepdims=True))
        aff_ref[:, rs] = e / jnp.sum(e, axis=0, keepdims=True)


def _outproj(attn, four, ga, gb, x, wa, wf, wo, g, wr_t, *, tm, chunk):
    b, s, d = x.shape
    dt = d // LANES
    tok = lambda bi, i: (bi, i, 0)
    const = lambda bi, i: (0, 0)
    xn, h, aff = pl.pallas_call(
        functools.partial(_outproj_kernel, chunk=chunk),
        grid=(b, s // tm),
        in_specs=[pl.BlockSpec((None, tm, ATTN_WIDTH), tok),
                  pl.BlockSpec((None, tm, FOURIER_WIDTH), tok),
                  pl.BlockSpec((None, tm, d), tok),
                  pl.BlockSpec((None, tm, d), tok),
                  pl.BlockSpec((None, tm, d), tok),
                  pl.BlockSpec((ATTN_WIDTH, d), const),
                  pl.BlockSpec((FOURIER_WIDTH, d), const),
                  pl.BlockSpec((d, d), const),
                  pl.BlockSpec((1, d), const),
                  pl.BlockSpec((N_EXPERTS, d), const)],
        out_specs=[pl.BlockSpec((None, tm, d), tok),
                   pl.BlockSpec((None, tm * dt, LANES), tok),
                   pl.BlockSpec((None, N_EXPERTS, tm), lambda bi, i: (bi, 0, i))],
        out_shape=[jax.ShapeDtypeStruct((b, s, d), F32),
                   jax.ShapeDtypeStruct((b, s * dt, LANES), F32),
                   jax.ShapeDtypeStruct((b, N_EXPERTS, s), F32)],
        compiler_params=_cparams(("parallel", "parallel")),
        name="outproj_router",
    )(attn, four, ga, gb, x, wa, wf, wo, g.reshape(1, d), wr_t.astype(BF16))
    return xn, h.reshape(b, s, dt, LANES), aff


def _tri(n, kind):
    r = lax.broadcasted_iota(I32, (n, n), 0)
    c = lax.broadcasted_iota(I32, (n, n), 1)
    cond = {"row_le_col": r <= c, "row_lt_col": r < c, "col_le_row": c <= r, "col_lt_row": c < r}[kind]
    return jnp.where(cond, 1.0, 0.0).astype(BF16)


def _split128(v):
    hi = jnp.floor(v * (1.0 / LANES))
    return hi, v - hi * LANES


def _routing_kernel(aff_ref, idx_ref, pos_ref, rank_ref, gt_scr, eq_scr, need_scr, *, cap):
    ne, nj, ni = aff_ref.shape
    bits = lax.bitcast_convert_type(aff_ref[...], I32)

    def bisect(it, thr):
        cand = thr | jnp.left_shift(jnp.int32(1), 30 - it)
        cnt = jnp.sum(jnp.where(bits >= cand, 1.0, 0.0), axis=(1, 2), keepdims=True)
        return jnp.where(cnt >= cap, cand, thr)

    thr = lax.fori_loop(0, 31, bisect, jnp.zeros((ne, 1, 1), I32))
    gt = jnp.where(bits > thr, 1.0, 0.0)
    gt_scr[...] = gt
    eq_scr[...] = jnp.where(bits == thr, 1.0, 0.0)
    need = cap - jnp.sum(gt, axis=(1, 2), keepdims=True)
    need_scr[...] = jnp.broadcast_to(need, (ne, SUBLANES, ni))

    ones_sq = jnp.ones((ni, ni), BF16)
    ones_row = jnp.ones((SUBLANES, ni), BF16)
    ones_rowj = jnp.ones((SUBLANES, nj), BF16)
    u_incl = _tri(ni, "row_le_col")
    uj_strict = _tri(nj, "row_lt_col")
    lj_strict = _tri(nj, "col_lt_row")
    l_incl = _tri(ni, "col_le_row")
    bdot = lambda a, b: jnp.dot(a, b, preferred_element_type=F32)

    c_lane = lax.broadcasted_iota(I32, (nj, cap), 1).astype(F32)
    c_row = lax.broadcasted_iota(I32, (SUBLANES, cap), 1).astype(F32)
    j_sub = lax.broadcasted_iota(I32, (nj, cap), 0).astype(F32)

    def slots(e, carry):
        eq = eq_scr[e]
        eqb = eq.astype(BF16)
        tie_rank = bdot(lj_strict, bdot(eqb, ones_sq).astype(BF16)) + bdot(eqb, u_incl)
        take = jnp.where(tie_rank <= need_scr[e][0:1, :], eq, 0.0)
        sel = jnp.maximum(gt_scr[e], take)
        selb = sel.astype(BF16)
        lc = bdot(selb, u_incl)
        lc_t = _nt_dot(l_incl, selb)
        n_rep = bdot(selb, ones_sq)
        s_rep = bdot(lj_strict, n_rep.astype(BF16)) + n_rep
        n_lane = _nt_dot(ones_row, selb)
        pex_lane = bdot(n_lane.astype(BF16), uj_strict)
        s_wide = jnp.concatenate([s_rep] * (cap // ni), axis=1) if cap > ni else s_rep[:, :cap]
        jc = bdot(ones_rowj, jnp.where(s_wide <= c_lane, 1.0, 0.0).astype(BF16))
        onehot = jnp.where(j_sub == jnp.broadcast_to(jc[0:1], (nj, cap)), 1.0, 0.0).astype(BF16)
        phi, plo = _split128(pex_lane)
        r = c_row - (bdot(phi.astype(BF16), onehot) * LANES + bdot(plo.astype(BF16), onehot))
        lcs = bdot(lc_t.astype(BF16), onehot)
        ic = bdot(ones_row, jnp.where(lcs <= jnp.broadcast_to(r[0:1], (ni, cap)), 1.0, 0.0).astype(BF16))
        idx_ref[e] = (jc * ni + ic).astype(I32)
        pos_ref[e] = pex_lane.astype(I32)
        rank_ref[e] = jnp.where(sel > 0.0, lc - 1.0, -1.0)
        return carry

    lax.fori_loop(0, ne, slots, 0)


def _routing(aff, *, cap):
    b, ne, s = aff.shape
    nj = s // LANES
    aff4 = aff.reshape(b, ne, nj, LANES)
    per_b = lambda bi: (bi, 0, 0, 0)
    idx, pos, rank = pl.pallas_call(
        functools.partial(_routing_kernel, cap=cap),
        grid=(b,),
        in_specs=[pl.BlockSpec((None, ne, nj, LANES), per_b)],
        out_specs=[pl.BlockSpec((None, ne, SUBLANES, cap), per_b),
                   pl.BlockSpec((None, ne, SUBLANES, nj), per_b),
                   pl.BlockSpec((None, ne, nj, LANES), per_b)],
        out_shape=[jax.ShapeDtypeStruct((b, ne, SUBLANES, cap), I32),
                   jax.ShapeDtypeStruct((b, ne, SUBLANES, nj), I32),
                   jax.ShapeDtypeStruct((b, ne, nj, LANES), F32)],
        scratch_shapes=[pltpu.VMEM((ne, nj, LANES), F32)] * 2 + [pltpu.VMEM((ne, SUBLANES, LANES), F32)],
        compiler_params=_cparams(("parallel",)),
        name="expert_choice_routing",
    )(aff4)
    return idx[:, :, 0, :], pos[:, :, 0, :], rank


def _ffn_kernel(idx_cur, idx_nxt, h_hbm, wg_ref, wu_ref, wd_ref, y_ref, xs, xsb, gsem,
                *, cap, n_pairs, n_ftiles, per_step, m_chunk):
    p = pl.program_id(0)
    j = pl.program_id(1)
    n_slots = n_ftiles * per_step
    pitch = xs.shape[0] // n_slots
    dt = pitch - 1
    last_pair = n_pairs - 1
    b_cur = p // N_EXPERTS
    b_nxt = jnp.minimum(p + 1, last_pair) // N_EXPERTS

    def gather(ref, bq, c):
        cc = jnp.minimum(c, cap - 1)
        tok = ref[cc >> LANE_BITS, cc & (LANES - 1)]
        dst = xs.at[pl.ds(c * pitch, dt)]
        return pltpu.make_async_copy(h_hbm.at[bq, tok], dst, gsem)

    def wait_gathers():
        tiles = xs.at[pl.ds(0, n_slots * dt)]
        pltpu.make_async_copy(tiles, tiles, gsem).wait()

    @pl.when((p == 0) & (j == 0))
    def _():
        def start(c, carry):
            gather(idx_cur, b_cur, c).start()
            return carry

        lax.fori_loop(0, n_slots, start, 0)

    @pl.when(j == 0)
    def _():
        wait_gathers()
        for c in range(dt):
            xsb[:, c * LANES:(c + 1) * LANES] = xs[pl.ds(c, cap, stride=pitch), :].astype(BF16)
        y_ref[...] = jnp.zeros_like(y_ref)

    wg = wg_ref[...].astype(BF16)
    wu = wu_ref[...].astype(BF16)
    wd = wd_ref[...].astype(BF16)
    n_groups = cap // m_chunk
    per_group = per_step // n_groups
    for m in range(n_groups):
        for t in range(per_group):
            gather(idx_nxt, b_nxt, j * per_step + (m * per_group + t)).start()
        rows = pl.ds(m * m_chunk, m_chunk)
        x = xsb[rows, :]
        hg = jnp.dot(x, wg, preferred_element_type=F32)
        hu = jnp.dot(x, wu, preferred_element_type=F32)
        act = (hg * _sigmoid(hg)) * hu
        y_ref[rows, :] += jnp.dot(act.astype(BF16), wd, preferred_element_type=F32)

    @pl.when((p == last_pair) & (j == n_ftiles - 1))
    def _():
        wait_gathers()


def _expert_ffn(h, idx, wg, wu, wd, layer, *, cap, tf, m_chunk):
    b, s, dt, _ = h.shape
    d = dt * LANES
    ne, dff = wg.shape[1], wg.shape[3]
    n_ftiles = dff // tf
    n_pairs = b * ne
    n_groups = cap // m_chunk
    per_step = -(-cap // (n_ftiles * n_groups)) * n_groups
    n_slots = n_ftiles * per_step
    tab = idx.reshape(n_pairs, cap // LANES, LANES)
    tab_spec = lambda f: pl.BlockSpec((None, cap // LANES, LANES), lambda p, j: (f(p), 0, 0),
                                      memory_space=pltpu.SMEM)
    return pl.pallas_call(
        functools.partial(_ffn_kernel, cap=cap, n_pairs=n_pairs, n_ftiles=n_ftiles, per_step=per_step,
                          m_chunk=m_chunk),
        grid=(n_pairs, n_ftiles),
        in_specs=[tab_spec(lambda p: p),
                  tab_spec(lambda p: jnp.minimum(p + 1, n_pairs - 1)),
                  pl.BlockSpec(memory_space=pl.ANY),
                  pl.BlockSpec((None, None, d, tf), lambda p, j: (layer, p % ne, 0, j)),
                  pl.BlockSpec((None, None, d, tf), lambda p, j: (layer, p % ne, 0, j)),
                  pl.BlockSpec((None, None, tf, d), lambda p, j: (layer, p % ne, j, 0))],
        out_specs=pl.BlockSpec((None, None, cap, d), lambda p, j: (p // ne, p % ne, 0, 0)),
        out_shape=jax.ShapeDtypeStruct((b, ne, cap, d), F32),
        scratch_shapes=[pltpu.VMEM((n_slots * (dt + 1), LANES), F32), pltpu.VMEM((cap, d), BF16),
                        pltpu.SemaphoreType.DMA(())],
        compiler_params=_cparams(("arbitrary", "arbitrary")),
        name="expert_swiglu",
    )(tab, tab, h, wg, wu, wd)


def _combine_kernel(pos_ref, rank_ref, aff_ref, x_ref, g_ref, y_hbm, o_ref, ybuf, acc, sem,
                    *, cap, nj, n_tiles, final_norm):
    b = pl.program_id(0)
    j = pl.program_id(1)
    t = b * nj + j
    ne = N_EXPERTS
    ring = COMBINE_AHEAD + 1
    assert ring & (ring - 1) == 0
    in_ring = lambda k: k & (ring - 1)

    def nominal(bq, e, jq, r):
        return ((pos_ref[bq * ne + e, jq] >> 3) << 3) + r * EWIN

    def window(bq, e, jq, r):
        return pl.multiple_of(jnp.minimum(nominal(bq, e, jq, r), cap - EWIN), SUBLANES)

    def start_fetch(ahead, r, slot):
        jq = j + ahead
        wrap = (jq >= nj).astype(I32)
        bq = b + wrap
        jq = jq - wrap * nj
        past = bq * nj + jq >= n_tiles
        bq = jnp.where(past, n_tiles // nj - 1, bq)
        jq = jnp.where(past, nj - 1, jq)
        for e in range(ne):
            pltpu.make_async_copy(y_hbm.at[bq, e, pl.ds(window(bq, e, jq, r), EWIN)],
                                  ybuf.at[slot, pl.ds(e * EWIN, EWIN)], sem.at[slot]).start()

    def wait_fetch(slot):
        pltpu.make_async_copy(ybuf.at[slot], ybuf.at[slot], sem.at[slot]).wait()

    rk = rank_ref[...]
    af = aff_ref[...]
    row = lax.broadcasted_iota(I32, (EWIN, LANES), 0).astype(F32)

    def add_round(r, slot):
        parts = []
        for e in range(ne):
            p0 = pos_ref[b * ne + e, j]
            local = row + (window(b, e, j, r) - p0).astype(F32)
            want = jnp.where(local >= jnp.maximum(nominal(b, e, j, r) - p0, 0).astype(F32), local, -2.0)
            parts.append(jnp.where(rk[e:e + 1, :] == want, af[e:e + 1, :], 0.0))
        gate_t = jnp.concatenate(parts, axis=0)
        onehot = jnp.where(gate_t != 0.0, 1.0, 0.0).T.astype(BF16)
        row_gate = jnp.sum(gate_t, axis=1, keepdims=True)
        scaled = jnp.where(row_gate != 0.0, ybuf[slot] * row_gate, 0.0)
        hi = scaled.astype(BF16)
        lo = (scaled - hi.astype(F32)).astype(BF16)
        return jnp.dot(onehot, hi, preferred_element_type=F32) + jnp.dot(onehot, lo, preferred_element_type=F32)

    @pl.when(t == 0)
    def _():
        for ahead in range(COMBINE_AHEAD):
            start_fetch(ahead, 0, ahead)

    start_fetch(COMBINE_AHEAD, 0, in_ring(t + COMBINE_AHEAD))
    slot0 = in_ring(t)
    wait_fetch(slot0)
    acc[...] = add_round(0, slot0)

    span = jnp.int32(1)
    for e in range(ne):
        span = jnp.maximum(span, pos_ref[b * ne + e, j + 1] - nominal(b, e, j, 0))
    n_rounds = (span + EWIN - 1) // EWIN

    def extra_round(r, carry):
        start_fetch(0, r, ring)
        wait_fetch(ring)
        acc[...] += add_round(r, ring)
        return carry

    lax.fori_loop(1, n_rounds, extra_round, 0)

    @pl.when(t == n_tiles - 1)
    def _():
        for ahead in range(1, COMBINE_AHEAD + 1):
            wait_fetch(in_ring(t + ahead))

    y = x_ref[...] + acc[...]
    if final_norm:
        ms = jnp.mean(y * y, axis=-1, keepdims=True)
        y = (y * lax.rsqrt(ms + EPS)) * g_ref[...]
    o_ref[...] = y


def _combine(x, y, pos, rank, aff, g_final, *, cap, final_norm):
    b, s, d = x.shape
    nj = s // LANES
    ne = N_EXPERTS
    pos_tab = jnp.concatenate([pos, jnp.full((b, ne, 1), cap, I32)], axis=2).reshape(b * ne, nj + 1)
    rank_t = jnp.transpose(rank, (0, 2, 1, 3))
    return pl.pallas_call(
        functools.partial(_combine_kernel, cap=cap, nj=nj, n_tiles=b * nj, final_norm=final_norm),
        grid_spec=pltpu.PrefetchScalarGridSpec(
            num_scalar_prefetch=1,
            grid=(b, nj),
            in_specs=[pl.BlockSpec((None, None, ne, LANES), lambda bi, j, tab: (bi, j, 0, 0)),
                      pl.BlockSpec((None, ne, LANES), lambda bi, j, tab: (bi, 0, j)),
                      pl.BlockSpec((None, LANES, d), lambda bi, j, tab: (bi, j, 0)),
                      pl.BlockSpec((1, d), lambda bi, j, tab: (0, 0)),
                      pl.BlockSpec(memory_space=pl.ANY)],
            out_specs=pl.BlockSpec((None, LANES, d), lambda bi, j, tab: (bi, j, 0)),
            scratch_shapes=[pltpu.VMEM((COMBINE_AHEAD + 2, ne * EWIN, d), F32), pltpu.VMEM((LANES, d), F32),
                            pltpu.SemaphoreType.DMA((COMBINE_AHEAD + 2,))]),
        out_shape=jax.ShapeDtypeStruct((b, s, d), F32),
        compiler_params=_cparams(("arbitrary", "arbitrary")),
        name="moe_combine",
    )(pos_tab, rank_t, aff, x, g_final.reshape(1, d), y)


def kernel(x, rel_bias, g_mix, w_in, attn_sink, w_attn_proj, w_fourier_proj, w_out, g_ffn, w_router,
           w_exp_gate, w_exp_up, w_exp_down, g_final):
    b, s, d = x.shape
    depth = g_mix.shape[0]
    cap = CAPACITY_FACTOR * s // N_EXPERTS
    bias = _bias_table(rel_bias)
    tables = _fourier_tables(int(round(np.sqrt(s))))
    for l in range(depth):
        q, k, v, vs, f, ga, gb = _inproj(x.reshape(b * s, d), g_mix[l], w_in[l].astype(BF16), tm=1024, chunk=256)
        shp = lambda a: a.reshape(b, s, a.shape[-1])
        attn = _attention(shp(q), shp(k), shp(v), shp(vs), attn_sink[l], bias, blocks_per_step=4)
        four = _fourier(shp(f), tables, rows=8)
        xn, h, aff = _outproj(attn, four, shp(ga), shp(gb), x,
                              w_attn_proj[l].astype(BF16), w_fourier_proj[l].astype(BF16),
                              w_out[l].astype(BF16), g_ffn[l], w_router[l].T, tm=1024, chunk=256)
        idx, pos, rank = _routing(aff, cap=cap)
        y = _expert_ffn(h, idx, w_exp_gate, w_exp_up, w_exp_down, l, cap=cap, tf=256, m_chunk=min(cap, 512))
        x = _combine(xn, y, pos, rank, aff, g_final, cap=cap, final_norm=(l == depth - 1))
    return x
```

```python
import functools

import numpy as np
import jax
import jax.numpy as jnp
from jax import lax
from jax.experimental import pallas as pl
from jax.experimental.pallas import tpu as pltpu

F32 = jnp.float32
BF16 = jnp.bfloat16
I32 = jnp.int32

N_HEADS = 8
N_KV_HEADS = 2
HEAD_DIM = 64
ATTN_WIDTH = N_HEADS * HEAD_DIM
KV_WIDTH = N_KV_HEADS * HEAD_DIM
WINDOW = 128
BLOCK = 128
N_BUCKETS = 32
MAX_DISTANCE = 128
N_FOURIER_GROUPS = 4
FOURIER_GROUP_DIM = 128
FOURIER_WIDTH = N_FOURIER_GROUPS * FOURIER_GROUP_DIM
N_EXPERTS = 16
CAPACITY_FACTOR = 2
EPS = 1e-6
NEG_INF = -1e30

LANES = 128
LANE_BITS = 7
SUBLANES = 8
VMEM_LIMIT = 56 * 1024 * 1024
EWIN = 32
DFT_PASSES = 1
COMBINE_AHEAD = 3


def _cparams(sem):
    return pltpu.CompilerParams(dimension_semantics=sem, vmem_limit_bytes=VMEM_LIMIT)


def _nt_dot(a, b, **kw):
    return lax.dot_general(a, b, (((1,), (1,)), ((), ())), preferred_element_type=F32, **kw)


def _sigmoid(x):
    return 1.0 / (1.0 + jnp.exp(-x))


def _t5_bucket(rel):
    half = N_BUCKETS // 2
    max_exact = half // 2
    ret = (rel > 0).astype(jnp.int32) * half
    n = jnp.abs(rel)
    nf = jnp.maximum(n, 1).astype(jnp.float32)
    large = max_exact + (jnp.log(nf / max_exact) / np.float32(np.log(MAX_DISTANCE / max_exact))
                         * (half - max_exact)).astype(jnp.int32)
    large = jnp.minimum(large, half - 1)
    return ret + jnp.where(n < max_exact, n, large)


def _bias_kernel(relb_ref, bucket_ref, o_ref):
    bk = bucket_ref[...]
    col = lax.broadcasted_iota(I32, bk.shape, 1)
    for h in range(N_HEADS):
        acc = jnp.full(bk.shape, NEG_INF, F32)
        for b in range(N_BUCKETS):
            acc = jnp.where(bk == b, relb_ref[b, h], acc)
        o_ref[0, h] = jnp.where(col >= BLOCK, acc, NEG_INF)
        o_ref[1, h] = acc
        o_ref[2, h] = jnp.where(col < 2 * BLOCK, acc, NEG_INF)


def _bias_table(rel_bias):
    q_loc = jnp.arange(BLOCK)
    k_loc = jnp.arange(3 * BLOCK) - BLOCK
    rel = k_loc[None, :] - q_loc[:, None]
    bucket = jnp.where(jnp.abs(rel) <= WINDOW, _t5_bucket(rel), -1).astype(I32)
    table = pl.pallas_call(
        _bias_kernel,
        out_shape=jax.ShapeDtypeStruct((3, N_HEADS, BLOCK, 3 * BLOCK), F32),
        in_specs=[pl.BlockSpec(memory_space=pltpu.SMEM),
                  pl.BlockSpec(memory_space=pltpu.VMEM)],
        out_specs=pl.BlockSpec(memory_space=pltpu.VMEM),
        name="rel_bias_table",
    )(rel_bias.astype(F32), bucket)
    return table.reshape(3, N_HEADS * BLOCK, 3 * BLOCK)


def _inproj_kernel(x_ref, g_ref, w_ref, q_ref, k_ref, v_ref, vs_ref, f_ref, ga_ref, gb_ref, *, d_model, chunk):
    for r in range(x_ref.shape[0] // chunk):
        rs = pl.ds(r * chunk, chunk)
        x = x_ref[rs, :]
        ms = jnp.mean(x * x, axis=-1, keepdims=True)
        h = (x * lax.rsqrt(ms + EPS)) * g_ref[...]
        z = jnp.dot(h.astype(BF16), w_ref[...], preferred_element_type=F32)
        o = 0
        q_ref[rs, :] = (z[:, o:o + ATTN_WIDTH] * (HEAD_DIM ** -0.5)).astype(BF16)
        o += ATTN_WIDTH
        k_ref[rs, :] = z[:, o:o + KV_WIDTH].astype(BF16)
        o += KV_WIDTH
        v_ref[rs, :] = z[:, o:o + KV_WIDTH].astype(BF16)
        half = KV_WIDTH // 2
        vs_ref[rs, :] = jnp.concatenate([z[:, o + half:o + KV_WIDTH], z[:, o:o + half]], axis=1).astype(BF16)
        o += KV_WIDTH
        f_ref[rs, :] = z[:, o:o + FOURIER_WIDTH]
        o += FOURIER_WIDTH
        ga_ref[rs, :] = _sigmoid(z[:, o:o + d_model]).astype(BF16)
        o += d_model
        gb_ref[rs, :] = _sigmoid(z[:, o:o + d_model]).astype(BF16)


def _inproj(x2, g, w_bf16, *, tm, chunk):
    n, d = x2.shape
    in_w = w_bf16.shape[1]
    widths = (ATTN_WIDTH, KV_WIDTH, KV_WIDTH, KV_WIDTH, FOURIER_WIDTH, d, d)
    dtypes = (BF16, BF16, BF16, BF16, F32, BF16, BF16)
    row = lambda i: (i, 0)
    return pl.pallas_call(
        functools.partial(_inproj_kernel, d_model=d, chunk=chunk),
        grid=(n // tm,),
        in_specs=[pl.BlockSpec((tm, d), row),
                  pl.BlockSpec((1, d), lambda i: (0, 0)),
                  pl.BlockSpec((d, in_w), lambda i: (0, 0))],
        out_specs=[pl.BlockSpec((tm, w), row) for w in widths],
        out_shape=[jax.ShapeDtypeStruct((n, w), dt) for w, dt in zip(widths, dtypes)],
        compiler_params=_cparams(("parallel",)),
        name="inproj",
    )(x2, g.reshape(1, d), w_bf16)


def _attn_kernel(sink_ref, q_ref, kp_ref, ko_ref, kn_ref, vp_ref, vo_ref, vn_ref, wp_ref, wo_ref, wn_ref,
                 bias_ref, o_ref, *, n_blocks, blocks_per_step):
    i = pl.program_id(1)
    kcat = jnp.concatenate([kp_ref[...], ko_ref[...], kn_ref[...]], axis=0)
    vcat = jnp.concatenate([vp_ref[...], vo_ref[...], vn_ref[...]], axis=0)
    wcat = jnp.concatenate([wp_ref[...], wo_ref[...], wn_ref[...]], axis=0)
    group = N_HEADS // N_KV_HEADS
    head = lambda a, h: a[:, h * HEAD_DIM:(h + 1) * HEAD_DIM]
    sink = jnp.concatenate([jnp.full((BLOCK, LANES), sink_ref[h], F32) for h in range(N_HEADS)], axis=0)
    low_v = lax.broadcasted_iota(I32, vcat.shape, 1) < HEAD_DIM
    low_o = lax.broadcasted_iota(I32, (BLOCK, LANES), 1) < HEAD_DIM
    vx = {(kv, half): jnp.where(low_v if half == 0 else jnp.logical_not(low_v),
                                vcat if kv == half else wcat, jnp.ones_like(vcat))
          for kv in range(N_KV_HEADS) for half in range(2)}
    for t in range(blocks_per_step):
        blk = i * blocks_per_step + t
        kind = jnp.where(blk == 0, 0, jnp.where(blk == n_blocks - 1, 2, 1))
        kw = kcat[t * BLOCK:(t + 3) * BLOCK]
        qb = q_ref[t * BLOCK:(t + 1) * BLOCK, :]
        s = jnp.concatenate([_nt_dot(head(qb, h), head(kw, h // group)) for h in range(N_HEADS)], axis=0)
        s = s + bias_ref[kind]
        m = jnp.maximum(jnp.broadcast_to(jnp.max(s, axis=-1, keepdims=True), sink.shape), sink)
        p = jnp.exp(s - jnp.concatenate([m] * 3, axis=1)).astype(BF16)
        sink_term = jnp.exp(sink - m)
        for pair in range(N_HEADS // 2):
            halves = []
            for half in range(2):
                h = 2 * pair + half
                rows = slice(h * BLOCK, (h + 1) * BLOCK)
                out = jnp.dot(p[rows], vx[h // group, half][t * BLOCK:(t + 3) * BLOCK],
                              preferred_element_type=F32)
                denom = pltpu.roll(out, HEAD_DIM, axis=1) + sink_term[rows]
                halves.append(out * (1.0 / denom))
            tile = jnp.where(low_o, halves[0], halves[1])
            o_ref[t * BLOCK:(t + 1) * BLOCK, pair * LANES:(pair + 1) * LANES] = tile.astype(o_ref.dtype)


def _attention(q, k, v, v_swapped, sink, bias, *, blocks_per_step):
    b, s, _ = q.shape
    assert N_KV_HEADS == 2 and KV_WIDTH == LANES and N_HEADS % 2 == 0
    n_blocks = s // BLOCK
    tq = blocks_per_step * BLOCK
    prev = lambda bi, i: (bi, jnp.maximum(i * blocks_per_step - 1, 0), 0)
    own = lambda bi, i: (bi, i, 0)
    nxt = lambda bi, i: (bi, jnp.minimum((i + 1) * blocks_per_step, n_blocks - 1), 0)
    kv_specs = [pl.BlockSpec((None, BLOCK, KV_WIDTH), prev),
                pl.BlockSpec((None, tq, KV_WIDTH), own),
                pl.BlockSpec((None, BLOCK, KV_WIDTH), nxt)]
    return pl.pallas_call(
        functools.partial(_attn_kernel, n_blocks=n_blocks, blocks_per_step=blocks_per_step),
        grid=(b, s // tq),
        in_specs=[pl.BlockSpec(memory_space=pltpu.SMEM),
                  pl.BlockSpec((None, tq, ATTN_WIDTH), own)] + kv_specs * 3
                 + [pl.BlockSpec((3, N_HEADS * BLOCK, 3 * BLOCK), lambda bi, i: (0, 0, 0))],
        out_specs=pl.BlockSpec((None, tq, ATTN_WIDTH), own),
        out_shape=jax.ShapeDtypeStruct((b, s, ATTN_WIDTH), BF16),
        compiler_params=_cparams(("parallel", "parallel")),
        name="window_attention",
    )(sink.astype(F32), q, k, k, k, v, v, v, v_swapped, v_swapped, v_swapped, bias)


def _hi_lo(x):
    hi = x.astype(BF16)
    return hi, (x - hi.astype(F32)).astype(BF16)


def _dft_operand(x, axis):
    if DFT_PASSES == 1:
        return x.astype(BF16)
    hi, lo = _hi_lo(x)
    return jnp.concatenate([hi, lo, hi], axis=axis)


def _dft1_kernel(x_ref, a_ref, tr_ref, ti_ref, x2, *, rows):
    a = a_ref[...]
    na = a.shape[0] // 2
    n_slabs = x2.shape[0]
    lanes = lambda c: slice(c * LANES, (c + 1) * LANES)
    for c in range(n_slabs):
        x2[c] = x_ref[:, :, lanes(c)].reshape(x2.shape[1:])
    for jj in range(rows):
        strided = pl.ds(jj, na, stride=rows)
        xj = jnp.concatenate([x2[c, strided, :] for c in range(n_slabs)], axis=1)
        t = jnp.dot(a, _dft_operand(xj, 0), preferred_element_type=F32)
        tr_ref[:, jj, :] = t[:na]
        ti_ref[:, jj, :] = t[na:]


def _dft2_kernel(tr_ref, ti_ref, gr_ref, gi_ref, cs_ref, o_ref, o2, *, rows):
    cs = cs_ref[...]
    na = o_ref.shape[0]
    for aa in range(rows):
        t = _dft_operand(jnp.concatenate([tr_ref[aa], ti_ref[aa]], axis=0), 0)
        zr = jnp.dot(gr_ref[aa], t, preferred_element_type=F32)
        zi = jnp.dot(gi_ref[aa], t, preferred_element_type=F32)
        for g in range(N_FOURIER_GROUPS):
            sl = slice(g * FOURIER_GROUP_DIM, (g + 1) * FOURIER_GROUP_DIM)
            z = _dft_operand(jnp.concatenate([zr[:, sl], zi[:, sl]], axis=1), 1)
            y = jnp.dot(z, cs, preferred_element_type=F32)
            o2[g, pl.ds(aa, na, stride=rows), :] = y
    for g in range(N_FOURIER_GROUPS):
        sl = slice(g * FOURIER_GROUP_DIM, (g + 1) * FOURIER_GROUP_DIM)
        o_ref[:, :, sl] = o2[g].reshape(na, rows, FOURIER_GROUP_DIM)


def _fourier_tables(na):
    def pieces(a, axis):
        hi = a.astype(BF16)
        if DFT_PASSES == 1:
            return jnp.asarray(hi)
        lo = (a - hi.astype(np.float64)).astype(BF16)
        return jnp.asarray(np.concatenate([hi, hi, lo], axis=axis))

    n = na * na
    k = np.arange(na, dtype=np.float64)
    ang = 2.0 * np.pi * np.outer(k, k) / na
    a1 = pieces(np.concatenate([np.cos(ang), -np.sin(ang)], axis=0), 1)
    ka = np.arange(na, dtype=np.float64)[:, None, None]
    kb = np.arange(na, dtype=np.float64)[None, :, None]
    nl = np.arange(na, dtype=np.float64)[None, None, :]
    th = 2.0 * np.pi * ((nl * (ka + na * kb)) % n) / n
    gr, gi = np.cos(th), -np.sin(th)
    g_re = pieces(np.concatenate([gr, -gi], axis=2), 2)
    g_im = pieces(np.concatenate([gi, gr], axis=2), 2)
    kc = np.arange(FOURIER_GROUP_DIM, dtype=np.float64)
    angc = 2.0 * np.pi * np.outer(kc, kc) / FOURIER_GROUP_DIM
    scale = 1.0 / np.sqrt(float(n) * FOURIER_GROUP_DIM)
    cs = pieces(np.concatenate([np.cos(angc), np.sin(angc)], axis=0) * scale, 0)
    return a1, g_re, g_im, cs


def _fourier(f, tables, *, rows):
    b, s, w = f.shape
    na = int(round(np.sqrt(s)))
    assert na * na == s and na % rows == 0
    a1, g_re, g_im, cs = tables
    x4 = f.reshape(b, na, na, w)
    const2 = lambda bi, j: (0, 0)
    tr, ti = pl.pallas_call(
        functools.partial(_dft1_kernel, rows=rows),
        grid=(b, na // rows),
        in_specs=[pl.BlockSpec((None, na, rows, w), lambda bi, j: (bi, 0, j, 0)),
                  pl.BlockSpec(a1.shape, const2)],
        out_specs=[pl.BlockSpec((None, na, rows, w), lambda bi, j: (bi, 0, j, 0))] * 2,
        out_shape=[jax.ShapeDtypeStruct((b, na, na, w), F32)] * 2,
        scratch_shapes=[pltpu.VMEM((w // LANES, na * rows, LANES), F32)],
        compiler_params=_cparams(("parallel", "parallel")),
        name="seq_dft_stage1",
    )(x4, a1)
    y = pl.pallas_call(
        functools.partial(_dft2_kernel, rows=rows),
        grid=(b, na // rows),
        in_specs=[pl.BlockSpec((None, rows, na, w), lambda bi, a: (bi, a, 0, 0))] * 2
                 + [pl.BlockSpec((rows,) + g_re.shape[1:], lambda bi, a: (a, 0, 0))] * 2
                 + [pl.BlockSpec(cs.shape, const2)],
        out_specs=pl.BlockSpec((None, na, rows, w), lambda bi, a: (bi, 0, a, 0)),
        out_shape=jax.ShapeDtypeStruct((b, na, na, w), F32),
        scratch_shapes=[pltpu.VMEM((N_FOURIER_GROUPS, na * rows, FOURIER_GROUP_DIM), F32)],
        compiler_params=_cparams(("parallel", "parallel")),
        name="seq_dft_stage2",
    )(tr, ti, g_re, g_im, cs)
    return y.reshape(b, s, w)


def _outproj_kernel(a_ref, f_ref, ga_ref, gb_ref, x_ref, wa_ref, wf_ref, wo_ref, g_ref, wr_ref,
                    xn_ref, h_ref, aff_ref, *, chunk):
    for r in range(x_ref.shape[0] // chunk):
        rs = pl.ds(r * chunk, chunk)
        a = jnp.dot(a_ref[rs, :], wa_ref[...], preferred_element_type=F32)
        fo = jnp.dot(f_ref[rs, :].astype(BF16), wf_ref[...], preferred_element_type=F32)
        merged = ga_ref[rs, :].astype(F32) * a + gb_ref[rs, :].astype(F32) * fo
        xn = x_ref[rs, :] + jnp.dot(merged.astype(BF16), wo_ref[...], preferred_element_type=F32)
        xn_ref[rs, :] = xn
        ms = jnp.mean(xn * xn, axis=-1, keepdims=True)
        h = (xn * lax.rsqrt(ms + EPS)) * g_ref[...]
        dt = h.shape[1] // LANES
        for c in range(dt):
            h_ref[pl.ds(r * chunk * dt + c, chunk, stride=dt), :] = h[:, c * LANES:(c + 1) * LANES]
        logits = _nt_dot(wr_ref[...], h.astype(BF16))
        e = jnp.exp(logits - jnp.max(logits, axis=0, keepdims=True))
        aff_ref[:, rs] = e / jnp.sum(e, axis=0, keepdims=True)


def _outproj(attn, four, ga, gb, x, wa, wf, wo, g, wr_t, *, tm, chunk):
    b, s, d = x.shape
    dt = d // LANES
    tok = lambda bi, i: (bi, i, 0)
    const = lambda bi, i: (0, 0)
    xn, h, aff = pl.pallas_call(
        functools.partial(_outproj_kernel, chunk=chunk),
        grid=(b, s // tm),
        in_specs=[pl.BlockSpec((None, tm, ATTN_WIDTH), tok),
                  pl.BlockSpec((None, tm, FOURIER_WIDTH), tok),
                  pl.BlockSpec((None, tm, d), tok),
                  pl.BlockSpec((None, tm, d), tok),
                  pl.BlockSpec((None, tm, d), tok),
                  pl.BlockSpec((ATTN_WIDTH, d), const),
                  pl.BlockSpec((FOURIER_WIDTH, d), const),
                  pl.BlockSpec((d, d), const),
                  pl.BlockSpec((1, d), const),
                  pl.BlockSpec((N_EXPERTS, d), const)],
        out_specs=[pl.BlockSpec((None, tm, d), tok),
                   pl.BlockSpec((None, tm * dt, LANES), tok),
                   pl.BlockSpec((None, N_EXPERTS, tm), lambda bi, i: (bi, 0, i))],
        out_shape=[jax.ShapeDtypeStruct((b, s, d), F32),
                   jax.ShapeDtypeStruct((b, s * dt, LANES), F32),
                   jax.ShapeDtypeStruct((b, N_EXPERTS, s), F32)],
        compiler_params=_cparams(("parallel", "parallel")),
        name="outproj_router",
    )(attn, four, ga, gb, x, wa, wf, wo, g.reshape(1, d), wr_t.astype(BF16))
    return xn, h.reshape(b, s, dt, LANES), aff


def _tri(n, kind):
    r = lax.broadcasted_iota(I32, (n, n), 0)
    c = lax.broadcasted_iota(I32, (n, n), 1)
    cond = {"row_le_col": r <= c, "row_lt_col": r < c, "col_le_row": c <= r, "col_lt_row": c < r}[kind]
    return jnp.where(cond, 1.0, 0.0).astype(BF16)


def _split128(v):
    hi = jnp.floor(v * (1.0 / LANES))
    return hi, v - hi * LANES


def _routing_kernel(aff_ref, idx_ref, pos_ref, rank_ref, gt_scr, eq_scr, need_scr, *, cap):
    ne, nj, ni = aff_ref.shape
    bits = lax.bitcast_convert_type(aff_ref[...], I32)

    def bisect(it, thr):
        cand = thr | jnp.left_shift(jnp.int32(1), 30 - it)
        cnt = jnp.sum(jnp.where(bits >= cand, 1.0, 0.0), axis=(1, 2), keepdims=True)
        return jnp.where(cnt >= cap, cand, thr)

    thr = lax.fori_loop(0, 31, bisect, jnp.zeros((ne, 1, 1), I32))
    gt = jnp.where(bits > thr, 1.0, 0.0)
    gt_scr[...] = gt
    eq_scr[...] = jnp.where(bits == thr, 1.0, 0.0)
    need = cap - jnp.sum(gt, axis=(1, 2), keepdims=True)
    need_scr[...] = jnp.broadcast_to(need, (ne, SUBLANES, ni))

    ones_sq = jnp.ones((ni, ni), BF16)
    ones_row = jnp.ones((SUBLANES, ni), BF16)
    ones_rowj = jnp.ones((SUBLANES, nj), BF16)
    u_incl = _tri(ni, "row_le_col")
    uj_strict = _tri(nj, "row_lt_col")
    lj_strict = _tri(nj, "col_lt_row")
    l_incl = _tri(ni, "col_le_row")
    bdot = lambda a, b: jnp.dot(a, b, preferred_element_type=F32)

    c_lane = lax.broadcasted_iota(I32, (nj, cap), 1).astype(F32)
    c_row = lax.broadcasted_iota(I32, (SUBLANES, cap), 1).astype(F32)
    j_sub = lax.broadcasted_iota(I32, (nj, cap), 0).astype(F32)

    def slots(e, carry):
        eq = eq_scr[e]
        eqb = eq.astype(BF16)
        tie_rank = bdot(lj_strict, bdot(eqb, ones_sq).astype(BF16)) + bdot(eqb, u_incl)
        take = jnp.where(tie_rank <= need_scr[e][0:1, :], eq, 0.0)
        sel = jnp.maximum(gt_scr[e], take)
        selb = sel.astype(BF16)
        lc = bdot(selb, u_incl)
        lc_t = _nt_dot(l_incl, selb)
        n_rep = bdot(selb, ones_sq)
        s_rep = bdot(lj_strict, n_rep.astype(BF16)) + n_rep
        n_lane = _nt_dot(ones_row, selb)
        pex_lane = bdot(n_lane.astype(BF16), uj_strict)
        s_wide = jnp.concatenate([s_rep] * (cap // ni), axis=1) if cap > ni else s_rep[:, :cap]
        jc = bdot(ones_rowj, jnp.where(s_wide <= c_lane, 1.0, 0.0).astype(BF16))
        onehot = jnp.where(j_sub == jnp.broadcast_to(jc[0:1], (nj, cap)), 1.0, 0.0).astype(BF16)
        phi, plo = _split128(pex_lane)
        r = c_row - (bdot(phi.astype(BF16), onehot) * LANES + bdot(plo.astype(BF16), onehot))
        lcs = bdot(lc_t.astype(BF16), onehot)
        ic = bdot(ones_row, jnp.where(lcs <= jnp.broadcast_to(r[0:1], (ni, cap)), 1.0, 0.0).astype(BF16))
        idx_ref[e] = (jc * ni + ic).astype(I32)
        pos_ref[e] = pex_lane.astype(I32)
        rank_ref[e] = jnp.where(sel > 0.0, lc - 1.0, -1.0)
        return carry

    lax.fori_loop(0, ne, slots, 0)


def _routing(aff, *, cap):
    b, ne, s = aff.shape
    nj = s // LANES
    aff4 = aff.reshape(b, ne, nj, LANES)
    per_b = lambda bi: (bi, 0, 0, 0)
    idx, pos, rank = pl.pallas_call(
        functools.partial(_routing_kernel, cap=cap),
        grid=(b,),
        in_specs=[pl.BlockSpec((None, ne, nj, LANES), per_b)],
        out_specs=[pl.BlockSpec((None, ne, SUBLANES, cap), per_b),
                   pl.BlockSpec((None, ne, SUBLANES, nj), per_b),
                   pl.BlockSpec((None, ne, nj, LANES), per_b)],
        out_shape=[jax.ShapeDtypeStruct((b, ne, SUBLANES, cap), I32),
                   jax.ShapeDtypeStruct((b, ne, SUBLANES, nj), I32),
                   jax.ShapeDtypeStruct((b, ne, nj, LANES), F32)],
        scratch_shapes=[pltpu.VMEM((ne, nj, LANES), F32)] * 2 + [pltpu.VMEM((ne, SUBLANES, LANES), F32)],
        compiler_params=_cparams(("parallel",)),
        name="expert_choice_routing",
    )(aff4)
    return idx[:, :, 0, :], pos[:, :, 0, :], rank


def _ffn_kernel(idx_cur, idx_nxt, h_hbm, wg_ref, wu_ref, wd_ref, y_ref, xs, xsb, gsem,
                *, cap, n_pairs, n_ftiles, per_step, m_chunk):
    p = pl.program_id(0)
    j = pl.program_id(1)
    n_slots = n_ftiles * per_step
    pitch = xs.shape[0] // n_slots
    dt = pitch - 1
    last_pair = n_pairs - 1
    b_cur = p // N_EXPERTS
    b_nxt = jnp.minimum(p + 1, last_pair) // N_EXPERTS

    def gather(ref, bq, c):
        cc = jnp.minimum(c, cap - 1)
        tok = ref[cc >> LANE_BITS, cc & (LANES - 1)]
        dst = xs.at[pl.ds(c * pitch, dt)]
        return pltpu.make_async_copy(h_hbm.at[bq, tok], dst, gsem)

    def wait_gathers():
        tiles = xs.at[pl.ds(0, n_slots * dt)]
        pltpu.make_async_copy(tiles, tiles, gsem).wait()

    @pl.when((p == 0) & (j == 0))
    def _():
        def start(c, carry):
            gather(idx_cur, b_cur, c).start()
            return carry

        lax.fori_loop(0, n_slots, start, 0)

    n_groups = cap // m_chunk
    per_group = per_step // n_groups

    def ff_tile(first):
        wg = wg_ref[...].astype(BF16)
        wu = wu_ref[...].astype(BF16)
        wd = wd_ref[...].astype(BF16)
        for m in range(n_groups):
            for t in range(per_group):
                gather(idx_nxt, b_nxt, j * per_step + (m * per_group + t)).start()
            rows = pl.ds(m * m_chunk, m_chunk)
            x = xsb[rows, :]
            hg = jnp.dot(x, wg, preferred_element_type=F32)
            hu = jnp.dot(x, wu, preferred_element_type=F32)
            act = (hg * _sigmoid(hg)) * hu
            part = jnp.dot(act.astype(BF16), wd, preferred_element_type=F32)
            if first:
                y_ref[rows, :] = part
            else:
                y_ref[rows, :] += part

    @pl.when(j == 0)
    def _():
        wait_gathers()
        for c in range(dt):
            xsb[:, c * LANES:(c + 1) * LANES] = xs[pl.ds(c, cap, stride=pitch), :].astype(BF16)
        ff_tile(True)

    @pl.when(j > 0)
    def _():
        ff_tile(False)

    @pl.when((p == last_pair) & (j == n_ftiles - 1))
    def _():
        wait_gathers()


def _expert_ffn(h, idx, wg, wu, wd, layer, *, cap, tf, m_chunk):
    b, s, dt, _ = h.shape
    d = dt * LANES
    ne, dff = wg.shape[1], wg.shape[3]
    n_ftiles = dff // tf
    n_pairs = b * ne
    n_groups = cap // m_chunk
    per_step = -(-cap // (n_ftiles * n_groups)) * n_groups
    n_slots = n_ftiles * per_step
    tab = idx.reshape(n_pairs, cap // LANES, LANES)
    tab_spec = lambda f: pl.BlockSpec((None, cap // LANES, LANES), lambda p, j: (f(p), 0, 0),
                                      memory_space=pltpu.SMEM)
    return pl.pallas_call(
        functools.partial(_ffn_kernel, cap=cap, n_pairs=n_pairs, n_ftiles=n_ftiles, per_step=per_step,
                          m_chunk=m_chunk),
        grid=(n_pairs, n_ftiles),
        in_specs=[tab_spec(lambda p: p),
                  tab_spec(lambda p: jnp.minimum(p + 1, n_pairs - 1)),
                  pl.BlockSpec(memory_space=pl.ANY),
                  pl.BlockSpec((None, None, d, tf), lambda p, j: (layer, p % ne, 0, j)),
                  pl.BlockSpec((None, None, d, tf), lambda p, j: (layer, p % ne, 0, j)),
                  pl.BlockSpec((None, None, tf, d), lambda p, j: (layer, p % ne, j, 0))],
        out_specs=pl.BlockSpec((None, None, cap, d), lambda p, j: (p // ne, p % ne, 0, 0)),
        out_shape=jax.ShapeDtypeStruct((b, ne, cap, d), F32),
        scratch_shapes=[pltpu.VMEM((n_slots * (dt + 1), LANES), F32), pltpu.VMEM((cap, d), BF16),
                        pltpu.SemaphoreType.DMA(())],
        compiler_params=_cparams(("arbitrary", "arbitrary")),
        name="expert_swiglu",
    )(tab, tab, h, wg, wu, wd)


def _combine_kernel(pos_ref, rank_ref, aff_ref, x_ref, g_ref, y_hbm, o_ref, ybuf, acc, sem,
                    *, cap, nj, n_tiles, final_norm):
    b = pl.program_id(0)
    j = pl.program_id(1)
    t = b * nj + j
    ne = N_EXPERTS
    ring = COMBINE_AHEAD + 1
    assert ring & (ring - 1) == 0
    in_ring = lambda k: k & (ring - 1)

    def nominal(bq, e, jq, r):
        return ((pos_ref[bq * ne + e, jq] >> 3) << 3) + r * EWIN

    def window(bq, e, jq, r):
        return pl.multiple_of(jnp.minimum(nominal(bq, e, jq, r), cap - EWIN), SUBLANES)

    def start_fetch(ahead, r, slot):
        jq = j + ahead
        wrap = (jq >= nj).astype(I32)
        bq = b + wrap
        jq = jq - wrap * nj
        past = bq * nj + jq >= n_tiles
        bq = jnp.where(past, n_tiles // nj - 1, bq)
        jq = jnp.where(past, nj - 1, jq)
        for e in range(ne):
            pltpu.make_async_copy(y_hbm.at[bq, e, pl.ds(window(bq, e, jq, r), EWIN)],
                                  ybuf.at[slot, pl.ds(e * EWIN, EWIN)], sem.at[slot]).start()

    def wait_fetch(slot):
        pltpu.make_async_copy(ybuf.at[slot], ybuf.at[slot], sem.at[slot]).wait()

    rk = rank_ref[...]
    af = aff_ref[...]
    row = lax.broadcasted_iota(I32, (EWIN, LANES), 0).astype(F32)

    def add_round(r, slot):
        parts = []
        for e in range(ne):
            p0 = pos_ref[b * ne + e, j]
            local = row + (window(b, e, j, r) - p0).astype(F32)
            want = jnp.where(local >= jnp.maximum(nominal(b, e, j, r) - p0, 0).astype(F32), local, -2.0)
            parts.append(jnp.where(rk[e:e + 1, :] == want, af[e:e + 1, :], 0.0))
        gate_t = jnp.concatenate(parts, axis=0)
        onehot = jnp.where(gate_t != 0.0, 1.0, 0.0).T.astype(BF16)
        row_gate = jnp.sum(gate_t, axis=1, keepdims=True)
        scaled = jnp.where(row_gate != 0.0, ybuf[slot] * row_gate, 0.0)
        hi = scaled.astype(BF16)
        lo = (scaled - hi.astype(F32)).astype(BF16)
        return jnp.dot(onehot, hi, preferred_element_type=F32) + jnp.dot(onehot, lo, preferred_element_type=F32)

    @pl.when(t == 0)
    def _():
        for ahead in range(COMBINE_AHEAD):
            start_fetch(ahead, 0, ahead)

    start_fetch(COMBINE_AHEAD, 0, in_ring(t + COMBINE_AHEAD))
    slot0 = in_ring(t)
    wait_fetch(slot0)
    acc[...] = add_round(0, slot0)

    span = jnp.int32(1)
    for e in range(ne):
        span = jnp.maximum(span, pos_ref[b * ne + e, j + 1] - nominal(b, e, j, 0))
    n_rounds = (span + EWIN - 1) // EWIN

    def extra_round(r, carry):
        start_fetch(0, r, ring)
        wait_fetch(ring)
        acc[...] += add_round(r, ring)
        return carry

    lax.fori_loop(1, n_rounds, extra_round, 0)

    @pl.when(t == n_tiles - 1)
    def _():
        for ahead in range(1, COMBINE_AHEAD + 1):
            wait_fetch(in_ring(t + ahead))

    y = x_ref[...] + acc[...]
    if final_norm:
        ms = jnp.mean(y * y, axis=-1, keepdims=True)
        y = (y * lax.rsqrt(ms + EPS)) * g_ref[...]
    o_ref[...] = y


def _combine(x, y, pos, rank, aff, g_final, *, cap, final_norm):
    b, s, d = x.shape
    nj = s // LANES
    ne = N_EXPERTS
    pos_tab = jnp.concatenate([pos, jnp.full((b, ne, 1), cap, I32)], axis=2).reshape(b * ne, nj + 1)
    rank_t = jnp.transpose(rank, (0, 2, 1, 3))
    return pl.pallas_call(
        functools.partial(_combine_kernel, cap=cap, nj=nj, n_tiles=b * nj, final_norm=final_norm),
        grid_spec=pltpu.PrefetchScalarGridSpec(
            num_scalar_prefetch=1,
            grid=(b, nj),
            in_specs=[pl.BlockSpec((None, None, ne, LANES), lambda bi, j, tab: (bi, j, 0, 0)),
                      pl.BlockSpec((None, ne, LANES), lambda bi, j, tab: (bi, 0, j)),
                      pl.BlockSpec((None, LANES, d), lambda bi, j, tab: (bi, j, 0)),
                      pl.BlockSpec((1, d), lambda bi, j, tab: (0, 0)),
                      pl.BlockSpec(memory_space=pl.ANY)],
            out_specs=pl.BlockSpec((None, LANES, d), lambda bi, j, tab: (bi, j, 0)),
            scratch_shapes=[pltpu.VMEM((COMBINE_AHEAD + 2, ne * EWIN, d), F32), pltpu.VMEM((LANES, d), F32),
                            pltpu.SemaphoreType.DMA((COMBINE_AHEAD + 2,))]),
        out_shape=jax.ShapeDtypeStruct((b, s, d), F32),
        compiler_params=_cparams(("arbitrary", "arbitrary")),
        name="moe_combine",
    )(pos_tab, rank_t, aff, x, g_final.reshape(1, d), y)


def kernel(x, rel_bias, g_mix, w_in, attn_sink, w_attn_proj, w_fourier_proj, w_out, g_ffn, w_router,
           w_exp_gate, w_exp_up, w_exp_down, g_final):
    b, s, d = x.shape
    depth = g_mix.shape[0]
    cap = CAPACITY_FACTOR * s // N_EXPERTS
    bias = _bias_table(rel_bias)
    tables = _fourier_tables(int(round(np.sqrt(s))))
    for l in range(depth):
        q, k, v, vs, f, ga, gb = _inproj(x.reshape(b * s, d), g_mix[l], w_in[l].astype(BF16), tm=1024, chunk=256)
        shp = lambda a: a.reshape(b, s, a.shape[-1])
        attn = _attention(shp(q), shp(k), shp(v), shp(vs), attn_sink[l], bias, blocks_per_step=8)
        four = _fourier(shp(f), tables, rows=8)
        xn, h, aff = _outproj(attn, four, shp(ga), shp(gb), x,
                              w_attn_proj[l].astype(BF16), w_fourier_proj[l].astype(BF16),
                              w_out[l].astype(BF16), g_ffn[l], w_router[l].T, tm=1024, chunk=256)
        idx, pos, rank = _routing(aff, cap=cap)
        y = _expert_ffn(h, idx, w_exp_gate, w_exp_up, w_exp_down, l, cap=cap, tf=256, m_chunk=min(cap, 512))
        x = _combine(xn, y, pos, rank, aff, g_final, cap=cap, final_norm=(l == depth - 1))
    return x
```

```python
import functools

import numpy as np
import jax
import jax.numpy as jnp
from jax import lax
from jax.experimental import pallas as pl
from jax.experimental.pallas import tpu as pltpu

F32 = jnp.float32
BF16 = jnp.bfloat16
I32 = jnp.int32

N_HEADS = 8
N_KV_HEADS = 2
HEAD_DIM = 64
ATTN_WIDTH = N_HEADS * HEAD_DIM
KV_WIDTH = N_KV_HEADS * HEAD_DIM
WINDOW = 128
BLOCK = 128
N_BUCKETS = 32
MAX_DISTANCE = 128
N_FOURIER_GROUPS = 4
FOURIER_GROUP_DIM = 128
FOURIER_WIDTH = N_FOURIER_GROUPS * FOURIER_GROUP_DIM
N_EXPERTS = 16
CAPACITY_FACTOR = 2
EPS = 1e-6
NEG_INF = -1e30

LANES = 128
LANE_BITS = 7
SUBLANES = 8
VMEM_LIMIT = 56 * 1024 * 1024
EWIN = 32
COMBINE_AHEAD = 3


def _cparams(sem):
    return pltpu.CompilerParams(dimension_semantics=sem, vmem_limit_bytes=VMEM_LIMIT)


def _nt_dot(a, b, **kw):
    return lax.dot_general(a, b, (((1,), (1,)), ((), ())), preferred_element_type=F32, **kw)


def _sigmoid(x):
    return 1.0 / (1.0 + jnp.exp(-x))


def _t5_bucket(rel):
    half = N_BUCKETS // 2
    max_exact = half // 2
    ret = (rel > 0).astype(jnp.int32) * half
    n = jnp.abs(rel)
    nf = jnp.maximum(n, 1).astype(jnp.float32)
    large = max_exact + (jnp.log(nf / max_exact) / np.float32(np.log(MAX_DISTANCE / max_exact))
                         * (half - max_exact)).astype(jnp.int32)
    large = jnp.minimum(large, half - 1)
    return ret + jnp.where(n < max_exact, n, large)


def _bias_kernel(relb_ref, bucket_ref, o_ref):
    bk = bucket_ref[...]
    col = lax.broadcasted_iota(I32, bk.shape, 1)
    for h in range(N_HEADS):
        acc = jnp.full(bk.shape, NEG_INF, F32)
        for b in range(N_BUCKETS):
            acc = jnp.where(bk == b, relb_ref[b, h], acc)
        o_ref[0, h] = jnp.where(col >= BLOCK, acc, NEG_INF)
        o_ref[1, h] = acc
        o_ref[2, h] = jnp.where(col < 2 * BLOCK, acc, NEG_INF)


def _bias_table(rel_bias):
    q_loc = jnp.arange(BLOCK)
    k_loc = jnp.arange(3 * BLOCK) - BLOCK
    rel = k_loc[None, :] - q_loc[:, None]
    bucket = jnp.where(jnp.abs(rel) <= WINDOW, _t5_bucket(rel), -1).astype(I32)
    table = pl.pallas_call(
        _bias_kernel,
        out_shape=jax.ShapeDtypeStruct((3, N_HEADS, BLOCK, 3 * BLOCK), F32),
        in_specs=[pl.BlockSpec(memory_space=pltpu.SMEM),
                  pl.BlockSpec(memory_space=pltpu.VMEM)],
        out_specs=pl.BlockSpec(memory_space=pltpu.VMEM),
        name="rel_bias_table",
    )(rel_bias.astype(F32), bucket)
    return table.reshape(3, N_HEADS * BLOCK, 3 * BLOCK)


def _inproj_kernel(x_ref, g_ref, w_ref, q_ref, k_ref, v_ref, vs_ref, f_ref, ga_ref, gb_ref, *, d_model, chunk):
    for r in range(x_ref.shape[0] // chunk):
        rs = pl.ds(r * chunk, chunk)
        x = x_ref[rs, :]
        ms = jnp.mean(x * x, axis=-1, keepdims=True)
        h = (x * lax.rsqrt(ms + EPS)) * g_ref[...]
        z = jnp.dot(h.astype(BF16), w_ref[...], preferred_element_type=F32)
        o = 0
        q_ref[rs, :] = (z[:, o:o + ATTN_WIDTH] * (HEAD_DIM ** -0.5)).astype(BF16)
        o += ATTN_WIDTH
        k_ref[rs, :] = z[:, o:o + KV_WIDTH].astype(BF16)
        o += KV_WIDTH
        v_ref[rs, :] = z[:, o:o + KV_WIDTH].astype(BF16)
        half = KV_WIDTH // 2
        vs_ref[rs, :] = jnp.concatenate([z[:, o + half:o + KV_WIDTH], z[:, o:o + half]], axis=1).astype(BF16)
        o += KV_WIDTH
        f_ref[rs, :] = z[:, o:o + FOURIER_WIDTH]
        o += FOURIER_WIDTH
        ga_ref[rs, :] = _sigmoid(z[:, o:o + d_model]).astype(BF16)
        o += d_model
        gb_ref[rs, :] = _sigmoid(z[:, o:o + d_model]).astype(BF16)


def _inproj(x2, g, w_bf16, *, tm, chunk):
    n, d = x2.shape
    in_w = w_bf16.shape[1]
    widths = (ATTN_WIDTH, KV_WIDTH, KV_WIDTH, KV_WIDTH, FOURIER_WIDTH, d, d)
    dtypes = (BF16, BF16, BF16, BF16, F32, BF16, BF16)
    row = lambda i: (i, 0)
    return pl.pallas_call(
        functools.partial(_inproj_kernel, d_model=d, chunk=chunk),
        grid=(n // tm,),
        in_specs=[pl.BlockSpec((tm, d), row),
                  pl.BlockSpec((1, d), lambda i: (0, 0)),
                  pl.BlockSpec((d, in_w), lambda i: (0, 0))],
        out_specs=[pl.BlockSpec((tm, w), row) for w in widths],
        out_shape=[jax.ShapeDtypeStruct((n, w), dt) for w, dt in zip(widths, dtypes)],
        compiler_params=_cparams(("parallel",)),
        name="inproj",
    )(x2, g.reshape(1, d), w_bf16)


def _attn_kernel(sink_ref, q_ref, kp_ref, ko_ref, kn_ref, vp_ref, vo_ref, vn_ref, wp_ref, wo_ref, wn_ref,
                 bias_ref, o_ref, *, n_blocks, blocks_per_step):
    i = pl.program_id(1)
    kcat = jnp.concatenate([kp_ref[...], ko_ref[...], kn_ref[...]], axis=0)
    vcat = jnp.concatenate([vp_ref[...], vo_ref[...], vn_ref[...]], axis=0)
    wcat = jnp.concatenate([wp_ref[...], wo_ref[...], wn_ref[...]], axis=0)
    group = N_HEADS // N_KV_HEADS
    head = lambda a, h: a[:, h * HEAD_DIM:(h + 1) * HEAD_DIM]
    sink = jnp.concatenate([jnp.full((BLOCK, LANES), sink_ref[h], F32) for h in range(N_HEADS)], axis=0)
    low_v = lax.broadcasted_iota(I32, vcat.shape, 1) < HEAD_DIM
    low_o = lax.broadcasted_iota(I32, (BLOCK, LANES), 1) < HEAD_DIM
    vx = {(kv, half): jnp.where(low_v if half == 0 else jnp.logical_not(low_v),
                                vcat if kv == half else wcat, jnp.ones_like(vcat))
          for kv in range(N_KV_HEADS) for half in range(2)}
    for t in range(blocks_per_step):
        blk = i * blocks_per_step + t
        kind = jnp.where(blk == 0, 0, jnp.where(blk == n_blocks - 1, 2, 1))
        kw = kcat[t * BLOCK:(t + 3) * BLOCK]
        qb = q_ref[t * BLOCK:(t + 1) * BLOCK, :]
        s = jnp.concatenate([_nt_dot(head(qb, h), head(kw, h // group)) for h in range(N_HEADS)], axis=0)
        s = s + bias_ref[kind]
        m = jnp.maximum(jnp.broadcast_to(jnp.max(s, axis=-1, keepdims=True), sink.shape), sink)
        p = jnp.exp(s - jnp.concatenate([m] * 3, axis=1)).astype(BF16)
        sink_term = jnp.exp(sink - m)
        for pair in range(N_HEADS // 2):
            halves = []
            for half in range(2):
                h = 2 * pair + half
                rows = slice(h * BLOCK, (h + 1) * BLOCK)
                out = jnp.dot(p[rows], vx[h // group, half][t * BLOCK:(t + 3) * BLOCK],
                              preferred_element_type=F32)
                denom = pltpu.roll(out, HEAD_DIM, axis=1) + sink_term[rows]
                halves.append(out * (1.0 / denom))
            tile = jnp.where(low_o, halves[0], halves[1])
            o_ref[t * BLOCK:(t + 1) * BLOCK, pair * LANES:(pair + 1) * LANES] = tile.astype(o_ref.dtype)


def _attention(q, k, v, v_swapped, sink, bias, *, blocks_per_step):
    b, s, _ = q.shape
    assert N_KV_HEADS == 2 and KV_WIDTH == LANES and N_HEADS % 2 == 0
    n_blocks = s // BLOCK
    tq = blocks_per_step * BLOCK
    prev = lambda bi, i: (bi, jnp.maximum(i * blocks_per_step - 1, 0), 0)
    own = lambda bi, i: (bi, i, 0)
    nxt = lambda bi, i: (bi, jnp.minimum((i + 1) * blocks_per_step, n_blocks - 1), 0)
    kv_specs = [pl.BlockSpec((None, BLOCK, KV_WIDTH), prev),
                pl.BlockSpec((None, tq, KV_WIDTH), own),
                pl.BlockSpec((None, BLOCK, KV_WIDTH), nxt)]
    return pl.pallas_call(
        functools.partial(_attn_kernel, n_blocks=n_blocks, blocks_per_step=blocks_per_step),
        grid=(b, s // tq),
        in_specs=[pl.BlockSpec(memory_space=pltpu.SMEM),
                  pl.BlockSpec((None, tq, ATTN_WIDTH), own)] + kv_specs * 3
                 + [pl.BlockSpec((3, N_HEADS * BLOCK, 3 * BLOCK), lambda bi, i: (0, 0, 0))],
        out_specs=pl.BlockSpec((None, tq, ATTN_WIDTH), own),
        out_shape=jax.ShapeDtypeStruct((b, s, ATTN_WIDTH), BF16),
        compiler_params=_cparams(("parallel", "parallel")),
        name="window_attention",
    )(sink.astype(F32), q, k, k, k, v, v, v, v_swapped, v_swapped, v_swapped, bias)


def _dft1_kernel(x_ref, a_ref, t_ref, x2, *, rows):
    a = a_ref[...]
    na = a.shape[0] // 2
    n_slabs = x2.shape[0]
    lanes = lambda c: slice(c * LANES, (c + 1) * LANES)
    for c in range(n_slabs):
        x2[c] = x_ref[:, :, lanes(c)].reshape(x2.shape[1:])
    for jj in range(rows):
        strided = pl.ds(jj, na, stride=rows)
        xj = jnp.concatenate([x2[c, strided, :] for c in range(n_slabs)], axis=1)
        t = jnp.dot(a, xj.astype(BF16), preferred_element_type=F32)
        t_ref[:, jj, :] = pltpu.bitcast(t.astype(BF16), jnp.uint32)


def _dft2_kernel(t_ref, gr_ref, gi_ref, cs_ref, o_ref, o2, *, rows):
    cs = cs_ref[...]
    na = o_ref.shape[0]
    for aa in range(rows):
        t = pltpu.bitcast(t_ref[aa], BF16)
        zr = jnp.dot(gr_ref[aa], t, preferred_element_type=F32)
        zi = jnp.dot(gi_ref[aa], t, preferred_element_type=F32)
        for g in range(N_FOURIER_GROUPS):
            sl = slice(g * FOURIER_GROUP_DIM, (g + 1) * FOURIER_GROUP_DIM)
            z = jnp.concatenate([zr[:, sl], zi[:, sl]], axis=1).astype(BF16)
            y = jnp.dot(z, cs, preferred_element_type=F32)
            o2[g, pl.ds(aa, na, stride=rows), :] = y
    for g in range(N_FOURIER_GROUPS):
        sl = slice(g * FOURIER_GROUP_DIM, (g + 1) * FOURIER_GROUP_DIM)
        o_ref[:, :, sl] = o2[g].reshape(na, rows, FOURIER_GROUP_DIM)


def _fourier_tables(na):
    def interleave(even, odd, axis):
        both = np.stack([even, odd], axis=axis + 1)
        shape = list(even.shape)
        shape[axis] *= 2
        return jnp.asarray(both.reshape(shape).astype(BF16))

    n = na * na
    k = np.arange(na, dtype=np.float64)
    ang = 2.0 * np.pi * np.outer(k, k) / na
    a1 = interleave(np.cos(ang), -np.sin(ang), 0)
    ka = np.arange(na, dtype=np.float64)[:, None, None]
    kb = np.arange(na, dtype=np.float64)[None, :, None]
    nl = np.arange(na, dtype=np.float64)[None, None, :]
    th = 2.0 * np.pi * ((nl * (ka + na * kb)) % n) / n
    gr, gi = np.cos(th), -np.sin(th)
    g_re = interleave(gr, -gi, 2)
    g_im = interleave(gi, gr, 2)
    kc = np.arange(FOURIER_GROUP_DIM, dtype=np.float64)
    angc = 2.0 * np.pi * np.outer(kc, kc) / FOURIER_GROUP_DIM
    scale = 1.0 / np.sqrt(float(n) * FOURIER_GROUP_DIM)
    cs = np.concatenate([np.cos(angc), np.sin(angc)], axis=0) * scale
    return a1, g_re, g_im, jnp.asarray(cs.astype(BF16))


def _fourier(f, tables, *, rows):
    b, s, w = f.shape
    na = int(round(np.sqrt(s)))
    assert na * na == s and na % rows == 0
    a1, g_re, g_im, cs = tables
    x4 = f.reshape(b, na, na, w)
    const2 = lambda bi, j: (0, 0)
    t = pl.pallas_call(
        functools.partial(_dft1_kernel, rows=rows),
        grid=(b, na // rows),
        in_specs=[pl.BlockSpec((None, na, rows, w), lambda bi, j: (bi, 0, j, 0)),
                  pl.BlockSpec(a1.shape, const2)],
        out_specs=pl.BlockSpec((None, na, rows, w), lambda bi, j: (bi, 0, j, 0)),
        out_shape=jax.ShapeDtypeStruct((b, na, na, w), jnp.uint32),
        scratch_shapes=[pltpu.VMEM((w // LANES, na * rows, LANES), F32)],
        compiler_params=_cparams(("parallel", "parallel")),
        name="seq_dft_stage1",
    )(x4, a1)
    y = pl.pallas_call(
        functools.partial(_dft2_kernel, rows=rows),
        grid=(b, na // rows),
        in_specs=[pl.BlockSpec((None, rows, na, w), lambda bi, a: (bi, a, 0, 0))]
                 + [pl.BlockSpec((rows,) + g_re.shape[1:], lambda bi, a: (a, 0, 0))] * 2
                 + [pl.BlockSpec(cs.shape, const2)],
        out_specs=pl.BlockSpec((None, na, rows, w), lambda bi, a: (bi, 0, a, 0)),
        out_shape=jax.ShapeDtypeStruct((b, na, na, w), F32),
        scratch_shapes=[pltpu.VMEM((N_FOURIER_GROUPS, na * rows, FOURIER_GROUP_DIM), F32)],
        compiler_params=_cparams(("parallel", "parallel")),
        name="seq_dft_stage2",
    )(t, g_re, g_im, cs)
    return y.reshape(b, s, w)


def _outproj_kernel(a_ref, f_ref, ga_ref, gb_ref, x_ref, wa_ref, wf_ref, wo_ref, g_ref, wr_ref,
                    xn_ref, h_ref, aff_ref, *, chunk):
    for r in range(x_ref.shape[0] // chunk):
        rs = pl.ds(r * chunk, chunk)
        a = jnp.dot(a_ref[rs, :], wa_ref[...], preferred_element_type=F32)
        fo = jnp.dot(f_ref[rs, :].astype(BF16), wf_ref[...], preferred_element_type=F32)
        merged = ga_ref[rs, :].astype(F32) * a + gb_ref[rs, :].astype(F32) * fo
        xn = x_ref[rs, :] + jnp.dot(merged.astype(BF16), wo_ref[...], preferred_element_type=F32)
        xn_ref[rs, :] = xn
        ms = jnp.mean(xn * xn, axis=-1, keepdims=True)
        h = (xn * lax.rsqrt(ms + EPS)) * g_ref[...]
        dt = h.shape[1] // LANES
        for c in range(dt):
            h_ref[pl.ds(r * chunk * dt + c, chunk, stride=dt), :] = h[:, c * LANES:(c + 1) * LANES]
        logits = _nt_dot(wr_ref[...], h.astype(BF16))
        e = jnp.exp(logits - jnp.max(logits, axis=0, keepdims=True))
        aff_ref[:, rs] = e / jnp.sum(e, axis=0, keepdims=True)


def _outproj(attn, four, ga, gb, x, wa, wf, wo, g, wr_t, *, tm, chunk):
    b, s, d = x.shape
    dt = d // LANES
    tok = lambda bi, i: (bi, i, 0)
    const = lambda bi, i: (0, 0)
    xn, h, aff = pl.pallas_call(
        functools.partial(_outproj_kernel, chunk=chunk),
        grid=(b, s // tm),
        in_specs=[pl.BlockSpec((None, tm, ATTN_WIDTH), tok),
                  pl.BlockSpec((None, tm, FOURIER_WIDTH), tok),
                  pl.BlockSpec((None, tm, d), tok),
                  pl.BlockSpec((None, tm, d), tok),
                  pl.BlockSpec((None, tm, d), tok),
                  pl.BlockSpec((ATTN_WIDTH, d), const),
                  pl.BlockSpec((FOURIER_WIDTH, d), const),
                  pl.BlockSpec((d, d), const),
                  pl.BlockSpec((1, d), const),
                  pl.BlockSpec((N_EXPERTS, d), const)],
        out_specs=[pl.BlockSpec((None, tm, d), tok),
                   pl.BlockSpec((None, tm * dt, LANES), tok),
                   pl.BlockSpec((None, N_EXPERTS, tm), lambda bi, i: (bi, 0, i))],
        out_shape=[jax.ShapeDtypeStruct((b, s, d), F32),
                   jax.ShapeDtypeStruct((b, s * dt, LANES), F32),
                   jax.ShapeDtypeStruct((b, N_EXPERTS, s), F32)],
        compiler_params=_cparams(("parallel", "parallel")),
        name="outproj_router",
    )(attn, four, ga, gb, x, wa, wf, wo, g.reshape(1, d), wr_t.astype(BF16))
    return xn, h.reshape(b, s, dt, LANES), aff


def _tri(n, kind):
    r = lax.broadcasted_iota(I32, (n, n), 0)
    c = lax.broadcasted_iota(I32, (n, n), 1)
    cond = {"row_le_col": r <= c, "row_lt_col": r < c, "col_le_row": c <= r, "col_lt_row": c < r}[kind]
    return jnp.where(cond, 1.0, 0.0).astype(BF16)


def _split128(v):
    hi = jnp.floor(v * (1.0 / LANES))
    return hi, v - hi * LANES


def _routing_kernel(aff_ref, idx_ref, pos_ref, rank_ref, gt_scr, eq_scr, need_scr, *, cap):
    ne, nj, ni = aff_ref.shape
    bits = lax.bitcast_convert_type(aff_ref[...], I32)

    def bisect(it, thr):
        cand = thr | jnp.left_shift(jnp.int32(1), 30 - it)
        cnt = jnp.sum(jnp.where(bits >= cand, 1.0, 0.0), axis=(1, 2), keepdims=True)
        return jnp.where(cnt >= cap, cand, thr)

    thr = lax.fori_loop(0, 31, bisect, jnp.zeros((ne, 1, 1), I32))
    gt = jnp.where(bits > thr, 1.0, 0.0)
    gt_scr[...] = gt
    eq_scr[...] = jnp.where(bits == thr, 1.0, 0.0)
    need = cap - jnp.sum(gt, axis=(1, 2), keepdims=True)
    need_scr[...] = jnp.broadcast_to(need, (ne, SUBLANES, ni))

    ones_sq = jnp.ones((ni, ni), BF16)
    ones_row = jnp.ones((SUBLANES, ni), BF16)
    ones_rowj = jnp.ones((SUBLANES, nj), BF16)
    u_incl = _tri(ni, "row_le_col")
    uj_strict = _tri(nj, "row_lt_col")
    lj_strict = _tri(nj, "col_lt_row")
    l_incl = _tri(ni, "col_le_row")
    bdot = lambda a, b: jnp.dot(a, b, preferred_element_type=F32)

    c_lane = lax.broadcasted_iota(I32, (nj, cap), 1).astype(F32)
    c_row = lax.broadcasted_iota(I32, (SUBLANES, cap), 1).astype(F32)
    j_sub = lax.broadcasted_iota(I32, (nj, cap), 0).astype(F32)

    def slots(e, carry):
        eq = eq_scr[e]
        eqb = eq.astype(BF16)
        tie_rank = bdot(lj_strict, bdot(eqb, ones_sq).astype(BF16)) + bdot(eqb, u_incl)
        take = jnp.where(tie_rank <= need_scr[e][0:1, :], eq, 0.0)
        sel = jnp.maximum(gt_scr[e], take)
        selb = sel.astype(BF16)
        lc = bdot(selb, u_incl)
        lc_t = _nt_dot(l_incl, selb)
        n_rep = bdot(selb, ones_sq)
        s_rep = bdot(lj_strict, n_rep.astype(BF16)) + n_rep
        n_lane = _nt_dot(ones_row, selb)
        pex_lane = bdot(n_lane.astype(BF16), uj_strict)
        s_wide = jnp.concatenate([s_rep] * (cap // ni), axis=1) if cap > ni else s_rep[:, :cap]
        jc = bdot(ones_rowj, jnp.where(s_wide <= c_lane, 1.0, 0.0).astype(BF16))
        onehot = jnp.where(j_sub == jnp.broadcast_to(jc[0:1], (nj, cap)), 1.0, 0.0).astype(BF16)
        phi, plo = _split128(pex_lane)
        r = c_row - (bdot(phi.astype(BF16), onehot) * LANES + bdot(plo.astype(BF16), onehot))
        lcs = bdot(lc_t.astype(BF16), onehot)
        ic = bdot(ones_row, jnp.where(lcs <= jnp.broadcast_to(r[0:1], (ni, cap)), 1.0, 0.0).astype(BF16))
        idx_ref[e] = (jc * ni + ic).astype(I32)
        pos_ref[e] = pex_lane.astype(I32)
        rank_ref[e] = jnp.where(sel > 0.0, lc - 1.0, -1.0)
        return carry

    lax.fori_loop(0, ne, slots, 0)


def _routing(aff, *, cap):
    b, ne, s = aff.shape
    nj = s // LANES
    aff4 = aff.reshape(b, ne, nj, LANES)
    per_b = lambda bi: (bi, 0, 0, 0)
    idx, pos, rank = pl.pallas_call(
        functools.partial(_routing_kernel, cap=cap),
        grid=(b,),
        in_specs=[pl.BlockSpec((None, ne, nj, LANES), per_b)],
        out_specs=[pl.BlockSpec((None, ne, SUBLANES, cap), per_b),
                   pl.BlockSpec((None, ne, SUBLANES, nj), per_b),
                   pl.BlockSpec((None, ne, nj, LANES), per_b)],
        out_shape=[jax.ShapeDtypeStruct((b, ne, SUBLANES, cap), I32),
                   jax.ShapeDtypeStruct((b, ne, SUBLANES, nj), I32),
                   jax.ShapeDtypeStruct((b, ne, nj, LANES), F32)],
        scratch_shapes=[pltpu.VMEM((ne, nj, LANES), F32)] * 2 + [pltpu.VMEM((ne, SUBLANES, LANES), F32)],
        compiler_params=_cparams(("parallel",)),
        name="expert_choice_routing",
    )(aff4)
    return idx[:, :, 0, :], pos[:, :, 0, :], rank


def _ffn_kernel(idx_cur, idx_nxt, h_hbm, wg_ref, wu_ref, wd_ref, y_ref, xs, xsb, gsem,
                *, cap, n_pairs, n_ftiles, per_step, m_chunk):
    p = pl.program_id(0)
    j = pl.program_id(1)
    n_slots = n_ftiles * per_step
    pitch = xs.shape[0] // n_slots
    dt = pitch - 1
    last_pair = n_pairs - 1
    b_cur = p // N_EXPERTS
    b_nxt = jnp.minimum(p + 1, last_pair) // N_EXPERTS

    def gather(ref, bq, c):
        cc = jnp.minimum(c, cap - 1)
        tok = ref[cc >> LANE_BITS, cc & (LANES - 1)]
        dst = xs.at[pl.ds(c * pitch, dt)]
        return pltpu.make_async_copy(h_hbm.at[bq, tok], dst, gsem)

    def wait_gathers():
        tiles = xs.at[pl.ds(0, n_slots * dt)]
        pltpu.make_async_copy(tiles, tiles, gsem).wait()

    @pl.when((p == 0) & (j == 0))
    def _():
        def start(c, carry):
            gather(idx_cur, b_cur, c).start()
            return carry

        lax.fori_loop(0, n_slots, start, 0)

    n_groups = cap // m_chunk
    per_group = per_step // n_groups

    def ff_tile(first):
        wg = wg_ref[...].astype(BF16)
        wu = wu_ref[...].astype(BF16)
        wd = wd_ref[...].astype(BF16)
        for m in range(n_groups):
            for t in range(per_group):
                gather(idx_nxt, b_nxt, j * per_step + (m * per_group + t)).start()
            rows = pl.ds(m * m_chunk, m_chunk)
            x = xsb[rows, :]
            hg = jnp.dot(x, wg, preferred_element_type=F32)
            hu = jnp.dot(x, wu, preferred_element_type=F32)
            act = (hg * _sigmoid(hg)) * hu
            part = jnp.dot(act.astype(BF16), wd, preferred_element_type=F32)
            if first:
                y_ref[rows, :] = part
            else:
                y_ref[rows, :] += part

    @pl.when(j == 0)
    def _():
        wait_gathers()
        for c in range(dt):
            xsb[:, c * LANES:(c + 1) * LANES] = xs[pl.ds(c, cap, stride=pitch), :].astype(BF16)
        ff_tile(True)

    @pl.when(j > 0)
    def _():
        ff_tile(False)

    @pl.when((p == last_pair) & (j == n_ftiles - 1))
    def _():
        wait_gathers()


def _expert_ffn(h, idx, wg, wu, wd, layer, *, cap, tf, m_chunk):
    b, s, dt, _ = h.shape
    d = dt * LANES
    ne, dff = wg.shape[1], wg.shape[3]
    n_ftiles = dff // tf
    n_pairs = b * ne
    n_groups = cap // m_chunk
    per_step = -(-cap // (n_ftiles * n_groups)) * n_groups
    n_slots = n_ftiles * per_step
    tab = idx.reshape(n_pairs, cap // LANES, LANES)
    tab_spec = lambda f: pl.BlockSpec((None, cap // LANES, LANES), lambda p, j: (f(p), 0, 0),
                                      memory_space=pltpu.SMEM)
    return pl.pallas_call(
        functools.partial(_ffn_kernel, cap=cap, n_pairs=n_pairs, n_ftiles=n_ftiles, per_step=per_step,
                          m_chunk=m_chunk),
        grid=(n_pairs, n_ftiles),
        in_specs=[tab_spec(lambda p: p),
                  tab_spec(lambda p: jnp.minimum(p + 1, n_pairs - 1)),
                  pl.BlockSpec(memory_space=pl.ANY),
                  pl.BlockSpec((None, None, d, tf), lambda p, j: (layer, p % ne, 0, j)),
                  pl.BlockSpec((None, None, d, tf), lambda p, j: (layer, p % ne, 0, j)),
                  pl.BlockSpec((None, None, tf, d), lambda p, j: (layer, p % ne, j, 0))],
        out_specs=pl.BlockSpec((None, None, cap, d), lambda p, j: (p // ne, p % ne, 0, 0)),
        out_shape=jax.ShapeDtypeStruct((b, ne, cap, d), F32),
        scratch_shapes=[pltpu.VMEM((n_slots * (dt + 1), LANES), F32), pltpu.VMEM((cap, d), BF16),
                        pltpu.SemaphoreType.DMA(())],
        compiler_params=_cparams(("arbitrary", "arbitrary")),
        name="expert_swiglu",
    )(tab, tab, h, wg, wu, wd)


def _combine_kernel(pos_ref, rank_ref, aff_ref, x_ref, g_ref, y_hbm, o_ref, ybuf, acc, sem,
                    *, cap, nj, n_tiles, final_norm):
    b = pl.program_id(0)
    j = pl.program_id(1)
    t = b * nj + j
    ne = N_EXPERTS
    ring = COMBINE_AHEAD + 1
    assert ring & (ring - 1) == 0
    in_ring = lambda k: k & (ring - 1)

    def nominal(bq, e, jq, r):
        return ((pos_ref[bq * ne + e, jq] >> 3) << 3) + r * EWIN

    def window(bq, e, jq, r):
        return pl.multiple_of(jnp.minimum(nominal(bq, e, jq, r), cap - EWIN), SUBLANES)

    def start_fetch(ahead, r, slot):
        jq = j + ahead
        wrap = (jq >= nj).astype(I32)
        bq = b + wrap
        jq = jq - wrap * nj
        past = bq * nj + jq >= n_tiles
        bq = jnp.where(past, n_tiles // nj - 1, bq)
        jq = jnp.where(past, nj - 1, jq)
        for e in range(ne):
            pltpu.make_async_copy(y_hbm.at[bq, e, pl.ds(window(bq, e, jq, r), EWIN)],
                                  ybuf.at[slot, pl.ds(e * EWIN, EWIN)], sem.at[slot]).start()

    def wait_fetch(slot):
        pltpu.make_async_copy(ybuf.at[slot], ybuf.at[slot], sem.at[slot]).wait()

    rk = rank_ref[...]
    af = aff_ref[...]
    row = lax.broadcasted_iota(I32, (EWIN, LANES), 0).astype(F32)

    def add_round(r, slot):
        parts = []
        for e in range(ne):
            p0 = pos_ref[b * ne + e, j]
            local = row + (window(b, e, j, r) - p0).astype(F32)
            want = jnp.where(local >= jnp.maximum(nominal(b, e, j, r) - p0, 0).astype(F32), local, -2.0)
            parts.append(jnp.where(rk[e:e + 1, :] == want, af[e:e + 1, :], 0.0))
        gate_t = jnp.concatenate(parts, axis=0)
        onehot = jnp.where(gate_t != 0.0, 1.0, 0.0).T.astype(BF16)
        row_gate = jnp.sum(gate_t, axis=1, keepdims=True)
        scaled = jnp.where(row_gate != 0.0, ybuf[slot] * row_gate, 0.0)
        hi = scaled.astype(BF16)
        lo = (scaled - hi.astype(F32)).astype(BF16)
        return jnp.dot(onehot, hi, preferred_element_type=F32) + jnp.dot(onehot, lo, preferred_element_type=F32)

    @pl.when(t == 0)
    def _():
        for ahead in range(COMBINE_AHEAD):
            start_fetch(ahead, 0, ahead)

    start_fetch(COMBINE_AHEAD, 0, in_ring(t + COMBINE_AHEAD))
    slot0 = in_ring(t)
    wait_fetch(slot0)
    acc[...] = add_round(0, slot0)

    span = jnp.int32(1)
    for e in range(ne):
        span = jnp.maximum(span, pos_ref[b * ne + e, j + 1] - nominal(b, e, j, 0))
    n_rounds = (span + EWIN - 1) // EWIN

    def extra_round(r, carry):
        start_fetch(0, r, ring)
        wait_fetch(ring)
        acc[...] += add_round(r, ring)
        return carry

    lax.fori_loop(1, n_rounds, extra_round, 0)

    @pl.when(t == n_tiles - 1)
    def _():
        for ahead in range(1, COMBINE_AHEAD + 1):
            wait_fetch(in_ring(t + ahead))

    y = x_ref[...] + acc[...]
    if final_norm:
        ms = jnp.mean(y * y, axis=-1, keepdims=True)
        y = (y * lax.rsqrt(ms + EPS)) * g_ref[...]
    o_ref[...] = y


def _combine(x, y, pos, rank, aff, g_final, *, cap, final_norm):
    b, s, d = x.shape
    nj = s // LANES
    ne = N_EXPERTS
    pos_tab = jnp.concatenate([pos, jnp.full((b, ne, 1), cap, I32)], axis=2).reshape(b * ne, nj + 1)
    rank_t = jnp.transpose(rank, (0, 2, 1, 3))
    return pl.pallas_call(
        functools.partial(_combine_kernel, cap=cap, nj=nj, n_tiles=b * nj, final_norm=final_norm),
        grid_spec=pltpu.PrefetchScalarGridSpec(
            num_scalar_prefetch=1,
            grid=(b, nj),
            in_specs=[pl.BlockSpec((None, None, ne, LANES), lambda bi, j, tab: (bi, j, 0, 0)),
                      pl.BlockSpec((None, ne, LANES), lambda bi, j, tab: (bi, 0, j)),
                      pl.BlockSpec((None, LANES, d), lambda bi, j, tab: (bi, j, 0)),
                      pl.BlockSpec((1, d), lambda bi, j, tab: (0, 0)),
                      pl.BlockSpec(memory_space=pl.ANY)],
            out_specs=pl.BlockSpec((None, LANES, d), lambda bi, j, tab: (bi, j, 0)),
            scratch_shapes=[pltpu.VMEM((COMBINE_AHEAD + 2, ne * EWIN, d), F32), pltpu.VMEM((LANES, d), F32),
                            pltpu.SemaphoreType.DMA((COMBINE_AHEAD + 2,))]),
        out_shape=jax.ShapeDtypeStruct((b, s, d), F32),
        compiler_params=_cparams(("arbitrary", "arbitrary")),
        name="moe_combine",
    )(pos_tab, rank_t, aff, x, g_final.reshape(1, d), y)


def kernel(x, rel_bias, g_mix, w_in, attn_sink, w_attn_proj, w_fourier_proj, w_out, g_ffn, w_router,
           w_exp_gate, w_exp_up, w_exp_down, g_final):
    b, s, d = x.shape
    depth = g_mix.shape[0]
    cap = CAPACITY_FACTOR * s // N_EXPERTS
    bias = _bias_table(rel_bias)
    tables = _fourier_tables(int(round(np.sqrt(s))))
    for l in range(depth):
        q, k, v, vs, f, ga, gb = _inproj(x.reshape(b * s, d), g_mix[l], w_in[l].astype(BF16), tm=1024, chunk=256)
        shp = lambda a: a.reshape(b, s, a.shape[-1])
        attn = _attention(shp(q), shp(k), shp(v), shp(vs), attn_sink[l], bias, blocks_per_step=8)
        four = _fourier(shp(f), tables, rows=8)
        xn, h, aff = _outproj(attn, four, shp(ga), shp(gb), x,
                              w_attn_proj[l].astype(BF16), w_fourier_proj[l].astype(BF16),
                              w_out[l].astype(BF16), g_ffn[l], w_router[l].T, tm=1024, chunk=256)
        idx, pos, rank = _routing(aff, cap=cap)
        y = _expert_ffn(h, idx, w_exp_gate, w_exp_up, w_exp_down, l, cap=cap, tf=256, m_chunk=min(cap, 512))
        x = _combine(xn, y, pos, rank, aff, g_final, cap=cap, final_norm=(l == depth - 1))
    return x
```

```python
import functools

import numpy as np
import jax
import jax.numpy as jnp
from jax import lax
from jax.experimental import pallas as pl
from jax.experimental.pallas import tpu as pltpu

F32 = jnp.float32
BF16 = jnp.bfloat16
I32 = jnp.int32

N_HEADS = 8
N_KV_HEADS = 2
HEAD_DIM = 64
ATTN_WIDTH = N_HEADS * HEAD_DIM
KV_WIDTH = N_KV_HEADS * HEAD_DIM
WINDOW = 128
BLOCK = 128
N_BUCKETS = 32
MAX_DISTANCE = 128
N_FOURIER_GROUPS = 4
FOURIER_GROUP_DIM = 128
FOURIER_WIDTH = N_FOURIER_GROUPS * FOURIER_GROUP_DIM
N_EXPERTS = 16
CAPACITY_FACTOR = 2
EPS = 1e-6
NEG_INF = -1e30

LANES = 128
LANE_BITS = 7
SUBLANES = 8
VMEM_LIMIT = 56 * 1024 * 1024
EWIN = 32
COMBINE_AHEAD = 3


def _cparams(sem):
    return pltpu.CompilerParams(dimension_semantics=sem, vmem_limit_bytes=VMEM_LIMIT)


def _nt_dot(a, b, **kw):
    return lax.dot_general(a, b, (((1,), (1,)), ((), ())), preferred_element_type=F32, **kw)


def _sigmoid(x):
    return 1.0 / (1.0 + jnp.exp(-x))


def _t5_bucket(rel):
    half = N_BUCKETS // 2
    max_exact = half // 2
    ret = (rel > 0).astype(jnp.int32) * half
    n = jnp.abs(rel)
    nf = jnp.maximum(n, 1).astype(jnp.float32)
    large = max_exact + (jnp.log(nf / max_exact) / np.float32(np.log(MAX_DISTANCE / max_exact))
                         * (half - max_exact)).astype(jnp.int32)
    large = jnp.minimum(large, half - 1)
    return ret + jnp.where(n < max_exact, n, large)


def _bias_kernel(relb_ref, bucket_ref, o_ref):
    bk = bucket_ref[...]
    col = lax.broadcasted_iota(I32, bk.shape, 1)
    for h in range(N_HEADS):
        acc = jnp.full(bk.shape, NEG_INF, F32)
        for b in range(N_BUCKETS):
            acc = jnp.where(bk == b, relb_ref[b, h], acc)
        o_ref[0, h] = jnp.where(col >= BLOCK, acc, NEG_INF)
        o_ref[1, h] = acc
        o_ref[2, h] = jnp.where(col < 2 * BLOCK, acc, NEG_INF)


def _bias_table(rel_bias):
    q_loc = jnp.arange(BLOCK)
    k_loc = jnp.arange(3 * BLOCK) - BLOCK
    rel = k_loc[None, :] - q_loc[:, None]
    bucket = jnp.where(jnp.abs(rel) <= WINDOW, _t5_bucket(rel), -1).astype(I32)
    table = pl.pallas_call(
        _bias_kernel,
        out_shape=jax.ShapeDtypeStruct((3, N_HEADS, BLOCK, 3 * BLOCK), F32),
        in_specs=[pl.BlockSpec(memory_space=pltpu.SMEM),
                  pl.BlockSpec(memory_space=pltpu.VMEM)],
        out_specs=pl.BlockSpec(memory_space=pltpu.VMEM),
        name="rel_bias_table",
    )(rel_bias.astype(F32), bucket)
    return table.reshape(3, N_HEADS * BLOCK, 3 * BLOCK)


def _inproj_kernel(x_ref, g_ref, w_ref, q_ref, k_ref, v_ref, vs_ref, f_ref, ga_ref, gb_ref, *, d_model, chunk):
    for r in range(x_ref.shape[0] // chunk):
        rs = pl.ds(r * chunk, chunk)
        x = x_ref[rs, :]
        ms = jnp.mean(x * x, axis=-1, keepdims=True)
        h = (x * lax.rsqrt(ms + EPS)) * g_ref[...]
        z = jnp.dot(h.astype(BF16), w_ref[...], preferred_element_type=F32)
        o = 0
        q_ref[rs, :] = (z[:, o:o + ATTN_WIDTH] * (HEAD_DIM ** -0.5)).astype(BF16)
        o += ATTN_WIDTH
        k_ref[rs, :] = z[:, o:o + KV_WIDTH].astype(BF16)
        o += KV_WIDTH
        v_ref[rs, :] = z[:, o:o + KV_WIDTH].astype(BF16)
        half = KV_WIDTH // 2
        vs_ref[rs, :] = jnp.concatenate([z[:, o + half:o + KV_WIDTH], z[:, o:o + half]], axis=1).astype(BF16)
        o += KV_WIDTH
        f_ref[rs, :] = z[:, o:o + FOURIER_WIDTH]
        o += FOURIER_WIDTH
        ga_ref[rs, :] = _sigmoid(z[:, o:o + d_model]).astype(BF16)
        o += d_model
        gb_ref[rs, :] = _sigmoid(z[:, o:o + d_model]).astype(BF16)


def _inproj(x2, g, w_bf16, *, tm, chunk):
    n, d = x2.shape
    in_w = w_bf16.shape[1]
    widths = (ATTN_WIDTH, KV_WIDTH, KV_WIDTH, KV_WIDTH, FOURIER_WIDTH, d, d)
    dtypes = (BF16, BF16, BF16, BF16, F32, BF16, BF16)
    row = lambda i: (i, 0)
    return pl.pallas_call(
        functools.partial(_inproj_kernel, d_model=d, chunk=chunk),
        grid=(n // tm,),
        in_specs=[pl.BlockSpec((tm, d), row),
                  pl.BlockSpec((1, d), lambda i: (0, 0)),
                  pl.BlockSpec((d, in_w), lambda i: (0, 0))],
        out_specs=[pl.BlockSpec((tm, w), row) for w in widths],
        out_shape=[jax.ShapeDtypeStruct((n, w), dt) for w, dt in zip(widths, dtypes)],
        compiler_params=_cparams(("parallel",)),
        name="inproj",
    )(x2, g.reshape(1, d), w_bf16)


def _attn_kernel(sink_ref, q_ref, kp_ref, ko_ref, kn_ref, vp_ref, vo_ref, vn_ref, wp_ref, wo_ref, wn_ref,
                 bias_ref, o_ref, *, n_blocks, blocks_per_step):
    i = pl.program_id(1)
    kcat = jnp.concatenate([kp_ref[...], ko_ref[...], kn_ref[...]], axis=0)
    vcat = jnp.concatenate([vp_ref[...], vo_ref[...], vn_ref[...]], axis=0)
    wcat = jnp.concatenate([wp_ref[...], wo_ref[...], wn_ref[...]], axis=0)
    group = N_HEADS // N_KV_HEADS
    head = lambda a, h: a[:, h * HEAD_DIM:(h + 1) * HEAD_DIM]
    sink = jnp.concatenate([jnp.full((BLOCK, LANES), sink_ref[h], F32) for h in range(N_HEADS)], axis=0)
    low_v = lax.broadcasted_iota(I32, vcat.shape, 1) < HEAD_DIM
    low_o = lax.broadcasted_iota(I32, (BLOCK, LANES), 1) < HEAD_DIM
    vx = {(kv, half): jnp.where(low_v if half == 0 else jnp.logical_not(low_v),
                                vcat if kv == half else wcat, jnp.ones_like(vcat))
          for kv in range(N_KV_HEADS) for half in range(2)}
    for t in range(blocks_per_step):
        blk = i * blocks_per_step + t
        kind = jnp.where(blk == 0, 0, jnp.where(blk == n_blocks - 1, 2, 1))
        kw = kcat[t * BLOCK:(t + 3) * BLOCK]
        qb = q_ref[t * BLOCK:(t + 1) * BLOCK, :]
        s = jnp.concatenate([_nt_dot(head(qb, h), head(kw, h // group)) for h in range(N_HEADS)], axis=0)
        s = s + bias_ref[kind]
        m = jnp.maximum(jnp.broadcast_to(jnp.max(s, axis=-1, keepdims=True), sink.shape), sink)
        p = jnp.exp(s - jnp.concatenate([m] * 3, axis=1)).astype(BF16)
        sink_term = jnp.exp(sink - m)
        for pair in range(N_HEADS // 2):
            halves = []
            for half in range(2):
                h = 2 * pair + half
                rows = slice(h * BLOCK, (h + 1) * BLOCK)
                out = jnp.dot(p[rows], vx[h // group, half][t * BLOCK:(t + 3) * BLOCK],
                              preferred_element_type=F32)
                denom = pltpu.roll(out, HEAD_DIM, axis=1) + sink_term[rows]
                halves.append(out * (1.0 / denom))
            tile = jnp.where(low_o, halves[0], halves[1])
            o_ref[t * BLOCK:(t + 1) * BLOCK, pair * LANES:(pair + 1) * LANES] = tile.astype(o_ref.dtype)


def _attention(q, k, v, v_swapped, sink, bias, *, blocks_per_step):
    b, s, _ = q.shape
    assert N_KV_HEADS == 2 and KV_WIDTH == LANES and N_HEADS % 2 == 0
    n_blocks = s // BLOCK
    tq = blocks_per_step * BLOCK
    prev = lambda bi, i: (bi, jnp.maximum(i * blocks_per_step - 1, 0), 0)
    own = lambda bi, i: (bi, i, 0)
    nxt = lambda bi, i: (bi, jnp.minimum((i + 1) * blocks_per_step, n_blocks - 1), 0)
    kv_specs = [pl.BlockSpec((None, BLOCK, KV_WIDTH), prev),
                pl.BlockSpec((None, tq, KV_WIDTH), own),
                pl.BlockSpec((None, BLOCK, KV_WIDTH), nxt)]
    return pl.pallas_call(
        functools.partial(_attn_kernel, n_blocks=n_blocks, blocks_per_step=blocks_per_step),
        grid=(b, s // tq),
        in_specs=[pl.BlockSpec(memory_space=pltpu.SMEM),
                  pl.BlockSpec((None, tq, ATTN_WIDTH), own)] + kv_specs * 3
                 + [pl.BlockSpec((3, N_HEADS * BLOCK, 3 * BLOCK), lambda bi, i: (0, 0, 0))],
        out_specs=pl.BlockSpec((None, tq, ATTN_WIDTH), own),
        out_shape=jax.ShapeDtypeStruct((b, s, ATTN_WIDTH), BF16),
        compiler_params=_cparams(("parallel", "parallel")),
        name="window_attention",
    )(sink.astype(F32), q, k, k, k, v, v, v, v_swapped, v_swapped, v_swapped, bias)


def _dft1_kernel(x_ref, a_ref, t_ref, x2, *, rows):
    a = a_ref[...]
    na = a.shape[0] // 2
    n_slabs = x2.shape[0]
    lanes = lambda c: slice(c * LANES, (c + 1) * LANES)
    for c in range(n_slabs):
        x2[c] = x_ref[:, :, lanes(c)].reshape(x2.shape[1:])
    for jj in range(rows):
        strided = pl.ds(jj, na, stride=rows)
        xj = jnp.concatenate([x2[c, strided, :] for c in range(n_slabs)], axis=1)
        t = jnp.dot(a, xj.astype(BF16), preferred_element_type=F32)
        t_ref[:, jj, :] = pltpu.bitcast(t.astype(BF16), jnp.uint32)


def _dft2_kernel(t_ref, gr_ref, gi_ref, cs_ref, o_ref, o2, *, rows):
    cs = cs_ref[...]
    na = o_ref.shape[0]
    pairs = rows // 2

    def bf16_bits(y):
        return lax.bitcast_convert_type(y.astype(BF16).astype(F32), jnp.uint32)

    for aa in range(rows):
        t = pltpu.bitcast(t_ref[aa], BF16)
        zr = jnp.dot(gr_ref[aa], t, preferred_element_type=F32)
        zi = jnp.dot(gi_ref[aa], t, preferred_element_type=F32)
        for g in range(N_FOURIER_GROUPS):
            sl = slice(g * FOURIER_GROUP_DIM, (g + 1) * FOURIER_GROUP_DIM)
            z = jnp.concatenate([zr[:, sl], zi[:, sl]], axis=1).astype(BF16)
            y = jnp.dot(z, cs, preferred_element_type=F32)
            rows_q = pl.ds(aa // 2, na, stride=pairs)
            if aa % 2 == 0:
                o2[g, rows_q, :] = bf16_bits(y) >> 16
            else:
                o2[g, rows_q, :] = o2[g, rows_q, :] | bf16_bits(y)
    for g in range(N_FOURIER_GROUPS):
        sl = slice(g * FOURIER_GROUP_DIM, (g + 1) * FOURIER_GROUP_DIM)
        o_ref[:, :, sl] = o2[g].reshape(na, pairs, FOURIER_GROUP_DIM)


def _fourier_tables(na):
    def interleave(even, odd, axis):
        both = np.stack([even, odd], axis=axis + 1)
        shape = list(even.shape)
        shape[axis] *= 2
        return jnp.asarray(both.reshape(shape).astype(BF16))

    n = na * na
    k = np.arange(na, dtype=np.float64)
    ang = 2.0 * np.pi * np.outer(k, k) / na
    a1 = interleave(np.cos(ang), -np.sin(ang), 0)
    ka = np.arange(na, dtype=np.float64)[:, None, None]
    kb = np.arange(na, dtype=np.float64)[None, :, None]
    nl = np.arange(na, dtype=np.float64)[None, None, :]
    th = 2.0 * np.pi * ((nl * (ka + na * kb)) % n) / n
    gr, gi = np.cos(th), -np.sin(th)
    g_re = interleave(gr, -gi, 2)
    g_im = interleave(gi, gr, 2)
    kc = np.arange(FOURIER_GROUP_DIM, dtype=np.float64)
    angc = 2.0 * np.pi * np.outer(kc, kc) / FOURIER_GROUP_DIM
    scale = 1.0 / np.sqrt(float(n) * FOURIER_GROUP_DIM)
    cs = np.concatenate([np.cos(angc), np.sin(angc)], axis=0) * scale
    return a1, g_re, g_im, jnp.asarray(cs.astype(BF16))


def _fourier(f, tables, *, rows, rows2):
    b, s, w = f.shape
    na = int(round(np.sqrt(s)))
    assert na * na == s and na % rows == 0 and na % rows2 == 0 and rows2 % (2 * SUBLANES) == 0
    a1, g_re, g_im, cs = tables
    x4 = f.reshape(b, na, na, w)
    const2 = lambda bi, j: (0, 0)
    t = pl.pallas_call(
        functools.partial(_dft1_kernel, rows=rows),
        grid=(b, na // rows),
        in_specs=[pl.BlockSpec((None, na, rows, w), lambda bi, j: (bi, 0, j, 0)),
                  pl.BlockSpec(a1.shape, const2)],
        out_specs=pl.BlockSpec((None, na, rows, w), lambda bi, j: (bi, 0, j, 0)),
        out_shape=jax.ShapeDtypeStruct((b, na, na, w), jnp.uint32),
        scratch_shapes=[pltpu.VMEM((w // LANES, na * rows, LANES), F32)],
        compiler_params=_cparams(("parallel", "parallel")),
        name="seq_dft_stage1",
    )(x4, a1)
    y = pl.pallas_call(
        functools.partial(_dft2_kernel, rows=rows2),
        grid=(b, na // rows2),
        in_specs=[pl.BlockSpec((None, rows2, na, w), lambda bi, a: (bi, a, 0, 0))]
                 + [pl.BlockSpec((rows2,) + g_re.shape[1:], lambda bi, a: (a, 0, 0))] * 2
                 + [pl.BlockSpec(cs.shape, const2)],
        out_specs=pl.BlockSpec((None, na, rows2 // 2, w), lambda bi, a: (bi, 0, a, 0)),
        out_shape=jax.ShapeDtypeStruct((b, na, na // 2, w), jnp.uint32),
        scratch_shapes=[pltpu.VMEM((N_FOURIER_GROUPS, na * rows2 // 2, FOURIER_GROUP_DIM), jnp.uint32)],
        compiler_params=_cparams(("parallel", "parallel")),
        name="seq_dft_stage2",
    )(t, g_re, g_im, cs)
    return y.reshape(b, s // 2, w)


def _outproj_kernel(a_ref, f_ref, ga_ref, gb_ref, x_ref, wa_ref, wf_ref, wo_ref, g_ref, wr_ref,
                    xn_ref, h_ref, aff_ref, *, chunk):
    for r in range(x_ref.shape[0] // chunk):
        rs = pl.ds(r * chunk, chunk)
        a = jnp.dot(a_ref[rs, :], wa_ref[...], preferred_element_type=F32)
        four = pltpu.bitcast(f_ref[pl.ds(r * chunk // 2, chunk // 2), :], BF16)
        fo = jnp.dot(four, wf_ref[...], preferred_element_type=F32)
        merged = ga_ref[rs, :].astype(F32) * a + gb_ref[rs, :].astype(F32) * fo
        xn = x_ref[rs, :] + jnp.dot(merged.astype(BF16), wo_ref[...], preferred_element_type=F32)
        xn_ref[rs, :] = xn
        ms = jnp.mean(xn * xn, axis=-1, keepdims=True)
        h = (xn * lax.rsqrt(ms + EPS)) * g_ref[...]
        dt = h.shape[1] // LANES
        for c in range(dt):
            h_ref[pl.ds(r * chunk * dt + c, chunk, stride=dt), :] = h[:, c * LANES:(c + 1) * LANES]
        logits = _nt_dot(wr_ref[...], h.astype(BF16))
        e = jnp.exp(logits - jnp.max(logits, axis=0, keepdims=True))
        aff_ref[:, rs] = e / jnp.sum(e, axis=0, keepdims=True)


def _outproj(attn, four, ga, gb, x, wa, wf, wo, g, wr_t, *, tm, chunk):
    b, s, d = x.shape
    dt = d // LANES
    tok = lambda bi, i: (bi, i, 0)
    const = lambda bi, i: (0, 0)
    xn, h, aff = pl.pallas_call(
        functools.partial(_outproj_kernel, chunk=chunk),
        grid=(b, s // tm),
        in_specs=[pl.BlockSpec((None, tm, ATTN_WIDTH), tok),
                  pl.BlockSpec((None, tm // 2, FOURIER_WIDTH), tok),
                  pl.BlockSpec((None, tm, d), tok),
                  pl.BlockSpec((None, tm, d), tok),
                  pl.BlockSpec((None, tm, d), tok),
                  pl.BlockSpec((ATTN_WIDTH, d), const),
                  pl.BlockSpec((FOURIER_WIDTH, d), const),
                  pl.BlockSpec((d, d), const),
                  pl.BlockSpec((1, d), const),
                  pl.BlockSpec((N_EXPERTS, d), const)],
        out_specs=[pl.BlockSpec((None, tm, d), tok),
                   pl.BlockSpec((None, tm * dt, LANES), tok),
                   pl.BlockSpec((None, N_EXPERTS, tm), lambda bi, i: (bi, 0, i))],
        out_shape=[jax.ShapeDtypeStruct((b, s, d), F32),
                   jax.ShapeDtypeStruct((b, s * dt, LANES), F32),
                   jax.ShapeDtypeStruct((b, N_EXPERTS, s), F32)],
        compiler_params=_cparams(("parallel", "parallel")),
        name="outproj_router",
    )(attn, four, ga, gb, x, wa, wf, wo, g.reshape(1, d), wr_t.astype(BF16))
    return xn, h.reshape(b, s, dt, LANES), aff


def _tri(n, kind):
    r = lax.broadcasted_iota(I32, (n, n), 0)
    c = lax.broadcasted_iota(I32, (n, n), 1)
    cond = {"row_le_col": r <= c, "row_lt_col": r < c, "col_le_row": c <= r, "col_lt_row": c < r}[kind]
    return jnp.where(cond, 1.0, 0.0).astype(BF16)


def _split128(v):
    hi = jnp.floor(v * (1.0 / LANES))
    return hi, v - hi * LANES


def _routing_kernel(aff_ref, idx_ref, pos_ref, rank_ref, gt_scr, eq_scr, need_scr, *, cap):
    ne, nj, ni = aff_ref.shape
    bits = lax.bitcast_convert_type(aff_ref[...], I32)

    def bisect(it, thr):
        cand = thr | jnp.left_shift(jnp.int32(1), 30 - it)
        cnt = jnp.sum(jnp.where(bits >= cand, 1.0, 0.0), axis=(1, 2), keepdims=True)
        return jnp.where(cnt >= cap, cand, thr)

    thr = lax.fori_loop(0, 31, bisect, jnp.zeros((ne, 1, 1), I32))
    gt = jnp.where(bits > thr, 1.0, 0.0)
    gt_scr[...] = gt
    eq_scr[...] = jnp.where(bits == thr, 1.0, 0.0)
    need = cap - jnp.sum(gt, axis=(1, 2), keepdims=True)
    need_scr[...] = jnp.broadcast_to(need, (ne, SUBLANES, ni))

    ones_sq = jnp.ones((ni, ni), BF16)
    ones_row = jnp.ones((SUBLANES, ni), BF16)
    ones_rowj = jnp.ones((SUBLANES, nj), BF16)
    u_incl = _tri(ni, "row_le_col")
    uj_strict = _tri(nj, "row_lt_col")
    lj_strict = _tri(nj, "col_lt_row")
    l_incl = _tri(ni, "col_le_row")
    bdot = lambda a, b: jnp.dot(a, b, preferred_element_type=F32)

    c_lane = lax.broadcasted_iota(I32, (nj, cap), 1).astype(F32)
    c_row = lax.broadcasted_iota(I32, (SUBLANES, cap), 1).astype(F32)
    j_sub = lax.broadcasted_iota(I32, (nj, cap), 0).astype(F32)

    def slots(e, carry):
        eq = eq_scr[e]
        eqb = eq.astype(BF16)
        tie_rank = bdot(lj_strict, bdot(eqb, ones_sq).astype(BF16)) + bdot(eqb, u_incl)
        take = jnp.where(tie_rank <= need_scr[e][0:1, :], eq, 0.0)
        sel = jnp.maximum(gt_scr[e], take)
        selb = sel.astype(BF16)
        lc = bdot(selb, u_incl)
        lc_t = _nt_dot(l_incl, selb)
        n_rep = bdot(selb, ones_sq)
        s_rep = bdot(lj_strict, n_rep.astype(BF16)) + n_rep
        n_lane = _nt_dot(ones_row, selb)
        pex_lane = bdot(n_lane.astype(BF16), uj_strict)
        s_wide = jnp.concatenate([s_rep] * (cap // ni), axis=1) if cap > ni else s_rep[:, :cap]
        jc = bdot(ones_rowj, jnp.where(s_wide <= c_lane, 1.0, 0.0).astype(BF16))
        onehot = jnp.where(j_sub == jnp.broadcast_to(jc[0:1], (nj, cap)), 1.0, 0.0).astype(BF16)
        phi, plo = _split128(pex_lane)
        r = c_row - (bdot(phi.astype(BF16), onehot) * LANES + bdot(plo.astype(BF16), onehot))
        lcs = bdot(lc_t.astype(BF16), onehot)
        ic = bdot(ones_row, jnp.where(lcs <= jnp.broadcast_to(r[0:1], (ni, cap)), 1.0, 0.0).astype(BF16))
        idx_ref[e] = (jc * ni + ic).astype(I32)
        pos_ref[e] = pex_lane.astype(I32)
        rank_ref[e] = jnp.where(sel > 0.0, lc - 1.0, -1.0)
        return carry

    lax.fori_loop(0, ne, slots, 0)


def _routing(aff, *, cap):
    b, ne, s = aff.shape
    nj = s // LANES
    aff4 = aff.reshape(b, ne, nj, LANES)
    per_b = lambda bi: (bi, 0, 0, 0)
    idx, pos, rank = pl.pallas_call(
        functools.partial(_routing_kernel, cap=cap),
        grid=(b,),
        in_specs=[pl.BlockSpec((None, ne, nj, LANES), per_b)],
        out_specs=[pl.BlockSpec((None, ne, SUBLANES, cap), per_b),
                   pl.BlockSpec((None, ne, SUBLANES, nj), per_b),
                   pl.BlockSpec((None, ne, nj, LANES), per_b)],
        out_shape=[jax.ShapeDtypeStruct((b, ne, SUBLANES, cap), I32),
                   jax.ShapeDtypeStruct((b, ne, SUBLANES, nj), I32),
                   jax.ShapeDtypeStruct((b, ne, nj, LANES), F32)],
        scratch_shapes=[pltpu.VMEM((ne, nj, LANES), F32)] * 2 + [pltpu.VMEM((ne, SUBLANES, LANES), F32)],
        compiler_params=_cparams(("parallel",)),
        name="expert_choice_routing",
    )(aff4)
    return idx[:, :, 0, :], pos[:, :, 0, :], rank


def _ffn_kernel(idx_cur, idx_nxt, h_hbm, wg_ref, wu_ref, wd_ref, y_ref, xs, xsb, gsem,
                *, cap, n_pairs, n_ftiles, per_step, m_chunk):
    p = pl.program_id(0)
    j = pl.program_id(1)
    n_slots = n_ftiles * per_step
    pitch = xs.shape[0] // n_slots
    dt = pitch - 1
    last_pair = n_pairs - 1
    b_cur = p // N_EXPERTS
    b_nxt = jnp.minimum(p + 1, last_pair) // N_EXPERTS

    def gather(ref, bq, c):
        cc = jnp.minimum(c, cap - 1)
        tok = ref[cc >> LANE_BITS, cc & (LANES - 1)]
        dst = xs.at[pl.ds(c * pitch, dt)]
        return pltpu.make_async_copy(h_hbm.at[bq, tok], dst, gsem)

    def wait_gathers():
        tiles = xs.at[pl.ds(0, n_slots * dt)]
        pltpu.make_async_copy(tiles, tiles, gsem).wait()

    @pl.when((p == 0) & (j == 0))
    def _():
        def start(c, carry):
            gather(idx_cur, b_cur, c).start()
            return carry

        lax.fori_loop(0, n_slots, start, 0)

    n_groups = cap // m_chunk
    per_group = per_step // n_groups

    def ff_tile(first):
        wg = wg_ref[...].astype(BF16)
        wu = wu_ref[...].astype(BF16)
        wd = wd_ref[...].astype(BF16)
        for m in range(n_groups):
            for t in range(per_group):
                gather(idx_nxt, b_nxt, j * per_step + (m * per_group + t)).start()
            rows = pl.ds(m * m_chunk, m_chunk)
            x = xsb[rows, :]
            hg = jnp.dot(x, wg, preferred_element_type=F32)
            hu = jnp.dot(x, wu, preferred_element_type=F32)
            act = (hg * _sigmoid(hg)) * hu
            part = jnp.dot(act.astype(BF16), wd, preferred_element_type=F32)
            if first:
                y_ref[rows, :] = part
            else:
                y_ref[rows, :] += part

    @pl.when(j == 0)
    def _():
        wait_gathers()
        for c in range(dt):
            xsb[:, c * LANES:(c + 1) * LANES] = xs[pl.ds(c, cap, stride=pitch), :].astype(BF16)
        ff_tile(True)

    @pl.when(j > 0)
    def _():
        ff_tile(False)

    @pl.when((p == last_pair) & (j == n_ftiles - 1))
    def _():
        wait_gathers()


def _expert_ffn(h, idx, wg, wu, wd, layer, *, cap, tf, m_chunk):
    b, s, dt, _ = h.shape
    d = dt * LANES
    ne, dff = wg.shape[1], wg.shape[3]
    n_ftiles = dff // tf
    n_pairs = b * ne
    n_groups = cap // m_chunk
    per_step = -(-cap // (n_ftiles * n_groups)) * n_groups
    n_slots = n_ftiles * per_step
    tab = idx.reshape(n_pairs, cap // LANES, LANES)
    tab_spec = lambda f: pl.BlockSpec((None, cap // LANES, LANES), lambda p, j: (f(p), 0, 0),
                                      memory_space=pltpu.SMEM)
    return pl.pallas_call(
        functools.partial(_ffn_kernel, cap=cap, n_pairs=n_pairs, n_ftiles=n_ftiles, per_step=per_step,
                          m_chunk=m_chunk),
        grid=(n_pairs, n_ftiles),
        in_specs=[tab_spec(lambda p: p),
                  tab_spec(lambda p: jnp.minimum(p + 1, n_pairs - 1)),
                  pl.BlockSpec(memory_space=pl.ANY),
                  pl.BlockSpec((None, None, d, tf), lambda p, j: (layer, p % ne, 0, j)),
                  pl.BlockSpec((None, None, d, tf), lambda p, j: (layer, p % ne, 0, j)),
                  pl.BlockSpec((None, None, tf, d), lambda p, j: (layer, p % ne, j, 0))],
        out_specs=pl.BlockSpec((None, None, cap, d), lambda p, j: (p // ne, p % ne, 0, 0)),
        out_shape=jax.ShapeDtypeStruct((b, ne, cap, d), F32),
        scratch_shapes=[pltpu.VMEM((n_slots * (dt + 1), LANES), F32), pltpu.VMEM((cap, d), BF16),
                        pltpu.SemaphoreType.DMA(())],
        compiler_params=_cparams(("arbitrary", "arbitrary")),
        name="expert_swiglu",
    )(tab, tab, h, wg, wu, wd)


def _combine_kernel(pos_ref, rank_ref, aff_ref, x_ref, g_ref, y_hbm, o_ref, ybuf, acc, sem,
                    *, cap, nj, n_tiles, final_norm):
    b = pl.program_id(0)
    j = pl.program_id(1)
    t = b * nj + j
    ne = N_EXPERTS
    ring = COMBINE_AHEAD + 1
    assert ring & (ring - 1) == 0
    in_ring = lambda k: k & (ring - 1)

    def nominal(bq, e, jq, r):
        return ((pos_ref[bq * ne + e, jq] >> 3) << 3) + r * EWIN

    def window(bq, e, jq, r):
        return pl.multiple_of(jnp.minimum(nominal(bq, e, jq, r), cap - EWIN), SUBLANES)

    def start_fetch(ahead, r, slot):
        jq = j + ahead
        wrap = (jq >= nj).astype(I32)
        bq = b + wrap
        jq = jq - wrap * nj
        past = bq * nj + jq >= n_tiles
        bq = jnp.where(past, n_tiles // nj - 1, bq)
        jq = jnp.where(past, nj - 1, jq)
        for e in range(ne):
            pltpu.make_async_copy(y_hbm.at[bq, e, pl.ds(window(bq, e, jq, r), EWIN)],
                                  ybuf.at[slot, pl.ds(e * EWIN, EWIN)], sem.at[slot]).start()

    def wait_fetch(slot):
        pltpu.make_async_copy(ybuf.at[slot], ybuf.at[slot], sem.at[slot]).wait()

    rk = rank_ref[...]
    af = aff_ref[...]
    row = lax.broadcasted_iota(I32, (EWIN, LANES), 0).astype(F32)

    def add_round(r, slot):
        parts = []
        for e in range(ne):
            p0 = pos_ref[b * ne + e, j]
            local = row + (window(b, e, j, r) - p0).astype(F32)
            want = jnp.where(local >= jnp.maximum(nominal(b, e, j, r) - p0, 0).astype(F32), local, -2.0)
            parts.append(jnp.where(rk[e:e + 1, :] == want, af[e:e + 1, :], 0.0))
        gate_t = jnp.concatenate(parts, axis=0)
        onehot = jnp.where(gate_t != 0.0, 1.0, 0.0).T.astype(BF16)
        row_gate = jnp.sum(gate_t, axis=1, keepdims=True)
        scaled = jnp.where(row_gate != 0.0, ybuf[slot] * row_gate, 0.0)
        hi = scaled.astype(BF16)
        lo = (scaled - hi.astype(F32)).astype(BF16)
        return jnp.dot(onehot, hi, preferred_element_type=F32) + jnp.dot(onehot, lo, preferred_element_type=F32)

    @pl.when(t == 0)
    def _():
        for ahead in range(COMBINE_AHEAD):
            start_fetch(ahead, 0, ahead)

    start_fetch(COMBINE_AHEAD, 0, in_ring(t + COMBINE_AHEAD))
    slot0 = in_ring(t)
    wait_fetch(slot0)
    acc[...] = add_round(0, slot0)

    span = jnp.int32(1)
    for e in range(ne):
        span = jnp.maximum(span, pos_ref[b * ne + e, j + 1] - nominal(b, e, j, 0))
    n_rounds = (span + EWIN - 1) // EWIN

    def extra_round(r, carry):
        start_fetch(0, r, ring)
        wait_fetch(ring)
        acc[...] += add_round(r, ring)
        return carry

    lax.fori_loop(1, n_rounds, extra_round, 0)

    @pl.when(t == n_tiles - 1)
    def _():
        for ahead in range(1, COMBINE_AHEAD + 1):
            wait_fetch(in_ring(t + ahead))

    y = x_ref[...] + acc[...]
    if final_norm:
        ms = jnp.mean(y * y, axis=-1, keepdims=True)
        y = (y * lax.rsqrt(ms + EPS)) * g_ref[...]
    o_ref[...] = y


def _combine(x, y, pos, rank, aff, g_final, *, cap, final_norm):
    b, s, d = x.shape
    nj = s // LANES
    ne = N_EXPERTS
    pos_tab = jnp.concatenate([pos, jnp.full((b, ne, 1), cap, I32)], axis=2).reshape(b * ne, nj + 1)
    rank_t = jnp.transpose(rank, (0, 2, 1, 3))
    return pl.pallas_call(
        functools.partial(_combine_kernel, cap=cap, nj=nj, n_tiles=b * nj, final_norm=final_norm),
        grid_spec=pltpu.PrefetchScalarGridSpec(
            num_scalar_prefetch=1,
            grid=(b, nj),
            in_specs=[pl.BlockSpec((None, None, ne, LANES), lambda bi, j, tab: (bi, j, 0, 0)),
                      pl.BlockSpec((None, ne, LANES), lambda bi, j, tab: (bi, 0, j)),
                      pl.BlockSpec((None, LANES, d), lambda bi, j, tab: (bi, j, 0)),
                      pl.BlockSpec((1, d), lambda bi, j, tab: (0, 0)),
                      pl.BlockSpec(memory_space=pl.ANY)],
            out_specs=pl.BlockSpec((None, LANES, d), lambda bi, j, tab: (bi, j, 0)),
            scratch_shapes=[pltpu.VMEM((COMBINE_AHEAD + 2, ne * EWIN, d), F32), pltpu.VMEM((LANES, d), F32),
                            pltpu.SemaphoreType.DMA((COMBINE_AHEAD + 2,))]),
        out_shape=jax.ShapeDtypeStruct((b, s, d), F32),
        compiler_params=_cparams(("arbitrary", "arbitrary")),
        name="moe_combine",
    )(pos_tab, rank_t, aff, x, g_final.reshape(1, d), y)


def kernel(x, rel_bias, g_mix, w_in, attn_sink, w_attn_proj, w_fourier_proj, w_out, g_ffn, w_router,
           w_exp_gate, w_exp_up, w_exp_down, g_final):
    b, s, d = x.shape
    depth = g_mix.shape[0]
    cap = CAPACITY_FACTOR * s // N_EXPERTS
    bias = _bias_table(rel_bias)
    tables = _fourier_tables(int(round(np.sqrt(s))))
    for l in range(depth):
        q, k, v, vs, f, ga, gb = _inproj(x.reshape(b * s, d), g_mix[l], w_in[l].astype(BF16), tm=1024, chunk=256)
        shp = lambda a: a.reshape(b, s, a.shape[-1])
        attn = _attention(shp(q), shp(k), shp(v), shp(vs), attn_sink[l], bias, blocks_per_step=8)
        four = _fourier(shp(f), tables, rows=8, rows2=16)
        xn, h, aff = _outproj(attn, four, shp(ga), shp(gb), x,
                              w_attn_proj[l].astype(BF16), w_fourier_proj[l].astype(BF16),
                              w_out[l].astype(BF16), g_ffn[l], w_router[l].T, tm=1024, chunk=256)
        idx, pos, rank = _routing(aff, cap=cap)
        y = _expert_ffn(h, idx, w_exp_gate, w_exp_up, w_exp_down, l, cap=cap, tf=256, m_chunk=min(cap, 512))
        x = _combine(xn, y, pos, rank, aff, g_final, cap=cap, final_norm=(l == depth - 1))
    return x
```

```python
import functools

import numpy as np
import jax
import jax.numpy as jnp
from jax import lax
from jax.experimental import pallas as pl
from jax.experimental.pallas import tpu as pltpu

F32 = jnp.float32
BF16 = jnp.bfloat16
I32 = jnp.int32

N_HEADS = 8
N_KV_HEADS = 2
HEAD_DIM = 64
ATTN_WIDTH = N_HEADS * HEAD_DIM
KV_WIDTH = N_KV_HEADS * HEAD_DIM
WINDOW = 128
BLOCK = 128
N_BUCKETS = 32
MAX_DISTANCE = 128
N_FOURIER_GROUPS = 4
FOURIER_GROUP_DIM = 128
FOURIER_WIDTH = N_FOURIER_GROUPS * FOURIER_GROUP_DIM
N_EXPERTS = 16
CAPACITY_FACTOR = 2
EPS = 1e-6
NEG_INF = -1e30

LANES = 128
LANE_BITS = 7
SUBLANES = 8
VMEM_LIMIT = 56 * 1024 * 1024
EWIN = 32
COMBINE_AHEAD = 3


def _cparams(sem):
    return pltpu.CompilerParams(dimension_semantics=sem, vmem_limit_bytes=VMEM_LIMIT)


def _nt_dot(a, b, **kw):
    return lax.dot_general(a, b, (((1,), (1,)), ((), ())), preferred_element_type=F32, **kw)


def _sigmoid(x):
    return 1.0 / (1.0 + jnp.exp(-x))


def _bf16_bits(y):
    return lax.bitcast_convert_type(y.astype(BF16).astype(F32), jnp.uint32)


def _t5_bucket(rel):
    half = N_BUCKETS // 2
    max_exact = half // 2
    ret = (rel > 0).astype(jnp.int32) * half
    n = jnp.abs(rel)
    nf = jnp.maximum(n, 1).astype(jnp.float32)
    large = max_exact + (jnp.log(nf / max_exact) / np.float32(np.log(MAX_DISTANCE / max_exact))
                         * (half - max_exact)).astype(jnp.int32)
    large = jnp.minimum(large, half - 1)
    return ret + jnp.where(n < max_exact, n, large)


def _bias_kernel(relb_ref, bucket_ref, o_ref):
    bk = bucket_ref[...]
    col = lax.broadcasted_iota(I32, bk.shape, 1)
    for h in range(N_HEADS):
        acc = jnp.full(bk.shape, NEG_INF, F32)
        for b in range(N_BUCKETS):
            acc = jnp.where(bk == b, relb_ref[b, h], acc)
        o_ref[0, h] = jnp.where(col >= BLOCK, acc, NEG_INF)
        o_ref[1, h] = acc
        o_ref[2, h] = jnp.where(col < 2 * BLOCK, acc, NEG_INF)


def _bias_table(rel_bias):
    q_loc = jnp.arange(BLOCK)
    k_loc = jnp.arange(3 * BLOCK) - BLOCK
    rel = k_loc[None, :] - q_loc[:, None]
    bucket = jnp.where(jnp.abs(rel) <= WINDOW, _t5_bucket(rel), -1).astype(I32)
    table = pl.pallas_call(
        _bias_kernel,
        out_shape=jax.ShapeDtypeStruct((3, N_HEADS, BLOCK, 3 * BLOCK), F32),
        in_specs=[pl.BlockSpec(memory_space=pltpu.SMEM),
                  pl.BlockSpec(memory_space=pltpu.VMEM)],
        out_specs=pl.BlockSpec(memory_space=pltpu.VMEM),
        name="rel_bias_table",
    )(rel_bias.astype(F32), bucket)
    return table.reshape(3, N_HEADS * BLOCK, 3 * BLOCK)


def _inproj_kernel(x_ref, g_ref, w_ref, q_ref, k_ref, v_ref, vs_ref, f_ref, ga_ref, gb_ref,
                   *, d_model, chunk, na):
    for r in range(x_ref.shape[0] // chunk):
        rs = pl.ds(r * chunk, chunk)
        x = x_ref[rs, :]
        ms = jnp.mean(x * x, axis=-1, keepdims=True)
        h = (x * lax.rsqrt(ms + EPS)) * g_ref[...]
        z = jnp.dot(h.astype(BF16), w_ref[...], preferred_element_type=F32)
        o = 0
        q_ref[rs, :] = (z[:, o:o + ATTN_WIDTH] * (HEAD_DIM ** -0.5)).astype(BF16)
        o += ATTN_WIDTH
        k_ref[rs, :] = z[:, o:o + KV_WIDTH].astype(BF16)
        o += KV_WIDTH
        v_ref[rs, :] = z[:, o:o + KV_WIDTH].astype(BF16)
        half = KV_WIDTH // 2
        vs_ref[rs, :] = jnp.concatenate([z[:, o + half:o + KV_WIDTH], z[:, o:o + half]], axis=1).astype(BF16)
        o += KV_WIDTH
        for m in range(chunk // (2 * na)):
            lo = z[2 * m * na:(2 * m + 1) * na, o:o + FOURIER_WIDTH]
            hi = z[(2 * m + 1) * na:(2 * m + 2) * na, o:o + FOURIER_WIDTH]
            f_ref[pl.ds(r * chunk // 2 + m * na, na), :] = (_bf16_bits(lo) >> 16) | _bf16_bits(hi)
        o += FOURIER_WIDTH
        ga_ref[rs, :] = _sigmoid(z[:, o:o + d_model]).astype(BF16)
        o += d_model
        gb_ref[rs, :] = _sigmoid(z[:, o:o + d_model]).astype(BF16)


def _inproj(x2, g, w_bf16, *, tm, chunk, na):
    n, d = x2.shape
    in_w = w_bf16.shape[1]
    assert chunk % (2 * na) == 0 and na % SUBLANES == 0
    widths = (ATTN_WIDTH, KV_WIDTH, KV_WIDTH, KV_WIDTH, FOURIER_WIDTH, d, d)
    dtypes = (BF16, BF16, BF16, BF16, jnp.uint32, BF16, BF16)
    halved = (False, False, False, False, True, False, False)
    row = lambda i: (i, 0)
    return pl.pallas_call(
        functools.partial(_inproj_kernel, d_model=d, chunk=chunk, na=na),
        grid=(n // tm,),
        in_specs=[pl.BlockSpec((tm, d), row),
                  pl.BlockSpec((1, d), lambda i: (0, 0)),
                  pl.BlockSpec((d, in_w), lambda i: (0, 0))],
        out_specs=[pl.BlockSpec((tm // 2 if hv else tm, w), row) for w, hv in zip(widths, halved)],
        out_shape=[jax.ShapeDtypeStruct((n // 2 if hv else n, w), dt)
                   for w, dt, hv in zip(widths, dtypes, halved)],
        compiler_params=_cparams(("parallel",)),
        name="inproj",
    )(x2, g.reshape(1, d), w_bf16)


def _attn_kernel(sink_ref, q_ref, kp_ref, ko_ref, kn_ref, vp_ref, vo_ref, vn_ref, wp_ref, wo_ref, wn_ref,
                 bias_ref, o_ref, *, n_blocks, blocks_per_step):
    i = pl.program_id(1)
    kcat = jnp.concatenate([kp_ref[...], ko_ref[...], kn_ref[...]], axis=0)
    vcat = jnp.concatenate([vp_ref[...], vo_ref[...], vn_ref[...]], axis=0)
    wcat = jnp.concatenate([wp_ref[...], wo_ref[...], wn_ref[...]], axis=0)
    group = N_HEADS // N_KV_HEADS
    head = lambda a, h: a[:, h * HEAD_DIM:(h + 1) * HEAD_DIM]
    sink = jnp.concatenate([jnp.full((BLOCK, LANES), sink_ref[h], F32) for h in range(N_HEADS)], axis=0)
    low_v = lax.broadcasted_iota(I32, vcat.shape, 1) < HEAD_DIM
    low_o = lax.broadcasted_iota(I32, (BLOCK, LANES), 1) < HEAD_DIM
    vx = {(kv, half): jnp.where(low_v if half == 0 else jnp.logical_not(low_v),
                                vcat if kv == half else wcat, jnp.ones_like(vcat))
          for kv in range(N_KV_HEADS) for half in range(2)}
    for t in range(blocks_per_step):
        blk = i * blocks_per_step + t
        kind = jnp.where(blk == 0, 0, jnp.where(blk == n_blocks - 1, 2, 1))
        kw = kcat[t * BLOCK:(t + 3) * BLOCK]
        qb = q_ref[t * BLOCK:(t + 1) * BLOCK, :]
        s = jnp.concatenate([_nt_dot(head(qb, h), head(kw, h // group)) for h in range(N_HEADS)], axis=0)
        s = s + bias_ref[kind]
        m = jnp.maximum(jnp.broadcast_to(jnp.max(s, axis=-1, keepdims=True), sink.shape), sink)
        p = jnp.exp(s - jnp.concatenate([m] * 3, axis=1)).astype(BF16)
        sink_term = jnp.exp(sink - m)
        for pair in range(N_HEADS // 2):
            halves = []
            for half in range(2):
                h = 2 * pair + half
                rows = slice(h * BLOCK, (h + 1) * BLOCK)
                out = jnp.dot(p[rows], vx[h // group, half][t * BLOCK:(t + 3) * BLOCK],
                              preferred_element_type=F32)
                denom = pltpu.roll(out, HEAD_DIM, axis=1) + sink_term[rows]
                halves.append(out * (1.0 / denom))
            tile = jnp.where(low_o, halves[0], halves[1])
            o_ref[t * BLOCK:(t + 1) * BLOCK, pair * LANES:(pair + 1) * LANES] = tile.astype(o_ref.dtype)


def _attention(q, k, v, v_swapped, sink, bias, *, blocks_per_step):
    b, s, _ = q.shape
    assert N_KV_HEADS == 2 and KV_WIDTH == LANES and N_HEADS % 2 == 0
    n_blocks = s // BLOCK
    tq = blocks_per_step * BLOCK
    prev = lambda bi, i: (bi, jnp.maximum(i * blocks_per_step - 1, 0), 0)
    own = lambda bi, i: (bi, i, 0)
    nxt = lambda bi, i: (bi, jnp.minimum((i + 1) * blocks_per_step, n_blocks - 1), 0)
    kv_specs = [pl.BlockSpec((None, BLOCK, KV_WIDTH), prev),
                pl.BlockSpec((None, tq, KV_WIDTH), own),
                pl.BlockSpec((None, BLOCK, KV_WIDTH), nxt)]
    return pl.pallas_call(
        functools.partial(_attn_kernel, n_blocks=n_blocks, blocks_per_step=blocks_per_step),
        grid=(b, s // tq),
        in_specs=[pl.BlockSpec(memory_space=pltpu.SMEM),
                  pl.BlockSpec((None, tq, ATTN_WIDTH), own)] + kv_specs * 3
                 + [pl.BlockSpec((3, N_HEADS * BLOCK, 3 * BLOCK), lambda bi, i: (0, 0, 0))],
        out_specs=pl.BlockSpec((None, tq, ATTN_WIDTH), own),
        out_shape=jax.ShapeDtypeStruct((b, s, ATTN_WIDTH), BF16),
        compiler_params=_cparams(("parallel", "parallel")),
        name="window_attention",
    )(sink.astype(F32), q, k, k, k, v, v, v, v_swapped, v_swapped, v_swapped, bias)


def _dft1_kernel(x_ref, a_ref, t_ref, x2, *, rows):
    a = a_ref[...]
    na = a.shape[0] // 2
    n_slabs = x2.shape[0]
    lanes = lambda c: slice(c * LANES, (c + 1) * LANES)
    for c in range(n_slabs):
        x2[c] = x_ref[:, :, lanes(c)].reshape(x2.shape[1:])
    for jj in range(rows):
        strided = pl.ds(jj, na // 2, stride=rows)
        xj = pltpu.bitcast(jnp.concatenate([x2[c, strided, :] for c in range(n_slabs)], axis=1), BF16)
        t = jnp.dot(a, xj, preferred_element_type=F32)
        t_ref[:, jj, :] = pltpu.bitcast(t.astype(BF16), jnp.uint32)


def _dft2_kernel(t_ref, gr_ref, gi_ref, cs_ref, o_ref, o2, *, rows):
    cs = cs_ref[...]
    na = o_ref.shape[0]
    pairs = rows // 2
    for aa in range(rows):
        t = pltpu.bitcast(t_ref[aa], BF16)
        zr = jnp.dot(gr_ref[aa], t, preferred_element_type=F32)
        zi = jnp.dot(gi_ref[aa], t, preferred_element_type=F32)
        for g in range(N_FOURIER_GROUPS):
            sl = slice(g * FOURIER_GROUP_DIM, (g + 1) * FOURIER_GROUP_DIM)
            z = jnp.concatenate([zr[:, sl], zi[:, sl]], axis=1).astype(BF16)
            y = jnp.dot(z, cs, preferred_element_type=F32)
            rows_q = pl.ds(aa // 2, na, stride=pairs)
            if aa % 2 == 0:
                o2[g, rows_q, :] = _bf16_bits(y) >> 16
            else:
                o2[g, rows_q, :] = o2[g, rows_q, :] | _bf16_bits(y)
    for g in range(N_FOURIER_GROUPS):
        sl = slice(g * FOURIER_GROUP_DIM, (g + 1) * FOURIER_GROUP_DIM)
        o_ref[:, :, sl] = o2[g].reshape(na, pairs, FOURIER_GROUP_DIM)


def _fourier_tables(na):
    def interleave(even, odd, axis):
        both = np.stack([even, odd], axis=axis + 1)
        shape = list(even.shape)
        shape[axis] *= 2
        return jnp.asarray(both.reshape(shape).astype(BF16))

    n = na * na
    k = np.arange(na, dtype=np.float64)
    ang = 2.0 * np.pi * np.outer(k, k) / na
    a1 = interleave(np.cos(ang), -np.sin(ang), 0)
    ka = np.arange(na, dtype=np.float64)[:, None, None]
    kb = np.arange(na, dtype=np.float64)[None, :, None]
    nl = np.arange(na, dtype=np.float64)[None, None, :]
    th = 2.0 * np.pi * ((nl * (ka + na * kb)) % n) / n
    gr, gi = np.cos(th), -np.sin(th)
    g_re = interleave(gr, -gi, 2)
    g_im = interleave(gi, gr, 2)
    kc = np.arange(FOURIER_GROUP_DIM, dtype=np.float64)
    angc = 2.0 * np.pi * np.outer(kc, kc) / FOURIER_GROUP_DIM
    scale = 1.0 / np.sqrt(float(n) * FOURIER_GROUP_DIM)
    cs = np.concatenate([np.cos(angc), np.sin(angc)], axis=0) * scale
    return a1, g_re, g_im, jnp.asarray(cs.astype(BF16))


def _fourier(f, tables, *, rows, rows2):
    b, s, w = f.shape[0], 2 * f.shape[1], f.shape[2]
    na = int(round(np.sqrt(s)))
    assert na * na == s and na % rows == 0 and na % rows2 == 0 and rows2 % (2 * SUBLANES) == 0
    a1, g_re, g_im, cs = tables
    x4 = f.reshape(b, na // 2, na, w)
    const2 = lambda bi, j: (0, 0)
    t = pl.pallas_call(
        functools.partial(_dft1_kernel, rows=rows),
        grid=(b, na // rows),
        in_specs=[pl.BlockSpec((None, na // 2, rows, w), lambda bi, j: (bi, 0, j, 0)),
                  pl.BlockSpec(a1.shape, const2)],
        out_specs=pl.BlockSpec((None, na, rows, w), lambda bi, j: (bi, 0, j, 0)),
        out_shape=jax.ShapeDtypeStruct((b, na, na, w), jnp.uint32),
        scratch_shapes=[pltpu.VMEM((w // LANES, na // 2 * rows, LANES), jnp.uint32)],
        compiler_params=_cparams(("parallel", "parallel")),
        name="seq_dft_stage1",
    )(x4, a1)
    y = pl.pallas_call(
        functools.partial(_dft2_kernel, rows=rows2),
        grid=(b, na // rows2),
        in_specs=[pl.BlockSpec((None, rows2, na, w), lambda bi, a: (bi, a, 0, 0))]
                 + [pl.BlockSpec((rows2,) + g_re.shape[1:], lambda bi, a: (a, 0, 0))] * 2
                 + [pl.BlockSpec(cs.shape, const2)],
        out_specs=pl.BlockSpec((None, na, rows2 // 2, w), lambda bi, a: (bi, 0, a, 0)),
        out_shape=jax.ShapeDtypeStruct((b, na, na // 2, w), jnp.uint32),
        scratch_shapes=[pltpu.VMEM((N_FOURIER_GROUPS, na * rows2 // 2, FOURIER_GROUP_DIM), jnp.uint32)],
        compiler_params=_cparams(("parallel", "parallel")),
        name="seq_dft_stage2",
    )(t, g_re, g_im, cs)
    return y.reshape(b, s // 2, w)


def _outproj_kernel(a_ref, f_ref, ga_ref, gb_ref, x_ref, wa_ref, wf_ref, wo_ref, g_ref, wr_ref,
                    xn_ref, h_ref, aff_ref, *, chunk):
    for r in range(x_ref.shape[0] // chunk):
        rs = pl.ds(r * chunk, chunk)
        a = jnp.dot(a_ref[rs, :], wa_ref[...], preferred_element_type=F32)
        four = pltpu.bitcast(f_ref[pl.ds(r * chunk // 2, chunk // 2), :], BF16)
        fo = jnp.dot(four, wf_ref[...], preferred_element_type=F32)
        merged = ga_ref[rs, :].astype(F32) * a + gb_ref[rs, :].astype(F32) * fo
        xn = x_ref[rs, :] + jnp.dot(merged.astype(BF16), wo_ref[...], preferred_element_type=F32)
        xn_ref[rs, :] = xn
        ms = jnp.mean(xn * xn, axis=-1, keepdims=True)
        h = (xn * lax.rsqrt(ms + EPS)) * g_ref[...]
        dt = h.shape[1] // LANES
        for c in range(dt):
            h_ref[pl.ds(r * chunk * dt + c, chunk, stride=dt), :] = h[:, c * LANES:(c + 1) * LANES]
        logits = _nt_dot(wr_ref[...], h.astype(BF16))
        e = jnp.exp(logits - jnp.max(logits, axis=0, keepdims=True))
        aff_ref[:, rs] = e / jnp.sum(e, axis=0, keepdims=True)


def _outproj(attn, four, ga, gb, x, wa, wf, wo, g, wr_t, *, tm, chunk):
    b, s, d = x.shape
    dt = d // LANES
    tok = lambda bi, i: (bi, i, 0)
    const = lambda bi, i: (0, 0)
    xn, h, aff = pl.pallas_call(
        functools.partial(_outproj_kernel, chunk=chunk),
        grid=(b, s // tm),
        in_specs=[pl.BlockSpec((None, tm, ATTN_WIDTH), tok),
                  pl.BlockSpec((None, tm // 2, FOURIER_WIDTH), tok),
                  pl.BlockSpec((None, tm, d), tok),
                  pl.BlockSpec((None, tm, d), tok),
                  pl.BlockSpec((None, tm, d), tok),
                  pl.BlockSpec((ATTN_WIDTH, d), const),
                  pl.BlockSpec((FOURIER_WIDTH, d), const),
                  pl.BlockSpec((d, d), const),
                  pl.BlockSpec((1, d), const),
                  pl.BlockSpec((N_EXPERTS, d), const)],
        out_specs=[pl.BlockSpec((None, tm, d), tok),
                   pl.BlockSpec((None, tm * dt, LANES), tok),
                   pl.BlockSpec((None, N_EXPERTS, tm), lambda bi, i: (bi, 0, i))],
        out_shape=[jax.ShapeDtypeStruct((b, s, d), F32),
                   jax.ShapeDtypeStruct((b, s * dt, LANES), F32),
                   jax.ShapeDtypeStruct((b, N_EXPERTS, s), F32)],
        compiler_params=_cparams(("parallel", "parallel")),
        name="outproj_router",
    )(attn, four, ga, gb, x, wa, wf, wo, g.reshape(1, d), wr_t.astype(BF16))
    return xn, h.reshape(b, s, dt, LANES), aff


def _tri(n, kind):
    r = lax.broadcasted_iota(I32, (n, n), 0)
    c = lax.broadcasted_iota(I32, (n, n), 1)
    cond = {"row_le_col": r <= c, "row_lt_col": r < c, "col_le_row": c <= r, "col_lt_row": c < r}[kind]
    return jnp.where(cond, 1.0, 0.0).astype(BF16)


def _split128(v):
    hi = jnp.floor(v * (1.0 / LANES))
    return hi, v - hi * LANES


def _routing_kernel(aff_ref, idx_ref, pos_ref, rank_ref, gt_scr, eq_scr, need_scr, *, cap):
    ne, nj, ni = aff_ref.shape
    bits = lax.bitcast_convert_type(aff_ref[...], I32)

    def bisect(it, thr):
        cand = thr | jnp.left_shift(jnp.int32(1), 30 - it)
        cnt = jnp.sum(jnp.where(bits >= cand, 1.0, 0.0), axis=(1, 2), keepdims=True)
        return jnp.where(cnt >= cap, cand, thr)

    thr = lax.fori_loop(0, 31, bisect, jnp.zeros((ne, 1, 1), I32))
    gt = jnp.where(bits > thr, 1.0, 0.0)
    gt_scr[...] = gt
    eq_scr[...] = jnp.where(bits == thr, 1.0, 0.0)
    need = cap - jnp.sum(gt, axis=(1, 2), keepdims=True)
    need_scr[...] = jnp.broadcast_to(need, (ne, SUBLANES, ni))

    ones_sq = jnp.ones((ni, ni), BF16)
    ones_row = jnp.ones((SUBLANES, ni), BF16)
    ones_rowj = jnp.ones((SUBLANES, nj), BF16)
    u_incl = _tri(ni, "row_le_col")
    uj_strict = _tri(nj, "row_lt_col")
    lj_strict = _tri(nj, "col_lt_row")
    l_incl = _tri(ni, "col_le_row")
    bdot = lambda a, b: jnp.dot(a, b, preferred_element_type=F32)

    c_lane = lax.broadcasted_iota(I32, (nj, cap), 1).astype(F32)
    c_row = lax.broadcasted_iota(I32, (SUBLANES, cap), 1).astype(F32)
    j_sub = lax.broadcasted_iota(I32, (nj, cap), 0).astype(F32)

    def slots(e, carry):
        eq = eq_scr[e]
        eqb = eq.astype(BF16)
        tie_rank = bdot(lj_strict, bdot(eqb, ones_sq).astype(BF16)) + bdot(eqb, u_incl)
        take = jnp.where(tie_rank <= need_scr[e][0:1, :], eq, 0.0)
        sel = jnp.maximum(gt_scr[e], take)
        selb = sel.astype(BF16)
        lc = bdot(selb, u_incl)
        lc_t = _nt_dot(l_incl, selb)
        n_rep = bdot(selb, ones_sq)
        s_rep = bdot(lj_strict, n_rep.astype(BF16)) + n_rep
        n_lane = _nt_dot(ones_row, selb)
        pex_lane = bdot(n_lane.astype(BF16), uj_strict)
        s_wide = jnp.concatenate([s_rep] * (cap // ni), axis=1) if cap > ni else s_rep[:, :cap]
        jc = bdot(ones_rowj, jnp.where(s_wide <= c_lane, 1.0, 0.0).astype(BF16))
        onehot = jnp.where(j_sub == jnp.broadcast_to(jc[0:1], (nj, cap)), 1.0, 0.0).astype(BF16)
        phi, plo = _split128(pex_lane)
        r = c_row - (bdot(phi.astype(BF16), onehot) * LANES + bdot(plo.astype(BF16), onehot))
        lcs = bdot(lc_t.astype(BF16), onehot)
        ic = bdot(ones_row, jnp.where(lcs <= jnp.broadcast_to(r[0:1], (ni, cap)), 1.0, 0.0).astype(BF16))
        idx_ref[e] = (jc * ni + ic).astype(I32)
        pos_ref[e] = pex_lane.astype(I32)
        rank_ref[e] = jnp.where(sel > 0.0, lc - 1.0, -1.0)
        return carry

    lax.fori_loop(0, ne, slots, 0)


def _routing(aff, *, cap):
    b, ne, s = aff.shape
    nj = s // LANES
    aff4 = aff.reshape(b, ne, nj, LANES)
    per_b = lambda bi: (bi, 0, 0, 0)
    idx, pos, rank = pl.pallas_call(
        functools.partial(_routing_kernel, cap=cap),
        grid=(b,),
        in_specs=[pl.BlockSpec((None, ne, nj, LANES), per_b)],
        out_specs=[pl.BlockSpec((None, ne, SUBLANES, cap), per_b),
                   pl.BlockSpec((None, ne, SUBLANES, nj), per_b),
                   pl.BlockSpec((None, ne, nj, LANES), per_b)],
        out_shape=[jax.ShapeDtypeStruct((b, ne, SUBLANES, cap), I32),
                   jax.ShapeDtypeStruct((b, ne, SUBLANES, nj), I32),
                   jax.ShapeDtypeStruct((b, ne, nj, LANES), F32)],
        scratch_shapes=[pltpu.VMEM((ne, nj, LANES), F32)] * 2 + [pltpu.VMEM((ne, SUBLANES, LANES), F32)],
        compiler_params=_cparams(("parallel",)),
        name="expert_choice_routing",
    )(aff4)
    return idx[:, :, 0, :], pos[:, :, 0, :], rank


def _ffn_kernel(idx_cur, idx_nxt, h_hbm, wg_ref, wu_ref, wd_ref, y_ref, xs, xsb, gsem,
                *, cap, n_pairs, n_ftiles, per_step, m_chunk):
    p = pl.program_id(0)
    j = pl.program_id(1)
    n_slots = n_ftiles * per_step
    pitch = xs.shape[0] // n_slots
    dt = pitch - 1
    last_pair = n_pairs - 1
    b_cur = p // N_EXPERTS
    b_nxt = jnp.minimum(p + 1, last_pair) // N_EXPERTS

    def gather(ref, bq, c):
        cc = jnp.minimum(c, cap - 1)
        tok = ref[cc >> LANE_BITS, cc & (LANES - 1)]
        dst = xs.at[pl.ds(c * pitch, dt)]
        return pltpu.make_async_copy(h_hbm.at[bq, tok], dst, gsem)

    def wait_gathers():
        tiles = xs.at[pl.ds(0, n_slots * dt)]
        pltpu.make_async_copy(tiles, tiles, gsem).wait()

    @pl.when((p == 0) & (j == 0))
    def _():
        def start(c, carry):
            gather(idx_cur, b_cur, c).start()
            return carry

        lax.fori_loop(0, n_slots, start, 0)

    n_groups = cap // m_chunk
    per_group = per_step // n_groups

    def ff_tile(first):
        wg = wg_ref[...].astype(BF16)
        wu = wu_ref[...].astype(BF16)
        wd = wd_ref[...].astype(BF16)
        for m in range(n_groups):
            for t in range(per_group):
                gather(idx_nxt, b_nxt, j * per_step + (m * per_group + t)).start()
            rows = pl.ds(m * m_chunk, m_chunk)
            x = xsb[rows, :]
            hg = jnp.dot(x, wg, preferred_element_type=F32)
            hu = jnp.dot(x, wu, preferred_element_type=F32)
            act = (hg * _sigmoid(hg)) * hu
            part = jnp.dot(act.astype(BF16), wd, preferred_element_type=F32)
            if first:
                y_ref[rows, :] = part
            else:
                y_ref[rows, :] += part

    @pl.when(j == 0)
    def _():
        wait_gathers()
        for c in range(dt):
            xsb[:, c * LANES:(c + 1) * LANES] = xs[pl.ds(c, cap, stride=pitch), :].astype(BF16)
        ff_tile(True)

    @pl.when(j > 0)
    def _():
        ff_tile(False)

    @pl.when((p == last_pair) & (j == n_ftiles - 1))
    def _():
        wait_gathers()


def _expert_ffn(h, idx, wg, wu, wd, layer, *, cap, tf, m_chunk):
    b, s, dt, _ = h.shape
    d = dt * LANES
    ne, dff = wg.shape[1], wg.shape[3]
    n_ftiles = dff // tf
    n_pairs = b * ne
    n_groups = cap // m_chunk
    per_step = -(-cap // (n_ftiles * n_groups)) * n_groups
    n_slots = n_ftiles * per_step
    tab = idx.reshape(n_pairs, cap // LANES, LANES)
    tab_spec = lambda f: pl.BlockSpec((None, cap // LANES, LANES), lambda p, j: (f(p), 0, 0),
                                      memory_space=pltpu.SMEM)
    return pl.pallas_call(
        functools.partial(_ffn_kernel, cap=cap, n_pairs=n_pairs, n_ftiles=n_ftiles, per_step=per_step,
                          m_chunk=m_chunk),
        grid=(n_pairs, n_ftiles),
        in_specs=[tab_spec(lambda p: p),
                  tab_spec(lambda p: jnp.minimum(p + 1, n_pairs - 1)),
                  pl.BlockSpec(memory_space=pl.ANY),
                  pl.BlockSpec((None, None, d, tf), lambda p, j: (layer, p % ne, 0, j)),
                  pl.BlockSpec((None, None, d, tf), lambda p, j: (layer, p % ne, 0, j)),
                  pl.BlockSpec((None, None, tf, d), lambda p, j: (layer, p % ne, j, 0))],
        out_specs=pl.BlockSpec((None, None, cap, d), lambda p, j: (p // ne, p % ne, 0, 0)),
        out_shape=jax.ShapeDtypeStruct((b, ne, cap, d), F32),
        scratch_shapes=[pltpu.VMEM((n_slots * (dt + 1), LANES), F32), pltpu.VMEM((cap, d), BF16),
                        pltpu.SemaphoreType.DMA(())],
        compiler_params=_cparams(("arbitrary", "arbitrary")),
        name="expert_swiglu",
    )(tab, tab, h, wg, wu, wd)


def _combine_kernel(pos_ref, rank_ref, aff_ref, x_ref, g_ref, y_hbm, o_ref, ybuf, acc, sem,
                    *, cap, nj, n_tiles, final_norm):
    b = pl.program_id(0)
    j = pl.program_id(1)
    t = b * nj + j
    ne = N_EXPERTS
    ring = COMBINE_AHEAD + 1
    assert ring & (ring - 1) == 0
    in_ring = lambda k: k & (ring - 1)

    def nominal(bq, e, jq, r):
        return ((pos_ref[bq * ne + e, jq] >> 3) << 3) + r * EWIN

    def window(bq, e, jq, r):
        return pl.multiple_of(jnp.minimum(nominal(bq, e, jq, r), cap - EWIN), SUBLANES)

    def start_fetch(ahead, r, slot):
        jq = j + ahead
        wrap = (jq >= nj).astype(I32)
        bq = b + wrap
        jq = jq - wrap * nj
        past = bq * nj + jq >= n_tiles
        bq = jnp.where(past, n_tiles // nj - 1, bq)
        jq = jnp.where(past, nj - 1, jq)
        for e in range(ne):
            pltpu.make_async_copy(y_hbm.at[bq, e, pl.ds(window(bq, e, jq, r), EWIN)],
                                  ybuf.at[slot, pl.ds(e * EWIN, EWIN)], sem.at[slot]).start()

    def wait_fetch(slot):
        pltpu.make_async_copy(ybuf.at[slot], ybuf.at[slot], sem.at[slot]).wait()

    rk = rank_ref[...]
    af = aff_ref[...]
    row = lax.broadcasted_iota(I32, (EWIN, LANES), 0).astype(F32)

    def add_round(r, slot):
        parts = []
        for e in range(ne):
            p0 = pos_ref[b * ne + e, j]
            local = row + (window(b, e, j, r) - p0).astype(F32)
            want = jnp.where(local >= jnp.maximum(nominal(b, e, j, r) - p0, 0).astype(F32), local, -2.0)
            parts.append(jnp.where(rk[e:e + 1, :] == want, af[e:e + 1, :], 0.0))
        gate_t = jnp.concatenate(parts, axis=0)
        onehot = jnp.where(gate_t != 0.0, 1.0, 0.0).T.astype(BF16)
        row_gate = jnp.sum(gate_t, axis=1, keepdims=True)
        scaled = jnp.where(row_gate != 0.0, ybuf[slot] * row_gate, 0.0)
        hi = scaled.astype(BF16)
        lo = (scaled - hi.astype(F32)).astype(BF16)
        return jnp.dot(onehot, hi, preferred_element_type=F32) + jnp.dot(onehot, lo, preferred_element_type=F32)

    @pl.when(t == 0)
    def _():
        for ahead in range(COMBINE_AHEAD):
            start_fetch(ahead, 0, ahead)

    start_fetch(COMBINE_AHEAD, 0, in_ring(t + COMBINE_AHEAD))
    slot0 = in_ring(t)
    wait_fetch(slot0)
    acc[...] = add_round(0, slot0)

    span = jnp.int32(1)
    for e in range(ne):
        span = jnp.maximum(span, pos_ref[b * ne + e, j + 1] - nominal(b, e, j, 0))
    n_rounds = (span + EWIN - 1) // EWIN

    def extra_round(r, carry):
        start_fetch(0, r, ring)
        wait_fetch(ring)
        acc[...] += add_round(r, ring)
        return carry

    lax.fori_loop(1, n_rounds, extra_round, 0)

    @pl.when(t == n_tiles - 1)
    def _():
        for ahead in range(1, COMBINE_AHEAD + 1):
            wait_fetch(in_ring(t + ahead))

    y = x_ref[...] + acc[...]
    if final_norm:
        ms = jnp.mean(y * y, axis=-1, keepdims=True)
        y = (y * lax.rsqrt(ms + EPS)) * g_ref[...]
    o_ref[...] = y


def _combine(x, y, pos, rank, aff, g_final, *, cap, final_norm):
    b, s, d = x.shape
    nj = s // LANES
    ne = N_EXPERTS
    pos_tab = jnp.concatenate([pos, jnp.full((b, ne, 1), cap, I32)], axis=2).reshape(b * ne, nj + 1)
    rank_t = jnp.transpose(rank, (0, 2, 1, 3))
    return pl.pallas_call(
        functools.partial(_combine_kernel, cap=cap, nj=nj, n_tiles=b * nj, final_norm=final_norm),
        grid_spec=pltpu.PrefetchScalarGridSpec(
            num_scalar_prefetch=1,
            grid=(b, nj),
            in_specs=[pl.BlockSpec((None, None, ne, LANES), lambda bi, j, tab: (bi, j, 0, 0)),
                      pl.BlockSpec((None, ne, LANES), lambda bi, j, tab: (bi, 0, j)),
                      pl.BlockSpec((None, LANES, d), lambda bi, j, tab: (bi, j, 0)),
                      pl.BlockSpec((1, d), lambda bi, j, tab: (0, 0)),
                      pl.BlockSpec(memory_space=pl.ANY)],
            out_specs=pl.BlockSpec((None, LANES, d), lambda bi, j, tab: (bi, j, 0)),
            scratch_shapes=[pltpu.VMEM((COMBINE_AHEAD + 2, ne * EWIN, d), F32), pltpu.VMEM((LANES, d), F32),
                            pltpu.SemaphoreType.DMA((COMBINE_AHEAD + 2,))]),
        out_shape=jax.ShapeDtypeStruct((b, s, d), F32),
        compiler_params=_cparams(("arbitrary", "arbitrary")),
        name="moe_combine",
    )(pos_tab, rank_t, aff, x, g_final.reshape(1, d), y)


def kernel(x, rel_bias, g_mix, w_in, attn_sink, w_attn_proj, w_fourier_proj, w_out, g_ffn, w_router,
           w_exp_gate, w_exp_up, w_exp_down, g_final):
    b, s, d = x.shape
    depth = g_mix.shape[0]
    cap = CAPACITY_FACTOR * s // N_EXPERTS
    bias = _bias_table(rel_bias)
    na = int(round(np.sqrt(s)))
    tables = _fourier_tables(na)
    for l in range(depth):
        q, k, v, vs, f, ga, gb = _inproj(x.reshape(b * s, d), g_mix[l], w_in[l].astype(BF16), tm=1024, chunk=256, na=na)
        shp = lambda a: a.reshape(b, s, a.shape[-1])
        attn = _attention(shp(q), shp(k), shp(v), shp(vs), attn_sink[l], bias, blocks_per_step=8)
        four = _fourier(f.reshape(b, s // 2, f.shape[-1]), tables, rows=8, rows2=16)
        xn, h, aff = _outproj(attn, four, shp(ga), shp(gb), x,
                              w_attn_proj[l].astype(BF16), w_fourier_proj[l].astype(BF16),
                              w_out[l].astype(BF16), g_ffn[l], w_router[l].T, tm=1024, chunk=256)
        idx, pos, rank = _routing(aff, cap=cap)
        y = _expert_ffn(h, idx, w_exp_gate, w_exp_up, w_exp_down, l, cap=cap, tf=256, m_chunk=min(cap, 512))
        x = _combine(xn, y, pos, rank, aff, g_final, cap=cap, final_norm=(l == depth - 1))
    return x
```

```python
import functools

import numpy as np
import jax
import jax.numpy as jnp
from jax import lax
from jax.experimental import pallas as pl
from jax.experimental.pallas import tpu as pltpu

F32 = jnp.float32
BF16 = jnp.bfloat16
I32 = jnp.int32

N_HEADS = 8
N_KV_HEADS = 2
HEAD_DIM = 64
ATTN_WIDTH = N_HEADS * HEAD_DIM
KV_WIDTH = N_KV_HEADS * HEAD_DIM
WINDOW = 128
BLOCK = 128
N_BUCKETS = 32
MAX_DISTANCE = 128
N_FOURIER_GROUPS = 4
FOURIER_GROUP_DIM = 128
FOURIER_WIDTH = N_FOURIER_GROUPS * FOURIER_GROUP_DIM
N_EXPERTS = 16
CAPACITY_FACTOR = 2
EPS = 1e-6
NEG_INF = -1e30

LANES = 128
LANE_BITS = 7
SUBLANES = 8
VMEM_LIMIT = 56 * 1024 * 1024
EWIN = 32
COMBINE_AHEAD = 3


def _cparams(sem):
    return pltpu.CompilerParams(dimension_semantics=sem, vmem_limit_bytes=VMEM_LIMIT)


def _nt_dot(a, b, **kw):
    return lax.dot_general(a, b, (((1,), (1,)), ((), ())), preferred_element_type=F32, **kw)


def _sigmoid(x):
    return 1.0 / (1.0 + jnp.exp(-x))


def _bf16_bits(y):
    return lax.bitcast_convert_type(y.astype(BF16).astype(F32), jnp.uint32)


def _t5_bucket(rel):
    half = N_BUCKETS // 2
    max_exact = half // 2
    ret = (rel > 0).astype(jnp.int32) * half
    n = jnp.abs(rel)
    nf = jnp.maximum(n, 1).astype(jnp.float32)
    large = max_exact + (jnp.log(nf / max_exact) / np.float32(np.log(MAX_DISTANCE / max_exact))
                         * (half - max_exact)).astype(jnp.int32)
    large = jnp.minimum(large, half - 1)
    return ret + jnp.where(n < max_exact, n, large)


def _bias_kernel(relb_ref, bucket_ref, o_ref):
    bk = bucket_ref[...]
    col = lax.broadcasted_iota(I32, bk.shape, 1)
    for h in range(N_HEADS):
        acc = jnp.full(bk.shape, NEG_INF, F32)
        for b in range(N_BUCKETS):
            acc = jnp.where(bk == b, relb_ref[b, h], acc)
        o_ref[0, h] = jnp.where(col >= BLOCK, acc, NEG_INF)
        o_ref[1, h] = acc
        o_ref[2, h] = jnp.where(col < 2 * BLOCK, acc, NEG_INF)


def _bias_table(rel_bias):
    q_loc = jnp.arange(BLOCK)
    k_loc = jnp.arange(3 * BLOCK) - BLOCK
    rel = k_loc[None, :] - q_loc[:, None]
    bucket = jnp.where(jnp.abs(rel) <= WINDOW, _t5_bucket(rel), -1).astype(I32)
    table = pl.pallas_call(
        _bias_kernel,
        out_shape=jax.ShapeDtypeStruct((3, N_HEADS, BLOCK, 3 * BLOCK), F32),
        in_specs=[pl.BlockSpec(memory_space=pltpu.SMEM),
                  pl.BlockSpec(memory_space=pltpu.VMEM)],
        out_specs=pl.BlockSpec(memory_space=pltpu.VMEM),
        name="rel_bias_table",
    )(rel_bias.astype(F32), bucket)
    return table.reshape(3, N_HEADS * BLOCK, 3 * BLOCK)


def _inproj_kernel(x_ref, g_ref, w_ref, q_ref, k_ref, v_ref, vs_ref, f_ref, ga_ref, gb_ref,
                   *, d_model, chunk, na):
    for r in range(x_ref.shape[0] // chunk):
        rs = pl.ds(r * chunk, chunk)
        x = x_ref[rs, :]
        ms = jnp.mean(x * x, axis=-1, keepdims=True)
        h = (x * lax.rsqrt(ms + EPS)) * g_ref[...]
        z = jnp.dot(h.astype(BF16), w_ref[...], preferred_element_type=F32)
        o = 0
        q_ref[rs, :] = (z[:, o:o + ATTN_WIDTH] * (HEAD_DIM ** -0.5)).astype(BF16)
        o += ATTN_WIDTH
        k_ref[rs, :] = z[:, o:o + KV_WIDTH].astype(BF16)
        o += KV_WIDTH
        v_ref[rs, :] = z[:, o:o + KV_WIDTH].astype(BF16)
        half = KV_WIDTH // 2
        vs_ref[rs, :] = jnp.concatenate([z[:, o + half:o + KV_WIDTH], z[:, o:o + half]], axis=1).astype(BF16)
        o += KV_WIDTH
        for m in range(chunk // (2 * na)):
            lo = z[2 * m * na:(2 * m + 1) * na, o:o + FOURIER_WIDTH]
            hi = z[(2 * m + 1) * na:(2 * m + 2) * na, o:o + FOURIER_WIDTH]
            f_ref[pl.ds(r * chunk // 2 + m * na, na), :] = (_bf16_bits(lo) >> 16) | _bf16_bits(hi)
        o += FOURIER_WIDTH
        ga_ref[rs, :] = _sigmoid(z[:, o:o + d_model]).astype(BF16)
        o += d_model
        gb_ref[rs, :] = _sigmoid(z[:, o:o + d_model]).astype(BF16)


def _inproj(x2, g, w_bf16, *, tm, chunk, na):
    n, d = x2.shape
    in_w = w_bf16.shape[1]
    assert chunk % (2 * na) == 0 and na % SUBLANES == 0
    widths = (ATTN_WIDTH, KV_WIDTH, KV_WIDTH, KV_WIDTH, FOURIER_WIDTH, d, d)
    dtypes = (BF16, BF16, BF16, BF16, jnp.uint32, BF16, BF16)
    halved = (False, False, False, False, True, False, False)
    row = lambda i: (i, 0)
    return pl.pallas_call(
        functools.partial(_inproj_kernel, d_model=d, chunk=chunk, na=na),
        grid=(n // tm,),
        in_specs=[pl.BlockSpec((tm, d), row),
                  pl.BlockSpec((1, d), lambda i: (0, 0)),
                  pl.BlockSpec((d, in_w), lambda i: (0, 0))],
        out_specs=[pl.BlockSpec((tm // 2 if hv else tm, w), row) for w, hv in zip(widths, halved)],
        out_shape=[jax.ShapeDtypeStruct((n // 2 if hv else n, w), dt)
                   for w, dt, hv in zip(widths, dtypes, halved)],
        compiler_params=_cparams(("parallel",)),
        name="inproj",
    )(x2, g.reshape(1, d), w_bf16)


def _attn_kernel(sink_ref, q_ref, kp_ref, ko_ref, kn_ref, vp_ref, vo_ref, vn_ref, wp_ref, wo_ref, wn_ref,
                 bias_ref, o_ref, *, n_blocks, blocks_per_step):
    i = pl.program_id(1)
    kcat = jnp.concatenate([kp_ref[...], ko_ref[...], kn_ref[...]], axis=0)
    vcat = jnp.concatenate([vp_ref[...], vo_ref[...], vn_ref[...]], axis=0)
    wcat = jnp.concatenate([wp_ref[...], wo_ref[...], wn_ref[...]], axis=0)
    group = N_HEADS // N_KV_HEADS
    head = lambda a, h: a[:, h * HEAD_DIM:(h + 1) * HEAD_DIM]
    sink = jnp.concatenate([jnp.full((BLOCK, LANES), sink_ref[h], F32) for h in range(N_HEADS)], axis=0)
    low_v = lax.broadcasted_iota(I32, vcat.shape, 1) < HEAD_DIM
    low_o = lax.broadcasted_iota(I32, (BLOCK, LANES), 1) < HEAD_DIM
    vx = {(kv, half): jnp.where(low_v if half == 0 else jnp.logical_not(low_v),
                                vcat if kv == half else wcat, jnp.ones_like(vcat))
          for kv in range(N_KV_HEADS) for half in range(2)}
    for t in range(blocks_per_step):
        blk = i * blocks_per_step + t
        kind = jnp.where(blk == 0, 0, jnp.where(blk == n_blocks - 1, 2, 1))
        kw = kcat[t * BLOCK:(t + 3) * BLOCK]
        qb = q_ref[t * BLOCK:(t + 1) * BLOCK, :]
        s = jnp.concatenate([_nt_dot(head(qb, h), head(kw, h // group)) for h in range(N_HEADS)], axis=0)
        s = s + bias_ref[kind]
        m = jnp.maximum(jnp.broadcast_to(jnp.max(s, axis=-1, keepdims=True), sink.shape), sink)
        p = jnp.exp(s - jnp.concatenate([m] * 3, axis=1)).astype(BF16)
        sink_term = jnp.exp(sink - m)
        for pair in range(N_HEADS // 2):
            halves = []
            for half in range(2):
                h = 2 * pair + half
                rows = slice(h * BLOCK, (h + 1) * BLOCK)
                out = jnp.dot(p[rows], vx[h // group, half][t * BLOCK:(t + 3) * BLOCK],
                              preferred_element_type=F32)
                denom = pltpu.roll(out, HEAD_DIM, axis=1) + sink_term[rows]
                halves.append(out * (1.0 / denom))
            tile = jnp.where(low_o, halves[0], halves[1])
            o_ref[t * BLOCK:(t + 1) * BLOCK, pair * LANES:(pair + 1) * LANES] = tile.astype(o_ref.dtype)


def _attention(q, k, v, v_swapped, sink, bias, *, blocks_per_step):
    b, s, _ = q.shape
    assert N_KV_HEADS == 2 and KV_WIDTH == LANES and N_HEADS % 2 == 0
    n_blocks = s // BLOCK
    tq = blocks_per_step * BLOCK
    prev = lambda bi, i: (bi, jnp.maximum(i * blocks_per_step - 1, 0), 0)
    own = lambda bi, i: (bi, i, 0)
    nxt = lambda bi, i: (bi, jnp.minimum((i + 1) * blocks_per_step, n_blocks - 1), 0)
    kv_specs = [pl.BlockSpec((None, BLOCK, KV_WIDTH), prev),
                pl.BlockSpec((None, tq, KV_WIDTH), own),
                pl.BlockSpec((None, BLOCK, KV_WIDTH), nxt)]
    return pl.pallas_call(
        functools.partial(_attn_kernel, n_blocks=n_blocks, blocks_per_step=blocks_per_step),
        grid=(b, s // tq),
        in_specs=[pl.BlockSpec(memory_space=pltpu.SMEM),
                  pl.BlockSpec((None, tq, ATTN_WIDTH), own)] + kv_specs * 3
                 + [pl.BlockSpec((3, N_HEADS * BLOCK, 3 * BLOCK), lambda bi, i: (0, 0, 0))],
        out_specs=pl.BlockSpec((None, tq, ATTN_WIDTH), own),
        out_shape=jax.ShapeDtypeStruct((b, s, ATTN_WIDTH), BF16),
        compiler_params=_cparams(("parallel", "parallel")),
        name="window_attention",
    )(sink.astype(F32), q, k, k, k, v, v, v, v_swapped, v_swapped, v_swapped, bias)


def _dft1_kernel(x_ref, a_ref, t_ref, x2, *, rows):
    a = a_ref[...]
    na = a.shape[0] // 2
    n_slabs = x2.shape[0]
    lanes = lambda c: slice(c * LANES, (c + 1) * LANES)
    for c in range(n_slabs):
        x2[c] = x_ref[:, :, lanes(c)].reshape(x2.shape[1:])
    for jj in range(rows):
        strided = pl.ds(jj, na // 2, stride=rows)
        xj = pltpu.bitcast(jnp.concatenate([x2[c, strided, :] for c in range(n_slabs)], axis=1), BF16)
        t = jnp.dot(a, xj, preferred_element_type=F32)
        t_ref[:, jj, :] = pltpu.bitcast(t.astype(BF16), jnp.uint32)


def _dft2_kernel(t_ref, gr_ref, gi_ref, cs_ref, o_ref, o2, *, rows):
    cs = cs_ref[...]
    na = o_ref.shape[0]
    pairs = rows // 2
    for aa in range(rows):
        t = pltpu.bitcast(t_ref[aa], BF16)
        zr = jnp.dot(gr_ref[aa], t, preferred_element_type=F32)
        zi = jnp.dot(gi_ref[aa], t, preferred_element_type=F32)
        for g in range(N_FOURIER_GROUPS):
            sl = slice(g * FOURIER_GROUP_DIM, (g + 1) * FOURIER_GROUP_DIM)
            z = jnp.concatenate([zr[:, sl], zi[:, sl]], axis=1).astype(BF16)
            y = jnp.dot(z, cs, preferred_element_type=F32)
            rows_q = pl.ds(aa // 2, na, stride=pairs)
            if aa % 2 == 0:
                o2[g, rows_q, :] = _bf16_bits(y) >> 16
            else:
                o2[g, rows_q, :] = o2[g, rows_q, :] | _bf16_bits(y)
    for g in range(N_FOURIER_GROUPS):
        sl = slice(g * FOURIER_GROUP_DIM, (g + 1) * FOURIER_GROUP_DIM)
        o_ref[:, :, sl] = o2[g].reshape(na, pairs, FOURIER_GROUP_DIM)


def _fourier_tables(na):
    def interleave(even, odd, axis):
        both = np.stack([even, odd], axis=axis + 1)
        shape = list(even.shape)
        shape[axis] *= 2
        return jnp.asarray(both.reshape(shape).astype(BF16))

    n = na * na
    k = np.arange(na, dtype=np.float64)
    ang = 2.0 * np.pi * np.outer(k, k) / na
    a1 = interleave(np.cos(ang), -np.sin(ang), 0)
    ka = np.arange(na, dtype=np.float64)[:, None, None]
    kb = np.arange(na, dtype=np.float64)[None, :, None]
    nl = np.arange(na, dtype=np.float64)[None, None, :]
    th = 2.0 * np.pi * ((nl * (ka + na * kb)) % n) / n
    gr, gi = np.cos(th), -np.sin(th)
    g_re = interleave(gr, -gi, 2)
    g_im = interleave(gi, gr, 2)
    kc = np.arange(FOURIER_GROUP_DIM, dtype=np.float64)
    angc = 2.0 * np.pi * np.outer(kc, kc) / FOURIER_GROUP_DIM
    scale = 1.0 / np.sqrt(float(n) * FOURIER_GROUP_DIM)
    cs = np.concatenate([np.cos(angc), np.sin(angc)], axis=0) * scale
    return a1, g_re, g_im, jnp.asarray(cs.astype(BF16))


def _fourier(f, tables, *, rows, rows2):
    b, s, w = f.shape[0], 2 * f.shape[1], f.shape[2]
    na = int(round(np.sqrt(s)))
    assert na * na == s and na % rows == 0 and na % rows2 == 0 and rows2 % (2 * SUBLANES) == 0
    a1, g_re, g_im, cs = tables
    x4 = f.reshape(b, na // 2, na, w)
    const2 = lambda bi, j: (0, 0)
    t = pl.pallas_call(
        functools.partial(_dft1_kernel, rows=rows),
        grid=(b, na // rows),
        in_specs=[pl.BlockSpec((None, na // 2, rows, w), lambda bi, j: (bi, 0, j, 0)),
                  pl.BlockSpec(a1.shape, const2)],
        out_specs=pl.BlockSpec((None, na, rows, w), lambda bi, j: (bi, 0, j, 0)),
        out_shape=jax.ShapeDtypeStruct((b, na, na, w), jnp.uint32),
        scratch_shapes=[pltpu.VMEM((w // LANES, na // 2 * rows, LANES), jnp.uint32)],
        compiler_params=_cparams(("parallel", "parallel")),
        name="seq_dft_stage1",
    )(x4, a1)
    y = pl.pallas_call(
        functools.partial(_dft2_kernel, rows=rows2),
        grid=(b, na // rows2),
        in_specs=[pl.BlockSpec((None, rows2, na, w), lambda bi, a: (bi, a, 0, 0))]
                 + [pl.BlockSpec((rows2,) + g_re.shape[1:], lambda bi, a: (a, 0, 0))] * 2
                 + [pl.BlockSpec(cs.shape, const2)],
        out_specs=pl.BlockSpec((None, na, rows2 // 2, w), lambda bi, a: (bi, 0, a, 0)),
        out_shape=jax.ShapeDtypeStruct((b, na, na // 2, w), jnp.uint32),
        scratch_shapes=[pltpu.VMEM((N_FOURIER_GROUPS, na * rows2 // 2, FOURIER_GROUP_DIM), jnp.uint32)],
        compiler_params=_cparams(("parallel", "parallel")),
        name="seq_dft_stage2",
    )(t, g_re, g_im, cs)
    return y.reshape(b, s // 2, w)


def _outproj_kernel(a_ref, f_ref, ga_ref, gb_ref, x_ref, wa_ref, wf_ref, wo_ref, g_ref, wr_ref,
                    xn_ref, h_ref, aff_ref, *, chunk):
    for r in range(x_ref.shape[0] // chunk):
        rs = pl.ds(r * chunk, chunk)
        a = jnp.dot(a_ref[rs, :], wa_ref[...], preferred_element_type=F32)
        four = pltpu.bitcast(f_ref[pl.ds(r * chunk // 2, chunk // 2), :], BF16)
        fo = jnp.dot(four, wf_ref[...], preferred_element_type=F32)
        merged = ga_ref[rs, :].astype(F32) * a + gb_ref[rs, :].astype(F32) * fo
        xn = x_ref[rs, :] + jnp.dot(merged.astype(BF16), wo_ref[...], preferred_element_type=F32)
        xn_ref[rs, :] = xn
        ms = jnp.mean(xn * xn, axis=-1, keepdims=True)
        h = (xn * lax.rsqrt(ms + EPS)) * g_ref[...]
        dt = h.shape[1] // LANES
        for c in range(dt):
            h_ref[pl.ds(r * chunk * dt + c, chunk, stride=dt), :] = h[:, c * LANES:(c + 1) * LANES]
        logits = _nt_dot(wr_ref[...], h.astype(BF16))
        e = jnp.exp(logits - jnp.max(logits, axis=0, keepdims=True))
        aff_ref[:, rs] = e / jnp.sum(e, axis=0, keepdims=True)


def _outproj(attn, four, ga, gb, x, wa, wf, wo, g, wr_t, *, tm, chunk):
    b, s, d = x.shape
    dt = d // LANES
    tok = lambda bi, i: (bi, i, 0)
    const = lambda bi, i: (0, 0)
    xn, h, aff = pl.pallas_call(
        functools.partial(_outproj_kernel, chunk=chunk),
        grid=(b, s // tm),
        in_specs=[pl.BlockSpec((None, tm, ATTN_WIDTH), tok),
                  pl.BlockSpec((None, tm // 2, FOURIER_WIDTH), tok),
                  pl.BlockSpec((None, tm, d), tok),
                  pl.BlockSpec((None, tm, d), tok),
                  pl.BlockSpec((None, tm, d), tok),
                  pl.BlockSpec((ATTN_WIDTH, d), const),
                  pl.BlockSpec((FOURIER_WIDTH, d), const),
                  pl.BlockSpec((d, d), const),
                  pl.BlockSpec((1, d), const),
                  pl.BlockSpec((N_EXPERTS, d), const)],
        out_specs=[pl.BlockSpec((None, tm, d), tok),
                   pl.BlockSpec((None, tm * dt, LANES), tok),
                   pl.BlockSpec((None, N_EXPERTS, tm), lambda bi, i: (bi, 0, i))],
        out_shape=[jax.ShapeDtypeStruct((b, s, d), F32),
                   jax.ShapeDtypeStruct((b, s * dt, LANES), F32),
                   jax.ShapeDtypeStruct((b, N_EXPERTS, s), F32)],
        compiler_params=_cparams(("parallel", "parallel")),
        name="outproj_router",
    )(attn, four, ga, gb, x, wa, wf, wo, g.reshape(1, d), wr_t.astype(BF16))
    return xn, h.reshape(b, s, dt, LANES), aff


def _tri(n, kind):
    r = lax.broadcasted_iota(I32, (n, n), 0)
    c = lax.broadcasted_iota(I32, (n, n), 1)
    cond = {"row_le_col": r <= c, "row_lt_col": r < c, "col_le_row": c <= r, "col_lt_row": c < r}[kind]
    return jnp.where(cond, 1.0, 0.0).astype(BF16)


def _split128(v):
    hi = jnp.floor(v * (1.0 / LANES))
    return hi, v - hi * LANES


def _routing_kernel(aff_ref, idx_ref, pos_ref, rank_ref, gt_scr, eq_scr, need_scr, *, cap):
    ne, nj, ni = aff_ref.shape
    bits = lax.bitcast_convert_type(aff_ref[...], I32)

    def bisect(it, thr):
        cand = thr | jnp.left_shift(jnp.int32(1), 30 - it)
        cnt = jnp.sum(jnp.where(bits >= cand, 1.0, 0.0), axis=(1, 2), keepdims=True)
        return jnp.where(cnt >= cap, cand, thr)

    thr = lax.fori_loop(0, 31, bisect, jnp.zeros((ne, 1, 1), I32))
    gt = jnp.where(bits > thr, 1.0, 0.0)
    gt_scr[...] = gt
    eq_scr[...] = jnp.where(bits == thr, 1.0, 0.0)
    need = cap - jnp.sum(gt, axis=(1, 2), keepdims=True)
    need_scr[...] = jnp.broadcast_to(need, (ne, SUBLANES, ni))

    ones_sq = jnp.ones((ni, ni), BF16)
    ones_row = jnp.ones((SUBLANES, ni), BF16)
    ones_rowj = jnp.ones((SUBLANES, nj), BF16)
    u_incl = _tri(ni, "row_le_col")
    uj_strict = _tri(nj, "row_lt_col")
    lj_strict = _tri(nj, "col_lt_row")
    l_incl = _tri(ni, "col_le_row")
    bdot = lambda a, b: jnp.dot(a, b, preferred_element_type=F32)

    c_lane = lax.broadcasted_iota(I32, (nj, cap), 1).astype(F32)
    c_row = lax.broadcasted_iota(I32, (SUBLANES, cap), 1).astype(F32)
    j_sub = lax.broadcasted_iota(I32, (nj, cap), 0).astype(F32)

    def slots(e, carry):
        eq = eq_scr[e]
        eqb = eq.astype(BF16)
        tie_rank = bdot(lj_strict, bdot(eqb, ones_sq).astype(BF16)) + bdot(eqb, u_incl)
        take = jnp.where(tie_rank <= need_scr[e][0:1, :], eq, 0.0)
        sel = jnp.maximum(gt_scr[e], take)
        selb = sel.astype(BF16)
        lc = bdot(selb, u_incl)
        lc_t = _nt_dot(l_incl, selb)
        n_rep = bdot(selb, ones_sq)
        s_rep = bdot(lj_strict, n_rep.astype(BF16)) + n_rep
        n_lane = _nt_dot(ones_row, selb)
        pex_lane = bdot(n_lane.astype(BF16), uj_strict)
        s_wide = jnp.concatenate([s_rep] * (cap // ni), axis=1) if cap > ni else s_rep[:, :cap]
        jc = bdot(ones_rowj, jnp.where(s_wide <= c_lane, 1.0, 0.0).astype(BF16))
        onehot = jnp.where(j_sub == jnp.broadcast_to(jc[0:1], (nj, cap)), 1.0, 0.0).astype(BF16)
        phi, plo = _split128(pex_lane)
        r = c_row - (bdot(phi.astype(BF16), onehot) * LANES + bdot(plo.astype(BF16), onehot))
        lcs = bdot(lc_t.astype(BF16), onehot)
        ic = bdot(ones_row, jnp.where(lcs <= jnp.broadcast_to(r[0:1], (ni, cap)), 1.0, 0.0).astype(BF16))
        idx_ref[e] = (jc * ni + ic).astype(I32)
        pos_ref[e] = pex_lane.astype(I32)
        rank_ref[e] = jnp.where(sel > 0.0, lc - 1.0, -1.0)
        return carry

    lax.fori_loop(0, ne, slots, 0)


def _routing(aff, *, cap):
    b, ne, s = aff.shape
    nj = s // LANES
    aff4 = aff.reshape(b, ne, nj, LANES)
    per_b = lambda bi: (bi, 0, 0, 0)
    idx, pos, rank = pl.pallas_call(
        functools.partial(_routing_kernel, cap=cap),
        grid=(b,),
        in_specs=[pl.BlockSpec((None, ne, nj, LANES), per_b)],
        out_specs=[pl.BlockSpec((None, ne, SUBLANES, cap), per_b),
                   pl.BlockSpec((None, ne, SUBLANES, nj), per_b),
                   pl.BlockSpec((None, ne, nj, LANES), per_b)],
        out_shape=[jax.ShapeDtypeStruct((b, ne, SUBLANES, cap), I32),
                   jax.ShapeDtypeStruct((b, ne, SUBLANES, nj), I32),
                   jax.ShapeDtypeStruct((b, ne, nj, LANES), F32)],
        scratch_shapes=[pltpu.VMEM((ne, nj, LANES), F32)] * 2 + [pltpu.VMEM((ne, SUBLANES, LANES), F32)],
        compiler_params=_cparams(("parallel",)),
        name="expert_choice_routing",
    )(aff4)
    return idx[:, :, 0, :], pos[:, :, 0, :], rank


def _ffn_kernel(idx_cur, idx_nxt, h_hbm, wg_ref, wu_ref, wd_ref, y_ref, xs, xsb, gsem,
                *, cap, n_pairs, n_ftiles, per_step, m_chunk):
    p = pl.program_id(0)
    j = pl.program_id(1)
    n_slots = n_ftiles * per_step
    pitch = xs.shape[0] // n_slots
    dt = pitch - 1
    last_pair = n_pairs - 1
    b_cur = p // N_EXPERTS
    b_nxt = jnp.minimum(p + 1, last_pair) // N_EXPERTS

    def gather(ref, bq, c):
        cc = jnp.minimum(c, cap - 1)
        tok = ref[cc >> LANE_BITS, cc & (LANES - 1)]
        dst = xs.at[pl.ds(c * pitch, dt)]
        return pltpu.make_async_copy(h_hbm.at[bq, tok], dst, gsem)

    def wait_gathers():
        tiles = xs.at[pl.ds(0, n_slots * dt)]
        pltpu.make_async_copy(tiles, tiles, gsem).wait()

    @pl.when((p == 0) & (j == 0))
    def _():
        def start(c, carry):
            gather(idx_cur, b_cur, c).start()
            return carry

        lax.fori_loop(0, n_slots, start, 0)

    n_groups = cap // m_chunk
    per_group = per_step // n_groups

    def ff_tile(first, first_slot, batches):
        wg = wg_ref[...].astype(BF16)
        wu = wu_ref[...].astype(BF16)
        wd = wd_ref[...].astype(BF16)
        for m in range(n_groups):
            for t in range(batches * per_group):
                gather(idx_nxt, b_nxt, first_slot + (m * batches * per_group + t)).start()
            rows = pl.ds(m * m_chunk, m_chunk)
            x = xsb[rows, :]
            hg = jnp.dot(x, wg, preferred_element_type=F32)
            hu = jnp.dot(x, wu, preferred_element_type=F32)
            act = (hg * _sigmoid(hg)) * hu
            part = jnp.dot(act.astype(BF16), wd, preferred_element_type=F32)
            if first:
                y_ref[rows, :] = part
            else:
                y_ref[rows, :] += part

    @pl.when(j == 0)
    def _():
        wait_gathers()
        for c in range(dt):
            xsb[:, c * LANES:(c + 1) * LANES] = xs[pl.ds(c, cap, stride=pitch), :].astype(BF16)
        ff_tile(True, 0, 0)

    @pl.when(j == 1)
    def _():
        ff_tile(False, 0, 2)

    @pl.when(j > 1)
    def _():
        ff_tile(False, j * per_step, 1)

    @pl.when((p == last_pair) & (j == n_ftiles - 1))
    def _():
        wait_gathers()


def _expert_ffn(h, idx, wg, wu, wd, layer, *, cap, tf, m_chunk):
    b, s, dt, _ = h.shape
    d = dt * LANES
    ne, dff = wg.shape[1], wg.shape[3]
    n_ftiles = dff // tf
    n_pairs = b * ne
    n_groups = cap // m_chunk
    per_step = -(-cap // (n_ftiles * n_groups)) * n_groups
    n_slots = n_ftiles * per_step
    tab = idx.reshape(n_pairs, cap // LANES, LANES)
    tab_spec = lambda f: pl.BlockSpec((None, cap // LANES, LANES), lambda p, j: (f(p), 0, 0),
                                      memory_space=pltpu.SMEM)
    return pl.pallas_call(
        functools.partial(_ffn_kernel, cap=cap, n_pairs=n_pairs, n_ftiles=n_ftiles, per_step=per_step,
                          m_chunk=m_chunk),
        grid=(n_pairs, n_ftiles),
        in_specs=[tab_spec(lambda p: p),
                  tab_spec(lambda p: jnp.minimum(p + 1, n_pairs - 1)),
                  pl.BlockSpec(memory_space=pl.ANY),
                  pl.BlockSpec((None, None, d, tf), lambda p, j: (layer, p % ne, 0, j)),
                  pl.BlockSpec((None, None, d, tf), lambda p, j: (layer, p % ne, 0, j)),
                  pl.BlockSpec((None, None, tf, d), lambda p, j: (layer, p % ne, j, 0))],
        out_specs=pl.BlockSpec((None, None, cap, d), lambda p, j: (p // ne, p % ne, 0, 0)),
        out_shape=jax.ShapeDtypeStruct((b, ne, cap, d), F32),
        scratch_shapes=[pltpu.VMEM((n_slots * (dt + 1), LANES), F32), pltpu.VMEM((cap, d), BF16),
                        pltpu.SemaphoreType.DMA(())],
        compiler_params=_cparams(("arbitrary", "arbitrary")),
        name="expert_swiglu",
    )(tab, tab, h, wg, wu, wd)


def _combine_kernel(pos_ref, rank_ref, aff_ref, x_ref, g_ref, y_hbm, o_ref, ybuf, acc, sem,
                    *, cap, nj, n_tiles, final_norm):
    b = pl.program_id(0)
    j = pl.program_id(1)
    t = b * nj + j
    ne = N_EXPERTS
    ring = COMBINE_AHEAD + 1
    assert ring & (ring - 1) == 0
    in_ring = lambda k: k & (ring - 1)

    def nominal(bq, e, jq, r):
        return ((pos_ref[bq * ne + e, jq] >> 3) << 3) + r * EWIN

    def window(bq, e, jq, r):
        return pl.multiple_of(jnp.minimum(nominal(bq, e, jq, r), cap - EWIN), SUBLANES)

    def start_fetch(ahead, r, slot):
        jq = j + ahead
        wrap = (jq >= nj).astype(I32)
        bq = b + wrap
        jq = jq - wrap * nj
        past = bq * nj + jq >= n_tiles
        bq = jnp.where(past, n_tiles // nj - 1, bq)
        jq = jnp.where(past, nj - 1, jq)
        for e in range(ne):
            pltpu.make_async_copy(y_hbm.at[bq, e, pl.ds(window(bq, e, jq, r), EWIN)],
                                  ybuf.at[slot, pl.ds(e * EWIN, EWIN)], sem.at[slot]).start()

    def wait_fetch(slot):
        pltpu.make_async_copy(ybuf.at[slot], ybuf.at[slot], sem.at[slot]).wait()

    rk = rank_ref[...]
    af = aff_ref[...]
    row = lax.broadcasted_iota(I32, (EWIN, LANES), 0).astype(F32)

    def add_round(r, slot):
        parts = []
        for e in range(ne):
            p0 = pos_ref[b * ne + e, j]
            local = row + (window(b, e, j, r) - p0).astype(F32)
            want = jnp.where(local >= jnp.maximum(nominal(b, e, j, r) - p0, 0).astype(F32), local, -2.0)
            parts.append(jnp.where(rk[e:e + 1, :] == want, af[e:e + 1, :], 0.0))
        gate_t = jnp.concatenate(parts, axis=0)
        onehot = jnp.where(gate_t != 0.0, 1.0, 0.0).T.astype(BF16)
        row_gate = jnp.sum(gate_t, axis=1, keepdims=True)
        scaled = jnp.where(row_gate != 0.0, ybuf[slot] * row_gate, 0.0)
        hi = scaled.astype(BF16)
        lo = (scaled - hi.astype(F32)).astype(BF16)
        return jnp.dot(onehot, hi, preferred_element_type=F32) + jnp.dot(onehot, lo, preferred_element_type=F32)

    @pl.when(t == 0)
    def _():
        for ahead in range(COMBINE_AHEAD):
            start_fetch(ahead, 0, ahead)

    start_fetch(COMBINE_AHEAD, 0, in_ring(t + COMBINE_AHEAD))
    slot0 = in_ring(t)
    wait_fetch(slot0)
    acc[...] = add_round(0, slot0)

    span = jnp.int32(1)
    for e in range(ne):
        span = jnp.maximum(span, pos_ref[b * ne + e, j + 1] - nominal(b, e, j, 0))
    n_rounds = (span + EWIN - 1) // EWIN

    def extra_round(r, carry):
        start_fetch(0, r, ring)
        wait_fetch(ring)
        acc[...] += add_round(r, ring)
        return carry

    lax.fori_loop(1, n_rounds, extra_round, 0)

    @pl.when(t == n_tiles - 1)
    def _():
        for ahead in range(1, COMBINE_AHEAD + 1):
            wait_fetch(in_ring(t + ahead))

    y = x_ref[...] + acc[...]
    if final_norm:
        ms = jnp.mean(y * y, axis=-1, keepdims=True)
        y = (y * lax.rsqrt(ms + EPS)) * g_ref[...]
    o_ref[...] = y


def _combine(x, y, pos, rank, aff, g_final, *, cap, final_norm):
    b, s, d = x.shape
    nj = s // LANES
    ne = N_EXPERTS
    pos_tab = jnp.concatenate([pos, jnp.full((b, ne, 1), cap, I32)], axis=2).reshape(b * ne, nj + 1)
    rank_t = jnp.transpose(rank, (0, 2, 1, 3))
    return pl.pallas_call(
        functools.partial(_combine_kernel, cap=cap, nj=nj, n_tiles=b * nj, final_norm=final_norm),
        grid_spec=pltpu.PrefetchScalarGridSpec(
            num_scalar_prefetch=1,
            grid=(b, nj),
            in_specs=[pl.BlockSpec((None, None, ne, LANES), lambda bi, j, tab: (bi, j, 0, 0)),
                      pl.BlockSpec((None, ne, LANES), lambda bi, j, tab: (bi, 0, j)),
                      pl.BlockSpec((None, LANES, d), lambda bi, j, tab: (bi, j, 0)),
                      pl.BlockSpec((1, d), lambda bi, j, tab: (0, 0)),
                      pl.BlockSpec(memory_space=pl.ANY)],
            out_specs=pl.BlockSpec((None, LANES, d), lambda bi, j, tab: (bi, j, 0)),
            scratch_shapes=[pltpu.VMEM((COMBINE_AHEAD + 2, ne * EWIN, d), F32), pltpu.VMEM((LANES, d), F32),
                            pltpu.SemaphoreType.DMA((COMBINE_AHEAD + 2,))]),
        out_shape=jax.ShapeDtypeStruct((b, s, d), F32),
        compiler_params=_cparams(("arbitrary", "arbitrary")),
        name="moe_combine",
    )(pos_tab, rank_t, aff, x, g_final.reshape(1, d), y)


def kernel(x, rel_bias, g_mix, w_in, attn_sink, w_attn_proj, w_fourier_proj, w_out, g_ffn, w_router,
           w_exp_gate, w_exp_up, w_exp_down, g_final):
    b, s, d = x.shape
    depth = g_mix.shape[0]
    cap = CAPACITY_FACTOR * s // N_EXPERTS
    bias = _bias_table(rel_bias)
    na = int(round(np.sqrt(s)))
    tables = _fourier_tables(na)
    for l in range(depth):
        q, k, v, vs, f, ga, gb = _inproj(x.reshape(b * s, d), g_mix[l], w_in[l].astype(BF16), tm=1024, chunk=256, na=na)
        shp = lambda a: a.reshape(b, s, a.shape[-1])
        attn = _attention(shp(q), shp(k), shp(v), shp(vs), attn_sink[l], bias, blocks_per_step=8)
        four = _fourier(f.reshape(b, s // 2, f.shape[-1]), tables, rows=8, rows2=16)
        xn, h, aff = _outproj(attn, four, shp(ga), shp(gb), x,
                              w_attn_proj[l].astype(BF16), w_fourier_proj[l].astype(BF16),
                              w_out[l].astype(BF16), g_ffn[l], w_router[l].T, tm=1024, chunk=256)
        idx, pos, rank = _routing(aff, cap=cap)
        y = _expert_ffn(h, idx, w_exp_gate, w_exp_up, w_exp_down, l, cap=cap, tf=256, m_chunk=min(cap, 512))
        x = _combine(xn, y, pos, rank, aff, g_final, cap=cap, final_norm=(l == depth - 1))
    return x
```

```python
import functools

import numpy as np
import jax
import jax.numpy as jnp
from jax import lax
from jax.experimental import pallas as pl
from jax.experimental.pallas import tpu as pltpu

F32 = jnp.float32
BF16 = jnp.bfloat16
I32 = jnp.int32

N_HEADS = 8
N_KV_HEADS = 2
HEAD_DIM = 64
ATTN_WIDTH = N_HEADS * HEAD_DIM
KV_WIDTH = N_KV_HEADS * HEAD_DIM
WINDOW = 128
BLOCK = 128
N_BUCKETS = 32
MAX_DISTANCE = 128
N_FOURIER_GROUPS = 4
FOURIER_GROUP_DIM = 128
FOURIER_WIDTH = N_FOURIER_GROUPS * FOURIER_GROUP_DIM
N_EXPERTS = 16
CAPACITY_FACTOR = 2
EPS = 1e-6
NEG_INF = -1e30

LANES = 128
LANE_BITS = 7
SUBLANES = 8
VMEM_LIMIT = 56 * 1024 * 1024
EWIN = 32
COMBINE_AHEAD = 3


def _cparams(sem):
    return pltpu.CompilerParams(dimension_semantics=sem, vmem_limit_bytes=VMEM_LIMIT)


def _nt_dot(a, b, **kw):
    return lax.dot_general(a, b, (((1,), (1,)), ((), ())), preferred_element_type=F32, **kw)


def _sigmoid(x):
    return 1.0 / (1.0 + jnp.exp(-x))


def _bf16_bits(y):
    return lax.bitcast_convert_type(y.astype(BF16).astype(F32), jnp.uint32)


def _t5_bucket(rel):
    half = N_BUCKETS // 2
    max_exact = half // 2
    ret = (rel > 0).astype(jnp.int32) * half
    n = jnp.abs(rel)
    nf = jnp.maximum(n, 1).astype(jnp.float32)
    large = max_exact + (jnp.log(nf / max_exact) / np.float32(np.log(MAX_DISTANCE / max_exact))
                         * (half - max_exact)).astype(jnp.int32)
    large = jnp.minimum(large, half - 1)
    return ret + jnp.where(n < max_exact, n, large)


def _bias_kernel(relb_ref, bucket_ref, o_ref):
    bk = bucket_ref[...]
    col = lax.broadcasted_iota(I32, bk.shape, 1)
    for h in range(N_HEADS):
        acc = jnp.full(bk.shape, NEG_INF, F32)
        for b in range(N_BUCKETS):
            acc = jnp.where(bk == b, relb_ref[b, h], acc)
        o_ref[0, h] = jnp.where(col >= BLOCK, acc, NEG_INF)
        o_ref[1, h] = acc
        o_ref[2, h] = jnp.where(col < 2 * BLOCK, acc, NEG_INF)


def _bias_table(rel_bias):
    q_loc = jnp.arange(BLOCK)
    k_loc = jnp.arange(3 * BLOCK) - BLOCK
    rel = k_loc[None, :] - q_loc[:, None]
    bucket = jnp.where(jnp.abs(rel) <= WINDOW, _t5_bucket(rel), -1).astype(I32)
    table = pl.pallas_call(
        _bias_kernel,
        out_shape=jax.ShapeDtypeStruct((3, N_HEADS, BLOCK, 3 * BLOCK), F32),
        in_specs=[pl.BlockSpec(memory_space=pltpu.SMEM),
                  pl.BlockSpec(memory_space=pltpu.VMEM)],
        out_specs=pl.BlockSpec(memory_space=pltpu.VMEM),
        name="rel_bias_table",
    )(rel_bias.astype(F32), bucket)
    return table.reshape(3, N_HEADS * BLOCK, 3 * BLOCK)


def _inproj_kernel(x_ref, g_ref, w_ref, q_ref, k_ref, v_ref, vs_ref, f_ref, ga_ref, gb_ref,
                   *, d_model, chunk, na):
    for r in range(x_ref.shape[0] // chunk):
        rs = pl.ds(r * chunk, chunk)
        x = x_ref[rs, :]
        ms = jnp.mean(x * x, axis=-1, keepdims=True)
        h = (x * lax.rsqrt(ms + EPS)) * g_ref[...]
        z = jnp.dot(h.astype(BF16), w_ref[...], preferred_element_type=F32)
        o = 0
        q_ref[rs, :] = (z[:, o:o + ATTN_WIDTH] * (HEAD_DIM ** -0.5)).astype(BF16)
        o += ATTN_WIDTH
        k_ref[rs, :] = z[:, o:o + KV_WIDTH].astype(BF16)
        o += KV_WIDTH
        v_ref[rs, :] = z[:, o:o + KV_WIDTH].astype(BF16)
        half = KV_WIDTH // 2
        vs_ref[rs, :] = jnp.concatenate([z[:, o + half:o + KV_WIDTH], z[:, o:o + half]], axis=1).astype(BF16)
        o += KV_WIDTH
        for m in range(chunk // (2 * na)):
            lo = z[2 * m * na:(2 * m + 1) * na, o:o + FOURIER_WIDTH]
            hi = z[(2 * m + 1) * na:(2 * m + 2) * na, o:o + FOURIER_WIDTH]
            f_ref[pl.ds(r * chunk // 2 + m * na, na), :] = (_bf16_bits(lo) >> 16) | _bf16_bits(hi)
        o += FOURIER_WIDTH
        ga_ref[rs, :] = _sigmoid(z[:, o:o + d_model]).astype(BF16)
        o += d_model
        gb_ref[rs, :] = _sigmoid(z[:, o:o + d_model]).astype(BF16)


def _inproj(x2, g, w_bf16, *, tm, chunk, na):
    n, d = x2.shape
    in_w = w_bf16.shape[1]
    assert chunk % (2 * na) == 0 and na % SUBLANES == 0
    widths = (ATTN_WIDTH, KV_WIDTH, KV_WIDTH, KV_WIDTH, FOURIER_WIDTH, d, d)
    dtypes = (BF16, BF16, BF16, BF16, jnp.uint32, BF16, BF16)
    halved = (False, False, False, False, True, False, False)
    row = lambda i: (i, 0)
    return pl.pallas_call(
        functools.partial(_inproj_kernel, d_model=d, chunk=chunk, na=na),
        grid=(n // tm,),
        in_specs=[pl.BlockSpec((tm, d), row),
                  pl.BlockSpec((1, d), lambda i: (0, 0)),
                  pl.BlockSpec((d, in_w), lambda i: (0, 0))],
        out_specs=[pl.BlockSpec((tm // 2 if hv else tm, w), row) for w, hv in zip(widths, halved)],
        out_shape=[jax.ShapeDtypeStruct((n // 2 if hv else n, w), dt)
                   for w, dt, hv in zip(widths, dtypes, halved)],
        compiler_params=_cparams(("parallel",)),
        name="inproj",
    )(x2, g.reshape(1, d), w_bf16)


def _attn_kernel(sink_ref, q_ref, kp_ref, ko_ref, kn_ref, vp_ref, vo_ref, vn_ref, wp_ref, wo_ref, wn_ref,
                 bias_ref, o_ref, *, n_blocks, blocks_per_step):
    i = pl.program_id(1)
    kcat = jnp.concatenate([kp_ref[...], ko_ref[...], kn_ref[...]], axis=0)
    vcat = jnp.concatenate([vp_ref[...], vo_ref[...], vn_ref[...]], axis=0)
    wcat = jnp.concatenate([wp_ref[...], wo_ref[...], wn_ref[...]], axis=0)
    group = N_HEADS // N_KV_HEADS
    head = lambda a, h: a[:, h * HEAD_DIM:(h + 1) * HEAD_DIM]
    sink = jnp.concatenate([jnp.full((BLOCK, LANES), sink_ref[h], F32) for h in range(N_HEADS)], axis=0)
    low_v = lax.broadcasted_iota(I32, vcat.shape, 1) < HEAD_DIM
    low_o = lax.broadcasted_iota(I32, (BLOCK, LANES), 1) < HEAD_DIM
    vx = {(kv, half): jnp.where(low_v if half == 0 else jnp.logical_not(low_v),
                                vcat if kv == half else wcat, jnp.ones_like(vcat))
          for kv in range(N_KV_HEADS) for half in range(2)}
    for t in range(blocks_per_step):
        blk = i * blocks_per_step + t
        kind = jnp.where(blk == 0, 0, jnp.where(blk == n_blocks - 1, 2, 1))
        kw = kcat[t * BLOCK:(t + 3) * BLOCK]
        qb = q_ref[t * BLOCK:(t + 1) * BLOCK, :]
        s = jnp.concatenate([_nt_dot(head(qb, h), head(kw, h // group)) for h in range(N_HEADS)], axis=0)
        s = s + bias_ref[kind]
        m = jnp.maximum(jnp.broadcast_to(jnp.max(s, axis=-1, keepdims=True), sink.shape), sink)
        p = jnp.exp(s - jnp.concatenate([m] * 3, axis=1)).astype(BF16)
        sink_term = jnp.exp(sink - m)
        for pair in range(N_HEADS // 2):
            halves = []
            for half in range(2):
                h = 2 * pair + half
                rows = slice(h * BLOCK, (h + 1) * BLOCK)
                out = jnp.dot(p[rows], vx[h // group, half][t * BLOCK:(t + 3) * BLOCK],
                              preferred_element_type=F32)
                denom = pltpu.roll(out, HEAD_DIM, axis=1) + sink_term[rows]
                halves.append(out * (1.0 / denom))
            tile = jnp.where(low_o, halves[0], halves[1])
            o_ref[t * BLOCK:(t + 1) * BLOCK, pair * LANES:(pair + 1) * LANES] = tile.astype(o_ref.dtype)


def _attention(q, k, v, v_swapped, sink, bias, *, blocks_per_step):
    b, s, _ = q.shape
    assert N_KV_HEADS == 2 and KV_WIDTH == LANES and N_HEADS % 2 == 0
    n_blocks = s // BLOCK
    tq = blocks_per_step * BLOCK
    prev = lambda bi, i: (bi, jnp.maximum(i * blocks_per_step - 1, 0), 0)
    own = lambda bi, i: (bi, i, 0)
    nxt = lambda bi, i: (bi, jnp.minimum((i + 1) * blocks_per_step, n_blocks - 1), 0)
    kv_specs = [pl.BlockSpec((None, BLOCK, KV_WIDTH), prev),
                pl.BlockSpec((None, tq, KV_WIDTH), own),
                pl.BlockSpec((None, BLOCK, KV_WIDTH), nxt)]
    return pl.pallas_call(
        functools.partial(_attn_kernel, n_blocks=n_blocks, blocks_per_step=blocks_per_step),
        grid=(b, s // tq),
        in_specs=[pl.BlockSpec(memory_space=pltpu.SMEM),
                  pl.BlockSpec((None, tq, ATTN_WIDTH), own)] + kv_specs * 3
                 + [pl.BlockSpec((3, N_HEADS * BLOCK, 3 * BLOCK), lambda bi, i: (0, 0, 0))],
        out_specs=pl.BlockSpec((None, tq, ATTN_WIDTH), own),
        out_shape=jax.ShapeDtypeStruct((b, s, ATTN_WIDTH), BF16),
        compiler_params=_cparams(("parallel", "parallel")),
        name="window_attention",
    )(sink.astype(F32), q, k, k, k, v, v, v, v_swapped, v_swapped, v_swapped, bias)


def _dft1_kernel(x_ref, a_ref, t_ref, x2, *, rows):
    a = a_ref[...]
    na = a.shape[0] // 2
    n_slabs = x2.shape[0]
    lanes = lambda c: slice(c * LANES, (c + 1) * LANES)
    for c in range(n_slabs):
        x2[c] = x_ref[:, :, lanes(c)].reshape(x2.shape[1:])
    for jj in range(rows):
        strided = pl.ds(jj, na // 2, stride=rows)
        xj = pltpu.bitcast(jnp.concatenate([x2[c, strided, :] for c in range(n_slabs)], axis=1), BF16)
        t = jnp.dot(a, xj, preferred_element_type=F32)
        t_ref[:, jj, :] = pltpu.bitcast(t.astype(BF16), jnp.uint32)


def _dft2_kernel(t_ref, gr_ref, gi_ref, cs_ref, o_ref, o2, *, rows):
    cs = cs_ref[...]
    na = o_ref.shape[0]
    pairs = rows // 2
    for aa in range(rows):
        t = pltpu.bitcast(t_ref[aa], BF16)
        zr = jnp.dot(gr_ref[aa], t, preferred_element_type=F32)
        zi = jnp.dot(gi_ref[aa], t, preferred_element_type=F32)
        for g in range(N_FOURIER_GROUPS):
            sl = slice(g * FOURIER_GROUP_DIM, (g + 1) * FOURIER_GROUP_DIM)
            z = jnp.concatenate([zr[:, sl], zi[:, sl]], axis=1).astype(BF16)
            y = jnp.dot(z, cs, preferred_element_type=F32)
            rows_q = pl.ds(aa // 2, na, stride=pairs)
            if aa % 2 == 0:
                o2[g, rows_q, :] = _bf16_bits(y) >> 16
            else:
                o2[g, rows_q, :] = o2[g, rows_q, :] | _bf16_bits(y)
    for g in range(N_FOURIER_GROUPS):
        sl = slice(g * FOURIER_GROUP_DIM, (g + 1) * FOURIER_GROUP_DIM)
        o_ref[:, :, sl] = o2[g].reshape(na, pairs, FOURIER_GROUP_DIM)


def _fourier_tables(na):
    def interleave(even, odd, axis):
        both = np.stack([even, odd], axis=axis + 1)
        shape = list(even.shape)
        shape[axis] *= 2
        return jnp.asarray(both.reshape(shape).astype(BF16))

    n = na * na
    k = np.arange(na, dtype=np.float64)
    ang = 2.0 * np.pi * np.outer(k, k) / na
    a1 = interleave(np.cos(ang), -np.sin(ang), 0)
    ka = np.arange(na, dtype=np.float64)[:, None, None]
    kb = np.arange(na, dtype=np.float64)[None, :, None]
    nl = np.arange(na, dtype=np.float64)[None, None, :]
    th = 2.0 * np.pi * ((nl * (ka + na * kb)) % n) / n
    gr, gi = np.cos(th), -np.sin(th)
    g_re = interleave(gr, -gi, 2)
    g_im = interleave(gi, gr, 2)
    kc = np.arange(FOURIER_GROUP_DIM, dtype=np.float64)
    angc = 2.0 * np.pi * np.outer(kc, kc) / FOURIER_GROUP_DIM
    scale = 1.0 / np.sqrt(float(n) * FOURIER_GROUP_DIM)
    cs = np.concatenate([np.cos(angc), np.sin(angc)], axis=0) * scale
    return a1, g_re, g_im, jnp.asarray(cs.astype(BF16))


def _fourier(f, tables, *, rows, rows2):
    b, s, w = f.shape[0], 2 * f.shape[1], f.shape[2]
    na = int(round(np.sqrt(s)))
    assert na * na == s and na % rows == 0 and na % rows2 == 0 and rows2 % (2 * SUBLANES) == 0
    a1, g_re, g_im, cs = tables
    x4 = f.reshape(b, na // 2, na, w)
    const2 = lambda bi, j: (0, 0)
    t = pl.pallas_call(
        functools.partial(_dft1_kernel, rows=rows),
        grid=(b, na // rows),
        in_specs=[pl.BlockSpec((None, na // 2, rows, w), lambda bi, j: (bi, 0, j, 0)),
                  pl.BlockSpec(a1.shape, const2)],
        out_specs=pl.BlockSpec((None, na, rows, w), lambda bi, j: (bi, 0, j, 0)),
        out_shape=jax.ShapeDtypeStruct((b, na, na, w), jnp.uint32),
        scratch_shapes=[pltpu.VMEM((w // LANES, na // 2 * rows, LANES), jnp.uint32)],
        compiler_params=_cparams(("parallel", "parallel")),
        name="seq_dft_stage1",
    )(x4, a1)
    y = pl.pallas_call(
        functools.partial(_dft2_kernel, rows=rows2),
        grid=(b, na // rows2),
        in_specs=[pl.BlockSpec((None, rows2, na, w), lambda bi, a: (bi, a, 0, 0))]
                 + [pl.BlockSpec((rows2,) + g_re.shape[1:], lambda bi, a: (a, 0, 0))] * 2
                 + [pl.BlockSpec(cs.shape, const2)],
        out_specs=pl.BlockSpec((None, na, rows2 // 2, w), lambda bi, a: (bi, 0, a, 0)),
        out_shape=jax.ShapeDtypeStruct((b, na, na // 2, w), jnp.uint32),
        scratch_shapes=[pltpu.VMEM((N_FOURIER_GROUPS, na * rows2 // 2, FOURIER_GROUP_DIM), jnp.uint32)],
        compiler_params=_cparams(("parallel", "parallel")),
        name="seq_dft_stage2",
    )(t, g_re, g_im, cs)
    return y.reshape(b, s // 2, w)


def _outproj_kernel(a_ref, f_ref, ga_ref, gb_ref, x_ref, wa_ref, wf_ref, wo_ref, g_ref, wr_ref,
                    xn_ref, h_ref, aff_ref, *, chunk):
    for r in range(x_ref.shape[0] // chunk):
        rs = pl.ds(r * chunk, chunk)
        a = jnp.dot(a_ref[rs, :], wa_ref[...], preferred_element_type=F32)
        four = pltpu.bitcast(f_ref[pl.ds(r * chunk // 2, chunk // 2), :], BF16)
        fo = jnp.dot(four, wf_ref[...], preferred_element_type=F32)
        merged = ga_ref[rs, :].astype(F32) * a + gb_ref[rs, :].astype(F32) * fo
        xn = x_ref[rs, :] + jnp.dot(merged.astype(BF16), wo_ref[...], preferred_element_type=F32)
        xn_ref[rs, :] = xn
        ms = jnp.mean(xn * xn, axis=-1, keepdims=True)
        h = (xn * lax.rsqrt(ms + EPS)) * g_ref[...]
        dt = h.shape[1] // LANES
        for c in range(dt):
            h_ref[pl.ds(r * chunk * dt + c, chunk, stride=dt), :] = h[:, c * LANES:(c + 1) * LANES]
        logits = _nt_dot(wr_ref[...], h.astype(BF16))
        e = jnp.exp(logits - jnp.max(logits, axis=0, keepdims=True))
        aff_ref[:, rs] = e / jnp.sum(e, axis=0, keepdims=True)


def _outproj(attn, four, ga, gb, x, wa, wf, wo, g, wr_t, *, tm, chunk):
    b, s, d = x.shape
    dt = d // LANES
    tok = lambda bi, i: (bi, i, 0)
    const = lambda bi, i: (0, 0)
    xn, h, aff = pl.pallas_call(
        functools.partial(_outproj_kernel, chunk=chunk),
        grid=(b, s // tm),
        in_specs=[pl.BlockSpec((None, tm, ATTN_WIDTH), tok),
                  pl.BlockSpec((None, tm // 2, FOURIER_WIDTH), tok),
                  pl.BlockSpec((None, tm, d), tok),
                  pl.BlockSpec((None, tm, d), tok),
                  pl.BlockSpec((None, tm, d), tok),
                  pl.BlockSpec((ATTN_WIDTH, d), const),
                  pl.BlockSpec((FOURIER_WIDTH, d), const),
                  pl.BlockSpec((d, d), const),
                  pl.BlockSpec((1, d), const),
                  pl.BlockSpec((N_EXPERTS, d), const)],
        out_specs=[pl.BlockSpec((None, tm, d), tok),
                   pl.BlockSpec((None, tm * dt, LANES), tok),
                   pl.BlockSpec((None, N_EXPERTS, tm), lambda bi, i: (bi, 0, i))],
        out_shape=[jax.ShapeDtypeStruct((b, s, d), F32),
                   jax.ShapeDtypeStruct((b, s * dt, LANES), F32),
                   jax.ShapeDtypeStruct((b, N_EXPERTS, s), F32)],
        compiler_params=_cparams(("parallel", "parallel")),
        name="outproj_router",
    )(attn, four, ga, gb, x, wa, wf, wo, g.reshape(1, d), wr_t.astype(BF16))
    return xn, h.reshape(b, s, dt, LANES), aff


def _tri(n, kind):
    r = lax.broadcasted_iota(I32, (n, n), 0)
    c = lax.broadcasted_iota(I32, (n, n), 1)
    cond = {"row_le_col": r <= c, "row_lt_col": r < c, "col_le_row": c <= r, "col_lt_row": c < r}[kind]
    return jnp.where(cond, 1.0, 0.0).astype(BF16)


def _split128(v):
    hi = jnp.floor(v * (1.0 / LANES))
    return hi, v - hi * LANES


def _routing_kernel(aff_ref, idx_ref, pos_ref, rank_ref, gt_scr, eq_scr, need_scr, *, cap):
    ne, nj, ni = aff_ref.shape
    bits = lax.bitcast_convert_type(aff_ref[...], I32)

    def bisect(it, thr):
        cand = thr | jnp.left_shift(jnp.int32(1), 30 - it)
        cnt = jnp.sum(jnp.where(bits >= cand, 1.0, 0.0), axis=(1, 2), keepdims=True)
        return jnp.where(cnt >= cap, cand, thr)

    thr = lax.fori_loop(0, 31, bisect, jnp.zeros((ne, 1, 1), I32))
    gt = jnp.where(bits > thr, 1.0, 0.0)
    gt_scr[...] = gt
    eq_scr[...] = jnp.where(bits == thr, 1.0, 0.0)
    need = cap - jnp.sum(gt, axis=(1, 2), keepdims=True)
    need_scr[...] = jnp.broadcast_to(need, (ne, SUBLANES, ni))

    ones_sq = jnp.ones((ni, ni), BF16)
    ones_row = jnp.ones((SUBLANES, ni), BF16)
    ones_rowj = jnp.ones((SUBLANES, nj), BF16)
    u_incl = _tri(ni, "row_le_col")
    uj_strict = _tri(nj, "row_lt_col")
    lj_strict = _tri(nj, "col_lt_row")
    l_incl = _tri(ni, "col_le_row")
    bdot = lambda a, b: jnp.dot(a, b, preferred_element_type=F32)

    c_lane = lax.broadcasted_iota(I32, (nj, cap), 1).astype(F32)
    c_row = lax.broadcasted_iota(I32, (SUBLANES, cap), 1).astype(F32)
    j_sub = lax.broadcasted_iota(I32, (nj, cap), 0).astype(F32)

    def slots(e, carry):
        eq = eq_scr[e]
        eqb = eq.astype(BF16)
        tie_rank = bdot(lj_strict, bdot(eqb, ones_sq).astype(BF16)) + bdot(eqb, u_incl)
        take = jnp.where(tie_rank <= need_scr[e][0:1, :], eq, 0.0)
        sel = jnp.maximum(gt_scr[e], take)
        selb = sel.astype(BF16)
        lc = bdot(selb, u_incl)
        lc_t = _nt_dot(l_incl, selb)
        n_rep = bdot(selb, ones_sq)
        s_rep = bdot(lj_strict, n_rep.astype(BF16)) + n_rep
        n_lane = _nt_dot(ones_row, selb)
        pex_lane = bdot(n_lane.astype(BF16), uj_strict)
        s_wide = jnp.concatenate([s_rep] * (cap // ni), axis=1) if cap > ni else s_rep[:, :cap]
        jc = bdot(ones_rowj, jnp.where(s_wide <= c_lane, 1.0, 0.0).astype(BF16))
        onehot = jnp.where(j_sub == jnp.broadcast_to(jc[0:1], (nj, cap)), 1.0, 0.0).astype(BF16)
        phi, plo = _split128(pex_lane)
        r = c_row - (bdot(phi.astype(BF16), onehot) * LANES + bdot(plo.astype(BF16), onehot))
        lcs = bdot(lc_t.astype(BF16), onehot)
        ic = bdot(ones_row, jnp.where(lcs <= jnp.broadcast_to(r[0:1], (ni, cap)), 1.0, 0.0).astype(BF16))
        idx_ref[e] = (jc * ni + ic).astype(I32)
        pos_ref[e] = pex_lane.astype(I32)
        rank_ref[e] = jnp.where(sel > 0.0, lc - 1.0, -1.0)
        return carry

    lax.fori_loop(0, ne, slots, 0)


def _routing(aff, *, cap):
    b, ne, s = aff.shape
    nj = s // LANES
    aff4 = aff.reshape(b, ne, nj, LANES)
    per_b = lambda bi: (bi, 0, 0, 0)
    idx, pos, rank = pl.pallas_call(
        functools.partial(_routing_kernel, cap=cap),
        grid=(b,),
        in_specs=[pl.BlockSpec((None, ne, nj, LANES), per_b)],
        out_specs=[pl.BlockSpec((None, ne, SUBLANES, cap), per_b),
                   pl.BlockSpec((None, ne, SUBLANES, nj), per_b),
                   pl.BlockSpec((None, ne, nj, LANES), per_b)],
        out_shape=[jax.ShapeDtypeStruct((b, ne, SUBLANES, cap), I32),
                   jax.ShapeDtypeStruct((b, ne, SUBLANES, nj), I32),
                   jax.ShapeDtypeStruct((b, ne, nj, LANES), F32)],
        scratch_shapes=[pltpu.VMEM((ne, nj, LANES), F32)] * 2 + [pltpu.VMEM((ne, SUBLANES, LANES), F32)],
        compiler_params=_cparams(("parallel",)),
        name="expert_choice_routing",
    )(aff4)
    return idx[:, :, 0, :], pos[:, :, 0, :], rank


def _ffn_kernel(idx_cur, idx_nxt, h_hbm, wg_ref, wu_ref, wd_ref, y_ref, xs, xsb, gsem,
                *, cap, n_pairs, n_ftiles, per_step, m_chunk):
    p = pl.program_id(0)
    j = pl.program_id(1)
    n_slots = n_ftiles * per_step
    pitch = xs.shape[0] // n_slots
    dt = pitch - 1
    last_pair = n_pairs - 1
    b_cur = p // N_EXPERTS
    b_nxt = jnp.minimum(p + 1, last_pair) // N_EXPERTS

    def gather(ref, bq, c):
        cc = jnp.minimum(c, cap - 1)
        tok = ref[cc >> LANE_BITS, cc & (LANES - 1)]
        dst = xs.at[pl.ds(c * pitch, dt)]
        return pltpu.make_async_copy(h_hbm.at[bq, tok], dst, gsem)

    def wait_gathers():
        tiles = xs.at[pl.ds(0, n_slots * dt)]
        pltpu.make_async_copy(tiles, tiles, gsem).wait()

    @pl.when((p == 0) & (j == 0))
    def _():
        def start(c, carry):
            gather(idx_cur, b_cur, c).start()
            return carry

        lax.fori_loop(0, n_slots, start, 0)

    n_groups = cap // m_chunk
    per_group = per_step // n_groups

    def ff_tile(first, first_slot, batches):
        wg = wg_ref[...].astype(BF16)
        wu = wu_ref[...].astype(BF16)
        wd = wd_ref[...].astype(BF16)
        for m in range(n_groups):
            for t in range(batches * per_group):
                gather(idx_nxt, b_nxt, first_slot + (m * batches * per_group + t)).start(priority=t % 2)
            rows = pl.ds(m * m_chunk, m_chunk)
            x = xsb[rows, :]
            hg = jnp.dot(x, wg, preferred_element_type=F32)
            hu = jnp.dot(x, wu, preferred_element_type=F32)
            act = (hg * _sigmoid(hg)) * hu
            part = jnp.dot(act.astype(BF16), wd, preferred_element_type=F32)
            if first:
                y_ref[rows, :] = part
            else:
                y_ref[rows, :] += part

    @pl.when(j == 0)
    def _():
        wait_gathers()
        for c in range(dt):
            xsb[:, c * LANES:(c + 1) * LANES] = xs[pl.ds(c, cap, stride=pitch), :].astype(BF16)
        ff_tile(True, 0, 0)

    @pl.when(j == 1)
    def _():
        ff_tile(False, 0, 2)

    @pl.when(j > 1)
    def _():
        ff_tile(False, j * per_step, 1)

    @pl.when((p == last_pair) & (j == n_ftiles - 1))
    def _():
        wait_gathers()


def _expert_ffn(h, idx, wg, wu, wd, layer, *, cap, tf, m_chunk):
    b, s, dt, _ = h.shape
    d = dt * LANES
    ne, dff = wg.shape[1], wg.shape[3]
    n_ftiles = dff // tf
    n_pairs = b * ne
    n_groups = cap // m_chunk
    per_step = -(-cap // (n_ftiles * n_groups)) * n_groups
    n_slots = n_ftiles * per_step
    tab = idx.reshape(n_pairs, cap // LANES, LANES)
    tab_spec = lambda f: pl.BlockSpec((None, cap // LANES, LANES), lambda p, j: (f(p), 0, 0),
                                      memory_space=pltpu.SMEM)
    return pl.pallas_call(
        functools.partial(_ffn_kernel, cap=cap, n_pairs=n_pairs, n_ftiles=n_ftiles, per_step=per_step,
                          m_chunk=m_chunk),
        grid=(n_pairs, n_ftiles),
        in_specs=[tab_spec(lambda p: p),
                  tab_spec(lambda p: jnp.minimum(p + 1, n_pairs - 1)),
                  pl.BlockSpec(memory_space=pl.ANY),
                  pl.BlockSpec((None, None, d, tf), lambda p, j: (layer, p % ne, 0, j)),
                  pl.BlockSpec((None, None, d, tf), lambda p, j: (layer, p % ne, 0, j)),
                  pl.BlockSpec((None, None, tf, d), lambda p, j: (layer, p % ne, j, 0))],
        out_specs=pl.BlockSpec((None, None, cap, d), lambda p, j: (p // ne, p % ne, 0, 0)),
        out_shape=jax.ShapeDtypeStruct((b, ne, cap, d), F32),
        scratch_shapes=[pltpu.VMEM((n_slots * (dt + 1), LANES), F32), pltpu.VMEM((cap, d), BF16),
                        pltpu.SemaphoreType.DMA(())],
        compiler_params=_cparams(("arbitrary", "arbitrary")),
        name="expert_swiglu",
    )(tab, tab, h, wg, wu, wd)


def _combine_kernel(pos_ref, rank_ref, aff_ref, x_ref, g_ref, y_hbm, o_ref, ybuf, acc, sem,
                    *, cap, nj, n_tiles, final_norm):
    b = pl.program_id(0)
    j = pl.program_id(1)
    t = b * nj + j
    ne = N_EXPERTS
    ring = COMBINE_AHEAD + 1
    assert ring & (ring - 1) == 0
    in_ring = lambda k: k & (ring - 1)

    def nominal(bq, e, jq, r):
        return ((pos_ref[bq * ne + e, jq] >> 3) << 3) + r * EWIN

    def window(bq, e, jq, r):
        return pl.multiple_of(jnp.minimum(nominal(bq, e, jq, r), cap - EWIN), SUBLANES)

    def start_fetch(ahead, r, slot):
        jq = j + ahead
        wrap = (jq >= nj).astype(I32)
        bq = b + wrap
        jq = jq - wrap * nj
        past = bq * nj + jq >= n_tiles
        bq = jnp.where(past, n_tiles // nj - 1, bq)
        jq = jnp.where(past, nj - 1, jq)
        for e in range(ne):
            pltpu.make_async_copy(y_hbm.at[bq, e, pl.ds(window(bq, e, jq, r), EWIN)],
                                  ybuf.at[slot, pl.ds(e * EWIN, EWIN)], sem.at[slot]).start(priority=e % 2)

    def wait_fetch(slot):
        pltpu.make_async_copy(ybuf.at[slot], ybuf.at[slot], sem.at[slot]).wait()

    rk = rank_ref[...]
    af = aff_ref[...]
    row = lax.broadcasted_iota(I32, (EWIN, LANES), 0).astype(F32)

    def add_round(r, slot):
        parts = []
        for e in range(ne):
            p0 = pos_ref[b * ne + e, j]
            local = row + (window(b, e, j, r) - p0).astype(F32)
            want = jnp.where(local >= jnp.maximum(nominal(b, e, j, r) - p0, 0).astype(F32), local, -2.0)
            parts.append(jnp.where(rk[e:e + 1, :] == want, af[e:e + 1, :], 0.0))
        gate_t = jnp.concatenate(parts, axis=0)
        onehot = jnp.where(gate_t != 0.0, 1.0, 0.0).T.astype(BF16)
        row_gate = jnp.sum(gate_t, axis=1, keepdims=True)
        scaled = jnp.where(row_gate != 0.0, ybuf[slot] * row_gate, 0.0)
        hi = scaled.astype(BF16)
        lo = (scaled - hi.astype(F32)).astype(BF16)
        return jnp.dot(onehot, hi, preferred_element_type=F32) + jnp.dot(onehot, lo, preferred_element_type=F32)

    @pl.when(t == 0)
    def _():
        for ahead in range(COMBINE_AHEAD):
            start_fetch(ahead, 0, ahead)

    start_fetch(COMBINE_AHEAD, 0, in_ring(t + COMBINE_AHEAD))
    slot0 = in_ring(t)
    wait_fetch(slot0)
    acc[...] = add_round(0, slot0)

    span = jnp.int32(1)
    for e in range(ne):
        span = jnp.maximum(span, pos_ref[b * ne + e, j + 1] - nominal(b, e, j, 0))
    n_rounds = (span + EWIN - 1) // EWIN

    def extra_round(r, carry):
        start_fetch(0, r, ring)
        wait_fetch(ring)
        acc[...] += add_round(r, ring)
        return carry

    lax.fori_loop(1, n_rounds, extra_round, 0)

    @pl.when(t == n_tiles - 1)
    def _():
        for ahead in range(1, COMBINE_AHEAD + 1):
            wait_fetch(in_ring(t + ahead))

    y = x_ref[...] + acc[...]
    if final_norm:
        ms = jnp.mean(y * y, axis=-1, keepdims=True)
        y = (y * lax.rsqrt(ms + EPS)) * g_ref[...]
    o_ref[...] = y


def _combine(x, y, pos, rank, aff, g_final, *, cap, final_norm):
    b, s, d = x.shape
    nj = s // LANES
    ne = N_EXPERTS
    pos_tab = jnp.concatenate([pos, jnp.full((b, ne, 1), cap, I32)], axis=2).reshape(b * ne, nj + 1)
    rank_t = jnp.transpose(rank, (0, 2, 1, 3))
    return pl.pallas_call(
        functools.partial(_combine_kernel, cap=cap, nj=nj, n_tiles=b * nj, final_norm=final_norm),
        grid_spec=pltpu.PrefetchScalarGridSpec(
            num_scalar_prefetch=1,
            grid=(b, nj),
            in_specs=[pl.BlockSpec((None, None, ne, LANES), lambda bi, j, tab: (bi, j, 0, 0)),
                      pl.BlockSpec((None, ne, LANES), lambda bi, j, tab: (bi, 0, j)),
                      pl.BlockSpec((None, LANES, d), lambda bi, j, tab: (bi, j, 0)),
                      pl.BlockSpec((1, d), lambda bi, j, tab: (0, 0)),
                      pl.BlockSpec(memory_space=pl.ANY)],
            out_specs=pl.BlockSpec((None, LANES, d), lambda bi, j, tab: (bi, j, 0)),
            scratch_shapes=[pltpu.VMEM((COMBINE_AHEAD + 2, ne * EWIN, d), F32), pltpu.VMEM((LANES, d), F32),
                            pltpu.SemaphoreType.DMA((COMBINE_AHEAD + 2,))]),
        out_shape=jax.ShapeDtypeStruct((b, s, d), F32),
        compiler_params=_cparams(("arbitrary", "arbitrary")),
        name="moe_combine",
    )(pos_tab, rank_t, aff, x, g_final.reshape(1, d), y)


def kernel(x, rel_bias, g_mix, w_in, attn_sink, w_attn_proj, w_fourier_proj, w_out, g_ffn, w_router,
           w_exp_gate, w_exp_up, w_exp_down, g_final):
    b, s, d = x.shape
    depth = g_mix.shape[0]
    cap = CAPACITY_FACTOR * s // N_EXPERTS
    bias = _bias_table(rel_bias)
    na = int(round(np.sqrt(s)))
    tables = _fourier_tables(na)
    for l in range(depth):
        q, k, v, vs, f, ga, gb = _inproj(x.reshape(b * s, d), g_mix[l], w_in[l].astype(BF16), tm=1024, chunk=256, na=na)
        shp = lambda a: a.reshape(b, s, a.shape[-1])
        attn = _attention(shp(q), shp(k), shp(v), shp(vs), attn_sink[l], bias, blocks_per_step=8)
        four = _fourier(f.reshape(b, s // 2, f.shape[-1]), tables, rows=8, rows2=16)
        xn, h, aff = _outproj(attn, four, shp(ga), shp(gb), x,
                              w_attn_proj[l].astype(BF16), w_fourier_proj[l].astype(BF16),
                              w_out[l].astype(BF16), g_ffn[l], w_router[l].T, tm=1024, chunk=256)
        idx, pos, rank = _routing(aff, cap=cap)
        y = _expert_ffn(h, idx, w_exp_gate, w_exp_up, w_exp_down, l, cap=cap, tf=256, m_chunk=min(cap, 512))
        x = _combine(xn, y, pos, rank, aff, g_final, cap=cap, final_norm=(l == depth - 1))
    return x
```

```python
import functools

import numpy as np
import jax
import jax.numpy as jnp
from jax import lax
from jax.experimental import pallas as pl
from jax.experimental.pallas import tpu as pltpu

F32 = jnp.float32
BF16 = jnp.bfloat16
I32 = jnp.int32

N_HEADS = 8
N_KV_HEADS = 2
HEAD_DIM = 64
ATTN_WIDTH = N_HEADS * HEAD_DIM
KV_WIDTH = N_KV_HEADS * HEAD_DIM
WINDOW = 128
BLOCK = 128
N_BUCKETS = 32
MAX_DISTANCE = 128
N_FOURIER_GROUPS = 4
FOURIER_GROUP_DIM = 128
FOURIER_WIDTH = N_FOURIER_GROUPS * FOURIER_GROUP_DIM
N_EXPERTS = 16
CAPACITY_FACTOR = 2
EPS = 1e-6
NEG_INF = -1e30

LANES = 128
LANE_BITS = 7
SUBLANES = 8
VMEM_LIMIT = 56 * 1024 * 1024
EWIN = 32
COMBINE_AHEAD = 3


def _cparams(sem):
    return pltpu.CompilerParams(dimension_semantics=sem, vmem_limit_bytes=VMEM_LIMIT)


def _nt_dot(a, b, **kw):
    return lax.dot_general(a, b, (((1,), (1,)), ((), ())), preferred_element_type=F32, **kw)


def _sigmoid(x):
    return 1.0 / (1.0 + jnp.exp(-x))


def _bf16_bits(y):
    return lax.bitcast_convert_type(y.astype(BF16).astype(F32), jnp.uint32)


def _t5_bucket(rel):
    half = N_BUCKETS // 2
    max_exact = half // 2
    ret = (rel > 0).astype(jnp.int32) * half
    n = jnp.abs(rel)
    nf = jnp.maximum(n, 1).astype(jnp.float32)
    large = max_exact + (jnp.log(nf / max_exact) / np.float32(np.log(MAX_DISTANCE / max_exact))
                         * (half - max_exact)).astype(jnp.int32)
    large = jnp.minimum(large, half - 1)
    return ret + jnp.where(n < max_exact, n, large)


def _bias_kernel(relb_ref, bucket_ref, o_ref):
    bk = bucket_ref[...]
    col = lax.broadcasted_iota(I32, bk.shape, 1)
    for h in range(N_HEADS):
        acc = jnp.full(bk.shape, NEG_INF, F32)
        for b in range(N_BUCKETS):
            acc = jnp.where(bk == b, relb_ref[b, h], acc)
        o_ref[0, h] = jnp.where(col >= BLOCK, acc, NEG_INF)
        o_ref[1, h] = acc
        o_ref[2, h] = jnp.where(col < 2 * BLOCK, acc, NEG_INF)


def _bias_table(rel_bias):
    q_loc = jnp.arange(BLOCK)
    k_loc = jnp.arange(3 * BLOCK) - BLOCK
    rel = k_loc[None, :] - q_loc[:, None]
    bucket = jnp.where(jnp.abs(rel) <= WINDOW, _t5_bucket(rel), -1).astype(I32)
    table = pl.pallas_call(
        _bias_kernel,
        out_shape=jax.ShapeDtypeStruct((3, N_HEADS, BLOCK, 3 * BLOCK), F32),
        in_specs=[pl.BlockSpec(memory_space=pltpu.SMEM),
                  pl.BlockSpec(memory_space=pltpu.VMEM)],
        out_specs=pl.BlockSpec(memory_space=pltpu.VMEM),
        name="rel_bias_table",
    )(rel_bias.astype(F32), bucket)
    return table.reshape(3, N_HEADS * BLOCK, 3 * BLOCK)


def _inproj_kernel(x_ref, g_ref, w_ref, q_ref, k_ref, v_ref, vs_ref, f_ref, ga_ref, gb_ref,
                   *, d_model, chunk, na):
    for r in range(x_ref.shape[0] // chunk):
        rs = pl.ds(r * chunk, chunk)
        x = x_ref[rs, :]
        ms = jnp.mean(x * x, axis=-1, keepdims=True)
        h = (x * lax.rsqrt(ms + EPS)) * g_ref[...]
        z = jnp.dot(h.astype(BF16), w_ref[...], preferred_element_type=F32)
        o = 0
        q_ref[rs, :] = (z[:, o:o + ATTN_WIDTH] * (HEAD_DIM ** -0.5)).astype(BF16)
        o += ATTN_WIDTH
        k_ref[rs, :] = z[:, o:o + KV_WIDTH].astype(BF16)
        o += KV_WIDTH
        v_ref[rs, :] = z[:, o:o + KV_WIDTH].astype(BF16)
        half = KV_WIDTH // 2
        vs_ref[rs, :] = jnp.concatenate([z[:, o + half:o + KV_WIDTH], z[:, o:o + half]], axis=1).astype(BF16)
        o += KV_WIDTH
        for m in range(chunk // (2 * na)):
            lo = z[2 * m * na:(2 * m + 1) * na, o:o + FOURIER_WIDTH]
            hi = z[(2 * m + 1) * na:(2 * m + 2) * na, o:o + FOURIER_WIDTH]
            f_ref[pl.ds(r * chunk // 2 + m * na, na), :] = (_bf16_bits(lo) >> 16) | _bf16_bits(hi)
        o += FOURIER_WIDTH
        ga_ref[rs, :] = _sigmoid(z[:, o:o + d_model]).astype(BF16)
        o += d_model
        gb_ref[rs, :] = _sigmoid(z[:, o:o + d_model]).astype(BF16)


def _inproj(x2, g, w_bf16, *, tm, chunk, na):
    n, d = x2.shape
    in_w = w_bf16.shape[1]
    assert chunk % (2 * na) == 0 and na % SUBLANES == 0
    widths = (ATTN_WIDTH, KV_WIDTH, KV_WIDTH, KV_WIDTH, FOURIER_WIDTH, d, d)
    dtypes = (BF16, BF16, BF16, BF16, jnp.uint32, BF16, BF16)
    halved = (False, False, False, False, True, False, False)
    row = lambda i: (i, 0)
    return pl.pallas_call(
        functools.partial(_inproj_kernel, d_model=d, chunk=chunk, na=na),
        grid=(n // tm,),
        in_specs=[pl.BlockSpec((tm, d), row),
                  pl.BlockSpec((1, d), lambda i: (0, 0)),
                  pl.BlockSpec((d, in_w), lambda i: (0, 0))],
        out_specs=[pl.BlockSpec((tm // 2 if hv else tm, w), row) for w, hv in zip(widths, halved)],
        out_shape=[jax.ShapeDtypeStruct((n // 2 if hv else n, w), dt)
                   for w, dt, hv in zip(widths, dtypes, halved)],
        compiler_params=_cparams(("parallel",)),
        name="inproj",
    )(x2, g.reshape(1, d), w_bf16)


def _attn_kernel(sink_ref, q_ref, kp_ref, ko_ref, kn_ref, vp_ref, vo_ref, vn_ref, wp_ref, wo_ref, wn_ref,
                 bias_ref, o_ref, *, n_blocks, blocks_per_step):
    i = pl.program_id(1)
    kcat = jnp.concatenate([kp_ref[...], ko_ref[...], kn_ref[...]], axis=0)
    vcat = jnp.concatenate([vp_ref[...], vo_ref[...], vn_ref[...]], axis=0)
    wcat = jnp.concatenate([wp_ref[...], wo_ref[...], wn_ref[...]], axis=0)
    group = N_HEADS // N_KV_HEADS
    head = lambda a, h: a[:, h * HEAD_DIM:(h + 1) * HEAD_DIM]
    sink = jnp.concatenate([jnp.full((BLOCK, LANES), sink_ref[h], F32) for h in range(N_HEADS)], axis=0)
    low_v = lax.broadcasted_iota(I32, vcat.shape, 1) < HEAD_DIM
    low_o = lax.broadcasted_iota(I32, (BLOCK, LANES), 1) < HEAD_DIM
    vx = {(kv, half): jnp.where(low_v if half == 0 else jnp.logical_not(low_v),
                                vcat if kv == half else wcat, jnp.ones_like(vcat))
          for kv in range(N_KV_HEADS) for half in range(2)}
    for t in range(blocks_per_step):
        blk = i * blocks_per_step + t
        kind = jnp.where(blk == 0, 0, jnp.where(blk == n_blocks - 1, 2, 1))
        kw = kcat[t * BLOCK:(t + 3) * BLOCK]
        qb = q_ref[t * BLOCK:(t + 1) * BLOCK, :]
        s = jnp.concatenate([_nt_dot(head(qb, h), head(kw, h // group)) for h in range(N_HEADS)], axis=0)
        s = s + bias_ref[kind]
        m = jnp.maximum(jnp.broadcast_to(jnp.max(s, axis=-1, keepdims=True), sink.shape), sink)
        p = jnp.exp(s - jnp.concatenate([m] * 3, axis=1)).astype(BF16)
        sink_term = jnp.exp(sink - m)
        for pair in range(N_HEADS // 2):
            halves = []
            for half in range(2):
                h = 2 * pair + half
                rows = slice(h * BLOCK, (h + 1) * BLOCK)
                out = jnp.dot(p[rows], vx[h // group, half][t * BLOCK:(t + 3) * BLOCK],
                              preferred_element_type=F32)
                denom = pltpu.roll(out, HEAD_DIM, axis=1) + sink_term[rows]
                halves.append(out * (1.0 / denom))
            tile = jnp.where(low_o, halves[0], halves[1])
            o_ref[t * BLOCK:(t + 1) * BLOCK, pair * LANES:(pair + 1) * LANES] = tile.astype(o_ref.dtype)


def _attention(q, k, v, v_swapped, sink, bias, *, blocks_per_step):
    b, s, _ = q.shape
    assert N_KV_HEADS == 2 and KV_WIDTH == LANES and N_HEADS % 2 == 0
    n_blocks = s // BLOCK
    tq = blocks_per_step * BLOCK
    prev = lambda bi, i: (bi, jnp.maximum(i * blocks_per_step - 1, 0), 0)
    own = lambda bi, i: (bi, i, 0)
    nxt = lambda bi, i: (bi, jnp.minimum((i + 1) * blocks_per_step, n_blocks - 1), 0)
    kv_specs = [pl.BlockSpec((None, BLOCK, KV_WIDTH), prev),
                pl.BlockSpec((None, tq, KV_WIDTH), own),
                pl.BlockSpec((None, BLOCK, KV_WIDTH), nxt)]
    return pl.pallas_call(
        functools.partial(_attn_kernel, n_blocks=n_blocks, blocks_per_step=blocks_per_step),
        grid=(b, s // tq),
        in_specs=[pl.BlockSpec(memory_space=pltpu.SMEM),
                  pl.BlockSpec((None, tq, ATTN_WIDTH), own)] + kv_specs * 3
                 + [pl.BlockSpec((3, N_HEADS * BLOCK, 3 * BLOCK), lambda bi, i: (0, 0, 0))],
        out_specs=pl.BlockSpec((None, tq, ATTN_WIDTH), own),
        out_shape=jax.ShapeDtypeStruct((b, s, ATTN_WIDTH), BF16),
        compiler_params=_cparams(("parallel", "parallel")),
        name="window_attention",
    )(sink.astype(F32), q, k, k, k, v, v, v, v_swapped, v_swapped, v_swapped, bias)


def _dft1_kernel(x_ref, a_ref, t_ref, x2, *, rows):
    a = a_ref[...]
    na = a.shape[0] // 2
    n_slabs = x2.shape[0]
    lanes = lambda c: slice(c * LANES, (c + 1) * LANES)
    for c in range(n_slabs):
        x2[c] = x_ref[:, :, lanes(c)].reshape(x2.shape[1:])
    for jj in range(rows):
        strided = pl.ds(jj, na // 2, stride=rows)
        xj = pltpu.bitcast(jnp.concatenate([x2[c, strided, :] for c in range(n_slabs)], axis=1), BF16)
        t = jnp.dot(a, xj, preferred_element_type=F32)
        t_ref[:, jj, :] = pltpu.bitcast(t.astype(BF16), jnp.uint32)


def _dft2_kernel(t_ref, gr_ref, gi_ref, cs_ref, o_ref, o2, *, rows):
    cs = cs_ref[...]
    na = o_ref.shape[0]
    pairs = rows // 2
    for aa in range(rows):
        t = pltpu.bitcast(t_ref[aa], BF16)
        zr = jnp.dot(gr_ref[aa], t, preferred_element_type=F32)
        zi = jnp.dot(gi_ref[aa], t, preferred_element_type=F32)
        for g in range(N_FOURIER_GROUPS):
            sl = slice(g * FOURIER_GROUP_DIM, (g + 1) * FOURIER_GROUP_DIM)
            z = jnp.concatenate([zr[:, sl], zi[:, sl]], axis=1).astype(BF16)
            y = jnp.dot(z, cs, preferred_element_type=F32)
            rows_q = pl.ds(aa // 2, na, stride=pairs)
            if aa % 2 == 0:
                o2[g, rows_q, :] = _bf16_bits(y) >> 16
            else:
                o2[g, rows_q, :] = o2[g, rows_q, :] | _bf16_bits(y)
    for g in range(N_FOURIER_GROUPS):
        sl = slice(g * FOURIER_GROUP_DIM, (g + 1) * FOURIER_GROUP_DIM)
        o_ref[:, :, sl] = o2[g].reshape(na, pairs, FOURIER_GROUP_DIM)


def _fourier_tables(na):
    def interleave(even, odd, axis):
        both = np.stack([even, odd], axis=axis + 1)
        shape = list(even.shape)
        shape[axis] *= 2
        return jnp.asarray(both.reshape(shape).astype(BF16))

    n = na * na
    k = np.arange(na, dtype=np.float64)
    ang = 2.0 * np.pi * np.outer(k, k) / na
    a1 = interleave(np.cos(ang), -np.sin(ang), 0)
    ka = np.arange(na, dtype=np.float64)[:, None, None]
    kb = np.arange(na, dtype=np.float64)[None, :, None]
    nl = np.arange(na, dtype=np.float64)[None, None, :]
    th = 2.0 * np.pi * ((nl * (ka + na * kb)) % n) / n
    gr, gi = np.cos(th), -np.sin(th)
    g_re = interleave(gr, -gi, 2)
    g_im = interleave(gi, gr, 2)
    kc = np.arange(FOURIER_GROUP_DIM, dtype=np.float64)
    angc = 2.0 * np.pi * np.outer(kc, kc) / FOURIER_GROUP_DIM
    scale = 1.0 / np.sqrt(float(n) * FOURIER_GROUP_DIM)
    cs = np.concatenate([np.cos(angc), np.sin(angc)], axis=0) * scale
    return a1, g_re, g_im, jnp.asarray(cs.astype(BF16))


def _fourier(f, tables, *, rows, rows2):
    b, s, w = f.shape[0], 2 * f.shape[1], f.shape[2]
    na = int(round(np.sqrt(s)))
    assert na * na == s and na % rows == 0 and na % rows2 == 0 and rows2 % (2 * SUBLANES) == 0
    a1, g_re, g_im, cs = tables
    x4 = f.reshape(b, na // 2, na, w)
    const2 = lambda bi, j: (0, 0)
    t = pl.pallas_call(
        functools.partial(_dft1_kernel, rows=rows),
        grid=(b, na // rows),
        in_specs=[pl.BlockSpec((None, na // 2, rows, w), lambda bi, j: (bi, 0, j, 0)),
                  pl.BlockSpec(a1.shape, const2)],
        out_specs=pl.BlockSpec((None, na, rows, w), lambda bi, j: (bi, 0, j, 0)),
        out_shape=jax.ShapeDtypeStruct((b, na, na, w), jnp.uint32),
        scratch_shapes=[pltpu.VMEM((w // LANES, na // 2 * rows, LANES), jnp.uint32)],
        compiler_params=_cparams(("parallel", "parallel")),
        name="seq_dft_stage1",
    )(x4, a1)
    y = pl.pallas_call(
        functools.partial(_dft2_kernel, rows=rows2),
        grid=(b, na // rows2),
        in_specs=[pl.BlockSpec((None, rows2, na, w), lambda bi, a: (bi, a, 0, 0))]
                 + [pl.BlockSpec((rows2,) + g_re.shape[1:], lambda bi, a: (a, 0, 0))] * 2
                 + [pl.BlockSpec(cs.shape, const2)],
        out_specs=pl.BlockSpec((None, na, rows2 // 2, w), lambda bi, a: (bi, 0, a, 0)),
        out_shape=jax.ShapeDtypeStruct((b, na, na // 2, w), jnp.uint32),
        scratch_shapes=[pltpu.VMEM((N_FOURIER_GROUPS, na * rows2 // 2, FOURIER_GROUP_DIM), jnp.uint32)],
        compiler_params=_cparams(("parallel", "parallel")),
        name="seq_dft_stage2",
    )(t, g_re, g_im, cs)
    return y.reshape(b, s // 2, w)


def _outproj_kernel(a_ref, f_ref, ga_ref, gb_ref, x_ref, wa_ref, wf_ref, wo_ref, g_ref, wr_ref,
                    xn_ref, h_ref, aff_ref, *, chunk):
    for r in range(x_ref.shape[0] // chunk):
        rs = pl.ds(r * chunk, chunk)
        a = jnp.dot(a_ref[rs, :], wa_ref[...], preferred_element_type=F32)
        four = pltpu.bitcast(f_ref[pl.ds(r * chunk // 2, chunk // 2), :], BF16)
        fo = jnp.dot(four, wf_ref[...], preferred_element_type=F32)
        merged = ga_ref[rs, :].astype(F32) * a + gb_ref[rs, :].astype(F32) * fo
        xn = x_ref[rs, :] + jnp.dot(merged.astype(BF16), wo_ref[...], preferred_element_type=F32)
        xn_ref[rs, :] = xn
        ms = jnp.mean(xn * xn, axis=-1, keepdims=True)
        h = (xn * lax.rsqrt(ms + EPS)) * g_ref[...]
        dt = h.shape[1] // LANES
        for c in range(dt):
            h_ref[pl.ds(r * chunk * dt + c, chunk, stride=dt), :] = h[:, c * LANES:(c + 1) * LANES]
        logits = _nt_dot(wr_ref[...], h.astype(BF16))
        e = jnp.exp(logits - jnp.max(logits, axis=0, keepdims=True))
        aff_ref[:, rs] = e / jnp.sum(e, axis=0, keepdims=True)


def _outproj(attn, four, ga, gb, x, wa, wf, wo, g, wr_t, *, tm, chunk):
    b, s, d = x.shape
    dt = d // LANES
    tok = lambda bi, i: (bi, i, 0)
    const = lambda bi, i: (0, 0)
    xn, h, aff = pl.pallas_call(
        functools.partial(_outproj_kernel, chunk=chunk),
        grid=(b, s // tm),
        in_specs=[pl.BlockSpec((None, tm, ATTN_WIDTH), tok),
                  pl.BlockSpec((None, tm // 2, FOURIER_WIDTH), tok),
                  pl.BlockSpec((None, tm, d), tok),
                  pl.BlockSpec((None, tm, d), tok),
                  pl.BlockSpec((None, tm, d), tok),
                  pl.BlockSpec((ATTN_WIDTH, d), const),
                  pl.BlockSpec((FOURIER_WIDTH, d), const),
                  pl.BlockSpec((d, d), const),
                  pl.BlockSpec((1, d), const),
                  pl.BlockSpec((N_EXPERTS, d), const)],
        out_specs=[pl.BlockSpec((None, tm, d), tok),
                   pl.BlockSpec((None, tm * dt, LANES), tok),
                   pl.BlockSpec((None, N_EXPERTS, tm), lambda bi, i: (bi, 0, i))],
        out_shape=[jax.ShapeDtypeStruct((b, s, d), F32),
                   jax.ShapeDtypeStruct((b, s * dt, LANES), F32),
                   jax.ShapeDtypeStruct((b, N_EXPERTS, s), F32)],
        compiler_params=_cparams(("parallel", "parallel")),
        name="outproj_router",
    )(attn, four, ga, gb, x, wa, wf, wo, g.reshape(1, d), wr_t.astype(BF16))
    return xn, h.reshape(b, s, dt, LANES), aff


def _tri(n, kind):
    r = lax.broadcasted_iota(I32, (n, n), 0)
    c = lax.broadcasted_iota(I32, (n, n), 1)
    cond = {"row_le_col": r <= c, "row_lt_col": r < c, "col_le_row": c <= r, "col_lt_row": c < r}[kind]
    return jnp.where(cond, 1.0, 0.0).astype(BF16)


def _split128(v):
    hi = jnp.floor(v * (1.0 / LANES))
    return hi, v - hi * LANES


def _routing_kernel(aff_ref, idx_ref, pos_ref, rank_ref, gt_scr, eq_scr, need_scr, *, cap):
    ne, nj, ni = aff_ref.shape
    bits = lax.bitcast_convert_type(aff_ref[...], I32)

    def bisect(it, thr):
        cand = thr | jnp.left_shift(jnp.int32(1), 30 - it)
        cnt = jnp.sum(jnp.where(bits >= cand, 1.0, 0.0), axis=(1, 2), keepdims=True)
        return jnp.where(cnt >= cap, cand, thr)

    thr = lax.fori_loop(0, 31, bisect, jnp.zeros((ne, 1, 1), I32))
    gt = jnp.where(bits > thr, 1.0, 0.0)
    gt_scr[...] = gt
    eq_scr[...] = jnp.where(bits == thr, 1.0, 0.0)
    need = cap - jnp.sum(gt, axis=(1, 2), keepdims=True)
    need_scr[...] = jnp.broadcast_to(need, (ne, SUBLANES, ni))

    ones_sq = jnp.ones((ni, ni), BF16)
    ones_row = jnp.ones((SUBLANES, ni), BF16)
    ones_rowj = jnp.ones((SUBLANES, nj), BF16)
    u_incl = _tri(ni, "row_le_col")
    uj_strict = _tri(nj, "row_lt_col")
    lj_strict = _tri(nj, "col_lt_row")
    l_incl = _tri(ni, "col_le_row")
    bdot = lambda a, b: jnp.dot(a, b, preferred_element_type=F32)

    c_lane = lax.broadcasted_iota(I32, (nj, cap), 1).astype(F32)
    c_row = lax.broadcasted_iota(I32, (SUBLANES, cap), 1).astype(F32)
    j_sub = lax.broadcasted_iota(I32, (nj, cap), 0).astype(F32)

    def slots(e, carry):
        eq = eq_scr[e]
        eqb = eq.astype(BF16)
        tie_rank = bdot(lj_strict, bdot(eqb, ones_sq).astype(BF16)) + bdot(eqb, u_incl)
        take = jnp.where(tie_rank <= need_scr[e][0:1, :], eq, 0.0)
        sel = jnp.maximum(gt_scr[e], take)
        selb = sel.astype(BF16)
        lc = bdot(selb, u_incl)
        lc_t = _nt_dot(l_incl, selb)
        n_rep = bdot(selb, ones_sq)
        s_rep = bdot(lj_strict, n_rep.astype(BF16)) + n_rep
        n_lane = _nt_dot(ones_row, selb)
        pex_lane = bdot(n_lane.astype(BF16), uj_strict)
        s_wide = jnp.concatenate([s_rep] * (cap // ni), axis=1) if cap > ni else s_rep[:, :cap]
        jc = bdot(ones_rowj, jnp.where(s_wide <= c_lane, 1.0, 0.0).astype(BF16))
        onehot = jnp.where(j_sub == jnp.broadcast_to(jc[0:1], (nj, cap)), 1.0, 0.0).astype(BF16)
        phi, plo = _split128(pex_lane)
        r = c_row - (bdot(phi.astype(BF16), onehot) * LANES + bdot(plo.astype(BF16), onehot))
        lcs = bdot(lc_t.astype(BF16), onehot)
        ic = bdot(ones_row, jnp.where(lcs <= jnp.broadcast_to(r[0:1], (ni, cap)), 1.0, 0.0).astype(BF16))
        idx_ref[e] = (jc * ni + ic).astype(I32)
        pos_ref[e] = pex_lane.astype(I32)
        rank_ref[e] = jnp.where(sel > 0.0, lc - 1.0, -1.0)
        return carry

    lax.fori_loop(0, ne, slots, 0, unroll=2)


def _routing(aff, *, cap):
    b, ne, s = aff.shape
    nj = s // LANES
    aff4 = aff.reshape(b, ne, nj, LANES)
    per_b = lambda bi: (bi, 0, 0, 0)
    idx, pos, rank = pl.pallas_call(
        functools.partial(_routing_kernel, cap=cap),
        grid=(b,),
        in_specs=[pl.BlockSpec((None, ne, nj, LANES), per_b)],
        out_specs=[pl.BlockSpec((None, ne, SUBLANES, cap), per_b),
                   pl.BlockSpec((None, ne, SUBLANES, nj), per_b),
                   pl.BlockSpec((None, ne, nj, LANES), per_b)],
        out_shape=[jax.ShapeDtypeStruct((b, ne, SUBLANES, cap), I32),
                   jax.ShapeDtypeStruct((b, ne, SUBLANES, nj), I32),
                   jax.ShapeDtypeStruct((b, ne, nj, LANES), F32)],
        scratch_shapes=[pltpu.VMEM((ne, nj, LANES), F32)] * 2 + [pltpu.VMEM((ne, SUBLANES, LANES), F32)],
        compiler_params=_cparams(("parallel",)),
        name="expert_choice_routing",
    )(aff4)
    return idx[:, :, 0, :], pos[:, :, 0, :], rank


def _ffn_kernel(idx_cur, idx_nxt, h_hbm, wg_ref, wu_ref, wd_ref, y_ref, xs, xsb, gsem,
                *, cap, n_pairs, n_ftiles, per_step, m_chunk):
    p = pl.program_id(0)
    j = pl.program_id(1)
    n_slots = n_ftiles * per_step
    pitch = xs.shape[0] // n_slots
    dt = pitch - 1
    last_pair = n_pairs - 1
    b_cur = p // N_EXPERTS
    b_nxt = jnp.minimum(p + 1, last_pair) // N_EXPERTS

    def gather(ref, bq, c):
        cc = jnp.minimum(c, cap - 1)
        tok = ref[cc >> LANE_BITS, cc & (LANES - 1)]
        dst = xs.at[pl.ds(c * pitch, dt)]
        return pltpu.make_async_copy(h_hbm.at[bq, tok], dst, gsem)

    def wait_gathers():
        tiles = xs.at[pl.ds(0, n_slots * dt)]
        pltpu.make_async_copy(tiles, tiles, gsem).wait()

    @pl.when((p == 0) & (j == 0))
    def _():
        def start(c, carry):
            gather(idx_cur, b_cur, c).start()
            return carry

        lax.fori_loop(0, n_slots, start, 0)

    n_groups = cap // m_chunk
    per_group = per_step // n_groups

    def ff_tile(first, first_slot, batches):
        wg = wg_ref[...].astype(BF16)
        wu = wu_ref[...].astype(BF16)
        wd = wd_ref[...].astype(BF16)
        for m in range(n_groups):
            for t in range(batches * per_group):
                gather(idx_nxt, b_nxt, first_slot + (m * batches * per_group + t)).start(priority=t % 2)
            rows = pl.ds(m * m_chunk, m_chunk)
            x = xsb[rows, :]
            hg = jnp.dot(x, wg, preferred_element_type=F32)
            hu = jnp.dot(x, wu, preferred_element_type=F32)
            act = (hg * _sigmoid(hg)) * hu
            part = jnp.dot(act.astype(BF16), wd, preferred_element_type=F32)
            if first:
                y_ref[rows, :] = part
            else:
                y_ref[rows, :] += part

    @pl.when(j == 0)
    def _():
        wait_gathers()
        for c in range(dt):
            xsb[:, c * LANES:(c + 1) * LANES] = xs[pl.ds(c, cap, stride=pitch), :].astype(BF16)
        ff_tile(True, 0, 0)

    @pl.when(j == 1)
    def _():
        ff_tile(False, 0, 2)

    @pl.when(j > 1)
    def _():
        ff_tile(False, j * per_step, 1)

    @pl.when((p == last_pair) & (j == n_ftiles - 1))
    def _():
        wait_gathers()


def _expert_ffn(h, idx, wg, wu, wd, layer, *, cap, tf, m_chunk):
    b, s, dt, _ = h.shape
    d = dt * LANES
    ne, dff = wg.shape[1], wg.shape[3]
    n_ftiles = dff // tf
    n_pairs = b * ne
    n_groups = cap // m_chunk
    per_step = -(-cap // (n_ftiles * n_groups)) * n_groups
    n_slots = n_ftiles * per_step
    tab = idx.reshape(n_pairs, cap // LANES, LANES)
    tab_spec = lambda f: pl.BlockSpec((None, cap // LANES, LANES), lambda p, j: (f(p), 0, 0),
                                      memory_space=pltpu.SMEM)
    return pl.pallas_call(
        functools.partial(_ffn_kernel, cap=cap, n_pairs=n_pairs, n_ftiles=n_ftiles, per_step=per_step,
                          m_chunk=m_chunk),
        grid=(n_pairs, n_ftiles),
        in_specs=[tab_spec(lambda p: p),
                  tab_spec(lambda p: jnp.minimum(p + 1, n_pairs - 1)),
                  pl.BlockSpec(memory_space=pl.ANY),
                  pl.BlockSpec((None, None, d, tf), lambda p, j: (layer, p % ne, 0, j)),
                  pl.BlockSpec((None, None, d, tf), lambda p, j: (layer, p % ne, 0, j)),
                  pl.BlockSpec((None, None, tf, d), lambda p, j: (layer, p % ne, j, 0))],
        out_specs=pl.BlockSpec((None, None, cap, d), lambda p, j: (p // ne, p % ne, 0, 0)),
        out_shape=jax.ShapeDtypeStruct((b, ne, cap, d), F32),
        scratch_shapes=[pltpu.VMEM((n_slots * (dt + 1), LANES), F32), pltpu.VMEM((cap, d), BF16),
                        pltpu.SemaphoreType.DMA(())],
        compiler_params=_cparams(("arbitrary", "arbitrary")),
        name="expert_swiglu",
    )(tab, tab, h, wg, wu, wd)


def _combine_kernel(pos_ref, rank_ref, aff_ref, x_ref, g_ref, y_hbm, o_ref, ybuf, acc, sem,
                    *, cap, nj, n_tiles, final_norm):
    b = pl.program_id(0)
    j = pl.program_id(1)
    t = b * nj + j
    ne = N_EXPERTS
    ring = COMBINE_AHEAD + 1
    assert ring & (ring - 1) == 0
    in_ring = lambda k: k & (ring - 1)

    def nominal(bq, e, jq, r):
        return ((pos_ref[bq * ne + e, jq] >> 3) << 3) + r * EWIN

    def window(bq, e, jq, r):
        return pl.multiple_of(jnp.minimum(nominal(bq, e, jq, r), cap - EWIN), SUBLANES)

    def start_fetch(ahead, r, slot):
        jq = j + ahead
        wrap = (jq >= nj).astype(I32)
        bq = b + wrap
        jq = jq - wrap * nj
        past = bq * nj + jq >= n_tiles
        bq = jnp.where(past, n_tiles // nj - 1, bq)
        jq = jnp.where(past, nj - 1, jq)
        for e in range(ne):
            pltpu.make_async_copy(y_hbm.at[bq, e, pl.ds(window(bq, e, jq, r), EWIN)],
                                  ybuf.at[slot, pl.ds(e * EWIN, EWIN)], sem.at[slot]).start(priority=e % 2)

    def wait_fetch(slot):
        pltpu.make_async_copy(ybuf.at[slot], ybuf.at[slot], sem.at[slot]).wait()

    rk = rank_ref[...]
    af = aff_ref[...]
    row = lax.broadcasted_iota(I32, (EWIN, LANES), 0).astype(F32)

    def add_round(r, slot):
        parts = []
        for e in range(ne):
            p0 = pos_ref[b * ne + e, j]
            local = row + (window(b, e, j, r) - p0).astype(F32)
            want = jnp.where(local >= jnp.maximum(nominal(b, e, j, r) - p0, 0).astype(F32), local, -2.0)
            parts.append(jnp.where(rk[e:e + 1, :] == want, af[e:e + 1, :], 0.0))
        gate_t = jnp.concatenate(parts, axis=0)
        onehot = jnp.where(gate_t != 0.0, 1.0, 0.0).T.astype(BF16)
        row_gate = jnp.sum(gate_t, axis=1, keepdims=True)
        scaled = jnp.where(row_gate != 0.0, ybuf[slot] * row_gate, 0.0)
        hi = scaled.astype(BF16)
        lo = (scaled - hi.astype(F32)).astype(BF16)
        return jnp.dot(onehot, hi, preferred_element_type=F32) + jnp.dot(onehot, lo, preferred_element_type=F32)

    @pl.when(t == 0)
    def _():
        for ahead in range(COMBINE_AHEAD):
            start_fetch(ahead, 0, ahead)

    start_fetch(COMBINE_AHEAD, 0, in_ring(t + COMBINE_AHEAD))
    slot0 = in_ring(t)
    wait_fetch(slot0)
    acc[...] = add_round(0, slot0)

    span = jnp.int32(1)
    for e in range(ne):
        span = jnp.maximum(span, pos_ref[b * ne + e, j + 1] - nominal(b, e, j, 0))
    n_rounds = (span + EWIN - 1) // EWIN

    def extra_round(r, carry):
        start_fetch(0, r, ring)
        wait_fetch(ring)
        acc[...] += add_round(r, ring)
        return carry

    lax.fori_loop(1, n_rounds, extra_round, 0)

    @pl.when(t == n_tiles - 1)
    def _():
        for ahead in range(1, COMBINE_AHEAD + 1):
            wait_fetch(in_ring(t + ahead))

    y = x_ref[...] + acc[...]
    if final_norm:
        ms = jnp.mean(y * y, axis=-1, keepdims=True)
        y = (y * lax.rsqrt(ms + EPS)) * g_ref[...]
    o_ref[...] = y


def _combine(x, y, pos, rank, aff, g_final, *, cap, final_norm):
    b, s, d = x.shape
    nj = s // LANES
    ne = N_EXPERTS
    pos_tab = jnp.concatenate([pos, jnp.full((b, ne, 1), cap, I32)], axis=2).reshape(b * ne, nj + 1)
    rank_t = jnp.transpose(rank, (0, 2, 1, 3))
    return pl.pallas_call(
        functools.partial(_combine_kernel, cap=cap, nj=nj, n_tiles=b * nj, final_norm=final_norm),
        grid_spec=pltpu.PrefetchScalarGridSpec(
            num_scalar_prefetch=1,
            grid=(b, nj),
            in_specs=[pl.BlockSpec((None, None, ne, LANES), lambda bi, j, tab: (bi, j, 0, 0)),
                      pl.BlockSpec((None, ne, LANES), lambda bi, j, tab: (bi, 0, j)),
                      pl.BlockSpec((None, LANES, d), lambda bi, j, tab: (bi, j, 0)),
                      pl.BlockSpec((1, d), lambda bi, j, tab: (0, 0)),
                      pl.BlockSpec(memory_space=pl.ANY)],
            out_specs=pl.BlockSpec((None, LANES, d), lambda bi, j, tab: (bi, j, 0)),
            scratch_shapes=[pltpu.VMEM((COMBINE_AHEAD + 2, ne * EWIN, d), F32), pltpu.VMEM((LANES, d), F32),
                            pltpu.SemaphoreType.DMA((COMBINE_AHEAD + 2,))]),
        out_shape=jax.ShapeDtypeStruct((b, s, d), F32),
        compiler_params=_cparams(("arbitrary", "arbitrary")),
        name="moe_combine",
    )(pos_tab, rank_t, aff, x, g_final.reshape(1, d), y)


def kernel(x, rel_bias, g_mix, w_in, attn_sink, w_attn_proj, w_fourier_proj, w_out, g_ffn, w_router,
           w_exp_gate, w_exp_up, w_exp_down, g_final):
    b, s, d = x.shape
    depth = g_mix.shape[0]
    cap = CAPACITY_FACTOR * s // N_EXPERTS
    bias = _bias_table(rel_bias)
    na = int(round(np.sqrt(s)))
    tables = _fourier_tables(na)
    for l in range(depth):
        q, k, v, vs, f, ga, gb = _inproj(x.reshape(b * s, d), g_mix[l], w_in[l].astype(BF16), tm=1024, chunk=256, na=na)
        shp = lambda a: a.reshape(b, s, a.shape[-1])
        attn = _attention(shp(q), shp(k), shp(v), shp(vs), attn_sink[l], bias, blocks_per_step=8)
        four = _fourier(f.reshape(b, s // 2, f.shape[-1]), tables, rows=8, rows2=16)
        xn, h, aff = _outproj(attn, four, shp(ga), shp(gb), x,
                              w_attn_proj[l].astype(BF16), w_fourier_proj[l].astype(BF16),
                              w_out[l].astype(BF16), g_ffn[l], w_router[l].T, tm=1024, chunk=256)
        idx, pos, rank = _routing(aff, cap=cap)
        y = _expert_ffn(h, idx, w_exp_gate, w_exp_up, w_exp_down, l, cap=cap, tf=256, m_chunk=min(cap, 512))
        x = _combine(xn, y, pos, rank, aff, g_final, cap=cap, final_norm=(l == depth - 1))
    return x
```
